```python
import math
import jax, jax.numpy as jnp
from jax import lax
import numpy as np

D_MODEL = 1024
BATCH = 2
SEQ = 8192
DEPTH = 2

CTX_LEN = 256
GRID_W = 64
N_DIR = 2
EPS = 1e-6
NEG_INF = -1e30

ATT_HEADS = 8
ATT_KV_HEADS = 2
ATT_HEAD_DIM = 64
ATT_WIDTH = ATT_HEADS * ATT_HEAD_DIM
ATT_KV_WIDTH = ATT_KV_HEADS * ATT_HEAD_DIM
WINDOW = 128
ATT_BLOCK = 128
ROPE_BASE = 10000.0

GDN_HEADS = 4
GDN_HEAD_DIM = 128
GDN_WIDTH = GDN_HEADS * GDN_HEAD_DIM
GDN_CHUNK = 64
GDN_CONV = 4

LRU_BLOCKS = 8
LRU_BLOCK_DIM = 64
LRU_WIDTH = LRU_BLOCKS * LRU_BLOCK_DIM
LRU_CONV = 4
LRU_C = 8.0

MIX_WIDTH = ATT_WIDTH + GDN_WIDTH + LRU_WIDTH

IN_SPLITS = (
    ("att_q", ATT_WIDTH), ("att_k", ATT_KV_WIDTH), ("att_v", ATT_KV_WIDTH), ("att_z", ATT_WIDTH),
    ("gdn_q", GDN_WIDTH), ("gdn_k", GDN_WIDTH), ("gdn_v", GDN_WIDTH),
    ("gdn_b", N_DIR * GDN_HEADS), ("gdn_a", N_DIR * GDN_HEADS), ("gdn_z", GDN_WIDTH),
    ("lru_x", LRU_WIDTH), ("lru_z", LRU_WIDTH),
)
IN_WIDTH = sum(w for _, w in IN_SPLITS)

kernel_name = "hybrid_parallel_group_dit_block"

F32 = jnp.float32


def split_cols(p):
    out = {}
    off = 0
    for name, w in IN_SPLITS:
        out[name] = p[..., off:off + w]
        off += w
    return out


def rms_norm(x, g):
    xf = x.astype(F32)
    y = xf * lax.rsqrt(jnp.mean(xf * xf, axis=-1, keepdims=True) + EPS)
    return (y * g.astype(F32)).astype(x.dtype)


def l2norm(t):
    return t * lax.rsqrt(jnp.sum(t * t, axis=-1, keepdims=True) + EPS)


def flip_if(t, d):
    return jnp.flip(t, axis=1) if d == 1 else t


def depthwise_conv(x, w, b=None):
    K, C = w.shape
    left = K // 2
    y = lax.conv_general_dilated(x, w[:, None, :].astype(x.dtype), (1,), [(left, K - 1 - left)],
                                 dimension_numbers=("NWC", "WIO", "NWC"), feature_group_count=C)
    if b is not None:
        y = y + b.astype(y.dtype)
    return y


def axial_rope(rows):
    t = jnp.arange(rows * GRID_W)
    row = (t // GRID_W).astype(F32)
    col = (t % GRID_W).astype(F32)
    n_freq = ATT_HEAD_DIM // 4
    inv = ROPE_BASE ** (-jnp.arange(n_freq, dtype=F32) / n_freq)
    ang = jnp.stack([row[:, None] * inv, col[:, None] * inv], axis=1)
    return jnp.cos(ang), jnp.sin(ang)


def apply_rope(x, cos, sin):
    B, T, H, dh = x.shape
    xr = x.astype(F32).reshape(B, T, H, 2, 2, dh // 4)
    a, b = xr[..., 0, :], xr[..., 1, :]
    c = cos[None, :, None]
    s = sin[None, :, None]
    out = jnp.stack([a * c - b * s, b * c + a * s], axis=-2)
    return out.reshape(B, T, H, dh).astype(x.dtype)


def band_attention(q, k, v, kc, vc, sink):
    B, T, Hq, dh = q.shape
    G = Hq // ATT_KV_HEADS
    nb = T // ATT_BLOCK
    qb = q.reshape(B, nb, ATT_BLOCK, ATT_KV_HEADS, G, dh)

    def band(t):
        tp = jnp.pad(t, ((0, 0), (ATT_BLOCK, ATT_BLOCK), (0, 0), (0, 0)))
        tp = tp.reshape(B, nb + 2, ATT_BLOCK, ATT_KV_HEADS, dh)
        return jnp.concatenate([tp[:, :-2], tp[:, 1:-1], tp[:, 2:]], axis=2)

    kb, vb = band(k), band(v)
    scale = dh ** -0.5
    s_loc = jnp.einsum("bnqhgd,bnkhd->bnhgqk", qb, kb, preferred_element_type=F32) * scale
    s_ctx = jnp.einsum("bnqhgd,bkhd->bnhgqk", qb, kc, preferred_element_type=F32) * scale
    blk = jnp.arange(nb)[:, None] * ATT_BLOCK
    qpos = blk + jnp.arange(ATT_BLOCK)
    kpos = blk - ATT_BLOCK + jnp.arange(3 * ATT_BLOCK)
    ok = ((jnp.abs(qpos[:, :, None] - kpos[:, None, :]) <= WINDOW)
          & (kpos >= 0)[:, None, :] & (kpos < T)[:, None, :])
    s_loc = jnp.where(ok[None, :, None, None], s_loc, NEG_INF)
    sk = sink.astype(F32).reshape(1, 1, ATT_KV_HEADS, G, 1)
    m = jnp.maximum(jnp.maximum(s_loc.max(-1), s_ctx.max(-1)), sk)
    e_loc = jnp.exp(s_loc - m[..., None])
    e_ctx = jnp.exp(s_ctx - m[..., None])
    denom = e_loc.sum(-1) + e_ctx.sum(-1) + jnp.exp(sk - m)
    o = (jnp.einsum("bnhgqk,bnkhd->bnhgqd", e_loc.astype(vb.dtype), vb, preferred_element_type=F32)
         + jnp.einsum("bnhgqk,bkhd->bnhgqd", e_ctx.astype(vc.dtype), vc, preferred_element_type=F32))
    o = o / denom[..., None]
    o = o.transpose(0, 1, 4, 2, 3, 5).reshape(B, T, Hq * dh)
    return o.astype(q.dtype)


def ctx_attention(qc, kc, vc, sink):
    B, L, Hq, dh = qc.shape
    G = Hq // ATT_KV_HEADS
    qg = qc.reshape(B, L, ATT_KV_HEADS, G, dh)
    s = jnp.einsum("bqhgd,bkhd->bhgqk", qg, kc, preferred_element_type=F32) * dh ** -0.5
    s_sink = jnp.broadcast_to(sink.astype(F32).reshape(1, ATT_KV_HEADS, G, 1, 1), (B, ATT_KV_HEADS, G, L, 1))
    p = jax.nn.softmax(jnp.concatenate([s, s_sink], axis=-1), axis=-1)[..., :L]
    o = jnp.einsum("bhgqk,bkhd->bqhgd", p.astype(vc.dtype), vc, preferred_element_type=F32)
    return o.reshape(B, L, Hq * dh).astype(qc.dtype)


def attention_mixer(pl, pc, sink, cos, sin, want_ctx):
    B, T, _ = pl["att_q"].shape
    L = pc["att_q"].shape[1]
    dh = ATT_HEAD_DIM
    q = apply_rope(pl["att_q"].reshape(B, T, ATT_HEADS, dh), cos, sin)
    k = apply_rope(pl["att_k"].reshape(B, T, ATT_KV_HEADS, dh), cos, sin)
    v = pl["att_v"].reshape(B, T, ATT_KV_HEADS, dh)
    kc = pc["att_k"].reshape(B, L, ATT_KV_HEADS, dh)
    vc = pc["att_v"].reshape(B, L, ATT_KV_HEADS, dh)
    o_lat = band_attention(q, k, v, kc, vc, sink) * jax.nn.silu(pl["att_z"])
    o_ctx = None
    if want_ctx:
        qc = pc["att_q"].reshape(B, L, ATT_HEADS, dh)
        o_ctx = ctx_attention(qc, kc, vc, sink) * jax.nn.silu(pc["att_z"])
    return o_lat, o_ctx


def gdn_chunked(q, k, v, beta, g, s0, want_out):
    B, T, H, dk = q.shape
    dv = v.shape[-1]
    C = GDN_CHUNK
    N = T // C

    def chunks(t):
        return jnp.moveaxis(t.reshape(B, N, C, H, *t.shape[3:]), 2, 3)

    q, k, v, beta, g = chunks(q), chunks(k), chunks(v), chunks(beta), chunks(g)
    G = jnp.cumsum(g, axis=-1)
    diff = G[..., :, None] - G[..., None, :]
    strict = jnp.tril(jnp.ones((C, C), bool), -1)
    incl = jnp.tril(jnp.ones((C, C), bool))
    kk = jnp.einsum("bnhid,bnhjd->bnhij", k, k)
    lmat = jnp.where(strict, beta[..., :, None] * jnp.exp(jnp.where(strict, diff, 0.0)) * kk, 0.0)
    a_mat = lmat + jnp.eye(C, dtype=F32)
    gam = jnp.exp(G)
    w = lax.linalg.triangular_solve(a_mat, (beta * gam)[..., None] * k,
                                    left_side=True, lower=True, unit_diagonal=True)
    u = lax.linalg.triangular_solve(a_mat, beta[..., None] * v,
                                    left_side=True, lower=True, unit_diagonal=True)
    kdec = jnp.exp(G[..., -1:] - G)[..., None] * k
    glast = jnp.exp(G[..., -1])
    xs = [w, u, kdec, glast]
    if want_out:
        qk = jnp.einsum("bnhid,bnhjd->bnhij", q, k)
        aqk = jnp.where(incl, jnp.exp(jnp.where(incl, diff, 0.0)) * qk, 0.0)
        xs = xs + [gam[..., None] * q, aqk]
    xs = [jnp.moveaxis(t, 1, 0) for t in xs]

    def step(s, inp):
        w_n, u_n, kdec_n, gl_n = inp[0], inp[1], inp[2], inp[3]
        u_n = u_n - jnp.einsum("bhcd,bhde->bhce", w_n, s)
        s_new = gl_n[..., None, None] * s + jnp.einsum("bhcd,bhce->bhde", kdec_n, u_n)
        if want_out:
            o = (jnp.einsum("bhcd,bhde->bhce", inp[4], s)
                 + jnp.einsum("bhij,bhje->bhie", inp[5], u_n))
            return s_new, o
        return s_new, None

    s_fin, o = lax.scan(step, s0, xs)
    if not want_out:
        return None, s_fin
    o = jnp.moveaxis(jnp.moveaxis(o, 0, 1), 3, 2).reshape(B, T, H, dv)
    return o, s_fin


def gdn_prepare(p, conv_w, a_log, dt_bias):
    qkv = jnp.concatenate([p["gdn_q"], p["gdn_k"], p["gdn_v"]], axis=-1)
    qkv = jax.nn.silu(depthwise_conv(qkv, conv_w)).astype(F32)
    B, T, _ = qkv.shape
    qkv = qkv.reshape(B, T, 3, GDN_HEADS, GDN_HEAD_DIM)
    q = l2norm(qkv[:, :, 0]) * GDN_HEAD_DIM ** -0.5
    k = l2norm(qkv[:, :, 1])
    v = qkv[:, :, 2]
    beta = jax.nn.sigmoid(p["gdn_b"].astype(F32).reshape(B, T, N_DIR, GDN_HEADS))
    g = -jnp.exp(a_log.astype(F32)) * jax.nn.softplus(
        p["gdn_a"].astype(F32).reshape(B, T, N_DIR, GDN_HEADS) + dt_bias.astype(F32))
    return q, k, v, beta, g


def gdn_mixer(pl, pc, conv_w, a_log, dt_bias, norm_w, want_ctx):
    lat = gdn_prepare(pl, conv_w, a_log, dt_bias)
    ctx = gdn_prepare(pc, conv_w, a_log, dt_bias)
    B = lat[0].shape[0]
    s0 = jnp.zeros((B, GDN_HEADS, GDN_HEAD_DIM, GDN_HEAD_DIM), F32)
    o_lat = 0.0
    o_ctx = 0.0
    for d in range(N_DIR):
        qc, kc, vc, bc, gc = ctx
        oc, sc = gdn_chunked(flip_if(qc, d), flip_if(kc, d), flip_if(vc, d),
                             flip_if(bc[:, :, d], d), flip_if(gc[:, :, d], d), s0, want_ctx)
        ql, kl, vl, bl, gl = lat
        ol, _ = gdn_chunked(flip_if(ql, d), flip_if(kl, d), flip_if(vl, d),
                            flip_if(bl[:, :, d], d), flip_if(gl[:, :, d], d), sc, True)
        o_lat = o_lat + flip_if(ol, d)
        if want_ctx:
            o_ctx = o_ctx + flip_if(oc, d)

    def finish(o, z):
        Bo, To = o.shape[:2]
        return rms_norm(o, norm_w).reshape(Bo, To, GDN_WIDTH) * jax.nn.silu(z.astype(F32))

    return finish(o_lat, pl["gdn_z"]), (finish(o_ctx, pc["gdn_z"]) if want_ctx else None)


def linear_scan(a, b, h0):
    b = b.at[:, 0].add(a[:, 0] * h0)

    def comb(l, r):
        return (l[0] * r[0], r[0] * l[1] + r[1])

    _, h = lax.associative_scan(comb, (a, b), axis=1)
    return h


def lru_gates(xc, w_r, b_r, w_i, b_i, lam):
    B, T, W = xc.shape
    xb = xc.reshape(B, T, LRU_BLOCKS, LRU_BLOCK_DIM)
    r = jax.nn.sigmoid(jnp.einsum("btnd,nde->btne", xb, w_r.astype(F32)).reshape(B, T, W) + b_r.astype(F32))
    i = jax.nn.sigmoid(jnp.einsum("btnd,nde->btne", xb, w_i.astype(F32)).reshape(B, T, W) + b_i.astype(F32))
    log_a = -LRU_C * r * jax.nn.softplus(-lam.astype(F32))
    a = jnp.exp(log_a)
    b = jnp.sqrt(-jnp.expm1(2.0 * log_a)) * (i * xc)
    return a, b


def lru_mixer(pl, pc, conv_w, conv_b, w_r, b_r, w_i, b_i, lam, want_ctx):
    xl = depthwise_conv(pl["lru_x"], conv_w, conv_b).astype(F32)
    xc = depthwise_conv(pc["lru_x"], conv_w, conv_b).astype(F32)
    B = xl.shape[0]
    h0 = jnp.zeros((B, LRU_WIDTH), F32)
    h_lat = 0.0
    h_ctx = 0.0
    for d in range(N_DIR):
        ac, bc = lru_gates(flip_if(xc, d), w_r[d], b_r[d], w_i[d], b_i[d], lam[d])
        hc = linear_scan(ac, bc, h0)
        al, bl = lru_gates(flip_if(xl, d), w_r[d], b_r[d], w_i[d], b_i[d], lam[d])
        hl = linear_scan(al, bl, hc[:, -1])
        h_lat = h_lat + flip_if(hl, d)
        if want_ctx:
            h_ctx = h_ctx + flip_if(hc, d)
    o_lat = h_lat * jax.nn.silu(pl["lru_z"].astype(F32))
    o_ctx = h_ctx * jax.nn.silu(pc["lru_z"].astype(F32)) if want_ctx else None
    return o_lat, o_ctx


def setup_inputs(seed: int = 0) -> dict:
    key = jax.random.key(seed)
    ks = jax.random.split(key, 24)
    D = D_MODEL

    def nrm(k, shape, s):
        return jax.random.normal(k, shape, F32) * s

    x = nrm(ks[0], (BATCH, SEQ, D), 1.0)
    c = nrm(ks[1], (BATCH, D), 1.0)
    ctx = nrm(ks[2], (BATCH, CTX_LEN, D), 1.0)
    c_ctx = nrm(ks[3], (D,), 1.0)
    norm_g = 1.0 + nrm(ks[4], (DEPTH, D), 0.1)
    w_mod = nrm(ks[5], (DEPTH, D, 3 * D), 0.5 * D ** -0.5)
    b_mod = nrm(ks[6], (DEPTH, 3 * D), 0.01)
    w_in = nrm(ks[7], (DEPTH, D, IN_WIDTH), D ** -0.5)
    att_sink = nrm(ks[8], (DEPTH, ATT_HEADS), 0.5)
    gdn_conv = nrm(ks[9], (DEPTH, GDN_CONV, 3 * GDN_WIDTH), GDN_CONV ** -0.5)
    gdn_a_log = jnp.log(jax.random.uniform(ks[10], (DEPTH, N_DIR, GDN_HEADS), F32, 1.0, 16.0))
    dt = jnp.exp(jax.random.uniform(ks[11], (DEPTH, N_DIR, GDN_HEADS), F32, math.log(1e-3), math.log(1e-1)))
    gdn_dt_bias = dt + jnp.log(-jnp.expm1(-dt))
    gdn_norm = 1.0 + nrm(ks[12], (DEPTH, GDN_HEAD_DIM), 0.1)
    lru_conv_w = nrm(ks[13], (DEPTH, LRU_CONV, LRU_WIDTH), LRU_CONV ** -0.5)
    lru_conv_b = nrm(ks[14], (DEPTH, LRU_WIDTH), 0.01)
    lru_w_r = nrm(ks[15], (DEPTH, N_DIR, LRU_BLOCKS, LRU_BLOCK_DIM, LRU_BLOCK_DIM), LRU_BLOCK_DIM ** -0.5)
    lru_b_r = nrm(ks[16], (DEPTH, N_DIR, LRU_WIDTH), 0.01)
    lru_w_i = nrm(ks[17], (DEPTH, N_DIR, LRU_BLOCKS, LRU_BLOCK_DIM, LRU_BLOCK_DIM), LRU_BLOCK_DIM ** -0.5)
    lru_b_i = nrm(ks[18], (DEPTH, N_DIR, LRU_WIDTH), 0.01)
    a0 = jax.random.uniform(ks[19], (DEPTH, N_DIR, LRU_WIDTH), F32, 0.9, 0.999) ** (1.0 / LRU_C)
    lru_lambda = jnp.log(a0) - jnp.log1p(-a0)
    w_out = nrm(ks[20], (DEPTH, MIX_WIDTH, D), MIX_WIDTH ** -0.5)
    final_g = 1.0 + nrm(ks[21], (D,), 0.1)
    return {"x": x, "c": c, "ctx": ctx, "c_ctx": c_ctx, "norm_g": norm_g, "w_mod": w_mod, "b_mod": b_mod,
            "w_in": w_in, "att_sink": att_sink, "gdn_conv": gdn_conv, "gdn_a_log": gdn_a_log,
            "gdn_dt_bias": gdn_dt_bias, "gdn_norm": gdn_norm, "lru_conv_w": lru_conv_w,
            "lru_conv_b": lru_conv_b, "lru_w_r": lru_w_r, "lru_b_r": lru_b_r, "lru_w_i": lru_w_i,
            "lru_b_i": lru_b_i, "lru_lambda": lru_lambda, "w_out": w_out, "final_g": final_g}


def reference(x, c, ctx, c_ctx, norm_g, w_mod, b_mod, w_in, att_sink, gdn_conv, gdn_a_log, gdn_dt_bias,
              gdn_norm, lru_conv_w, lru_conv_b, lru_w_r, lru_b_r, lru_w_i, lru_b_i, lru_lambda, w_out,
              final_g):
    B, T, D = x.shape
    ROWS = T // GRID_W
    cos, sin = axial_rope(ROWS)
    y = ctx
    for l in range(DEPTH):
        want_ctx = l < DEPTH - 1
        mod = jax.nn.silu(c) @ w_mod[l] + b_mod[l]
        mod_c = jax.nn.silu(c_ctx) @ w_mod[l] + b_mod[l]
        shift, scale, gate = jnp.split(mod[:, None, :], 3, axis=-1)
        shift_c, scale_c, gate_c = jnp.split(mod_c, 3)
        h = rms_norm(x, norm_g[l]) * (1.0 + scale) + shift
        hc = rms_norm(y, norm_g[l]) * (1.0 + scale_c) + shift_c
        pl = split_cols(h @ w_in[l])
        pc = split_cols(hc @ w_in[l])
        a_l, a_c = attention_mixer(pl, pc, att_sink[l], cos, sin, want_ctx)
        g_l, g_c = gdn_mixer(pl, pc, gdn_conv[l], gdn_a_log[l], gdn_dt_bias[l], gdn_norm[l], want_ctx)
        r_l, r_c = lru_mixer(pl, pc, lru_conv_w[l], lru_conv_b[l], lru_w_r[l], lru_b_r[l],
                             lru_w_i[l], lru_b_i[l], lru_lambda[l], want_ctx)
        o = jnp.concatenate([a_l.astype(x.dtype), g_l.astype(x.dtype), r_l.astype(x.dtype)], axis=-1) @ w_out[l]
        x = x + gate * o
        if want_ctx:
            oc = jnp.concatenate([a_c.astype(y.dtype), g_c.astype(y.dtype), r_c.astype(y.dtype)], axis=-1) @ w_out[l]
            y = y + gate_c * oc
    return rms_norm(x, final_g)
```

```python
import functools
import math

import numpy as np
import jax
import jax.numpy as jnp
from jax import lax
from jax.experimental import pallas as pl
from jax.experimental.pallas import tpu as pltpu

F32 = jnp.float32
BF16 = jnp.bfloat16

EPS = 1e-6
NEG_INF = -1e30
GRID_W = 64
ROPE_BASE = 10000.0

D_MODEL = 1024
ATT_HEADS = 8
ATT_KV_HEADS = 2
ATT_HEAD_DIM = 64
ATT_BLOCK = 128
GDN_HEADS = 4
GDN_HEAD_DIM = 128
GDN_CHUNK = 64
LRU_WIDTH = 512
LRU_BLOCKS = 8
LRU_BLOCK_DIM = 64
LRU_C = 8.0
N_DIR = 2
MIX = 512

TILE = 256
HALO = 8
LANES = 128
VMEM_LIMIT = 56 * 1024 * 1024

C_ATT_Q, C_ATT_Z, C_GDN_Q, C_GDN_K, C_GDN_V, C_GDN_Z, C_LRU_X, C_LRU_Z = (i * MIX for i in range(8))
C_ATT_K = 8 * MIX
C_ATT_V = C_ATT_K + LANES
C_GDN_B = C_ATT_V + LANES
C_GDN_A = C_GDN_B + LANES
N_IN = C_GDN_A + LANES


def _cparams(*sem):
    return pltpu.CompilerParams(dimension_semantics=sem, vmem_limit_bytes=VMEM_LIMIT)


def _sigmoid(x):
    return 1.0 / (1.0 + jnp.exp(-x))


def _silu(x):
    return x * _sigmoid(x)


def _softplus(x):
    return jnp.maximum(x, 0.0) + jnp.log1p(jnp.exp(-jnp.abs(x)))


def _dot(a, b):
    return jnp.dot(a.astype(BF16), b.astype(BF16), preferred_element_type=F32)


def _dot_nt(a, b):
    return lax.dot_general(a.astype(BF16), b.astype(BF16), (((1,), (1,)), ((), ())),
                           preferred_element_type=F32)


def _dot_tn(a, b):
    return lax.dot_general(a.astype(BF16), b.astype(BF16), (((0,), (0,)), ((), ())),
                           preferred_element_type=F32)


def _mod_kernel(ct_ref, w_ref, b_ref, o_ref):
    s = _silu(ct_ref[...])
    w = w_ref[0]
    rid = lax.broadcasted_iota(jnp.int32, (8, w.shape[1]), 0)
    out = jnp.zeros((8, w.shape[1]), F32)
    for r in range(3):
        row = jnp.sum(w * s[:, r:r + 1], axis=0, keepdims=True) + b_ref[0]
        out = jnp.where(rid == r, row, out)
    o_ref[0] = out


def _modulation(ct, w_mod, b_mod):
    depth, d, d3 = w_mod.shape
    return pl.pallas_call(
        _mod_kernel,
        grid=(depth, d3 // d),
        in_specs=[pl.BlockSpec((d, LANES), lambda l, j: (0, 0)),
                  pl.BlockSpec((1, d, d), lambda l, j: (l, 0, j)),
                  pl.BlockSpec((1, 1, d), lambda l, j: (l, 0, j))],
        out_specs=pl.BlockSpec((1, 8, d), lambda l, j: (l, 0, j)),
        out_shape=jax.ShapeDtypeStruct((depth, 8, d3), F32),
        compiler_params=_cparams("parallel", "parallel"),
        name="modulation",
    )(ct, w_mod, b_mod.reshape(depth, 1, d3))


def _rope(x, cos, sin):
    lane = lax.broadcasted_iota(jnp.int32, x.shape, 1)
    first = (lane % 32) < 16
    partner = jnp.where(first, pltpu.roll(x, LANES - 16, 1), pltpu.roll(x, 16, 1))
    return x * cos + partner * sin


def _inproj_kernel(x_ref, mod_ref, g_ref, cos_ref, sin_ref, w_ref, o_ref):
    x = x_ref[0]
    ms = jnp.mean(x * x, axis=-1, keepdims=True)
    y = x * lax.rsqrt(ms + EPS) * g_ref[...]
    shift = mod_ref[0, 0, 0:1, :]
    scale = mod_ref[0, 0, 1:2, :]
    hb = (y * (1.0 + scale) + shift).astype(BF16)
    cos = cos_ref[...]
    sin = sin_ref[...]
    for c0 in range(0, N_IN, MIX):
        res = jnp.dot(hb, w_ref[:, c0:c0 + MIX], preferred_element_type=F32)
        if c0 == C_ATT_Q:
            res = jnp.concatenate(
                [_rope(res[:, k:k + LANES], cos, sin) for k in range(0, MIX, LANES)], axis=1)
        if c0 == C_ATT_K:
            res = jnp.concatenate([_rope(res[:, :LANES], cos, sin), res[:, LANES:]], axis=1)
        o_ref[0, :, c0:c0 + MIX] = res


def _inproj(xs, modsel, g, cos_t, sin_t, w_perm):
    b, s, d = xs.shape
    nt = s // TILE
    return pl.pallas_call(
        _inproj_kernel,
        grid=(b, nt),
        in_specs=[pl.BlockSpec((1, TILE, d), lambda bi, i: (bi, i, 0)),
                  pl.BlockSpec((1, 1, 8, d), lambda bi, i: (bi, jnp.minimum(i, 1), 0, 0)),
                  pl.BlockSpec((1, d), lambda bi, i: (0, 0)),
                  pl.BlockSpec((TILE, LANES), lambda bi, i: (i, 0)),
                  pl.BlockSpec((TILE, LANES), lambda bi, i: (i, 0)),
                  pl.BlockSpec((d, N_IN), lambda bi, i: (0, 0))],
        out_specs=pl.BlockSpec((1, TILE, N_IN), lambda bi, i: (bi, i, 0)),
        out_shape=jax.ShapeDtypeStruct((b, s, N_IN), F32),
        compiler_params=_cparams("parallel", "parallel"),
        name="inproj",
    )(xs, modsel, g, cos_t, sin_t, w_perm)


def _att_kernel(sink_ref, q_ref, z_ref, kvp_ref, kvc_ref, kvn_ref, kvx_ref, o_ref, *, n_blocks):
    m = pl.program_id(1)
    first = TILE // ATT_BLOCK
    c_lo = jnp.where(m >= first + 1, 0, jnp.where(m >= first, ATT_BLOCK, 3 * ATT_BLOCK))
    c_hi = jnp.where(m < first, 0, jnp.where(m <= n_blocks - 2, 3 * ATT_BLOCK, 2 * ATT_BLOCK))

    q = q_ref[0] * (ATT_HEAD_DIM ** -0.5)
    kv = jnp.concatenate([kvp_ref[0], kvc_ref[0], kvn_ref[0], kvx_ref[0]], axis=0)
    k_all = kv[:, :LANES]
    v_all = kv[:, LANES:]
    nk = kv.shape[0]
    half = ATT_HEAD_DIM
    k_sw = pltpu.roll(k_all, half, 1)
    v_sw = pltpu.roll(v_all, half, 1)
    lane_k = lax.broadcasted_iota(jnp.int32, (nk, LANES), 1)
    lo_k = lane_k < half
    k_nat = k_all.astype(BF16)
    k_swb = k_sw.astype(BF16)
    v_nat = [jnp.where(lo_k, v_all, 0.0).astype(BF16), jnp.where(lo_k, 0.0, v_all).astype(BF16)]
    v_swp = [jnp.where(lo_k, v_sw, 0.0).astype(BF16), jnp.where(lo_k, 0.0, v_sw).astype(BF16)]

    r = lax.broadcasted_iota(jnp.int32, (ATT_BLOCK, nk), 0)
    c = lax.broadcasted_iota(jnp.int32, (ATT_BLOCK, nk), 1)
    in_win = jnp.abs(r - (c - ATT_BLOCK)) <= ATT_BLOCK
    blk_ok = jnp.logical_and(c >= c_lo, c < c_hi)
    ok = jnp.logical_or(c >= 3 * ATT_BLOCK, jnp.logical_and(in_win, blk_ok))
    bias = jnp.where(ok, 0.0, NEG_INF)

    lane_q = lax.broadcasted_iota(jnp.int32, (ATT_BLOCK, LANES), 1)
    lo_q = lane_q < half
    group = ATT_HEADS // ATT_KV_HEADS
    outs = []
    for t in range(ATT_HEADS // 2):
        q_t = q[:, t * LANES:(t + 1) * LANES]
        h = (2 * t) // group
        acc = jnp.zeros((ATT_BLOCK, LANES), F32)
        for e in range(2):
            j = 2 * t + e
            qm = jnp.where(lo_q, q_t, 0.0) if e == 0 else jnp.where(lo_q, 0.0, q_t)
            kx = k_nat if e == h else k_swb
            s = _dot_nt(qm, kx) + bias
            sk = sink_ref[j]
            mx = jnp.maximum(jnp.max(s, axis=-1, keepdims=True), sk)
            p = jnp.exp(s - mx)
            den = jnp.sum(p, axis=-1, keepdims=True) + jnp.exp(sk - mx)
            vx = v_nat[e] if e == h else v_swp[e]
            acc = acc + jnp.dot(p.astype(BF16), vx, preferred_element_type=F32) / den
        outs.append(acc)
    o = jnp.concatenate(outs, axis=1)
    o_ref[0] = o * _silu(z_ref[0])


def _attention(p, sink):
    b, s, _ = p.shape
    nb = s // ATT_BLOCK
    first = TILE // ATT_BLOCK
    kvcol = C_ATT_K // (2 * LANES)
    kv_spec = lambda f: pl.BlockSpec((1, ATT_BLOCK, 2 * LANES), f)
    return pl.pallas_call(
        functools.partial(_att_kernel, n_blocks=nb),
        grid=(b, nb),
        in_specs=[pl.BlockSpec(memory_space=pltpu.SMEM),
                  pl.BlockSpec((1, ATT_BLOCK, MIX), lambda bi, m: (bi, m, C_ATT_Q // MIX)),
                  pl.BlockSpec((1, ATT_BLOCK, MIX), lambda bi, m: (bi, m, C_ATT_Z // MIX)),
                  kv_spec(lambda bi, m: (bi, jnp.maximum(m - 1, first), kvcol)),
                  kv_spec(lambda bi, m: (bi, m, kvcol)),
                  kv_spec(lambda bi, m: (bi, jnp.minimum(m + 1, nb - 1), kvcol)),
                  pl.BlockSpec((1, TILE, 2 * LANES), lambda bi, m: (bi, 0, kvcol))],
        out_specs=pl.BlockSpec((1, ATT_BLOCK, MIX), lambda bi, m: (bi, m, 0)),
        out_shape=jax.ShapeDtypeStruct((b, s, MIX), F32),
        compiler_params=_cparams("parallel", "parallel"),
        name="attention",
    )(sink, p, p, p, p, p, p)


def _conv4(x, hp, hn, w, prev_ok, next_ok):
    hp = hp * prev_ok
    hn = hn * next_ok
    ext = jnp.concatenate([hp, x, hn], axis=0)
    n = x.shape[0]
    return (w[0:1] * ext[HALO - 2:HALO - 2 + n] + w[1:2] * ext[HALO - 1:HALO - 1 + n]
            + w[2:3] * ext[HALO:HALO + n] + w[3:4] * ext[HALO + 1:HALO + 1 + n])


def _halo_ok(i, n_tiles):
    prev_ok = (i >= 2).astype(F32)
    next_ok = jnp.logical_and(i != 0, i != n_tiles - 1).astype(F32)
    return prev_ok, next_ok


def _chunk_cumsum(g, reverse):
    n = g.shape[0]
    row = lax.broadcasted_iota(jnp.int32, g.shape, 0) % GDN_CHUNK
    s = 1
    while s < GDN_CHUNK:
        if reverse:
            g = g + jnp.where(row < GDN_CHUNK - s, pltpu.roll(g, n - s, 0), 0.0)
        else:
            g = g + jnp.where(row >= s, pltpu.roll(g, s, 0), 0.0)
        s *= 2
    return g


def _unit_tri_inverse(lm, eye, bd16, bd32):
    mneg = jnp.where(bd16, -lm, 0.0)
    x = eye + mneg
    pw = _dot(mneg, mneg)
    x = x + _dot(x, pw)
    pw = _dot(pw, pw)
    x = x + _dot(x, pw)
    pw = _dot(pw, pw)
    x = x + _dot(x, pw)
    c1 = jnp.where(jnp.logical_and(bd32, jnp.logical_not(bd16)), lm, 0.0)
    x = x - _dot(_dot(x, c1), x)
    c2 = jnp.where(bd32, 0.0, lm)
    x = x - _dot(_dot(x, c2), x)
    return x


def _gdn_prep_kernel(q_ref, qp_ref, qn_ref, k_ref, kp_ref, kn_ref, v_ref, vp_ref, vn_ref,
                     b_ref, a_ref, cw_ref, alog_ref, dtb_ref,
                     w_ref, u_ref, kd_ref, qg_ref, aqk_ref, gl_ref,
                     qs, ks, vs, gs, bs, *, n_tiles):
    i = pl.program_id(1)
    prev_ok, next_ok = _halo_ok(i, n_tiles)
    hd = GDN_HEAD_DIM

    def prep(x_ref, p_ref, n_ref, col):
        return _silu(_conv4(x_ref[0], p_ref[0], n_ref[0], cw_ref[:, col:col + MIX], prev_ok, next_ok))

    def l2n(t):
        parts = []
        for h in range(GDN_HEADS):
            th = t[:, h * hd:(h + 1) * hd]
            parts.append(th * lax.rsqrt(jnp.sum(th * th, axis=-1, keepdims=True) + EPS))
        return jnp.concatenate(parts, axis=1)

    qs[...] = l2n(prep(q_ref, qp_ref, qn_ref, 0)) * (hd ** -0.5)
    ks[...] = l2n(prep(k_ref, kp_ref, kn_ref, MIX))
    vs[...] = prep(v_ref, vp_ref, vn_ref, 2 * MIX)

    lane = lax.broadcasted_iota(jnp.int32, (TILE, LANES), 1)
    g = -jnp.exp(alog_ref[...]) * _softplus(a_ref[0] + dtb_ref[...])
    gs[...] = jnp.where(lane < GDN_HEADS, _chunk_cumsum(g, False), _chunk_cumsum(g, True))
    bs[...] = _sigmoid(b_ref[0])

    ri = lax.broadcasted_iota(jnp.int32, (GDN_CHUNK, GDN_CHUNK), 0)
    ci = lax.broadcasted_iota(jnp.int32, (GDN_CHUNK, GDN_CHUNK), 1)
    eye = (ri == ci).astype(F32)
    bd16 = (ri // 16) == (ci // 16)
    bd32 = (ri // 32) == (ci // 32)
    strict = [ci < ri, ci > ri]
    incl = [ci <= ri, ci >= ri]
    lane_row = lax.broadcasted_iota(jnp.int32, (1, LANES), 1)
    gl_ref[...] = jnp.zeros_like(gl_ref)

    def chunk(cidx, carry):
        r0 = pl.multiple_of(cidx * GDN_CHUNK, GDN_CHUNK)
        rows = pl.ds(r0, GDN_CHUNK)
        gc = gs[rows, :]
        bc = bs[rows, :]
        gct = gc.T
        g_end = jnp.where(lane_row < GDN_HEADS, gc[GDN_CHUNK - 1:GDN_CHUNK, :], gc[0:1, :])
        gl_ref[0, 0, pl.ds(cidx, 1), :] = jnp.exp(g_end)
        for h in range(GDN_HEADS):
            qh = qs[rows, h * hd:(h + 1) * hd]
            kh = ks[rows, h * hd:(h + 1) * hd]
            vh = vs[rows, h * hd:(h + 1) * hd]
            kk = _dot_nt(kh, kh)
            qk = _dot_nt(qh, kh)
            for d in range(N_DIR):
                c = d * GDN_HEADS + h
                gi = gc[:, c:c + 1]
                gj = gct[c:c + 1, :]
                beta = bc[:, c:c + 1]
                diff = gi - gj
                dec = jnp.exp(jnp.where(incl[d], diff, 0.0))
                lm = jnp.where(strict[d], beta * dec * kk, 0.0)
                aqk = jnp.where(incl[d], dec * qk, 0.0)
                tinv = _unit_tri_inverse(lm, eye, bd16, bd32)
                gam = jnp.exp(gi)
                rhs = jnp.concatenate([(beta * gam) * kh, beta * vh], axis=1)
                wu = _dot(tinv, rhs)
                kdec = jnp.exp(g_end[:, c:c + 1] - gi) * kh
                cols = pl.ds(c * hd, hd)
                w_ref[0, rows, cols] = wu[:, :hd].astype(BF16)
                u_ref[0, rows, cols] = wu[:, hd:]
                kd_ref[0, rows, cols] = kdec.astype(BF16)
                qg_ref[0, rows, cols] = (gam * qh).astype(BF16)
                aqk_ref[0, rows, cols] = jnp.concatenate([aqk, jnp.zeros_like(aqk)], axis=1).astype(BF16)
        return carry

    lax.fori_loop(0, TILE // GDN_CHUNK, chunk, 0)


def _gdn_prep(p, conv_w, alog, dtb):
    b, s, _ = p.shape
    nt = s // TILE
    hpt = TILE // HALO
    nh = s // HALO
    wide = N_DIR * GDN_HEADS * GDN_HEAD_DIM

    def tile_spec(col):
        return pl.BlockSpec((1, TILE, MIX), lambda bi, i: (bi, i, col // MIX))

    def prev_spec(col):
        return pl.BlockSpec((1, HALO, MIX), lambda bi, i: (bi, jnp.maximum(i * hpt - 1, 0), col // MIX))

    def next_spec(col):
        return pl.BlockSpec((1, HALO, MIX), lambda bi, i: (bi, jnp.minimum((i + 1) * hpt, nh - 1), col // MIX))

    in_specs = []
    for col in (C_GDN_Q, C_GDN_K, C_GDN_V):
        in_specs += [tile_spec(col), prev_spec(col), next_spec(col)]
    in_specs += [pl.BlockSpec((1, TILE, LANES), lambda bi, i: (bi, i, C_GDN_B // LANES)),
                 pl.BlockSpec((1, TILE, LANES), lambda bi, i: (bi, i, C_GDN_A // LANES)),
                 pl.BlockSpec((4, 3 * MIX), lambda bi, i: (0, 0)),
                 pl.BlockSpec((1, LANES), lambda bi, i: (0, 0)),
                 pl.BlockSpec((1, LANES), lambda bi, i: (0, 0))]
    wide_spec = pl.BlockSpec((1, TILE, wide), lambda bi, i: (bi, i, 0))
    out_specs = [wide_spec, wide_spec, wide_spec, wide_spec,
                 wide_spec,
                 pl.BlockSpec((1, 1, 8, LANES), lambda bi, i: (bi, i, 0, 0))]
    out_shape = [jax.ShapeDtypeStruct((b, s, wide), BF16),
                 jax.ShapeDtypeStruct((b, s, wide), F32),
                 jax.ShapeDtypeStruct((b, s, wide), BF16),
                 jax.ShapeDtypeStruct((b, s, wide), BF16),
                 jax.ShapeDtypeStruct((b, s, wide), BF16),
                 jax.ShapeDtypeStruct((b, nt, 8, LANES), F32)]
    scratch = [pltpu.VMEM((TILE, MIX), F32)] * 3 + [pltpu.VMEM((TILE, LANES), F32)] * 2
    return pl.pallas_call(
        functools.partial(_gdn_prep_kernel, n_tiles=nt),
        grid=(b, nt),
        in_specs=in_specs,
        out_specs=out_specs,
        out_shape=out_shape,
        scratch_shapes=scratch,
        compiler_params=_cparams("parallel", "parallel"),
        name="gdn_prep",
    )(p, p, p, p, p, p, p, p, p, p, p, conv_w, alog, dtb)


def _gdn_scan_kernel(wf, uf, kdf, qgf, af, glf, wb, ub, kdb, qgb, ab, glb, of_ref, ob_ref, state):
    j = pl.program_id(1)

    @pl.when(j == 0)
    def _():
        state[...] = jnp.zeros_like(state)

    hd = GDN_HEAD_DIM
    nch = TILE // GDN_CHUNK
    dirs = ((wf, uf, kdf, qgf, af, glf, of_ref), (wb, ub, kdb, qgb, ab, glb, ob_ref))
    for d, (w_r, u_r, kd_r, qg_r, a_r, gl_r, o_r) in enumerate(dirs):
        for step in range(nch):
            cidx = step if d == 0 else nch - 1 - step
            rows = slice(cidx * GDN_CHUNK, (cidx + 1) * GDN_CHUNK)
            for h in range(GDN_HEADS):
                cols = slice(h * hd, (h + 1) * hd)
                c = d * GDN_HEADS + h
                s = state[d, h]
                sb = s.astype(BF16)
                un = u_r[0, rows, cols] - jnp.dot(w_r[0, rows, cols], sb, preferred_element_type=F32)
                unb = un.astype(BF16)
                o = (jnp.dot(qg_r[0, rows, cols], sb, preferred_element_type=F32)
                     + jnp.dot(a_r[0, rows, h * hd:h * hd + GDN_CHUNK], unb,
                               preferred_element_type=F32))
                gl = gl_r[0, 0, cidx:cidx + 1, c:c + 1]
                state[d, h] = gl * s + _dot_tn(kd_r[0, rows, cols], unb)
                o_r[0, rows, cols] = o


def _mirror(j, nt):
    return jnp.where(j == 0, 0, nt - j)


def _gdn_scan(w, u, kd, qg, aqk, gl):
    b, s, _ = u.shape
    nt = s // TILE
    half = GDN_HEADS * GDN_HEAD_DIM

    def specs(d):
        idx = (lambda bi, j: (bi, j, d)) if d == 0 else (lambda bi, j: (bi, _mirror(j, nt), d))
        gidx = (lambda bi, j: (bi, j, 0, 0)) if d == 0 else (lambda bi, j: (bi, _mirror(j, nt), 0, 0))
        return [pl.BlockSpec((1, TILE, half), idx)] * 5 + [pl.BlockSpec((1, 1, 8, LANES), gidx)]

    return pl.pallas_call(
        _gdn_scan_kernel,
        grid=(b, nt),
        in_specs=specs(0) + specs(1),
        out_specs=[pl.BlockSpec((1, TILE, half), lambda bi, j: (bi, j, 0)),
                   pl.BlockSpec((1, TILE, half), lambda bi, j: (bi, _mirror(j, nt), 0))],
        out_shape=[jax.ShapeDtypeStruct((b, s, half), F32)] * 2,
        scratch_shapes=[pltpu.VMEM((N_DIR, GDN_HEADS, GDN_HEAD_DIM, GDN_HEAD_DIM), F32)],
        compiler_params=_cparams("parallel", "arbitrary"),
        name="gdn_scan",
    )(w, u, kd, qg, aqk, gl, w, u, kd, qg, aqk, gl)


def _lru_kernel(xf_ref, xfp_ref, xfn_ref, xb_ref, xbp_ref, xbn_ref, cw_ref, cb_ref, wg_ref, bg_ref, lam_ref,
                hf_ref, hb_ref, carry, *, n_tiles):
    j = pl.program_id(1)

    @pl.when(j == 0)
    def _():
        carry[...] = jnp.zeros_like(carry)

    groups = TILE // 8
    row8 = lax.broadcasted_iota(jnp.int32, (TILE, LRU_WIDTH), 0) % 8

    def one_dir(d, x_ref, p_ref, n_ref, tile_idx, o_ref):
        prev_ok, next_ok = _halo_ok(tile_idx, n_tiles)
        x = _conv4(x_ref[0], p_ref[0], n_ref[0], cw_ref[...], prev_ok, next_ok) + cb_ref[...]
        gates = jnp.dot(x.astype(BF16), wg_ref[:, d * 2 * LRU_WIDTH:(d + 1) * 2 * LRU_WIDTH],
                        preferred_element_type=F32)
        rg = _sigmoid(gates[:, :LRU_WIDTH] + bg_ref[2 * d:2 * d + 1, :])
        ig = _sigmoid(gates[:, LRU_WIDTH:] + bg_ref[2 * d + 1:2 * d + 2, :])
        log_a = -LRU_C * rg * _softplus(-lam_ref[d:d + 1, :])
        a = jnp.exp(log_a)
        bb = jnp.sqrt(-jnp.tanh(log_a) * (a * a + 1.0)) * (ig * x)
        s = 1
        while s < 8:
            if d == 0:
                keep = row8 >= s
                a_sh = jnp.where(keep, pltpu.roll(a, s, 0), 1.0)
                b_sh = jnp.where(keep, pltpu.roll(bb, s, 0), 0.0)
            else:
                keep = row8 < 8 - s
                a_sh = jnp.where(keep, pltpu.roll(a, TILE - s, 0), 1.0)
                b_sh = jnp.where(keep, pltpu.roll(bb, TILE - s, 0), 0.0)
            bb = a * b_sh + bb
            a = a * a_sh
            s *= 2
        h = carry[d]
        order = range(groups) if d == 0 else range(groups - 1, -1, -1)
        last = 7 if d == 0 else 0
        pieces = [None] * groups
        for gidx in order:
            ag = a[gidx * 8:(gidx + 1) * 8]
            bg = bb[gidx * 8:(gidx + 1) * 8]
            hg = ag * h + bg
            pieces[gidx] = hg
            h = hg[last:last + 1]
        carry[d] = h
        o_ref[0] = jnp.concatenate(pieces, axis=0)

    one_dir(0, xf_ref, xfp_ref, xfn_ref, j, hf_ref)
    one_dir(1, xb_ref, xbp_ref, xbn_ref, _mirror(j, n_tiles), hb_ref)


def _lru(p, conv_w, conv_b, wg, bg, lam):
    b, s, _ = p.shape
    nt = s // TILE
    hpt = TILE // HALO
    nh = s // HALO
    col = C_LRU_X // MIX

    def specs(tile_of):
        return [pl.BlockSpec((1, TILE, MIX), lambda bi, j: (bi, tile_of(j), col)),
                pl.BlockSpec((1, HALO, MIX), lambda bi, j: (bi, jnp.maximum(tile_of(j) * hpt - 1, 0), col)),
                pl.BlockSpec((1, HALO, MIX),
                             lambda bi, j: (bi, jnp.minimum((tile_of(j) + 1) * hpt, nh - 1), col))]

    full = lambda shape: pl.BlockSpec(shape, lambda bi, j: tuple(0 for _ in shape))
    return pl.pallas_call(
        functools.partial(_lru_kernel, n_tiles=nt),
        grid=(b, nt),
        in_specs=specs(lambda j: j) + specs(lambda j: _mirror(j, nt)) + [
            full((4, LRU_WIDTH)), full((1, LRU_WIDTH)), full((LRU_WIDTH, 4 * LRU_WIDTH)),
            full((4, LRU_WIDTH)), full((N_DIR, LRU_WIDTH))],
        out_specs=[pl.BlockSpec((1, TILE, MIX), lambda bi, j: (bi, j, 0)),
                   pl.BlockSpec((1, TILE, MIX), lambda bi, j: (bi, _mirror(j, nt), 0))],
        out_shape=[jax.ShapeDtypeStruct((b, s, MIX), F32)] * 2,
        scratch_shapes=[pltpu.VMEM((N_DIR, 1, LRU_WIDTH), F32)],
        compiler_params=_cparams("parallel", "arbitrary"),
        name="lru",
    )(p, p, p, p, p, p, conv_w, conv_b, wg, bg, lam)


def _outproj_kernel(x_ref, mod_ref, att_ref, of_ref, ob_ref, gz_ref, hf_ref, hb_ref, lz_ref,
                    gn_ref, w_ref, fg_ref, o_ref, *, final):
    hd = GDN_HEAD_DIM
    o = of_ref[0] + ob_ref[0]
    parts = []
    for h in range(GDN_HEADS):
        oh = o[:, h * hd:(h + 1) * hd]
        parts.append(oh * lax.rsqrt(jnp.mean(oh * oh, axis=-1, keepdims=True) + EPS) * gn_ref[...])
    gdn = jnp.concatenate(parts, axis=1) * _silu(gz_ref[0])
    lru = (hf_ref[0] + hb_ref[0]) * _silu(lz_ref[0])
    acc = (jnp.dot(att_ref[0].astype(BF16), w_ref[0:MIX, :], preferred_element_type=F32)
           + jnp.dot(gdn.astype(BF16), w_ref[MIX:2 * MIX, :], preferred_element_type=F32)
           + jnp.dot(lru.astype(BF16), w_ref[2 * MIX:3 * MIX, :], preferred_element_type=F32))
    gate = mod_ref[0, 0, 2:3, :]
    xn = x_ref[0] + gate * acc
    if final:
        ms = jnp.mean(xn * xn, axis=-1, keepdims=True)
        xn = xn * lax.rsqrt(ms + EPS) * fg_ref[...]
    o_ref[0] = xn


def _outproj(xs, modsel, att, of, ob, p, hf, hb, gdn_norm, w_out, final_g, final):
    b, s, d = xs.shape
    nt = s // TILE
    off = 1 if final else 0
    n_out = nt - off
    tok = lambda col: pl.BlockSpec((1, TILE, MIX), lambda bi, i: (bi, i + off, col))
    return pl.pallas_call(
        functools.partial(_outproj_kernel, final=final),
        grid=(b, n_out),
        in_specs=[pl.BlockSpec((1, TILE, d), lambda bi, i: (bi, i + off, 0)),
                  pl.BlockSpec((1, 1, 8, d), lambda bi, i: (bi, jnp.minimum(i + off, 1), 0, 0)),
                  tok(0), tok(0), tok(0), tok(C_GDN_Z // MIX), tok(0), tok(0), tok(C_LRU_Z // MIX),
                  pl.BlockSpec((1, GDN_HEAD_DIM), lambda bi, i: (0, 0)),
                  pl.BlockSpec((3 * MIX, d), lambda bi, i: (0, 0)),
                  pl.BlockSpec((1, d), lambda bi, i: (0, 0))],
        out_specs=pl.BlockSpec((1, TILE, d), lambda bi, i: (bi, i, 0)),
        out_shape=jax.ShapeDtypeStruct((b, n_out * TILE, d), F32),
        compiler_params=_cparams("parallel", "parallel"),
        name="outproj",
    )(xs, modsel, att, of, ob, p, hf, hb, p, gdn_norm, w_out, final_g)


def _rope_tables(s, ctx_len):
    t = np.arange(s - ctx_len)
    n_freq = ATT_HEAD_DIM // 4
    inv = ROPE_BASE ** (-np.arange(n_freq, dtype=np.float64) / n_freq)
    ang = [(t // GRID_W)[:, None] * inv, (t % GRID_W)[:, None] * inv]
    cos = np.concatenate([np.cos(ang[0]), np.cos(ang[0]), np.cos(ang[1]), np.cos(ang[1])], axis=1)
    sin = np.concatenate([-np.sin(ang[0]), np.sin(ang[0]), -np.sin(ang[1]), np.sin(ang[1])], axis=1)
    cos = np.concatenate([np.ones((ctx_len, ATT_HEAD_DIM)), cos], axis=0)
    sin = np.concatenate([np.zeros((ctx_len, ATT_HEAD_DIM)), sin], axis=0)
    return (jnp.asarray(np.tile(cos, (1, 2)), F32), jnp.asarray(np.tile(sin, (1, 2)), F32))


def _permute_w_in(w):
    widths = (512, 128, 128, 512, 512, 512, 512, 8, 8, 512, 512, 512)
    offs = np.concatenate([[0], np.cumsum(widths)])
    names = ("att_q", "att_k", "att_v", "att_z", "gdn_q", "gdn_k", "gdn_v", "gdn_b", "gdn_a", "gdn_z",
             "lru_x", "lru_z")
    col = {n: w[:, offs[i]:offs[i + 1]] for i, n in enumerate(names)}
    pad = jnp.zeros((w.shape[0], LANES - 8), w.dtype)
    return jnp.concatenate(
        [col["att_q"], col["att_z"], col["gdn_q"], col["gdn_k"], col["gdn_v"], col["gdn_z"], col["lru_x"],
         col["lru_z"], col["att_k"], col["att_v"], col["gdn_b"], pad, col["gdn_a"], pad], axis=1)


def _block_diag(w):
    eye = jnp.eye(LRU_BLOCKS, dtype=w.dtype)
    return jnp.einsum("nde,nm->ndme", w, eye).reshape(LRU_WIDTH, LRU_WIDTH)


def _pad_lanes(v):
    return jnp.pad(v.reshape(1, -1), ((0, 0), (0, LANES - v.size)))


def kernel(x, c, ctx, c_ctx, norm_g, w_mod, b_mod, w_in, att_sink, gdn_conv, gdn_a_log, gdn_dt_bias, gdn_norm,
           lru_conv_w, lru_conv_b, lru_w_r, lru_b_r, lru_w_i, lru_b_i, lru_lambda, w_out, final_g):
    b, t, d = x.shape
    ctx_len = ctx.shape[1]
    depth = w_in.shape[0]
    assert ctx_len == TILE and t % TILE == 0 and d == D_MODEL and b == 2
    s = ctx_len + t
    nt = s // TILE

    cos_t, sin_t = _rope_tables(s, ctx_len)
    ct = jnp.pad(jnp.concatenate([c, c_ctx[None]], axis=0).T, ((0, 0), (0, LANES - b - 1)))
    mods = _modulation(ct, w_mod, b_mod)

    xs = jnp.concatenate([ctx, x], axis=1)
    out = None
    for l in range(depth):
        m3 = mods[l, :3].reshape(3, 3, d)
        sel = jnp.stack([jnp.stack([m3[b], m3[bi]], axis=0) for bi in range(b)], axis=0)
        modsel = jnp.pad(sel, ((0, 0), (0, 0), (0, 5), (0, 0)))
        w_perm = _permute_w_in(w_in[l]).astype(BF16)
        p = _inproj(xs, modsel, norm_g[l].reshape(1, d), cos_t, sin_t, w_perm)

        att = _attention(p, att_sink[l])

        prep = _gdn_prep(p, gdn_conv[l], _pad_lanes(gdn_a_log[l]), _pad_lanes(gdn_dt_bias[l]))
        o_f, o_b = _gdn_scan(*prep)

        wg = jnp.concatenate([_block_diag(lru_w_r[l, 0]), _block_diag(lru_w_i[l, 0]),
                              _block_diag(lru_w_r[l, 1]), _block_diag(lru_w_i[l, 1])], axis=1).astype(BF16)
        bg = jnp.stack([lru_b_r[l, 0], lru_b_i[l, 0], lru_b_r[l, 1], lru_b_i[l, 1]], axis=0)
        h_f, h_b = _lru(p, lru_conv_w[l], lru_conv_b[l].reshape(1, -1), wg, bg, lru_lambda[l])

        final = l == depth - 1
        res = _outproj(xs, modsel, att, o_f, o_b, p, h_f, h_b, gdn_norm[l].reshape(1, -1),
                       w_out[l].astype(BF16), final_g.reshape(1, d), final)
        if final:
            out = res
        else:
            xs = res
    return out
```

```python
import functools
import math

import numpy as np
import jax
import jax.numpy as jnp
from jax import lax
from jax.experimental import pallas as pl
from jax.experimental.pallas import tpu as pltpu

F32 = jnp.float32
BF16 = jnp.bfloat16

EPS = 1e-6
NEG_INF = -1e30
GRID_W = 64
ROPE_BASE = 10000.0

D_MODEL = 1024
ATT_HEADS = 8
ATT_KV_HEADS = 2
ATT_HEAD_DIM = 64
ATT_BLOCK = 128
GDN_HEADS = 4
GDN_HEAD_DIM = 128
GDN_CHUNK = 64
LRU_WIDTH = 512
LRU_BLOCKS = 8
LRU_BLOCK_DIM = 64
LRU_C = 8.0
N_DIR = 2
MIX = 512
PACK = GDN_HEADS * GDN_CHUNK

TILE = 256
HALO = 8
LANES = 128
VMEM_LIMIT = 56 * 1024 * 1024

C_ATT_Q, C_ATT_Z, C_GDN_Q, C_GDN_K, C_GDN_V, C_GDN_Z, C_LRU_X, C_LRU_Z = (i * MIX for i in range(8))
C_ATT_K = 8 * MIX
C_ATT_V = C_ATT_K + LANES
C_GDN_B = C_ATT_V + LANES
C_GDN_A = C_GDN_B + LANES
N_IN = C_GDN_A + LANES


def _cparams(*sem):
    return pltpu.CompilerParams(dimension_semantics=sem, vmem_limit_bytes=VMEM_LIMIT)


def _sigmoid(x):
    return 1.0 / (1.0 + jnp.exp(-x))


def _silu(x):
    return x * _sigmoid(x)


def _softplus(x):
    return jnp.maximum(x, 0.0) + jnp.log1p(jnp.exp(-jnp.abs(x)))


def _dot(a, b):
    return jnp.dot(a.astype(BF16), b.astype(BF16), preferred_element_type=F32)


def _dot_nt(a, b):
    return lax.dot_general(a.astype(BF16), b.astype(BF16), (((1,), (1,)), ((), ())),
                           preferred_element_type=F32)


def _dot_tn(a, b):
    return lax.dot_general(a.astype(BF16), b.astype(BF16), (((0,), (0,)), ((), ())),
                           preferred_element_type=F32)


def _mod_kernel(ct_ref, w_ref, b_ref, o_ref):
    s = _silu(ct_ref[...])
    w = w_ref[0]
    rid = lax.broadcasted_iota(jnp.int32, (8, w.shape[1]), 0)
    out = jnp.zeros((8, w.shape[1]), F32)
    for r in range(3):
        row = jnp.sum(w * s[:, r:r + 1], axis=0, keepdims=True) + b_ref[0]
        out = jnp.where(rid == r, row, out)
    o_ref[0] = out


def _modulation(ct, w_mod, b_mod):
    depth, d, d3 = w_mod.shape
    return pl.pallas_call(
        _mod_kernel,
        grid=(depth, d3 // d),
        in_specs=[pl.BlockSpec((d, LANES), lambda l, j: (0, 0)),
                  pl.BlockSpec((1, d, d), lambda l, j: (l, 0, j)),
                  pl.BlockSpec((1, 1, d), lambda l, j: (l, 0, j))],
        out_specs=pl.BlockSpec((1, 8, d), lambda l, j: (l, 0, j)),
        out_shape=jax.ShapeDtypeStruct((depth, 8, d3), F32),
        compiler_params=_cparams("parallel", "parallel"),
        name="modulation",
    )(ct, w_mod, b_mod.reshape(depth, 1, d3))


def _rope(x, cos, sin):
    lane = lax.broadcasted_iota(jnp.int32, x.shape, 1)
    first = (lane % 32) < 16
    partner = jnp.where(first, pltpu.roll(x, LANES - 16, 1), pltpu.roll(x, 16, 1))
    return x * cos + partner * sin


def _inproj_kernel(x_ref, mod_ref, g_ref, cos_ref, sin_ref, w_ref, o_ref):
    x = x_ref[0]
    ms = jnp.mean(x * x, axis=-1, keepdims=True)
    y = x * lax.rsqrt(ms + EPS) * g_ref[...]
    shift = mod_ref[0, 0, 0:1, :]
    scale = mod_ref[0, 0, 1:2, :]
    hb = (y * (1.0 + scale) + shift).astype(BF16)
    cos = cos_ref[...]
    sin = sin_ref[...]
    for c0 in range(0, N_IN, MIX):
        res = jnp.dot(hb, w_ref[:, c0:c0 + MIX], preferred_element_type=F32)
        if c0 == C_ATT_Q:
            res = jnp.concatenate(
                [_rope(res[:, k:k + LANES], cos, sin) for k in range(0, MIX, LANES)], axis=1)
        if c0 == C_ATT_K:
            res = jnp.concatenate([_rope(res[:, :LANES], cos, sin), res[:, LANES:]], axis=1)
        o_ref[0, :, c0:c0 + MIX] = res


def _inproj(xs, modsel, g, cos_t, sin_t, w_perm):
    b, s, d = xs.shape
    nt = s // TILE
    return pl.pallas_call(
        _inproj_kernel,
        grid=(b, nt),
        in_specs=[pl.BlockSpec((1, TILE, d), lambda bi, i: (bi, i, 0)),
                  pl.BlockSpec((1, 1, 8, d), lambda bi, i: (bi, jnp.minimum(i, 1), 0, 0)),
                  pl.BlockSpec((1, d), lambda bi, i: (0, 0)),
                  pl.BlockSpec((TILE, LANES), lambda bi, i: (i, 0)),
                  pl.BlockSpec((TILE, LANES), lambda bi, i: (i, 0)),
                  pl.BlockSpec((d, N_IN), lambda bi, i: (0, 0))],
        out_specs=pl.BlockSpec((1, TILE, N_IN), lambda bi, i: (bi, i, 0)),
        out_shape=jax.ShapeDtypeStruct((b, s, N_IN), F32),
        compiler_params=_cparams("parallel", "parallel"),
        name="inproj",
    )(xs, modsel, g, cos_t, sin_t, w_perm)


def _att_kernel(sink_ref, q_ref, z_ref, kvp_ref, kvc_ref, kvn_ref, kvx_ref, o_ref, *, n_blocks):
    m = pl.program_id(1)
    first = TILE // ATT_BLOCK
    c_lo = jnp.where(m >= first + 1, 0, jnp.where(m >= first, ATT_BLOCK, 3 * ATT_BLOCK))
    c_hi = jnp.where(m < first, 0, jnp.where(m <= n_blocks - 2, 3 * ATT_BLOCK, 2 * ATT_BLOCK))

    q = q_ref[0] * (ATT_HEAD_DIM ** -0.5)
    kv = jnp.concatenate([kvp_ref[0], kvc_ref[0], kvn_ref[0], kvx_ref[0]], axis=0)
    k_all = kv[:, :LANES]
    v_all = kv[:, LANES:]
    nk = kv.shape[0]
    half = ATT_HEAD_DIM
    k_sw = pltpu.roll(k_all, half, 1)
    v_sw = pltpu.roll(v_all, half, 1)
    lane_k = lax.broadcasted_iota(jnp.int32, (nk, LANES), 1)
    lo_k = lane_k < half
    k_nat = k_all.astype(BF16)
    k_swb = k_sw.astype(BF16)
    v_nat = [jnp.where(lo_k, v_all, 0.0).astype(BF16), jnp.where(lo_k, 0.0, v_all).astype(BF16)]
    v_swp = [jnp.where(lo_k, v_sw, 0.0).astype(BF16), jnp.where(lo_k, 0.0, v_sw).astype(BF16)]

    r = lax.broadcasted_iota(jnp.int32, (ATT_BLOCK, nk), 0)
    c = lax.broadcasted_iota(jnp.int32, (ATT_BLOCK, nk), 1)
    in_win = jnp.abs(r - (c - ATT_BLOCK)) <= ATT_BLOCK
    blk_ok = jnp.logical_and(c >= c_lo, c < c_hi)
    ok = jnp.logical_or(c >= 3 * ATT_BLOCK, jnp.logical_and(in_win, blk_ok))
    bias = jnp.where(ok, 0.0, NEG_INF)

    lane_q = lax.broadcasted_iota(jnp.int32, (ATT_BLOCK, LANES), 1)
    lo_q = lane_q < half
    group = ATT_HEADS // ATT_KV_HEADS
    outs = []
    for t in range(ATT_HEADS // 2):
        q_t = q[:, t * LANES:(t + 1) * LANES]
        h = (2 * t) // group
        acc = jnp.zeros((ATT_BLOCK, LANES), F32)
        for e in range(2):
            j = 2 * t + e
            qm = jnp.where(lo_q, q_t, 0.0) if e == 0 else jnp.where(lo_q, 0.0, q_t)
            kx = k_nat if e == h else k_swb
            s = _dot_nt(qm, kx) + bias
            sk = sink_ref[j]
            mx = jnp.maximum(jnp.max(s, axis=-1, keepdims=True), sk)
            p = jnp.exp(s - mx)
            den = jnp.sum(p, axis=-1, keepdims=True) + jnp.exp(sk - mx)
            vx = v_nat[e] if e == h else v_swp[e]
            acc = acc + jnp.dot(p.astype(BF16), vx, preferred_element_type=F32) / den
        outs.append(acc)
    o = jnp.concatenate(outs, axis=1)
    o_ref[0] = o * _silu(z_ref[0])


def _attention(p, sink):
    b, s, _ = p.shape
    nb = s // ATT_BLOCK
    first = TILE // ATT_BLOCK
    kvcol = C_ATT_K // (2 * LANES)
    kv_spec = lambda f: pl.BlockSpec((1, ATT_BLOCK, 2 * LANES), f)
    return pl.pallas_call(
        functools.partial(_att_kernel, n_blocks=nb),
        grid=(b, nb),
        in_specs=[pl.BlockSpec(memory_space=pltpu.SMEM),
                  pl.BlockSpec((1, ATT_BLOCK, MIX), lambda bi, m: (bi, m, C_ATT_Q // MIX)),
                  pl.BlockSpec((1, ATT_BLOCK, MIX), lambda bi, m: (bi, m, C_ATT_Z // MIX)),
                  kv_spec(lambda bi, m: (bi, jnp.maximum(m - 1, first), kvcol)),
                  kv_spec(lambda bi, m: (bi, m, kvcol)),
                  kv_spec(lambda bi, m: (bi, jnp.minimum(m + 1, nb - 1), kvcol)),
                  pl.BlockSpec((1, TILE, 2 * LANES), lambda bi, m: (bi, 0, kvcol))],
        out_specs=pl.BlockSpec((1, ATT_BLOCK, MIX), lambda bi, m: (bi, m, 0)),
        out_shape=jax.ShapeDtypeStruct((b, s, MIX), F32),
        compiler_params=_cparams("parallel", "parallel"),
        name="attention",
    )(sink, p, p, p, p, p, p)


def _conv4(x, hp, hn, w, prev_ok, next_ok):
    hp = hp * prev_ok
    hn = hn * next_ok
    ext = jnp.concatenate([hp, x, hn], axis=0)
    n = x.shape[0]
    return (w[0:1] * ext[HALO - 2:HALO - 2 + n] + w[1:2] * ext[HALO - 1:HALO - 1 + n]
            + w[2:3] * ext[HALO:HALO + n] + w[3:4] * ext[HALO + 1:HALO + 1 + n])


def _halo_ok(i, n_tiles):
    prev_ok = (i >= 2).astype(F32)
    next_ok = jnp.logical_and(i != 0, i != n_tiles - 1).astype(F32)
    return prev_ok, next_ok


def _chunk_cumsum(g, reverse):
    n = g.shape[0]
    row = lax.broadcasted_iota(jnp.int32, g.shape, 0) % GDN_CHUNK
    s = 1
    while s < GDN_CHUNK:
        if reverse:
            g = g + jnp.where(row < GDN_CHUNK - s, pltpu.roll(g, n - s, 0), 0.0)
        else:
            g = g + jnp.where(row >= s, pltpu.roll(g, s, 0), 0.0)
        s *= 2
    return g


def _spread(cols, masks):
    out = cols[-1]
    for m, col in zip(masks[-2::-1], cols[-2::-1]):
        out = jnp.where(m, col, out)
    return out


def _head_block_diag(x, masks):
    return jnp.concatenate([jnp.where(m, x, 0.0) for m in masks], axis=0).astype(BF16)


def _gdn_prep_kernel(q_ref, qp_ref, qn_ref, k_ref, kp_ref, kn_ref, v_ref, vp_ref, vn_ref,
                     b_ref, a_ref, cw_ref, alog_ref, dtb_ref,
                     w_ref, u_ref, kd_ref, qg_ref, aqk_ref, gl_ref, *, n_tiles):
    i = pl.program_id(1)
    prev_ok, next_ok = _halo_ok(i, n_tiles)
    hd, ch, nh = GDN_HEAD_DIM, GDN_CHUNK, GDN_HEADS
    nch = TILE // ch

    def prep(x_ref, p_ref, n_ref, col):
        return _silu(_conv4(x_ref[0], p_ref[0], n_ref[0], cw_ref[:, col:col + MIX], prev_ok, next_ok))

    def l2n(t):
        parts = []
        for h in range(nh):
            th = t[:, h * hd:(h + 1) * hd]
            parts.append(th * lax.rsqrt(jnp.sum(th * th, axis=-1, keepdims=True) + EPS))
        return jnp.concatenate(parts, axis=1)

    q = l2n(prep(q_ref, qp_ref, qn_ref, 0)) * (hd ** -0.5)
    k = l2n(prep(k_ref, kp_ref, kn_ref, MIX))
    v = prep(v_ref, vp_ref, vn_ref, 2 * MIX)

    lane = lax.broadcasted_iota(jnp.int32, (TILE, LANES), 1)
    g = -jnp.exp(alog_ref[...]) * _softplus(a_ref[0] + dtb_ref[...])
    gcum = jnp.where(lane < nh, _chunk_cumsum(g, False), _chunk_cumsum(g, True))
    beta = _sigmoid(b_ref[0])

    ii = lax.broadcasted_iota(jnp.int32, (ch, PACK), 0)
    ll = lax.broadcasted_iota(jnp.int32, (ch, PACK), 1)
    jj = ll % ch
    eye_p = jj == ii
    eye_f = eye_p.astype(F32)
    bd16 = (jj // 16) == (ii // 16)
    bd32 = (jj // 32) == (ii // 32)
    merge_masks = (jnp.logical_and(bd32, jnp.logical_not(bd16)), jnp.logical_not(bd32))
    strict = [jj < ii, jj > ii]
    incl = [jj <= ii, jj >= ii]
    head_p = [(ll // ch) == h for h in range(nh)]
    lw = lax.broadcasted_iota(jnp.int32, (ch, MIX), 1) // hd
    head_w = [lw == h for h in range(nh)]
    lane_row = lax.broadcasted_iota(jnp.int32, (1, LANES), 1)

    def pdot(a, b):
        return jnp.dot(a.astype(BF16), _head_block_diag(b, head_p), preferred_element_type=F32)

    gl_rows = []
    units = []
    for cidx in range(nch):
        rows = slice(cidx * ch, (cidx + 1) * ch)
        kc, qc, vc, gc, bc = k[rows], q[rows], v[rows], gcum[rows], beta[rows]
        kq = _dot_nt(jnp.concatenate([kc, qc], axis=0), _head_block_diag(kc, head_w))
        g_end = jnp.where(lane_row < nh, gc[ch - 1:ch, :], gc[0:1, :])
        gl_rows.append(jnp.exp(g_end))
        for d in range(N_DIR):
            gcols = [gc[:, d * nh + h:d * nh + h + 1] for h in range(nh)]
            bcols = [bc[:, d * nh + h:d * nh + h + 1] for h in range(nh)]
            gcol_p = _spread(gcols, head_p)
            grow_p = jnp.sum(jnp.where(eye_p, gcol_p, 0.0), axis=0, keepdims=True)
            dec = jnp.exp(jnp.where(incl[d], gcol_p - grow_p, 0.0))
            lm = jnp.where(strict[d], _spread(bcols, head_p) * dec * kq[:ch], 0.0)
            aqk = jnp.where(incl[d], dec * kq[ch:], 0.0)
            aqk_ref[0, rows, d * PACK:(d + 1) * PACK] = aqk.astype(BF16)
            units.append((rows, d, lm, kc, qc, vc, gcols, bcols,
                          [g_end[:, d * nh + h:d * nh + h + 1] for h in range(nh)]))
    gl_rows.append(jnp.zeros((8 - nch, LANES), F32))
    gl_ref[0, 0] = jnp.concatenate(gl_rows, axis=0)

    lms = [u[2] for u in units]
    m = [jnp.where(bd16, -lm, 0.0) for lm in lms]
    x = [eye_f + mi for mi in m]
    pw = [pdot(mi, mi) for mi in m]
    for _ in range(2):
        r = [pdot(jnp.concatenate([xi, pi], axis=0), pi) for xi, pi in zip(x, pw)]
        x = [xi + ri[:ch] for xi, ri in zip(x, r)]
        pw = [ri[ch:] for ri in r]
    x = [xi + pdot(xi, pi) for xi, pi in zip(x, pw)]
    for mask in merge_masks:
        y = [pdot(xi, jnp.where(mask, lm, 0.0)) for xi, lm in zip(x, lms)]
        x = [xi - pdot(yi, xi) for xi, yi in zip(x, y)]

    for tinv, (rows, d, _, kc, qc, vc, gcols, bcols, gend_cols) in zip(x, units):
        g_w = _spread(gcols, head_w)
        beta_w = _spread(bcols, head_w)
        gam_w = jnp.exp(g_w)
        rhs = jnp.concatenate([_head_block_diag(beta_w * gam_w * kc, head_w), _head_block_diag(beta_w * vc, head_w)],
                              axis=1)
        wu = jnp.dot(tinv.astype(BF16), rhs, preferred_element_type=F32)
        cols = slice(d * MIX, (d + 1) * MIX)
        w_ref[0, rows, cols] = wu[:, :MIX].astype(BF16)
        u_ref[0, rows, cols] = wu[:, MIX:]
        kd_ref[0, rows, cols] = (jnp.exp(_spread(gend_cols, head_w) - g_w) * kc).astype(BF16)
        qg_ref[0, rows, cols] = (gam_w * qc).astype(BF16)


def _gdn_prep(p, conv_w, alog, dtb):
    b, s, _ = p.shape
    nt = s // TILE
    hpt = TILE // HALO
    nh = s // HALO
    wide = N_DIR * GDN_HEADS * GDN_HEAD_DIM

    def tile_spec(col):
        return pl.BlockSpec((1, TILE, MIX), lambda bi, i: (bi, i, col // MIX))

    def prev_spec(col):
        return pl.BlockSpec((1, HALO, MIX), lambda bi, i: (bi, jnp.maximum(i * hpt - 1, 0), col // MIX))

    def next_spec(col):
        return pl.BlockSpec((1, HALO, MIX), lambda bi, i: (bi, jnp.minimum((i + 1) * hpt, nh - 1), col // MIX))

    in_specs = []
    for col in (C_GDN_Q, C_GDN_K, C_GDN_V):
        in_specs += [tile_spec(col), prev_spec(col), next_spec(col)]
    in_specs += [pl.BlockSpec((1, TILE, LANES), lambda bi, i: (bi, i, C_GDN_B // LANES)),
                 pl.BlockSpec((1, TILE, LANES), lambda bi, i: (bi, i, C_GDN_A // LANES)),
                 pl.BlockSpec((4, 3 * MIX), lambda bi, i: (0, 0)),
                 pl.BlockSpec((1, LANES), lambda bi, i: (0, 0)),
                 pl.BlockSpec((1, LANES), lambda bi, i: (0, 0))]
    wide_spec = pl.BlockSpec((1, TILE, wide), lambda bi, i: (bi, i, 0))
    out_specs = [wide_spec, wide_spec, wide_spec, wide_spec,
                 pl.BlockSpec((1, TILE, N_DIR * PACK), lambda bi, i: (bi, i, 0)),
                 pl.BlockSpec((1, 1, 8, LANES), lambda bi, i: (bi, i, 0, 0))]
    out_shape = [jax.ShapeDtypeStruct((b, s, wide), BF16),
                 jax.ShapeDtypeStruct((b, s, wide), F32),
                 jax.ShapeDtypeStruct((b, s, wide), BF16),
                 jax.ShapeDtypeStruct((b, s, wide), BF16),
                 jax.ShapeDtypeStruct((b, s, N_DIR * PACK), BF16),
                 jax.ShapeDtypeStruct((b, nt, 8, LANES), F32)]
    return pl.pallas_call(
        functools.partial(_gdn_prep_kernel, n_tiles=nt),
        grid=(b, nt),
        in_specs=in_specs,
        out_specs=out_specs,
        out_shape=out_shape,
        compiler_params=_cparams("parallel", "parallel"),
        name="gdn_prep",
    )(p, p, p, p, p, p, p, p, p, p, p, conv_w, alog, dtb)


def _gdn_scan_kernel(wf, uf, kdf, qgf, af, glf, wb, ub, kdb, qgb, ab, glb, of_ref, ob_ref, state):
    j = pl.program_id(1)

    @pl.when(j == 0)
    def _():
        state[...] = jnp.zeros_like(state)

    hd, ch, nh = GDN_HEAD_DIM, GDN_CHUNK, GDN_HEADS
    nch = TILE // ch
    lw = lax.broadcasted_iota(jnp.int32, (ch, MIX), 1) // hd
    head_w = [lw == h for h in range(nh)]
    dirs = ((wf, uf, kdf, qgf, af, glf, of_ref), (wb, ub, kdb, qgb, ab, glb, ob_ref))
    for step in range(nch):
        cidx = [step, nch - 1 - step]
        rows = [slice(c * ch, (c + 1) * ch) for c in cidx]
        s_old = {}
        ws = {}
        for d, (w_r, _, _, qg_r, _, _, _) in enumerate(dirs):
            for h in range(nh):
                cols = slice(h * hd, (h + 1) * hd)
                s_old[d, h] = state[d, h]
                lhs = jnp.concatenate([w_r[0, rows[d], cols], qg_r[0, rows[d], cols]], axis=0)
                ws[d, h] = jnp.dot(lhs, s_old[d, h].astype(BF16), preferred_element_type=F32)
        for d, (_, u_r, kd_r, _, a_r, gl_r, o_r) in enumerate(dirs):
            un = u_r[0, rows[d], :] - jnp.concatenate([ws[d, h][:ch] for h in range(nh)], axis=1)
            inter = jnp.concatenate([ws[d, h][ch:] for h in range(nh)], axis=1)
            intra = jnp.dot(a_r[0, rows[d], :], _head_block_diag(un, head_w), preferred_element_type=F32)
            o_r[0, rows[d], :] = inter + intra
            unb = un.astype(BF16)
            for h in range(nh):
                cols = slice(h * hd, (h + 1) * hd)
                gl = gl_r[0, 0, cidx[d]:cidx[d] + 1, d * nh + h:d * nh + h + 1]
                state[d, h] = gl * s_old[d, h] + _dot_tn(kd_r[0, rows[d], cols], unb[:, cols])


def _mirror(j, nt):
    return jnp.where(j == 0, 0, nt - j)


def _gdn_scan(w, u, kd, qg, aqk, gl):
    b, s, _ = u.shape
    nt = s // TILE
    half = GDN_HEADS * GDN_HEAD_DIM

    def specs(d):
        idx = (lambda bi, j: (bi, j, d)) if d == 0 else (lambda bi, j: (bi, _mirror(j, nt), d))
        gidx = (lambda bi, j: (bi, j, 0, 0)) if d == 0 else (lambda bi, j: (bi, _mirror(j, nt), 0, 0))
        return [pl.BlockSpec((1, TILE, half), idx)] * 4 + [pl.BlockSpec((1, TILE, PACK), idx),
                                                           pl.BlockSpec((1, 1, 8, LANES), gidx)]

    return pl.pallas_call(
        _gdn_scan_kernel,
        grid=(b, nt),
        in_specs=specs(0) + specs(1),
        out_specs=[pl.BlockSpec((1, TILE, half), lambda bi, j: (bi, j, 0)),
                   pl.BlockSpec((1, TILE, half), lambda bi, j: (bi, _mirror(j, nt), 0))],
        out_shape=[jax.ShapeDtypeStruct((b, s, half), F32)] * 2,
        scratch_shapes=[pltpu.VMEM((N_DIR, GDN_HEADS, GDN_HEAD_DIM, GDN_HEAD_DIM), F32)],
        compiler_params=_cparams("parallel", "arbitrary"),
        name="gdn_scan",
    )(w, u, kd, qg, aqk, gl, w, u, kd, qg, aqk, gl)


def _lru_kernel(xf_ref, xfp_ref, xfn_ref, xb_ref, xbp_ref, xbn_ref, cw_ref, cb_ref, wg_ref, bg_ref, lam_ref,
                hf_ref, hb_ref, carry, *, n_tiles):
    j = pl.program_id(1)

    @pl.when(j == 0)
    def _():
        carry[...] = jnp.zeros_like(carry)

    groups = TILE // 8
    row8 = lax.broadcasted_iota(jnp.int32, (TILE, LRU_WIDTH), 0) % 8

    def one_dir(d, x_ref, p_ref, n_ref, tile_idx, o_ref):
        prev_ok, next_ok = _halo_ok(tile_idx, n_tiles)
        x = _conv4(x_ref[0], p_ref[0], n_ref[0], cw_ref[...], prev_ok, next_ok) + cb_ref[...]
        gates = jnp.dot(x.astype(BF16), wg_ref[:, d * 2 * LRU_WIDTH:(d + 1) * 2 * LRU_WIDTH],
                        preferred_element_type=F32)
        rg = _sigmoid(gates[:, :LRU_WIDTH] + bg_ref[2 * d:2 * d + 1, :])
        ig = _sigmoid(gates[:, LRU_WIDTH:] + bg_ref[2 * d + 1:2 * d + 2, :])
        log_a = -LRU_C * rg * _softplus(-lam_ref[d:d + 1, :])
        a = jnp.exp(log_a)
        bb = jnp.sqrt(-jnp.tanh(log_a) * (a * a + 1.0)) * (ig * x)
        s = 1
        while s < 8:
            if d == 0:
                keep = row8 >= s
                a_sh = jnp.where(keep, pltpu.roll(a, s, 0), 1.0)
                b_sh = jnp.where(keep, pltpu.roll(bb, s, 0), 0.0)
            else:
                keep = row8 < 8 - s
                a_sh = jnp.where(keep, pltpu.roll(a, TILE - s, 0), 1.0)
                b_sh = jnp.where(keep, pltpu.roll(bb, TILE - s, 0), 0.0)
            bb = a * b_sh + bb
            a = a * a_sh
            s *= 2
        h = carry[d]
        order = range(groups) if d == 0 else range(groups - 1, -1, -1)
        last = 7 if d == 0 else 0
        pieces = [None] * groups
        for gidx in order:
            ag = a[gidx * 8:(gidx + 1) * 8]
            bg = bb[gidx * 8:(gidx + 1) * 8]
            hg = ag * h + bg
            pieces[gidx] = hg
            h = hg[last:last + 1]
        carry[d] = h
        o_ref[0] = jnp.concatenate(pieces, axis=0)

    one_dir(0, xf_ref, xfp_ref, xfn_ref, j, hf_ref)
    one_dir(1, xb_ref, xbp_ref, xbn_ref, _mirror(j, n_tiles), hb_ref)


def _lru(p, conv_w, conv_b, wg, bg, lam):
    b, s, _ = p.shape
    nt = s // TILE
    hpt = TILE // HALO
    nh = s // HALO
    col = C_LRU_X // MIX

    def specs(tile_of):
        return [pl.BlockSpec((1, TILE, MIX), lambda bi, j: (bi, tile_of(j), col)),
                pl.BlockSpec((1, HALO, MIX), lambda bi, j: (bi, jnp.maximum(tile_of(j) * hpt - 1, 0), col)),
                pl.BlockSpec((1, HALO, MIX),
                             lambda bi, j: (bi, jnp.minimum((tile_of(j) + 1) * hpt, nh - 1), col))]

    full = lambda shape: pl.BlockSpec(shape, lambda bi, j: tuple(0 for _ in shape))
    return pl.pallas_call(
        functools.partial(_lru_kernel, n_tiles=nt),
        grid=(b, nt),
        in_specs=specs(lambda j: j) + specs(lambda j: _mirror(j, nt)) + [
            full((4, LRU_WIDTH)), full((1, LRU_WIDTH)), full((LRU_WIDTH, 4 * LRU_WIDTH)),
            full((4, LRU_WIDTH)), full((N_DIR, LRU_WIDTH))],
        out_specs=[pl.BlockSpec((1, TILE, MIX), lambda bi, j: (bi, j, 0)),
                   pl.BlockSpec((1, TILE, MIX), lambda bi, j: (bi, _mirror(j, nt), 0))],
        out_shape=[jax.ShapeDtypeStruct((b, s, MIX), F32)] * 2,
        scratch_shapes=[pltpu.VMEM((N_DIR, 1, LRU_WIDTH), F32)],
        compiler_params=_cparams("parallel", "arbitrary"),
        name="lru",
    )(p, p, p, p, p, p, conv_w, conv_b, wg, bg, lam)


def _outproj_kernel(x_ref, mod_ref, att_ref, of_ref, ob_ref, gz_ref, hf_ref, hb_ref, lz_ref,
                    gn_ref, w_ref, fg_ref, o_ref, *, final):
    hd = GDN_HEAD_DIM
    o = of_ref[0] + ob_ref[0]
    parts = []
    for h in range(GDN_HEADS):
        oh = o[:, h * hd:(h + 1) * hd]
        parts.append(oh * lax.rsqrt(jnp.mean(oh * oh, axis=-1, keepdims=True) + EPS) * gn_ref[...])
    gdn = jnp.concatenate(parts, axis=1) * _silu(gz_ref[0])
    lru = (hf_ref[0] + hb_ref[0]) * _silu(lz_ref[0])
    acc = (jnp.dot(att_ref[0].astype(BF16), w_ref[0:MIX, :], preferred_element_type=F32)
           + jnp.dot(gdn.astype(BF16), w_ref[MIX:2 * MIX, :], preferred_element_type=F32)
           + jnp.dot(lru.astype(BF16), w_ref[2 * MIX:3 * MIX, :], preferred_element_type=F32))
    gate = mod_ref[0, 0, 2:3, :]
    xn = x_ref[0] + gate * acc
    if final:
        ms = jnp.mean(xn * xn, axis=-1, keepdims=True)
        xn = xn * lax.rsqrt(ms + EPS) * fg_ref[...]
    o_ref[0] = xn


def _outproj(xs, modsel, att, of, ob, p, hf, hb, gdn_norm, w_out, final_g, final):
    b, s, d = xs.shape
    nt = s // TILE
    off = 1 if final else 0
    n_out = nt - off
    tok = lambda col: pl.BlockSpec((1, TILE, MIX), lambda bi, i: (bi, i + off, col))
    return pl.pallas_call(
        functools.partial(_outproj_kernel, final=final),
        grid=(b, n_out),
        in_specs=[pl.BlockSpec((1, TILE, d), lambda bi, i: (bi, i + off, 0)),
                  pl.BlockSpec((1, 1, 8, d), lambda bi, i: (bi, jnp.minimum(i + off, 1), 0, 0)),
                  tok(0), tok(0), tok(0), tok(C_GDN_Z // MIX), tok(0), tok(0), tok(C_LRU_Z // MIX),
                  pl.BlockSpec((1, GDN_HEAD_DIM), lambda bi, i: (0, 0)),
                  pl.BlockSpec((3 * MIX, d), lambda bi, i: (0, 0)),
                  pl.BlockSpec((1, d), lambda bi, i: (0, 0))],
        out_specs=pl.BlockSpec((1, TILE, d), lambda bi, i: (bi, i, 0)),
        out_shape=jax.ShapeDtypeStruct((b, n_out * TILE, d), F32),
        compiler_params=_cparams("parallel", "parallel"),
        name="outproj",
    )(xs, modsel, att, of, ob, p, hf, hb, p, gdn_norm, w_out, final_g)


def _rope_tables(s, ctx_len):
    t = np.arange(s - ctx_len)
    n_freq = ATT_HEAD_DIM // 4
    inv = ROPE_BASE ** (-np.arange(n_freq, dtype=np.float64) / n_freq)
    ang = [(t // GRID_W)[:, None] * inv, (t % GRID_W)[:, None] * inv]
    cos = np.concatenate([np.cos(ang[0]), np.cos(ang[0]), np.cos(ang[1]), np.cos(ang[1])], axis=1)
    sin = np.concatenate([-np.sin(ang[0]), np.sin(ang[0]), -np.sin(ang[1]), np.sin(ang[1])], axis=1)
    cos = np.concatenate([np.ones((ctx_len, ATT_HEAD_DIM)), cos], axis=0)
    sin = np.concatenate([np.zeros((ctx_len, ATT_HEAD_DIM)), sin], axis=0)
    return (jnp.asarray(np.tile(cos, (1, 2)), F32), jnp.asarray(np.tile(sin, (1, 2)), F32))


def _permute_w_in(w):
    widths = (512, 128, 128, 512, 512, 512, 512, 8, 8, 512, 512, 512)
    offs = np.concatenate([[0], np.cumsum(widths)])
    names = ("att_q", "att_k", "att_v", "att_z", "gdn_q", "gdn_k", "gdn_v", "gdn_b", "gdn_a", "gdn_z",
             "lru_x", "lru_z")
    col = {n: w[:, offs[i]:offs[i + 1]] for i, n in enumerate(names)}
    pad = jnp.zeros((w.shape[0], LANES - 8), w.dtype)
    return jnp.concatenate(
        [col["att_q"], col["att_z"], col["gdn_q"], col["gdn_k"], col["gdn_v"], col["gdn_z"], col["lru_x"],
         col["lru_z"], col["att_k"], col["att_v"], col["gdn_b"], pad, col["gdn_a"], pad], axis=1)


def _block_diag(w):
    eye = jnp.eye(LRU_BLOCKS, dtype=w.dtype)
    return jnp.einsum("nde,nm->ndme", w, eye).reshape(LRU_WIDTH, LRU_WIDTH)


def _pad_lanes(v):
    return jnp.pad(v.reshape(1, -1), ((0, 0), (0, LANES - v.size)))


def kernel(x, c, ctx, c_ctx, norm_g, w_mod, b_mod, w_in, att_sink, gdn_conv, gdn_a_log, gdn_dt_bias, gdn_norm,
           lru_conv_w, lru_conv_b, lru_w_r, lru_b_r, lru_w_i, lru_b_i, lru_lambda, w_out, final_g):
    b, t, d = x.shape
    ctx_len = ctx.shape[1]
    depth = w_in.shape[0]
    assert ctx_len == TILE and t % TILE == 0 and d == D_MODEL and b == 2
    s = ctx_len + t
    nt = s // TILE

    cos_t, sin_t = _rope_tables(s, ctx_len)
    ct = jnp.pad(jnp.concatenate([c, c_ctx[None]], axis=0).T, ((0, 0), (0, LANES - b - 1)))
    mods = _modulation(ct, w_mod, b_mod)

    xs = jnp.concatenate([ctx, x], axis=1)
    out = None
    for l in range(depth):
        m3 = mods[l, :3].reshape(3, 3, d)
        sel = jnp.stack([jnp.stack([m3[b], m3[bi]], axis=0) for bi in range(b)], axis=0)
        modsel = jnp.pad(sel, ((0, 0), (0, 0), (0, 5), (0, 0)))
        w_perm = _permute_w_in(w_in[l]).astype(BF16)
        p = _inproj(xs, modsel, norm_g[l].reshape(1, d), cos_t, sin_t, w_perm)

        att = _attention(p, att_sink[l])

        prep = _gdn_prep(p, gdn_conv[l], _pad_lanes(gdn_a_log[l]), _pad_lanes(gdn_dt_bias[l]))
        o_f, o_b = _gdn_scan(*prep)

        wg = jnp.concatenate([_block_diag(lru_w_r[l, 0]), _block_diag(lru_w_i[l, 0]),
                              _block_diag(lru_w_r[l, 1]), _block_diag(lru_w_i[l, 1])], axis=1).astype(BF16)
        bg = jnp.stack([lru_b_r[l, 0], lru_b_i[l, 0], lru_b_r[l, 1], lru_b_i[l, 1]], axis=0)
        h_f, h_b = _lru(p, lru_conv_w[l], lru_conv_b[l].reshape(1, -1), wg, bg, lru_lambda[l])

        final = l == depth - 1
        res = _outproj(xs, modsel, att, o_f, o_b, p, h_f, h_b, gdn_norm[l].reshape(1, -1),
                       w_out[l].astype(BF16), final_g.reshape(1, d), final)
        if final:
            out = res
        else:
            xs = res
    return out
```

```python
import functools
import math

import numpy as np
import jax
import jax.numpy as jnp
from jax import lax
from jax.experimental import pallas as pl
from jax.experimental.pallas import tpu as pltpu

F32 = jnp.float32
BF16 = jnp.bfloat16

EPS = 1e-6
NEG_INF = -1e30
LOG2E = math.log2(math.e)
GRID_W = 64
ROPE_BASE = 10000.0

D_MODEL = 1024
ATT_HEADS = 8
ATT_KV_HEADS = 2
ATT_HEAD_DIM = 64
ATT_BLOCK = 128
GDN_HEADS = 4
GDN_HEAD_DIM = 128
GDN_CHUNK = 64
LRU_WIDTH = 512
LRU_BLOCKS = 8
LRU_BLOCK_DIM = 64
LRU_C = 8.0
N_DIR = 2
MIX = 512
PACK = GDN_HEADS * GDN_CHUNK

TILE = 256
HALO = 8
LANES = 128
VMEM_LIMIT = 56 * 1024 * 1024

C_ATT_Q, C_ATT_Z, C_GDN_Q, C_GDN_K, C_GDN_V, C_GDN_Z, C_LRU_X, C_LRU_Z = (i * MIX for i in range(8))
C_ATT_K = 8 * MIX
C_ATT_V = C_ATT_K + LANES
C_GDN_B = C_ATT_V + LANES
C_GDN_A = C_GDN_B + LANES
N_IN = C_GDN_A + LANES


def _cparams(*sem):
    return pltpu.CompilerParams(dimension_semantics=sem, vmem_limit_bytes=VMEM_LIMIT)


def _sigmoid(x):
    return 1.0 / (1.0 + jnp.exp(-x))


def _silu(x):
    return x * _sigmoid(x)


def _softplus(x):
    return jnp.maximum(x, 0.0) + jnp.log1p(jnp.exp(-jnp.abs(x)))


def _dot_nt(a, b):
    return lax.dot_general(a.astype(BF16), b.astype(BF16), (((1,), (1,)), ((), ())),
                           preferred_element_type=F32)


def _dot_tn(a, b):
    return lax.dot_general(a.astype(BF16), b.astype(BF16), (((0,), (0,)), ((), ())),
                           preferred_element_type=F32)


def _mod_kernel(ct_ref, w_ref, b_ref, o_ref):
    s = _silu(ct_ref[...])
    w = w_ref[0]
    rid = lax.broadcasted_iota(jnp.int32, (8, w.shape[1]), 0)
    out = jnp.zeros((8, w.shape[1]), F32)
    for r in range(3):
        row = jnp.sum(w * s[:, r:r + 1], axis=0, keepdims=True) + b_ref[0]
        out = jnp.where(rid == r, row, out)
    o_ref[0] = out


def _modulation(ct, w_mod, b_mod):
    depth, d, d3 = w_mod.shape
    return pl.pallas_call(
        _mod_kernel,
        grid=(depth, d3 // d),
        in_specs=[pl.BlockSpec((d, LANES), lambda l, j: (0, 0)),
                  pl.BlockSpec((1, d, d), lambda l, j: (l, 0, j)),
                  pl.BlockSpec((1, 1, d), lambda l, j: (l, 0, j))],
        out_specs=pl.BlockSpec((1, 8, d), lambda l, j: (l, 0, j)),
        out_shape=jax.ShapeDtypeStruct((depth, 8, d3), F32),
        compiler_params=_cparams("parallel", "parallel"),
        name="modulation",
    )(ct, w_mod, b_mod.reshape(depth, 1, d3))


def _rope(x, cos, sin):
    lane = lax.broadcasted_iota(jnp.int32, x.shape, 1)
    first = (lane % 32) < 16
    partner = jnp.where(first, pltpu.roll(x, LANES - 16, 1), pltpu.roll(x, 16, 1))
    return x * cos + partner * sin


def _stream_tile(c_ref, x_ref, tile):
    is_ctx = (jnp.zeros((TILE, 1), jnp.int32) + tile) == 0
    return jnp.where(is_ctx, c_ref[0], x_ref[0])


def _stream_specs(d, off, shift=0):
    return [pl.BlockSpec((1, TILE, d), lambda bi, i: (bi, 0, 0)),
            pl.BlockSpec((1, TILE, d), lambda bi, i: (bi, jnp.maximum(i + shift - off, 1 - off), 0))]


def _inproj_kernel(c_ref, x_ref, mod_ref, g_ref, cos_ref, sin_ref, w_ref, o_ref):
    x = _stream_tile(c_ref, x_ref, pl.program_id(1))
    ms = jnp.mean(x * x, axis=-1, keepdims=True)
    y = x * lax.rsqrt(ms + EPS) * g_ref[...]
    shift = mod_ref[0, 0, 0:1, :]
    scale = mod_ref[0, 0, 1:2, :]
    hb = (y * (1.0 + scale) + shift).astype(BF16)
    cos = cos_ref[...]
    sin = sin_ref[...]
    for c0 in range(0, N_IN, MIX):
        res = jnp.dot(hb, w_ref[:, c0:c0 + MIX], preferred_element_type=F32)
        if c0 == C_ATT_Q:
            res = jnp.concatenate(
                [_rope(res[:, k:k + LANES], cos, sin) for k in range(0, MIX, LANES)], axis=1)
        if c0 == C_ATT_K:
            res = jnp.concatenate([_rope(res[:, :LANES], cos, sin), res[:, LANES:]], axis=1)
        o_ref[0, :, c0:c0 + MIX] = res


def _inproj(c_src, x_src, off, modsel, g, cos_t, sin_t, w_perm):
    b, _, d = x_src.shape
    s = x_src.shape[1] + off * TILE
    nt = s // TILE
    return pl.pallas_call(
        _inproj_kernel,
        grid=(b, nt),
        in_specs=_stream_specs(d, off) + [
                  pl.BlockSpec((1, 1, 8, d), lambda bi, i: (bi, jnp.minimum(i, 1), 0, 0)),
                  pl.BlockSpec((1, d), lambda bi, i: (0, 0)),
                  pl.BlockSpec((TILE, LANES), lambda bi, i: (i, 0)),
                  pl.BlockSpec((TILE, LANES), lambda bi, i: (i, 0)),
                  pl.BlockSpec((d, N_IN), lambda bi, i: (0, 0))],
        out_specs=pl.BlockSpec((1, TILE, N_IN), lambda bi, i: (bi, i, 0)),
        out_shape=jax.ShapeDtypeStruct((b, s, N_IN), F32),
        compiler_params=_cparams("parallel", "parallel"),
        name="inproj",
    )(c_src, x_src, modsel, g, cos_t, sin_t, w_perm)


def _att_kernel(sink_ref, q_ref, z_ref, kvp_ref, kvc_ref, kvn_ref, kvx_ref, o_ref, *, n_blocks):
    m = pl.program_id(1)
    first = TILE // ATT_BLOCK
    c_lo = jnp.where(m >= first + 1, 0, jnp.where(m >= first, ATT_BLOCK, 3 * ATT_BLOCK))
    c_hi = jnp.where(m < first, 0, jnp.where(m <= n_blocks - 2, 3 * ATT_BLOCK, 2 * ATT_BLOCK))
    n_band = 3 * ATT_BLOCK
    half = ATT_HEAD_DIM
    group = ATT_HEADS // ATT_KV_HEADS

    q = q_ref[0] * (ATT_HEAD_DIM ** -0.5 * LOG2E)
    kv = jnp.concatenate([kvp_ref[0], kvc_ref[0], kvn_ref[0], kvx_ref[0]], axis=0)
    nk = kv.shape[0]
    k_all, v_all = kv[:, :LANES], kv[:, LANES:]
    k_sw, v_sw = pltpu.roll(k_all, half, 1), pltpu.roll(v_all, half, 1)
    lo_k = lax.broadcasted_iota(jnp.int32, (nk, LANES), 1) < half
    k_dup = [jnp.where(lo_k, k_all, k_sw).astype(BF16), jnp.where(lo_k, k_sw, k_all).astype(BF16)]
    v_dup = [jnp.where(lo_k, v_all, v_sw).astype(BF16), jnp.where(lo_k, v_sw, v_all).astype(BF16)]

    r = lax.broadcasted_iota(jnp.int32, (ATT_BLOCK, n_band), 0)
    c = lax.broadcasted_iota(jnp.int32, (ATT_BLOCK, n_band), 1)
    ok = jnp.logical_and(jnp.abs(r - (c - ATT_BLOCK)) <= ATT_BLOCK, jnp.logical_and(c >= c_lo, c < c_hi))
    bias = jnp.where(ok, 0.0, NEG_INF)
    bias = jnp.concatenate([bias] * group, axis=0)

    lo_q = lax.broadcasted_iota(jnp.int32, (ATT_BLOCK, LANES), 1) < half
    outs = []
    for h in range(ATT_KV_HEADS):
        lhs, sinks = [], []
        for j in range(h * group, (h + 1) * group):
            q_t = q[:, (j // 2) * LANES:(j // 2 + 1) * LANES]
            lhs.append(jnp.where(lo_q, q_t, 0.0) if j % 2 == 0 else jnp.where(lo_q, 0.0, q_t))
            sinks.append(jnp.full((ATT_BLOCK, 1), sink_ref[j] * LOG2E, F32))
        qs = jnp.concatenate(lhs, axis=0).astype(BF16)
        sk = jnp.concatenate(sinks, axis=0)
        s = lax.dot_general(qs, k_dup[h], (((1,), (1,)), ((), ())), preferred_element_type=F32)
        s_band = s[:, :n_band] + bias
        s_ctx = s[:, n_band:]
        mx = jnp.maximum(jnp.maximum(jnp.max(s_band, axis=-1, keepdims=True),
                                     jnp.max(s_ctx, axis=-1, keepdims=True)), sk)
        p_band = jnp.exp2(s_band - mx)
        p_ctx = jnp.exp2(s_ctx - mx)
        den = (jnp.sum(p_band, axis=-1, keepdims=True) + jnp.sum(p_ctx, axis=-1, keepdims=True)
               + jnp.exp2(sk - mx))
        p = jnp.concatenate([p_band, p_ctx], axis=1).astype(BF16)
        res = jnp.dot(p, v_dup[h], preferred_element_type=F32) / den
        for t in range(group // 2):
            even = res[(2 * t) * ATT_BLOCK:(2 * t + 1) * ATT_BLOCK]
            odd = res[(2 * t + 1) * ATT_BLOCK:(2 * t + 2) * ATT_BLOCK]
            outs.append(jnp.where(lo_q, even, odd))
    o = jnp.concatenate(outs, axis=1)
    o_ref[0] = (o * _silu(z_ref[0])).astype(o_ref.dtype)


def _attention(p, sink):
    b, s, _ = p.shape
    nb = s // ATT_BLOCK
    first = TILE // ATT_BLOCK
    kvcol = C_ATT_K // (2 * LANES)
    kv_spec = lambda f: pl.BlockSpec((1, ATT_BLOCK, 2 * LANES), f)
    return pl.pallas_call(
        functools.partial(_att_kernel, n_blocks=nb),
        grid=(b, nb),
        in_specs=[pl.BlockSpec(memory_space=pltpu.SMEM),
                  pl.BlockSpec((1, ATT_BLOCK, MIX), lambda bi, m: (bi, m, C_ATT_Q // MIX)),
                  pl.BlockSpec((1, ATT_BLOCK, MIX), lambda bi, m: (bi, m, C_ATT_Z // MIX)),
                  kv_spec(lambda bi, m: (bi, jnp.maximum(m - 1, first), kvcol)),
                  kv_spec(lambda bi, m: (bi, m, kvcol)),
                  kv_spec(lambda bi, m: (bi, jnp.minimum(m + 1, nb - 1), kvcol)),
                  pl.BlockSpec((1, TILE, 2 * LANES), lambda bi, m: (bi, 0, kvcol))],
        out_specs=pl.BlockSpec((1, ATT_BLOCK, MIX), lambda bi, m: (bi, m, 0)),
        out_shape=jax.ShapeDtypeStruct((b, s, MIX), BF16),
        compiler_params=_cparams("parallel", "parallel"),
        name="attention",
    )(sink, p, p, p, p, p, p)


def _conv4(x, hp, hn, w, prev_ok, next_ok):
    hp = hp * prev_ok
    hn = hn * next_ok
    ext = jnp.concatenate([hp, x, hn], axis=0)
    n = x.shape[0]
    return (w[0:1] * ext[HALO - 2:HALO - 2 + n] + w[1:2] * ext[HALO - 1:HALO - 1 + n]
            + w[2:3] * ext[HALO:HALO + n] + w[3:4] * ext[HALO + 1:HALO + 1 + n])


def _halo_ok(i, n_tiles):
    prev_ok = (i >= 2).astype(F32)
    next_ok = jnp.logical_and(i != 0, i != n_tiles - 1).astype(F32)
    return prev_ok, next_ok


def _chunk_cumsum(g, reverse):
    n = g.shape[0]
    row = lax.broadcasted_iota(jnp.int32, g.shape, 0) % GDN_CHUNK
    s = 1
    while s < GDN_CHUNK:
        if reverse:
            g = g + jnp.where(row < GDN_CHUNK - s, pltpu.roll(g, n - s, 0), 0.0)
        else:
            g = g + jnp.where(row >= s, pltpu.roll(g, s, 0), 0.0)
        s *= 2
    return g


def _spread(cols, masks):
    out = cols[-1]
    for m, col in zip(masks[-2::-1], cols[-2::-1]):
        out = jnp.where(m, col, out)
    return out


def _head_block_diag(x, masks):
    return jnp.concatenate([jnp.where(m, x, 0.0) for m in masks], axis=0).astype(BF16)


def _gdn_prep_kernel(q_ref, qp_ref, qn_ref, k_ref, kp_ref, kn_ref, v_ref, vp_ref, vn_ref,
                     b_ref, a_ref, cw_ref, alog_ref, dtb_ref,
                     w_ref, u_ref, kd_ref, qg_ref, aqk_ref, gl_ref, *, n_tiles):
    i = pl.program_id(1)
    prev_ok, next_ok = _halo_ok(i, n_tiles)
    hd, ch, nh = GDN_HEAD_DIM, GDN_CHUNK, GDN_HEADS
    nch = TILE // ch

    def prep(x_ref, p_ref, n_ref, col):
        return _silu(_conv4(x_ref[0], p_ref[0], n_ref[0], cw_ref[:, col:col + MIX], prev_ok, next_ok))

    def l2n(t):
        parts = []
        for h in range(nh):
            th = t[:, h * hd:(h + 1) * hd]
            parts.append(th * lax.rsqrt(jnp.sum(th * th, axis=-1, keepdims=True) + EPS))
        return jnp.concatenate(parts, axis=1)

    q = l2n(prep(q_ref, qp_ref, qn_ref, 0)) * (hd ** -0.5)
    k = l2n(prep(k_ref, kp_ref, kn_ref, MIX))
    v = prep(v_ref, vp_ref, vn_ref, 2 * MIX)

    lane = lax.broadcasted_iota(jnp.int32, (TILE, LANES), 1)
    g = -jnp.exp(alog_ref[...]) * _softplus(a_ref[0] + dtb_ref[...])
    gcum = jnp.where(lane < nh, _chunk_cumsum(g, False), _chunk_cumsum(g, True))
    beta = _sigmoid(b_ref[0])

    ii = lax.broadcasted_iota(jnp.int32, (ch, PACK), 0)
    ll = lax.broadcasted_iota(jnp.int32, (ch, PACK), 1)
    jj = ll % ch
    eye_p = jj == ii
    eye_f = eye_p.astype(F32)
    bd16 = (jj // 16) == (ii // 16)
    bd32 = (jj // 32) == (ii // 32)
    merge_masks = (jnp.logical_and(bd32, jnp.logical_not(bd16)), jnp.logical_not(bd32))
    strict = [jj < ii, jj > ii]
    incl = [jj <= ii, jj >= ii]
    head_p = [(ll // ch) == h for h in range(nh)]
    lw = lax.broadcasted_iota(jnp.int32, (ch, MIX), 1) // hd
    head_w = [lw == h for h in range(nh)]
    lane_row = lax.broadcasted_iota(jnp.int32, (1, LANES), 1)

    def pdot(a, b):
        return jnp.dot(a.astype(BF16), _head_block_diag(b, head_p), preferred_element_type=F32)

    gl_rows = []
    units = []
    for cidx in range(nch):
        rows = slice(cidx * ch, (cidx + 1) * ch)
        kc, qc, vc, gc, bc = k[rows], q[rows], v[rows], gcum[rows], beta[rows]
        kq = _dot_nt(jnp.concatenate([kc, qc], axis=0), _head_block_diag(kc, head_w))
        g_end = jnp.where(lane_row < nh, gc[ch - 1:ch, :], gc[0:1, :])
        gl_rows.append(jnp.exp(g_end))
        for d in range(N_DIR):
            gcols = [gc[:, d * nh + h:d * nh + h + 1] for h in range(nh)]
            bcols = [bc[:, d * nh + h:d * nh + h + 1] for h in range(nh)]
            gcol_p = _spread(gcols, head_p)
            grow_p = jnp.sum(jnp.where(eye_p, gcol_p, 0.0), axis=0, keepdims=True)
            dec = jnp.exp(jnp.where(incl[d], gcol_p - grow_p, 0.0))
            lm = jnp.where(strict[d], _spread(bcols, head_p) * dec * kq[:ch], 0.0)
            aqk = jnp.where(incl[d], dec * kq[ch:], 0.0)
            aqk_ref[0, rows, d * PACK:(d + 1) * PACK] = aqk.astype(BF16)
            units.append((rows, d, lm, kc, qc, vc, gcols, bcols,
                          [g_end[:, d * nh + h:d * nh + h + 1] for h in range(nh)]))
    gl_rows.append(jnp.zeros((8 - nch, LANES), F32))
    gl_ref[0, 0] = jnp.concatenate(gl_rows, axis=0)

    lms = [u[2] for u in units]
    m = [jnp.where(bd16, -lm, 0.0) for lm in lms]
    x = [eye_f + mi for mi in m]
    pw = [pdot(mi, mi) for mi in m]
    for _ in range(2):
        r = [pdot(jnp.concatenate([xi, pi], axis=0), pi) for xi, pi in zip(x, pw)]
        x = [xi + ri[:ch] for xi, ri in zip(x, r)]
        pw = [ri[ch:] for ri in r]
    x = [xi + pdot(xi, pi) for xi, pi in zip(x, pw)]
    for mask in merge_masks:
        y = [pdot(xi, jnp.where(mask, lm, 0.0)) for xi, lm in zip(x, lms)]
        x = [xi - pdot(yi, xi) for xi, yi in zip(x, y)]

    for tinv, (rows, d, _, kc, qc, vc, gcols, bcols, gend_cols) in zip(x, units):
        g_w = _spread(gcols, head_w)
        beta_w = _spread(bcols, head_w)
        gam_w = jnp.exp(g_w)
        rhs = jnp.concatenate([_head_block_diag(beta_w * gam_w * kc, head_w), _head_block_diag(beta_w * vc, head_w)],
                              axis=1)
        wu = jnp.dot(tinv.astype(BF16), rhs, preferred_element_type=F32)
        cols = slice(d * MIX, (d + 1) * MIX)
        w_ref[0, rows, cols] = wu[:, :MIX].astype(BF16)
        u_ref[0, rows, cols] = wu[:, MIX:]
        kd_ref[0, rows, cols] = (jnp.exp(_spread(gend_cols, head_w) - g_w) * kc).astype(BF16)
        qg_ref[0, rows, cols] = (gam_w * qc).astype(BF16)


def _gdn_prep(p, conv_w, alog, dtb):
    b, s, _ = p.shape
    nt = s // TILE
    hpt = TILE // HALO
    nh = s // HALO
    wide = N_DIR * GDN_HEADS * GDN_HEAD_DIM

    def tile_spec(col):
        return pl.BlockSpec((1, TILE, MIX), lambda bi, i: (bi, i, col // MIX))

    def prev_spec(col):
        return pl.BlockSpec((1, HALO, MIX), lambda bi, i: (bi, jnp.maximum(i * hpt - 1, 0), col // MIX))

    def next_spec(col):
        return pl.BlockSpec((1, HALO, MIX), lambda bi, i: (bi, jnp.minimum((i + 1) * hpt, nh - 1), col // MIX))

    in_specs = []
    for col in (C_GDN_Q, C_GDN_K, C_GDN_V):
        in_specs += [tile_spec(col), prev_spec(col), next_spec(col)]
    in_specs += [pl.BlockSpec((1, TILE, LANES), lambda bi, i: (bi, i, C_GDN_B // LANES)),
                 pl.BlockSpec((1, TILE, LANES), lambda bi, i: (bi, i, C_GDN_A // LANES)),
                 pl.BlockSpec((4, 3 * MIX), lambda bi, i: (0, 0)),
                 pl.BlockSpec((1, LANES), lambda bi, i: (0, 0)),
                 pl.BlockSpec((1, LANES), lambda bi, i: (0, 0))]
    wide_spec = pl.BlockSpec((1, TILE, wide), lambda bi, i: (bi, i, 0))
    out_specs = [wide_spec, wide_spec, wide_spec, wide_spec,
                 pl.BlockSpec((1, TILE, N_DIR * PACK), lambda bi, i: (bi, i, 0)),
                 pl.BlockSpec((1, 1, 8, LANES), lambda bi, i: (bi, i, 0, 0))]
    out_shape = [jax.ShapeDtypeStruct((b, s, wide), BF16),
                 jax.ShapeDtypeStruct((b, s, wide), F32),
                 jax.ShapeDtypeStruct((b, s, wide), BF16),
                 jax.ShapeDtypeStruct((b, s, wide), BF16),
                 jax.ShapeDtypeStruct((b, s, N_DIR * PACK), BF16),
                 jax.ShapeDtypeStruct((b, nt, 8, LANES), F32)]
    return pl.pallas_call(
        functools.partial(_gdn_prep_kernel, n_tiles=nt),
        grid=(b, nt),
        in_specs=in_specs,
        out_specs=out_specs,
        out_shape=out_shape,
        compiler_params=_cparams("parallel", "parallel"),
        name="gdn_prep",
    )(p, p, p, p, p, p, p, p, p, p, p, conv_w, alog, dtb)


def _gdn_scan_kernel(wf, uf, kdf, qgf, af, glf, wb, ub, kdb, qgb, ab, glb, of_ref, ob_ref, state):
    j = pl.program_id(1)

    @pl.when(j == 0)
    def _():
        state[...] = jnp.zeros_like(state)

    hd, ch, nh = GDN_HEAD_DIM, GDN_CHUNK, GDN_HEADS
    nch = TILE // ch
    lw = lax.broadcasted_iota(jnp.int32, (ch, MIX), 1) // hd
    head_w = [lw == h for h in range(nh)]
    dirs = ((wf, uf, kdf, qgf, af, glf, of_ref), (wb, ub, kdb, qgb, ab, glb, ob_ref))
    for step in range(nch):
        cidx = [step, nch - 1 - step]
        rows = [slice(c * ch, (c + 1) * ch) for c in cidx]
        s_old = {}
        ws = {}
        for d, (w_r, _, _, qg_r, _, _, _) in enumerate(dirs):
            for h in range(nh):
                cols = slice(h * hd, (h + 1) * hd)
                s_old[d, h] = state[d, h]
                lhs = jnp.concatenate([w_r[0, rows[d], cols], qg_r[0, rows[d], cols]], axis=0)
                ws[d, h] = jnp.dot(lhs, s_old[d, h].astype(BF16), preferred_element_type=F32)
        for d, (_, u_r, kd_r, _, a_r, gl_r, o_r) in enumerate(dirs):
            un = u_r[0, rows[d], :] - jnp.concatenate([ws[d, h][:ch] for h in range(nh)], axis=1)
            inter = jnp.concatenate([ws[d, h][ch:] for h in range(nh)], axis=1)
            intra = jnp.dot(a_r[0, rows[d], :], _head_block_diag(un, head_w), preferred_element_type=F32)
            o_r[0, rows[d], :] = inter + intra
            unb = un.astype(BF16)
            for h in range(nh):
                cols = slice(h * hd, (h + 1) * hd)
                gl = gl_r[0, 0, cidx[d]:cidx[d] + 1, d * nh + h:d * nh + h + 1]
                state[d, h] = gl * s_old[d, h] + _dot_tn(kd_r[0, rows[d], cols], unb[:, cols])


def _mirror(j, nt):
    return jnp.where(j == 0, 0, nt - j)


def _gdn_scan(w, u, kd, qg, aqk, gl):
    b, s, _ = u.shape
    nt = s // TILE
    half = GDN_HEADS * GDN_HEAD_DIM

    def specs(d):
        idx = (lambda bi, j: (bi, j, d)) if d == 0 else (lambda bi, j: (bi, _mirror(j, nt), d))
        gidx = (lambda bi, j: (bi, j, 0, 0)) if d == 0 else (lambda bi, j: (bi, _mirror(j, nt), 0, 0))
        return [pl.BlockSpec((1, TILE, half), idx)] * 4 + [pl.BlockSpec((1, TILE, PACK), idx),
                                                           pl.BlockSpec((1, 1, 8, LANES), gidx)]

    return pl.pallas_call(
        _gdn_scan_kernel,
        grid=(b, nt),
        in_specs=specs(0) + specs(1),
        out_specs=[pl.BlockSpec((1, TILE, half), lambda bi, j: (bi, j, 0)),
                   pl.BlockSpec((1, TILE, half), lambda bi, j: (bi, _mirror(j, nt), 0))],
        out_shape=[jax.ShapeDtypeStruct((b, s, half), F32)] * 2,
        scratch_shapes=[pltpu.VMEM((N_DIR, GDN_HEADS, GDN_HEAD_DIM, GDN_HEAD_DIM), F32)],
        compiler_params=_cparams("parallel", "arbitrary"),
        name="gdn_scan",
    )(w, u, kd, qg, aqk, gl, w, u, kd, qg, aqk, gl)


def _lru_kernel(xf_ref, xfp_ref, xfn_ref, xb_ref, xbp_ref, xbn_ref, cw_ref, cb_ref, wg_ref, bg_ref, lam_ref,
                hf_ref, hb_ref, carry, *, n_tiles):
    j = pl.program_id(1)

    @pl.when(j == 0)
    def _():
        carry[...] = jnp.zeros_like(carry)

    seg_n = 8
    steps = TILE // seg_n
    sub = lax.broadcasted_iota(jnp.int32, (seg_n, LRU_WIDTH), 0)

    def one_dir(d, x_ref, p_ref, n_ref, tile_idx, o_ref):
        prev_ok, next_ok = _halo_ok(tile_idx, n_tiles)
        xt = jnp.swapaxes(x_ref[0].reshape(seg_n, steps, LRU_WIDTH), 0, 1)
        xs = [xt[st] for st in range(steps)]
        hp = p_ref[0] * prev_ok
        hn = n_ref[0] * next_ok
        p1 = jnp.where(sub == 0, hp[7:8], pltpu.roll(xs[steps - 1], 1, 0))
        p2 = jnp.where(sub == 0, hp[6:7], pltpu.roll(xs[steps - 2], 1, 0))
        n1 = jnp.where(sub == seg_n - 1, hn[0:1], pltpu.roll(xs[0], seg_n - 1, 0))
        ext = [p2, p1] + xs + [n1]
        cw = cw_ref[...]
        xh = jnp.concatenate(
            [cw[0:1] * ext[st] + cw[1:2] * ext[st + 1] + cw[2:3] * ext[st + 2] + cw[3:4] * ext[st + 3]
             for st in range(steps)], axis=0) + cb_ref[...]
        gates = jnp.dot(xh.astype(BF16), wg_ref[:, d * 2 * LRU_WIDTH:(d + 1) * 2 * LRU_WIDTH],
                        preferred_element_type=F32)
        tr = jnp.tanh(gates[:, :LRU_WIDTH] + bg_ref[2 * d:2 * d + 1, :])
        ti = jnp.tanh(gates[:, LRU_WIDTH:] + bg_ref[2 * d + 1:2 * d + 2, :])
        c_half = (-0.5 * LRU_C) * _softplus(-lam_ref[d:d + 1, :])
        log_a = c_half * tr + c_half
        a = jnp.exp(log_a)
        bb = jnp.sqrt(-jnp.tanh(log_a) * (a * a + 1.0)) * ((ti + 1.0) * xh)

        order = list(range(steps)) if d == 0 else list(range(steps - 1, -1, -1))
        piece = lambda t, st: t[st * seg_n:(st + 1) * seg_n]
        a_tot = piece(a, order[0])
        b_tot = piece(bb, order[0])
        for st in order[1:]:
            b_tot = piece(a, st) * b_tot + piece(bb, st)
            a_tot = piece(a, st) * a_tot
        s = 1
        while s < seg_n:
            if d == 0:
                keep = sub >= s
                a_sh = jnp.where(keep, pltpu.roll(a_tot, s, 0), 1.0)
                b_sh = jnp.where(keep, pltpu.roll(b_tot, s, 0), 0.0)
            else:
                keep = sub < seg_n - s
                a_sh = jnp.where(keep, pltpu.roll(a_tot, seg_n - s, 0), 1.0)
                b_sh = jnp.where(keep, pltpu.roll(b_tot, seg_n - s, 0), 0.0)
            b_tot = a_tot * b_sh + b_tot
            a_tot = a_tot * a_sh
            s *= 2
        h_in0 = carry[d]
        h_out = a_tot * h_in0 + b_tot
        if d == 0:
            h = jnp.where(sub == 0, h_in0, pltpu.roll(h_out, 1, 0))
            carry[d] = h_out[seg_n - 1:seg_n]
        else:
            h = jnp.where(sub == seg_n - 1, h_in0, pltpu.roll(h_out, seg_n - 1, 0))
            carry[d] = h_out[0:1]
        hs = [None] * steps
        for st in order:
            h = piece(a, st) * h + piece(bb, st)
            hs[st] = h
        o_ref[0] = jnp.swapaxes(jnp.stack(hs, axis=0), 0, 1).reshape(TILE, LRU_WIDTH)

    one_dir(0, xf_ref, xfp_ref, xfn_ref, j, hf_ref)
    one_dir(1, xb_ref, xbp_ref, xbn_ref, _mirror(j, n_tiles), hb_ref)


def _lru(p, conv_w, conv_b, wg, bg, lam):
    b, s, _ = p.shape
    nt = s // TILE
    hpt = TILE // HALO
    nh = s // HALO
    col = C_LRU_X // MIX

    def specs(tile_of):
        return [pl.BlockSpec((1, TILE, MIX), lambda bi, j: (bi, tile_of(j), col)),
                pl.BlockSpec((1, HALO, MIX), lambda bi, j: (bi, jnp.maximum(tile_of(j) * hpt - 1, 0), col)),
                pl.BlockSpec((1, HALO, MIX),
                             lambda bi, j: (bi, jnp.minimum((tile_of(j) + 1) * hpt, nh - 1), col))]

    full = lambda shape: pl.BlockSpec(shape, lambda bi, j: tuple(0 for _ in shape))
    return pl.pallas_call(
        functools.partial(_lru_kernel, n_tiles=nt),
        grid=(b, nt),
        in_specs=specs(lambda j: j) + specs(lambda j: _mirror(j, nt)) + [
            full((4, LRU_WIDTH)), full((1, LRU_WIDTH)), full((LRU_WIDTH, 4 * LRU_WIDTH)),
            full((4, LRU_WIDTH)), full((N_DIR, LRU_WIDTH))],
        out_specs=[pl.BlockSpec((1, TILE, MIX), lambda bi, j: (bi, j, 0)),
                   pl.BlockSpec((1, TILE, MIX), lambda bi, j: (bi, _mirror(j, nt), 0))],
        out_shape=[jax.ShapeDtypeStruct((b, s, MIX), F32)] * 2,
        scratch_shapes=[pltpu.VMEM((N_DIR, 1, LRU_WIDTH), F32)],
        compiler_params=_cparams("parallel", "arbitrary"),
        name="lru",
    )(p, p, p, p, p, p, conv_w, conv_b, wg, bg, lam)


def _outproj_kernel(c_ref, x_ref, mod_ref, att_ref, of_ref, ob_ref, gz_ref, hf_ref, hb_ref, lz_ref,
                    gn_ref, w_ref, fg_ref, o_ref, *, final, first_tile):
    hd = GDN_HEAD_DIM
    o = of_ref[0] + ob_ref[0]
    parts = []
    for h in range(GDN_HEADS):
        oh = o[:, h * hd:(h + 1) * hd]
        parts.append(oh * lax.rsqrt(jnp.mean(oh * oh, axis=-1, keepdims=True) + EPS) * gn_ref[...])
    gdn = jnp.concatenate(parts, axis=1) * _silu(gz_ref[0])
    lru = (hf_ref[0] + hb_ref[0]) * _silu(lz_ref[0])
    acc = (jnp.dot(att_ref[0], w_ref[0:MIX, :], preferred_element_type=F32)
           + jnp.dot(gdn.astype(BF16), w_ref[MIX:2 * MIX, :], preferred_element_type=F32)
           + jnp.dot(lru.astype(BF16), w_ref[2 * MIX:3 * MIX, :], preferred_element_type=F32))
    gate = mod_ref[0, 0, 2:3, :]
    xn = _stream_tile(c_ref, x_ref, pl.program_id(1) + first_tile) + gate * acc
    if final:
        ms = jnp.mean(xn * xn, axis=-1, keepdims=True)
        xn = xn * lax.rsqrt(ms + EPS) * fg_ref[...]
    o_ref[0] = xn


def _outproj(c_src, x_src, src_off, modsel, att, of, ob, p, hf, hb, gdn_norm, w_out, final_g, final):
    b, s, _ = p.shape
    d = x_src.shape[2]
    nt = s // TILE
    off = 1 if final else 0
    n_out = nt - off
    tok = lambda col: pl.BlockSpec((1, TILE, MIX), lambda bi, i: (bi, i + off, col))
    return pl.pallas_call(
        functools.partial(_outproj_kernel, final=final, first_tile=off),
        grid=(b, n_out),
        in_specs=_stream_specs(d, src_off, off) + [
                  pl.BlockSpec((1, 1, 8, d), lambda bi, i: (bi, jnp.minimum(i + off, 1), 0, 0)),
                  tok(0), tok(0), tok(0), tok(C_GDN_Z // MIX), tok(0), tok(0), tok(C_LRU_Z // MIX),
                  pl.BlockSpec((1, GDN_HEAD_DIM), lambda bi, i: (0, 0)),
                  pl.BlockSpec((3 * MIX, d), lambda bi, i: (0, 0)),
                  pl.BlockSpec((1, d), lambda bi, i: (0, 0))],
        out_specs=pl.BlockSpec((1, TILE, d), lambda bi, i: (bi, i, 0)),
        out_shape=jax.ShapeDtypeStruct((b, n_out * TILE, d), F32),
        compiler_params=_cparams("parallel", "parallel"),
        name="outproj",
    )(c_src, x_src, modsel, att, of, ob, p, hf, hb, p, gdn_norm, w_out, final_g)


def _rope_tables(s, ctx_len):
    t = np.arange(s - ctx_len)
    n_freq = ATT_HEAD_DIM // 4
    inv = ROPE_BASE ** (-np.arange(n_freq, dtype=np.float64) / n_freq)
    ang = [(t // GRID_W)[:, None] * inv, (t % GRID_W)[:, None] * inv]
    cos = np.concatenate([np.cos(ang[0]), np.cos(ang[0]), np.cos(ang[1]), np.cos(ang[1])], axis=1)
    sin = np.concatenate([-np.sin(ang[0]), np.sin(ang[0]), -np.sin(ang[1]), np.sin(ang[1])], axis=1)
    cos = np.concatenate([np.ones((ctx_len, ATT_HEAD_DIM)), cos], axis=0)
    sin = np.concatenate([np.zeros((ctx_len, ATT_HEAD_DIM)), sin], axis=0)
    return (jnp.asarray(np.tile(cos, (1, 2)), F32), jnp.asarray(np.tile(sin, (1, 2)), F32))


def _permute_w_in(w):
    widths = (512, 128, 128, 512, 512, 512, 512, 8, 8, 512, 512, 512)
    offs = np.concatenate([[0], np.cumsum(widths)])
    names = ("att_q", "att_k", "att_v", "att_z", "gdn_q", "gdn_k", "gdn_v", "gdn_b", "gdn_a", "gdn_z",
             "lru_x", "lru_z")
    col = {n: w[:, offs[i]:offs[i + 1]] for i, n in enumerate(names)}
    pad = jnp.zeros((w.shape[0], LANES - 8), w.dtype)
    return jnp.concatenate(
        [col["att_q"], col["att_z"], col["gdn_q"], col["gdn_k"], col["gdn_v"], col["gdn_z"], col["lru_x"],
         col["lru_z"], col["att_k"], col["att_v"], col["gdn_b"], pad, col["gdn_a"], pad], axis=1)


def _block_diag(w):
    eye = jnp.eye(LRU_BLOCKS, dtype=w.dtype)
    return jnp.einsum("nde,nm->ndme", w, eye).reshape(LRU_WIDTH, LRU_WIDTH)


def _pad_lanes(v):
    return jnp.pad(v.reshape(1, -1), ((0, 0), (0, LANES - v.size)))


def kernel(x, c, ctx, c_ctx, norm_g, w_mod, b_mod, w_in, att_sink, gdn_conv, gdn_a_log, gdn_dt_bias, gdn_norm,
           lru_conv_w, lru_conv_b, lru_w_r, lru_b_r, lru_w_i, lru_b_i, lru_lambda, w_out, final_g):
    b, t, d = x.shape
    ctx_len = ctx.shape[1]
    depth = w_in.shape[0]
    assert ctx_len == TILE and t % TILE == 0 and d == D_MODEL and b == 2
    s = ctx_len + t
    nt = s // TILE

    cos_t, sin_t = _rope_tables(s, ctx_len)
    ct = jnp.pad(jnp.concatenate([c, c_ctx[None]], axis=0).T, ((0, 0), (0, LANES - b - 1)))
    mods = _modulation(ct, w_mod, b_mod)

    c_src, x_src, src_off = ctx, x, 1
    out = None
    for l in range(depth):
        m3 = mods[l, :3].reshape(3, 3, d)
        sel = jnp.stack([jnp.stack([m3[b], m3[bi]], axis=0) for bi in range(b)], axis=0)
        modsel = jnp.pad(sel, ((0, 0), (0, 0), (0, 5), (0, 0)))
        w_perm = _permute_w_in(w_in[l].astype(BF16))
        p = _inproj(c_src, x_src, src_off, modsel, norm_g[l].reshape(1, d), cos_t, sin_t, w_perm)

        att = _attention(p, att_sink[l])

        prep = _gdn_prep(p, gdn_conv[l], _pad_lanes(gdn_a_log[l]), _pad_lanes(gdn_dt_bias[l]))
        o_f, o_b = _gdn_scan(*prep)

        wg = jnp.concatenate([_block_diag(lru_w_r[l, 0]), _block_diag(lru_w_i[l, 0]),
                              _block_diag(lru_w_r[l, 1]), _block_diag(lru_w_i[l, 1])], axis=1).astype(BF16)
        bg = 0.5 * jnp.stack([lru_b_r[l, 0], lru_b_i[l, 0], lru_b_r[l, 1], lru_b_i[l, 1]], axis=0)
        h_f, h_b = _lru(p, 0.5 * lru_conv_w[l], 0.5 * lru_conv_b[l].reshape(1, -1), wg, bg, lru_lambda[l])

        final = l == depth - 1
        res = _outproj(c_src, x_src, src_off, modsel, att, o_f, o_b, p, h_f, h_b, gdn_norm[l].reshape(1, -1),
                       w_out[l].astype(BF16), final_g.reshape(1, d), final)
        if final:
            out = res
        else:
            c_src, x_src, src_off = res, res, 0
    return out
```

```python
import functools
import math

import numpy as np
import jax
import jax.numpy as jnp
from jax import lax
from jax.experimental import pallas as pl
from jax.experimental.pallas import tpu as pltpu

F32 = jnp.float32
BF16 = jnp.bfloat16

EPS = 1e-6
NEG_INF = -1e30
LOG2E = math.log2(math.e)
GRID_W = 64
ROPE_BASE = 10000.0

D_MODEL = 1024
ATT_HEADS = 8
ATT_KV_HEADS = 2
ATT_HEAD_DIM = 64
ATT_BLOCK = 128
GDN_HEADS = 4
GDN_HEAD_DIM = 128
GDN_CHUNK = 64
LRU_WIDTH = 512
LRU_BLOCKS = 8
LRU_BLOCK_DIM = 64
LRU_C = 8.0
N_DIR = 2
MIX = 512
PACK = GDN_HEADS * GDN_CHUNK

TILE = 256
HALO = 8
LANES = 128
VMEM_LIMIT = 56 * 1024 * 1024

C_ATT_Q, C_ATT_Z, C_GDN_Q, C_GDN_K, C_GDN_V, C_GDN_Z, C_LRU_X, C_LRU_Z = (i * MIX for i in range(8))
C_ATT_K = 8 * MIX
C_ATT_V = C_ATT_K + LANES
C_GDN_B = C_ATT_V + LANES
C_GDN_A = C_GDN_B + LANES
N_IN = C_GDN_A + LANES


def _cparams(*sem):
    return pltpu.CompilerParams(dimension_semantics=sem, vmem_limit_bytes=VMEM_LIMIT)


def _sigmoid(x):
    return 1.0 / (1.0 + jnp.exp(-x))


def _silu(x):
    return x * _sigmoid(x)


def _softplus(x):
    return jnp.maximum(x, 0.0) + jnp.log1p(jnp.exp(-jnp.abs(x)))


def _dot_nt(a, b):
    return lax.dot_general(a.astype(BF16), b.astype(BF16), (((1,), (1,)), ((), ())),
                           preferred_element_type=F32)


def _dot_tn(a, b):
    return lax.dot_general(a.astype(BF16), b.astype(BF16), (((0,), (0,)), ((), ())),
                           preferred_element_type=F32)


def _mod_kernel(ct_ref, w_ref, b_ref, o_ref):
    s = _silu(ct_ref[...])
    w = w_ref[0]
    rid = lax.broadcasted_iota(jnp.int32, (8, w.shape[1]), 0)
    out = jnp.zeros((8, w.shape[1]), F32)
    for r in range(3):
        row = jnp.sum(w * s[:, r:r + 1], axis=0, keepdims=True) + b_ref[0]
        out = jnp.where(rid == r, row, out)
    o_ref[0] = out


def _modulation(ct, w_mod, b_mod):
    depth, d, d3 = w_mod.shape
    return pl.pallas_call(
        _mod_kernel,
        grid=(depth, d3 // d),
        in_specs=[pl.BlockSpec((d, LANES), lambda l, j: (0, 0)),
                  pl.BlockSpec((1, d, d), lambda l, j: (l, 0, j)),
                  pl.BlockSpec((1, 1, d), lambda l, j: (l, 0, j))],
        out_specs=pl.BlockSpec((1, 8, d), lambda l, j: (l, 0, j)),
        out_shape=jax.ShapeDtypeStruct((depth, 8, d3), F32),
        compiler_params=_cparams("parallel", "parallel"),
        name="modulation",
    )(ct, w_mod, b_mod.reshape(depth, 1, d3))


def _rope(x, cos, sin):
    lane = lax.broadcasted_iota(jnp.int32, x.shape, 1)
    first = (lane % 32) < 16
    partner = jnp.where(first, pltpu.roll(x, LANES - 16, 1), pltpu.roll(x, 16, 1))
    return x * cos + partner * sin


def _stream_tile(c_ref, x_ref, tile):
    is_ctx = (jnp.zeros((TILE, 1), jnp.int32) + tile) == 0
    return jnp.where(is_ctx, c_ref[0], x_ref[0])


def _stream_specs(d, off, shift=0):
    return [pl.BlockSpec((1, TILE, d), lambda bi, i: (bi, 0, 0)),
            pl.BlockSpec((1, TILE, d), lambda bi, i: (bi, jnp.maximum(i + shift - off, 1 - off), 0))]


def _inproj_kernel(c_ref, x_ref, mod_ref, g_ref, cos_ref, sin_ref, w_ref, o_ref):
    x = _stream_tile(c_ref, x_ref, pl.program_id(1))
    ms = jnp.mean(x * x, axis=-1, keepdims=True)
    y = x * lax.rsqrt(ms + EPS) * g_ref[...]
    shift = mod_ref[0, 0, 0:1, :]
    scale = mod_ref[0, 0, 1:2, :]
    hb = (y * (1.0 + scale) + shift).astype(BF16)
    cos = cos_ref[...]
    sin = sin_ref[...]
    for c0 in range(0, N_IN, MIX):
        res = jnp.dot(hb, w_ref[:, c0:c0 + MIX], preferred_element_type=F32)
        if c0 == C_ATT_Q:
            res = jnp.concatenate(
                [_rope(res[:, k:k + LANES], cos, sin) for k in range(0, MIX, LANES)], axis=1)
        if c0 == C_ATT_K:
            res = jnp.concatenate([_rope(res[:, :LANES], cos, sin), res[:, LANES:]], axis=1)
        o_ref[0, :, c0:c0 + MIX] = res


def _inproj(c_src, x_src, off, modsel, g, cos_t, sin_t, w_perm):
    b, _, d = x_src.shape
    s = x_src.shape[1] + off * TILE
    nt = s // TILE
    return pl.pallas_call(
        _inproj_kernel,
        grid=(b, nt),
        in_specs=_stream_specs(d, off) + [
                  pl.BlockSpec((1, 1, 8, d), lambda bi, i: (bi, jnp.minimum(i, 1), 0, 0)),
                  pl.BlockSpec((1, d), lambda bi, i: (0, 0)),
                  pl.BlockSpec((TILE, LANES), lambda bi, i: (i, 0)),
                  pl.BlockSpec((TILE, LANES), lambda bi, i: (i, 0)),
                  pl.BlockSpec((d, N_IN), lambda bi, i: (0, 0))],
        out_specs=pl.BlockSpec((1, TILE, N_IN), lambda bi, i: (bi, i, 0)),
        out_shape=jax.ShapeDtypeStruct((b, s, N_IN), F32),
        compiler_params=_cparams("parallel", "parallel"),
        name="inproj",
    )(c_src, x_src, modsel, g, cos_t, sin_t, w_perm)


def _att_kernel(sink_ref, q_ref, z_ref, kvp_ref, kvc_ref, kvn_ref, kvx_ref, o_ref, *, n_blocks):
    m = pl.program_id(1)
    first = TILE // ATT_BLOCK
    c_lo = jnp.where(m >= first + 1, 0, jnp.where(m >= first, ATT_BLOCK, 3 * ATT_BLOCK))
    c_hi = jnp.where(m < first, 0, jnp.where(m <= n_blocks - 2, 3 * ATT_BLOCK, 2 * ATT_BLOCK))
    n_band = 3 * ATT_BLOCK
    half = ATT_HEAD_DIM
    group = ATT_HEADS // ATT_KV_HEADS

    q = q_ref[0] * (ATT_HEAD_DIM ** -0.5 * LOG2E)
    kv = jnp.concatenate([kvp_ref[0], kvc_ref[0], kvn_ref[0], kvx_ref[0]], axis=0)
    nk = kv.shape[0]
    k_all, v_all = kv[:, :LANES], kv[:, LANES:]
    k_sw, v_sw = pltpu.roll(k_all, half, 1), pltpu.roll(v_all, half, 1)
    lo_k = lax.broadcasted_iota(jnp.int32, (nk, LANES), 1) < half
    k_dup = [jnp.where(lo_k, k_all, k_sw).astype(BF16), jnp.where(lo_k, k_sw, k_all).astype(BF16)]
    v_dup = [jnp.where(lo_k, v_all, v_sw).astype(BF16), jnp.where(lo_k, v_sw, v_all).astype(BF16)]

    r = lax.broadcasted_iota(jnp.int32, (ATT_BLOCK, n_band), 0)
    c = lax.broadcasted_iota(jnp.int32, (ATT_BLOCK, n_band), 1)
    ok = jnp.logical_and(jnp.abs(r - (c - ATT_BLOCK)) <= ATT_BLOCK, jnp.logical_and(c >= c_lo, c < c_hi))
    bias = jnp.where(ok, 0.0, NEG_INF)
    bias = jnp.concatenate([bias] * group, axis=0)

    lo_q = lax.broadcasted_iota(jnp.int32, (ATT_BLOCK, LANES), 1) < half
    outs = []
    for h in range(ATT_KV_HEADS):
        lhs, sinks = [], []
        for j in range(h * group, (h + 1) * group):
            q_t = q[:, (j // 2) * LANES:(j // 2 + 1) * LANES]
            lhs.append(jnp.where(lo_q, q_t, 0.0) if j % 2 == 0 else jnp.where(lo_q, 0.0, q_t))
            sinks.append(jnp.full((ATT_BLOCK, 1), sink_ref[j] * LOG2E, F32))
        qs = jnp.concatenate(lhs, axis=0).astype(BF16)
        sk = jnp.concatenate(sinks, axis=0)
        s = lax.dot_general(qs, k_dup[h], (((1,), (1,)), ((), ())), preferred_element_type=F32)
        s_band = s[:, :n_band] + bias
        s_ctx = s[:, n_band:]
        mx = jnp.maximum(jnp.maximum(jnp.max(s_band, axis=-1, keepdims=True),
                                     jnp.max(s_ctx, axis=-1, keepdims=True)), sk)
        p_band = jnp.exp2(s_band - mx)
        p_ctx = jnp.exp2(s_ctx - mx)
        den = (jnp.sum(p_band, axis=-1, keepdims=True) + jnp.sum(p_ctx, axis=-1, keepdims=True)
               + jnp.exp2(sk - mx))
        p = jnp.concatenate([p_band, p_ctx], axis=1).astype(BF16)
        res = jnp.dot(p, v_dup[h], preferred_element_type=F32) / den
        for t in range(group // 2):
            even = res[(2 * t) * ATT_BLOCK:(2 * t + 1) * ATT_BLOCK]
            odd = res[(2 * t + 1) * ATT_BLOCK:(2 * t + 2) * ATT_BLOCK]
            outs.append(jnp.where(lo_q, even, odd))
    o = jnp.concatenate(outs, axis=1)
    o_ref[0] = (o * _silu(z_ref[0])).astype(o_ref.dtype)


def _attention(p, sink):
    b, s, _ = p.shape
    nb = s // ATT_BLOCK
    first = TILE // ATT_BLOCK
    kvcol = C_ATT_K // (2 * LANES)
    kv_spec = lambda f: pl.BlockSpec((1, ATT_BLOCK, 2 * LANES), f)
    return pl.pallas_call(
        functools.partial(_att_kernel, n_blocks=nb),
        grid=(b, nb),
        in_specs=[pl.BlockSpec(memory_space=pltpu.SMEM),
                  pl.BlockSpec((1, ATT_BLOCK, MIX), lambda bi, m: (bi, m, C_ATT_Q // MIX)),
                  pl.BlockSpec((1, ATT_BLOCK, MIX), lambda bi, m: (bi, m, C_ATT_Z // MIX)),
                  kv_spec(lambda bi, m: (bi, jnp.maximum(m - 1, first), kvcol)),
                  kv_spec(lambda bi, m: (bi, m, kvcol)),
                  kv_spec(lambda bi, m: (bi, jnp.minimum(m + 1, nb - 1), kvcol)),
                  pl.BlockSpec((1, TILE, 2 * LANES), lambda bi, m: (bi, 0, kvcol))],
        out_specs=pl.BlockSpec((1, ATT_BLOCK, MIX), lambda bi, m: (bi, m, 0)),
        out_shape=jax.ShapeDtypeStruct((b, s, MIX), BF16),
        compiler_params=_cparams("parallel", "parallel"),
        name="attention",
    )(sink, p, p, p, p, p, p)


def _conv4(x, hp, hn, w, prev_ok, next_ok):
    hp = hp * prev_ok
    hn = hn * next_ok
    ext = jnp.concatenate([hp, x, hn], axis=0)
    n = x.shape[0]
    return (w[0:1] * ext[HALO - 2:HALO - 2 + n] + w[1:2] * ext[HALO - 1:HALO - 1 + n]
            + w[2:3] * ext[HALO:HALO + n] + w[3:4] * ext[HALO + 1:HALO + 1 + n])


def _halo_ok(i, n_tiles):
    prev_ok = (i >= 2).astype(F32)
    next_ok = jnp.logical_and(i != 0, i != n_tiles - 1).astype(F32)
    return prev_ok, next_ok


def _chunk_cumsum(g, reverse):
    n = g.shape[0]
    row = lax.broadcasted_iota(jnp.int32, g.shape, 0) % GDN_CHUNK
    s = 1
    while s < GDN_CHUNK:
        if reverse:
            g = g + jnp.where(row < GDN_CHUNK - s, pltpu.roll(g, n - s, 0), 0.0)
        else:
            g = g + jnp.where(row >= s, pltpu.roll(g, s, 0), 0.0)
        s *= 2
    return g


def _spread(cols, masks):
    out = cols[-1]
    for m, col in zip(masks[-2::-1], cols[-2::-1]):
        out = jnp.where(m, col, out)
    return out


def _head_block_diag(x, masks):
    return jnp.concatenate([jnp.where(m, x, 0.0) for m in masks], axis=0).astype(BF16)


def _gdn_prep_kernel(q_ref, qp_ref, qn_ref, k_ref, kp_ref, kn_ref, v_ref, vp_ref, vn_ref,
                     b_ref, a_ref, cw_ref, alog_ref, dtb_ref,
                     w_ref, u_ref, kd_ref, qg_ref, aqk_ref, gl_ref, *, n_tiles):
    i = pl.program_id(1)
    prev_ok, next_ok = _halo_ok(i, n_tiles)
    hd, ch, nh = GDN_HEAD_DIM, GDN_CHUNK, GDN_HEADS
    nch = TILE // ch

    def prep(x_ref, p_ref, n_ref, col):
        return _silu(_conv4(x_ref[0], p_ref[0], n_ref[0], cw_ref[:, col:col + MIX], prev_ok, next_ok))

    def l2n(t):
        parts = []
        for h in range(nh):
            th = t[:, h * hd:(h + 1) * hd]
            parts.append(th * lax.rsqrt(jnp.sum(th * th, axis=-1, keepdims=True) + EPS))
        return jnp.concatenate(parts, axis=1)

    q = l2n(prep(q_ref, qp_ref, qn_ref, 0)) * (hd ** -0.5)
    k = l2n(prep(k_ref, kp_ref, kn_ref, MIX))
    v = prep(v_ref, vp_ref, vn_ref, 2 * MIX)

    lane = lax.broadcasted_iota(jnp.int32, (TILE, LANES), 1)
    g = -jnp.exp(alog_ref[...]) * _softplus(a_ref[0] + dtb_ref[...])
    gcum = jnp.where(lane < nh, _chunk_cumsum(g, False), _chunk_cumsum(g, True))
    beta = _sigmoid(b_ref[0])

    ii = lax.broadcasted_iota(jnp.int32, (ch, PACK), 0)
    ll = lax.broadcasted_iota(jnp.int32, (ch, PACK), 1)
    jj = ll % ch
    eye_p = jj == ii
    eye_f = eye_p.astype(F32)
    bd16 = (jj // 16) == (ii // 16)
    bd32 = (jj // 32) == (ii // 32)
    merge_masks = (jnp.logical_and(bd32, jnp.logical_not(bd16)), jnp.logical_not(bd32))
    strict = [jj < ii, jj > ii]
    incl = [jj <= ii, jj >= ii]
    head_p = [(ll // ch) == h for h in range(nh)]
    lw = lax.broadcasted_iota(jnp.int32, (ch, MIX), 1) // hd
    head_w = [lw == h for h in range(nh)]
    lane_row = lax.broadcasted_iota(jnp.int32, (1, LANES), 1)

    def pdot(a, b):
        return jnp.dot(a.astype(BF16), _head_block_diag(b, head_p), preferred_element_type=F32)

    gl_rows = []
    units = []
    for cidx in range(nch):
        rows = slice(cidx * ch, (cidx + 1) * ch)
        kc, qc, vc, gc, bc = k[rows], q[rows], v[rows], gcum[rows], beta[rows]
        kq = _dot_nt(jnp.concatenate([kc, qc], axis=0), _head_block_diag(kc, head_w))
        g_end = jnp.where(lane_row < nh, gc[ch - 1:ch, :], gc[0:1, :])
        gl_rows.append(jnp.exp(g_end))
        for d in range(N_DIR):
            gcols = [gc[:, d * nh + h:d * nh + h + 1] for h in range(nh)]
            bcols = [bc[:, d * nh + h:d * nh + h + 1] for h in range(nh)]
            gcol_p = _spread(gcols, head_p)
            grow_p = jnp.sum(jnp.where(eye_p, gcol_p, 0.0), axis=0, keepdims=True)
            dec = jnp.exp(jnp.where(incl[d], gcol_p - grow_p, 0.0))
            lm = jnp.where(strict[d], _spread(bcols, head_p) * dec * kq[:ch], 0.0)
            aqk = jnp.where(incl[d], dec * kq[ch:], 0.0)
            aqk_ref[0, rows, d * PACK:(d + 1) * PACK] = aqk.astype(BF16)
            units.append((rows, d, lm, kc, qc, vc, gcols, bcols,
                          [g_end[:, d * nh + h:d * nh + h + 1] for h in range(nh)]))
    gl_rows.append(jnp.zeros((8 - nch, LANES), F32))
    gl_ref[0, 0] = jnp.concatenate(gl_rows, axis=0)

    lms = [u[2] for u in units]
    m = [jnp.where(bd16, -lm, 0.0) for lm in lms]
    x = [eye_f + mi for mi in m]
    pw = [pdot(mi, mi) for mi in m]
    for _ in range(2):
        r = [pdot(jnp.concatenate([xi, pi], axis=0), pi) for xi, pi in zip(x, pw)]
        x = [xi + ri[:ch] for xi, ri in zip(x, r)]
        pw = [ri[ch:] for ri in r]
    x = [xi + pdot(xi, pi) for xi, pi in zip(x, pw)]
    for mask in merge_masks:
        y = [pdot(xi, jnp.where(mask, lm, 0.0)) for xi, lm in zip(x, lms)]
        x = [xi - pdot(yi, xi) for xi, yi in zip(x, y)]

    for tinv, (rows, d, _, kc, qc, vc, gcols, bcols, gend_cols) in zip(x, units):
        slab = lambda t, h: t[:, h * hd:(h + 1) * hd]
        gam = [jnp.exp(gcol) for gcol in gcols]
        bgk = jnp.concatenate([(bcols[h] * gam[h]) * slab(kc, h) for h in range(nh)], axis=1)
        bv = jnp.concatenate([bcols[h] * slab(vc, h) for h in range(nh)], axis=1)
        rhs = jnp.concatenate([_head_block_diag(bgk, head_w), _head_block_diag(bv, head_w)], axis=1)
        wu = jnp.dot(tinv.astype(BF16), rhs, preferred_element_type=F32)
        cols = slice(d * MIX, (d + 1) * MIX)
        w_ref[0, rows, cols] = wu[:, :MIX].astype(BF16)
        u_ref[0, rows, cols] = wu[:, MIX:]
        kd_ref[0, rows, cols] = jnp.concatenate(
            [jnp.exp(gend_cols[h] - gcols[h]) * slab(kc, h) for h in range(nh)], axis=1).astype(BF16)
        qg_ref[0, rows, cols] = jnp.concatenate([gam[h] * slab(qc, h) for h in range(nh)], axis=1).astype(BF16)


def _gdn_prep(p, conv_w, alog, dtb):
    b, s, _ = p.shape
    nt = s // TILE
    hpt = TILE // HALO
    nh = s // HALO
    wide = N_DIR * GDN_HEADS * GDN_HEAD_DIM

    def tile_spec(col):
        return pl.BlockSpec((1, TILE, MIX), lambda bi, i: (bi, i, col // MIX))

    def prev_spec(col):
        return pl.BlockSpec((1, HALO, MIX), lambda bi, i: (bi, jnp.maximum(i * hpt - 1, 0), col // MIX))

    def next_spec(col):
        return pl.BlockSpec((1, HALO, MIX), lambda bi, i: (bi, jnp.minimum((i + 1) * hpt, nh - 1), col // MIX))

    in_specs = []
    for col in (C_GDN_Q, C_GDN_K, C_GDN_V):
        in_specs += [tile_spec(col), prev_spec(col), next_spec(col)]
    in_specs += [pl.BlockSpec((1, TILE, LANES), lambda bi, i: (bi, i, C_GDN_B // LANES)),
                 pl.BlockSpec((1, TILE, LANES), lambda bi, i: (bi, i, C_GDN_A // LANES)),
                 pl.BlockSpec((4, 3 * MIX), lambda bi, i: (0, 0)),
                 pl.BlockSpec((1, LANES), lambda bi, i: (0, 0)),
                 pl.BlockSpec((1, LANES), lambda bi, i: (0, 0))]
    wide_spec = pl.BlockSpec((1, TILE, wide), lambda bi, i: (bi, i, 0))
    out_specs = [wide_spec, wide_spec, wide_spec, wide_spec,
                 pl.BlockSpec((1, TILE, N_DIR * PACK), lambda bi, i: (bi, i, 0)),
                 pl.BlockSpec((1, 1, 8, LANES), lambda bi, i: (bi, i, 0, 0))]
    out_shape = [jax.ShapeDtypeStruct((b, s, wide), BF16),
                 jax.ShapeDtypeStruct((b, s, wide), F32),
                 jax.ShapeDtypeStruct((b, s, wide), BF16),
                 jax.ShapeDtypeStruct((b, s, wide), BF16),
                 jax.ShapeDtypeStruct((b, s, N_DIR * PACK), BF16),
                 jax.ShapeDtypeStruct((b, nt, 8, LANES), F32)]
    return pl.pallas_call(
        functools.partial(_gdn_prep_kernel, n_tiles=nt),
        grid=(b, nt),
        in_specs=in_specs,
        out_specs=out_specs,
        out_shape=out_shape,
        compiler_params=_cparams("parallel", "parallel"),
        name="gdn_prep",
    )(p, p, p, p, p, p, p, p, p, p, p, conv_w, alog, dtb)


def _gdn_scan_body(wf, uf, kdf, qgf, af, glf, wb, ub, kdb, qgb, ab, glb, of_ref, ob_ref, state):
    hd, ch, nh = GDN_HEAD_DIM, GDN_CHUNK, GDN_HEADS
    nch = TILE // ch
    lw = lax.broadcasted_iota(jnp.int32, (ch, MIX), 1) // hd
    head_w = [lw == h for h in range(nh)]
    dirs = ((wf, uf, kdf, qgf, af, glf, of_ref), (wb, ub, kdb, qgb, ab, glb, ob_ref))
    for step in range(nch):
        cidx = [step, nch - 1 - step]
        rows = [slice(c * ch, (c + 1) * ch) for c in cidx]
        s_old = {}
        ws = {}
        for d, (w_r, _, _, qg_r, _, _, _) in enumerate(dirs):
            for h in range(nh):
                cols = slice(h * hd, (h + 1) * hd)
                s_old[d, h] = state[d, h]
                lhs = jnp.concatenate([w_r[0, rows[d], cols], qg_r[0, rows[d], cols]], axis=0)
                ws[d, h] = jnp.dot(lhs, s_old[d, h].astype(BF16), preferred_element_type=F32)
            yield
        for d, (_, u_r, kd_r, _, a_r, gl_r, o_r) in enumerate(dirs):
            un = u_r[0, rows[d], :] - jnp.concatenate([ws[d, h][:ch] for h in range(nh)], axis=1)
            inter = jnp.concatenate([ws[d, h][ch:] for h in range(nh)], axis=1)
            intra = jnp.dot(a_r[0, rows[d], :], _head_block_diag(un, head_w), preferred_element_type=F32)
            o_r[0, rows[d], :] = (inter + intra).astype(o_r.dtype)
            unb = un.astype(BF16)
            for h in range(nh):
                cols = slice(h * hd, (h + 1) * hd)
                gl = gl_r[0, 0, cidx[d]:cidx[d] + 1, d * nh + h:d * nh + h + 1]
                state[d, h] = gl * s_old[d, h] + _dot_tn(kd_r[0, rows[d], cols], unb[:, cols])
            yield


def _mirror(j, nt):
    return jnp.where(j == 0, 0, nt - j)


def _lru_body(xf_ref, xfp_ref, xfn_ref, xb_ref, xbp_ref, xbn_ref, cw_ref, cb_ref, wg_ref, bg_ref, lam_ref,
              hf_ref, hb_ref, carry, *, n_tiles):
    j = pl.program_id(1)
    seg_n = 8
    steps = TILE // seg_n
    sub = lax.broadcasted_iota(jnp.int32, (seg_n, LRU_WIDTH), 0)

    def one_dir(d, x_ref, p_ref, n_ref, tile_idx, o_ref):
        prev_ok, next_ok = _halo_ok(tile_idx, n_tiles)
        xt = jnp.swapaxes(x_ref[0].reshape(seg_n, steps, LRU_WIDTH), 0, 1)
        xs = [xt[st] for st in range(steps)]
        hp = p_ref[0] * prev_ok
        hn = n_ref[0] * next_ok
        p1 = jnp.where(sub == 0, hp[7:8], pltpu.roll(xs[steps - 1], 1, 0))
        p2 = jnp.where(sub == 0, hp[6:7], pltpu.roll(xs[steps - 2], 1, 0))
        n1 = jnp.where(sub == seg_n - 1, hn[0:1], pltpu.roll(xs[0], seg_n - 1, 0))
        ext = [p2, p1] + xs + [n1]
        cw = cw_ref[...]
        xh = jnp.concatenate(
            [cw[0:1] * ext[st] + cw[1:2] * ext[st + 1] + cw[2:3] * ext[st + 2] + cw[3:4] * ext[st + 3]
             for st in range(steps)], axis=0) + cb_ref[...]
        yield
        gates = jnp.dot(xh.astype(BF16), wg_ref[:, d * 2 * LRU_WIDTH:(d + 1) * 2 * LRU_WIDTH],
                        preferred_element_type=F32)
        yield
        c_half = (-0.5 * LRU_C) * _softplus(-lam_ref[d:d + 1, :])
        a_blocks, b_blocks = [], []
        blk = TILE // 4
        for r0 in range(0, TILE, blk):
            gb = gates[r0:r0 + blk]
            tr = jnp.tanh(gb[:, :LRU_WIDTH] + bg_ref[2 * d:2 * d + 1, :])
            ti = jnp.tanh(gb[:, LRU_WIDTH:] + bg_ref[2 * d + 1:2 * d + 2, :])
            log_a = c_half * tr + c_half
            ab = jnp.exp(log_a)
            b_blocks.append(jnp.sqrt(-jnp.tanh(log_a) * (ab * ab + 1.0)) * ((ti + 1.0) * xh[r0:r0 + blk]))
            a_blocks.append(ab)
            yield
        a = jnp.concatenate(a_blocks, axis=0)
        bb = jnp.concatenate(b_blocks, axis=0)

        order = list(range(steps)) if d == 0 else list(range(steps - 1, -1, -1))
        piece = lambda t, st: t[st * seg_n:(st + 1) * seg_n]
        a_tot = piece(a, order[0])
        b_tot = piece(bb, order[0])
        for n, st in enumerate(order[1:]):
            b_tot = piece(a, st) * b_tot + piece(bb, st)
            a_tot = piece(a, st) * a_tot
            if n % 8 == 7:
                yield
        s = 1
        while s < seg_n:
            if d == 0:
                keep = sub >= s
                a_sh = jnp.where(keep, pltpu.roll(a_tot, s, 0), 1.0)
                b_sh = jnp.where(keep, pltpu.roll(b_tot, s, 0), 0.0)
            else:
                keep = sub < seg_n - s
                a_sh = jnp.where(keep, pltpu.roll(a_tot, seg_n - s, 0), 1.0)
                b_sh = jnp.where(keep, pltpu.roll(b_tot, seg_n - s, 0), 0.0)
            b_tot = a_tot * b_sh + b_tot
            a_tot = a_tot * a_sh
            s *= 2
        h_in0 = carry[d]
        h_out = a_tot * h_in0 + b_tot
        if d == 0:
            h = jnp.where(sub == 0, h_in0, pltpu.roll(h_out, 1, 0))
            carry[d] = h_out[seg_n - 1:seg_n]
        else:
            h = jnp.where(sub == seg_n - 1, h_in0, pltpu.roll(h_out, seg_n - 1, 0))
            carry[d] = h_out[0:1]
        hs = [None] * steps
        for n, st in enumerate(order):
            h = piece(a, st) * h + piece(bb, st)
            hs[st] = h
            if n % 8 == 7:
                yield
        o_ref[0] = jnp.swapaxes(jnp.stack(hs, axis=0), 0, 1).reshape(TILE, LRU_WIDTH).astype(o_ref.dtype)

    return [one_dir(0, xf_ref, xfp_ref, xfn_ref, j, hf_ref),
            one_dir(1, xb_ref, xbp_ref, xbn_ref, _mirror(j, n_tiles), hb_ref)]


N_SCAN_IN = 12
N_LRU_IN = 11


def _round_robin(gens, weights):
    live = list(zip(gens, weights))
    while live:
        nxt = []
        for g, w in live:
            alive = True
            for _ in range(w):
                try:
                    next(g)
                except StopIteration:
                    alive = False
                    break
            if alive:
                nxt.append((g, w))
        live = nxt


def _sweep_kernel(*refs, n_tiles):
    scan_in, lru_in = refs[:N_SCAN_IN], refs[N_SCAN_IN:N_SCAN_IN + N_LRU_IN]
    of_ref, ob_ref, hf_ref, hb_ref, state, carry = refs[N_SCAN_IN + N_LRU_IN:]

    @pl.when(pl.program_id(1) == 0)
    def _():
        state[...] = jnp.zeros_like(state)
        carry[...] = jnp.zeros_like(carry)

    _round_robin([_gdn_scan_body(*scan_in, of_ref, ob_ref, state)]
                 + _lru_body(*lru_in, hf_ref, hb_ref, carry, n_tiles=n_tiles), (1, 2, 2))


def _sweep(w, u, kd, qg, aqk, gl, p, conv_w, conv_b, wg, bg, lam):
    b, s, _ = u.shape
    nt = s // TILE
    half = GDN_HEADS * GDN_HEAD_DIM
    hpt = TILE // HALO
    nh = s // HALO
    col = C_LRU_X // MIX
    tile_of = (lambda j: j, lambda j: _mirror(j, nt))

    def scan_specs(d):
        idx = lambda bi, j: (bi, tile_of[d](j), d)
        gidx = lambda bi, j: (bi, tile_of[d](j), 0, 0)
        return [pl.BlockSpec((1, TILE, half), idx)] * 4 + [pl.BlockSpec((1, TILE, PACK), idx),
                                                           pl.BlockSpec((1, 1, 8, LANES), gidx)]

    def lru_specs(d):
        t = tile_of[d]
        return [pl.BlockSpec((1, TILE, MIX), lambda bi, j: (bi, t(j), col)),
                pl.BlockSpec((1, HALO, MIX), lambda bi, j: (bi, jnp.maximum(t(j) * hpt - 1, 0), col)),
                pl.BlockSpec((1, HALO, MIX), lambda bi, j: (bi, jnp.minimum((t(j) + 1) * hpt, nh - 1), col))]

    full = lambda shape: pl.BlockSpec(shape, lambda bi, j: tuple(0 for _ in shape))
    out_spec = lambda d: pl.BlockSpec((1, TILE, MIX), lambda bi, j: (bi, tile_of[d](j), 0))
    in_specs = (scan_specs(0) + scan_specs(1) + lru_specs(0) + lru_specs(1)
                + [full((4, LRU_WIDTH)), full((1, LRU_WIDTH)), full((LRU_WIDTH, 4 * LRU_WIDTH)),
                   full((4, LRU_WIDTH)), full((N_DIR, LRU_WIDTH))])
    assert len(in_specs) == N_SCAN_IN + N_LRU_IN
    return pl.pallas_call(
        functools.partial(_sweep_kernel, n_tiles=nt),
        grid=(b, nt),
        in_specs=in_specs,
        out_specs=[out_spec(0), out_spec(1), out_spec(0), out_spec(1)],
        out_shape=[jax.ShapeDtypeStruct((b, s, MIX), BF16)] * 4,
        scratch_shapes=[pltpu.VMEM((N_DIR, GDN_HEADS, GDN_HEAD_DIM, GDN_HEAD_DIM), F32),
                        pltpu.VMEM((N_DIR, 1, LRU_WIDTH), F32)],
        compiler_params=_cparams("parallel", "arbitrary"),
        name="sweep",
    )(w, u, kd, qg, aqk, gl, w, u, kd, qg, aqk, gl, p, p, p, p, p, p, conv_w, conv_b, wg, bg, lam)


def _outproj_kernel(c_ref, x_ref, mod_ref, att_ref, of_ref, ob_ref, gz_ref, hf_ref, hb_ref, lz_ref,
                    gn_ref, w_ref, fg_ref, o_ref, *, final, first_tile):
    hd = GDN_HEAD_DIM
    o = of_ref[0].astype(F32) + ob_ref[0].astype(F32)
    parts = []
    for h in range(GDN_HEADS):
        oh = o[:, h * hd:(h + 1) * hd]
        parts.append(oh * lax.rsqrt(jnp.mean(oh * oh, axis=-1, keepdims=True) + EPS) * gn_ref[...])
    gdn = jnp.concatenate(parts, axis=1) * _silu(gz_ref[0])
    lru = (hf_ref[0].astype(F32) + hb_ref[0].astype(F32)) * _silu(lz_ref[0])
    acc = (jnp.dot(att_ref[0], w_ref[0:MIX, :], preferred_element_type=F32)
           + jnp.dot(gdn.astype(BF16), w_ref[MIX:2 * MIX, :], preferred_element_type=F32)
           + jnp.dot(lru.astype(BF16), w_ref[2 * MIX:3 * MIX, :], preferred_element_type=F32))
    gate = mod_ref[0, 0, 2:3, :]
    xn = _stream_tile(c_ref, x_ref, pl.program_id(1) + first_tile) + gate * acc
    if final:
        ms = jnp.mean(xn * xn, axis=-1, keepdims=True)
        xn = xn * lax.rsqrt(ms + EPS) * fg_ref[...]
    o_ref[0] = xn


def _outproj(c_src, x_src, src_off, modsel, att, of, ob, p, hf, hb, gdn_norm, w_out, final_g, final):
    b, s, _ = p.shape
    d = x_src.shape[2]
    nt = s // TILE
    off = 1 if final else 0
    n_out = nt - off
    tok = lambda col: pl.BlockSpec((1, TILE, MIX), lambda bi, i: (bi, i + off, col))
    return pl.pallas_call(
        functools.partial(_outproj_kernel, final=final, first_tile=off),
        grid=(b, n_out),
        in_specs=_stream_specs(d, src_off, off) + [
                  pl.BlockSpec((1, 1, 8, d), lambda bi, i: (bi, jnp.minimum(i + off, 1), 0, 0)),
                  tok(0), tok(0), tok(0), tok(C_GDN_Z // MIX), tok(0), tok(0), tok(C_LRU_Z // MIX),
                  pl.BlockSpec((1, GDN_HEAD_DIM), lambda bi, i: (0, 0)),
                  pl.BlockSpec((3 * MIX, d), lambda bi, i: (0, 0)),
                  pl.BlockSpec((1, d), lambda bi, i: (0, 0))],
        out_specs=pl.BlockSpec((1, TILE, d), lambda bi, i: (bi, i, 0)),
        out_shape=jax.ShapeDtypeStruct((b, n_out * TILE, d), F32),
        compiler_params=_cparams("parallel", "parallel"),
        name="outproj",
    )(c_src, x_src, modsel, att, of, ob, p, hf, hb, p, gdn_norm, w_out, final_g)


def _rope_tables(s, ctx_len):
    t = np.arange(s - ctx_len)
    n_freq = ATT_HEAD_DIM // 4
    inv = ROPE_BASE ** (-np.arange(n_freq, dtype=np.float64) / n_freq)
    ang = [(t // GRID_W)[:, None] * inv, (t % GRID_W)[:, None] * inv]
    cos = np.concatenate([np.cos(ang[0]), np.cos(ang[0]), np.cos(ang[1]), np.cos(ang[1])], axis=1)
    sin = np.concatenate([-np.sin(ang[0]), np.sin(ang[0]), -np.sin(ang[1]), np.sin(ang[1])], axis=1)
    cos = np.concatenate([np.ones((ctx_len, ATT_HEAD_DIM)), cos], axis=0)
    sin = np.concatenate([np.zeros((ctx_len, ATT_HEAD_DIM)), sin], axis=0)
    return (jnp.asarray(np.tile(cos, (1, 2)), F32), jnp.asarray(np.tile(sin, (1, 2)), F32))


def _permute_w_in(w):
    widths = (512, 128, 128, 512, 512, 512, 512, 8, 8, 512, 512, 512)
    offs = np.concatenate([[0], np.cumsum(widths)])
    names = ("att_q", "att_k", "att_v", "att_z", "gdn_q", "gdn_k", "gdn_v", "gdn_b", "gdn_a", "gdn_z",
             "lru_x", "lru_z")
    col = {n: w[:, offs[i]:offs[i + 1]] for i, n in enumerate(names)}
    pad = jnp.zeros((w.shape[0], LANES - 8), w.dtype)
    return jnp.concatenate(
        [col["att_q"], col["att_z"], col["gdn_q"], col["gdn_k"], col["gdn_v"], col["gdn_z"], col["lru_x"],
         col["lru_z"], col["att_k"], col["att_v"], col["gdn_b"], pad, col["gdn_a"], pad], axis=1)


def _block_diag(w):
    eye = jnp.eye(LRU_BLOCKS, dtype=w.dtype)
    return jnp.einsum("nde,nm->ndme", w, eye).reshape(LRU_WIDTH, LRU_WIDTH)


def _pad_lanes(v):
    return jnp.pad(v.reshape(1, -1), ((0, 0), (0, LANES - v.size)))


def kernel(x, c, ctx, c_ctx, norm_g, w_mod, b_mod, w_in, att_sink, gdn_conv, gdn_a_log, gdn_dt_bias, gdn_norm,
           lru_conv_w, lru_conv_b, lru_w_r, lru_b_r, lru_w_i, lru_b_i, lru_lambda, w_out, final_g):
    b, t, d = x.shape
    ctx_len = ctx.shape[1]
    depth = w_in.shape[0]
    assert ctx_len == TILE and t % TILE == 0 and d == D_MODEL and b == 2
    s = ctx_len + t
    nt = s // TILE

    cos_t, sin_t = _rope_tables(s, ctx_len)
    ct = jnp.pad(jnp.concatenate([c, c_ctx[None]], axis=0).T, ((0, 0), (0, LANES - b - 1)))
    mods = _modulation(ct, w_mod, b_mod)

    c_src, x_src, src_off = ctx, x, 1
    out = None
    for l in range(depth):
        m3 = mods[l, :3].reshape(3, 3, d)
        sel = jnp.stack([jnp.stack([m3[b], m3[bi]], axis=0) for bi in range(b)], axis=0)
        modsel = jnp.pad(sel, ((0, 0), (0, 0), (0, 5), (0, 0)))
        w_perm = _permute_w_in(w_in[l].astype(BF16))
        p = _inproj(c_src, x_src, src_off, modsel, norm_g[l].reshape(1, d), cos_t, sin_t, w_perm)

        att = _attention(p, att_sink[l])

        prep = _gdn_prep(p, gdn_conv[l], _pad_lanes(gdn_a_log[l]), _pad_lanes(gdn_dt_bias[l]))

        wg = jnp.concatenate([_block_diag(lru_w_r[l, 0]), _block_diag(lru_w_i[l, 0]),
                              _block_diag(lru_w_r[l, 1]), _block_diag(lru_w_i[l, 1])], axis=1).astype(BF16)
        bg = 0.5 * jnp.stack([lru_b_r[l, 0], lru_b_i[l, 0], lru_b_r[l, 1], lru_b_i[l, 1]], axis=0)
        o_f, o_b, h_f, h_b = _sweep(*prep, p, 0.5 * lru_conv_w[l], 0.5 * lru_conv_b[l].reshape(1, -1), wg, bg,
                                    lru_lambda[l])

        final = l == depth - 1
        res = _outproj(c_src, x_src, src_off, modsel, att, o_f, o_b, p, h_f, h_b, gdn_norm[l].reshape(1, -1),
                       w_out[l].astype(BF16), final_g.reshape(1, d), final)
        if final:
            out = res
        else:
            c_src, x_src, src_off = res, res, 0
    return out
```

```python
import functools
import math

import numpy as np
import jax
import jax.numpy as jnp
from jax import lax
from jax.experimental import pallas as pl
from jax.experimental.pallas import tpu as pltpu

F32 = jnp.float32
BF16 = jnp.bfloat16

EPS = 1e-6
NEG_INF = -1e30
LOG2E = math.log2(math.e)
GRID_W = 64
ROPE_BASE = 10000.0

D_MODEL = 1024
ATT_HEADS = 8
ATT_KV_HEADS = 2
ATT_HEAD_DIM = 64
ATT_BLOCK = 128
GDN_HEADS = 4
GDN_HEAD_DIM = 128
GDN_CHUNK = 64
LRU_WIDTH = 512
LRU_BLOCKS = 8
LRU_BLOCK_DIM = 64
LRU_C = 8.0
N_DIR = 2
MIX = 512
PACK = GDN_HEADS * GDN_CHUNK

TILE = 256
HALO = 8
LANES = 128
VMEM_LIMIT = 56 * 1024 * 1024

C_ATT_Q, C_ATT_Z, C_GDN_Q, C_GDN_K, C_GDN_V, C_GDN_Z, C_LRU_X, C_LRU_Z = (i * MIX for i in range(8))
C_ATT_K = 8 * MIX
C_ATT_V = C_ATT_K + LANES
C_GDN_B = C_ATT_V + LANES
C_GDN_A = C_GDN_B + LANES
N_IN = C_GDN_A + LANES


def _cparams(*sem):
    return pltpu.CompilerParams(dimension_semantics=sem, vmem_limit_bytes=VMEM_LIMIT)


def _sigmoid(x):
    return 1.0 / (1.0 + jnp.exp(-x))


def _silu(x):
    return x * _sigmoid(x)


def _softplus(x):
    return jnp.maximum(x, 0.0) + jnp.log1p(jnp.exp(-jnp.abs(x)))


def _dot_nt(a, b):
    return lax.dot_general(a.astype(BF16), b.astype(BF16), (((1,), (1,)), ((), ())),
                           preferred_element_type=F32)


def _dot_tn(a, b):
    return lax.dot_general(a.astype(BF16), b.astype(BF16), (((0,), (0,)), ((), ())),
                           preferred_element_type=F32)


def _mod_kernel(ct_ref, w_ref, b_ref, o_ref):
    s = _silu(ct_ref[...])
    w = w_ref[0]
    rid = lax.broadcasted_iota(jnp.int32, (8, w.shape[1]), 0)
    out = jnp.zeros((8, w.shape[1]), F32)
    for r in range(3):
        row = jnp.sum(w * s[:, r:r + 1], axis=0, keepdims=True) + b_ref[0]
        out = jnp.where(rid == r, row, out)
    o_ref[0] = out


def _modulation(ct, w_mod, b_mod):
    depth, d, d3 = w_mod.shape
    return pl.pallas_call(
        _mod_kernel,
        grid=(depth, d3 // d),
        in_specs=[pl.BlockSpec((d, LANES), lambda l, j: (0, 0)),
                  pl.BlockSpec((1, d, d), lambda l, j: (l, 0, j)),
                  pl.BlockSpec((1, 1, d), lambda l, j: (l, 0, j))],
        out_specs=pl.BlockSpec((1, 8, d), lambda l, j: (l, 0, j)),
        out_shape=jax.ShapeDtypeStruct((depth, 8, d3), F32),
        compiler_params=_cparams("parallel", "parallel"),
        name="modulation",
    )(ct, w_mod, b_mod.reshape(depth, 1, d3))


def _rope(x, cos, sin):
    lane = lax.broadcasted_iota(jnp.int32, x.shape, 1)
    first = (lane % 32) < 16
    partner = jnp.where(first, pltpu.roll(x, LANES - 16, 1), pltpu.roll(x, 16, 1))
    return x * cos + partner * sin


def _stream_tile(c_ref, x_ref, tile):
    is_ctx = (jnp.zeros((TILE, 1), jnp.int32) + tile) == 0
    return jnp.where(is_ctx, c_ref[0], x_ref[0])


def _stream_specs(d, off, shift=0):
    return [pl.BlockSpec((1, TILE, d), lambda bi, i: (bi, 0, 0)),
            pl.BlockSpec((1, TILE, d), lambda bi, i: (bi, jnp.maximum(i + shift - off, 1 - off), 0))]


def _inproj_kernel(c_ref, x_ref, mod_ref, g_ref, cos_ref, sin_ref, w_ref, o_ref):
    x = _stream_tile(c_ref, x_ref, pl.program_id(1))
    ms = jnp.mean(x * x, axis=-1, keepdims=True)
    y = x * lax.rsqrt(ms + EPS) * g_ref[...]
    shift = mod_ref[0, 0, 0:1, :]
    scale = mod_ref[0, 0, 1:2, :]
    hb = (y * (1.0 + scale) + shift).astype(BF16)
    cos = cos_ref[...]
    sin = sin_ref[...]
    for c0 in range(0, N_IN, MIX):
        res = jnp.dot(hb, w_ref[:, c0:c0 + MIX], preferred_element_type=F32)
        if c0 == C_ATT_Q:
            res = jnp.concatenate(
                [_rope(res[:, k:k + LANES], cos, sin) for k in range(0, MIX, LANES)], axis=1)
        if c0 == C_ATT_K:
            res = jnp.concatenate([_rope(res[:, :LANES], cos, sin), res[:, LANES:]], axis=1)
        o_ref[0, :, c0:c0 + MIX] = res


def _inproj(c_src, x_src, off, modsel, g, cos_t, sin_t, w_perm):
    b, _, d = x_src.shape
    s = x_src.shape[1] + off * TILE
    nt = s // TILE
    return pl.pallas_call(
        _inproj_kernel,
        grid=(b, nt),
        in_specs=_stream_specs(d, off) + [
                  pl.BlockSpec((1, 1, 8, d), lambda bi, i: (bi, jnp.minimum(i, 1), 0, 0)),
                  pl.BlockSpec((1, d), lambda bi, i: (0, 0)),
                  pl.BlockSpec((TILE, LANES), lambda bi, i: (i, 0)),
                  pl.BlockSpec((TILE, LANES), lambda bi, i: (i, 0)),
                  pl.BlockSpec((d, N_IN), lambda bi, i: (0, 0))],
        out_specs=pl.BlockSpec((1, TILE, N_IN), lambda bi, i: (bi, i, 0)),
        out_shape=jax.ShapeDtypeStruct((b, s, N_IN), F32),
        compiler_params=_cparams("parallel", "parallel"),
        name="inproj",
    )(c_src, x_src, modsel, g, cos_t, sin_t, w_perm)


def _att_kernel(sink_ref, q_ref, z_ref, kvp_ref, kvc_ref, kvn_ref, kvx_ref, o_ref, *, n_blocks):
    i = pl.program_id(1)
    n_sub = TILE // ATT_BLOCK
    n_band = 3 * ATT_BLOCK
    half = ATT_HEAD_DIM
    group = ATT_HEADS // ATT_KV_HEADS

    kv = jnp.concatenate([kvp_ref[0], kvc_ref[0], kvn_ref[0], kvx_ref[0]], axis=0)
    nk = kv.shape[0]
    k_all, v_all = kv[:, :LANES], kv[:, LANES:]
    k_sw, v_sw = pltpu.roll(k_all, half, 1), pltpu.roll(v_all, half, 1)
    lo_k = lax.broadcasted_iota(jnp.int32, (nk, LANES), 1) < half
    k_dup = [jnp.where(lo_k, k_all, k_sw).astype(BF16), jnp.where(lo_k, k_sw, k_all).astype(BF16)]
    v_dup = [jnp.where(lo_k, v_all, v_sw).astype(BF16), jnp.where(lo_k, v_sw, v_all).astype(BF16)]
    n_loc = (n_sub + 2) * ATT_BLOCK

    def keys(t, sub):
        return jnp.concatenate([t[sub * ATT_BLOCK:sub * ATT_BLOCK + n_band], t[n_loc:]], axis=0)

    r = lax.broadcasted_iota(jnp.int32, (ATT_BLOCK, n_band), 0)
    c = lax.broadcasted_iota(jnp.int32, (ATT_BLOCK, n_band), 1)
    in_win = jnp.abs(r - (c - ATT_BLOCK)) <= ATT_BLOCK
    lo_q = lax.broadcasted_iota(jnp.int32, (ATT_BLOCK, LANES), 1) < half
    outs = [[None] * (ATT_HEADS // 2) for _ in range(n_sub)]

    def band_bias(sub):
        m = i * n_sub + sub
        c_lo = jnp.where(m >= n_sub + 1, 0, jnp.where(m >= n_sub, ATT_BLOCK, n_band))
        c_hi = jnp.where(m < n_sub, 0, jnp.where(m <= n_blocks - 2, n_band, 2 * ATT_BLOCK))
        return jnp.where(jnp.logical_and(in_win, jnp.logical_and(c >= c_lo, c < c_hi)), 0.0, NEG_INF)

    def stream(sub, h, bias):
        q = q_ref[0, sub * ATT_BLOCK:(sub + 1) * ATT_BLOCK, :] * (ATT_HEAD_DIM ** -0.5 * LOG2E)
        lhs = []
        for j in range(h * group, (h + 1) * group):
            q_t = q[:, (j // 2) * LANES:(j // 2 + 1) * LANES]
            lhs.append(jnp.where(lo_q, q_t, 0.0) if j % 2 == 0 else jnp.where(lo_q, 0.0, q_t))
        qs = jnp.concatenate(lhs, axis=0).astype(BF16)
        s = lax.dot_general(qs, keys(k_dup[h], sub), (((1,), (1,)), ((), ())), preferred_element_type=F32)
        yield
        p_blocks, den_blocks = [], []
        for e in range(group):
            sb = s[e * ATT_BLOCK:(e + 1) * ATT_BLOCK]
            sk = sink_ref[h * group + e] * LOG2E
            s_band = sb[:, :n_band] + bias
            s_ctx = sb[:, n_band:]
            mx = jnp.maximum(jnp.maximum(jnp.max(s_band, axis=-1, keepdims=True),
                                         jnp.max(s_ctx, axis=-1, keepdims=True)), sk)
            p_band = jnp.exp2(s_band - mx)
            p_ctx = jnp.exp2(s_ctx - mx)
            den_blocks.append(jnp.sum(p_band, axis=-1, keepdims=True) + jnp.sum(p_ctx, axis=-1, keepdims=True)
                              + jnp.exp2(sk - mx))
            p_blocks.append(jnp.concatenate([p_band, p_ctx], axis=1).astype(BF16))
            yield
        p = jnp.concatenate(p_blocks, axis=0)
        res = (jnp.dot(p, keys(v_dup[h], sub), preferred_element_type=F32)
               / jnp.concatenate(den_blocks, axis=0))
        for t in range(group // 2):
            even = res[(2 * t) * ATT_BLOCK:(2 * t + 1) * ATT_BLOCK]
            odd = res[(2 * t + 1) * ATT_BLOCK:(2 * t + 2) * ATT_BLOCK]
            outs[sub][h * (group // 2) + t] = jnp.where(lo_q, even, odd)
        yield

    for sub in range(n_sub):
        bias = band_bias(sub)
        gens = [stream(sub, h, bias) for h in range(ATT_KV_HEADS)]
        for g in gens:
            next(g)
        for g in gens:
            for _ in g:
                pass
    for sub in range(n_sub):
        rows = slice(sub * ATT_BLOCK, (sub + 1) * ATT_BLOCK)
        o = jnp.concatenate(outs[sub], axis=1)
        o_ref[0, rows, :] = (o * _silu(z_ref[0, rows, :])).astype(o_ref.dtype)


def _attention(p, sink):
    b, s, _ = p.shape
    nb = s // ATT_BLOCK
    n_sub = TILE // ATT_BLOCK
    kvcol = C_ATT_K // (2 * LANES)
    kv_spec = lambda f: pl.BlockSpec((1, ATT_BLOCK, 2 * LANES), f)
    return pl.pallas_call(
        functools.partial(_att_kernel, n_blocks=nb),
        grid=(b, s // TILE),
        in_specs=[pl.BlockSpec(memory_space=pltpu.SMEM),
                  pl.BlockSpec((1, TILE, MIX), lambda bi, i: (bi, i, C_ATT_Q // MIX)),
                  pl.BlockSpec((1, TILE, MIX), lambda bi, i: (bi, i, C_ATT_Z // MIX)),
                  kv_spec(lambda bi, i: (bi, jnp.maximum(i * n_sub - 1, n_sub), kvcol)),
                  pl.BlockSpec((1, TILE, 2 * LANES), lambda bi, i: (bi, i, kvcol)),
                  kv_spec(lambda bi, i: (bi, jnp.minimum((i + 1) * n_sub, nb - 1), kvcol)),
                  pl.BlockSpec((1, TILE, 2 * LANES), lambda bi, i: (bi, 0, kvcol))],
        out_specs=pl.BlockSpec((1, TILE, MIX), lambda bi, i: (bi, i, 0)),
        out_shape=jax.ShapeDtypeStruct((b, s, MIX), BF16),
        compiler_params=_cparams("parallel", "parallel"),
        name="attention",
    )(sink, p, p, p, p, p, p)


def _conv4(x, hp, hn, w, prev_ok, next_ok):
    hp = hp * prev_ok
    hn = hn * next_ok
    ext = jnp.concatenate([hp, x, hn], axis=0)
    n = x.shape[0]
    own = slice(HALO, HALO + n)
    back2, back1, fwd1 = (pltpu.roll(ext, sh, 0) for sh in (2, 1, ext.shape[0] - 1))
    return w[0:1] * back2[own] + w[1:2] * back1[own] + w[2:3] * ext[own] + w[3:4] * fwd1[own]


def _halo_ok(i, n_tiles):
    prev_ok = (i >= 2).astype(F32)
    next_ok = jnp.logical_and(i != 0, i != n_tiles - 1).astype(F32)
    return prev_ok, next_ok


def _chunk_cumsum(g, reverse):
    n = g.shape[0]
    row = lax.broadcasted_iota(jnp.int32, g.shape, 0) % GDN_CHUNK
    s = 1
    while s < GDN_CHUNK:
        if reverse:
            g = g + jnp.where(row < GDN_CHUNK - s, pltpu.roll(g, n - s, 0), 0.0)
        else:
            g = g + jnp.where(row >= s, pltpu.roll(g, s, 0), 0.0)
        s *= 2
    return g


def _spread(cols, masks):
    out = cols[-1]
    for m, col in zip(masks[-2::-1], cols[-2::-1]):
        out = jnp.where(m, col, out)
    return out


def _head_block_diag(x, masks):
    return jnp.concatenate([jnp.where(m, x, 0.0) for m in masks], axis=0).astype(BF16)


def _gdn_prep_kernel(q_ref, qp_ref, qn_ref, k_ref, kp_ref, kn_ref, v_ref, vp_ref, vn_ref,
                     b_ref, a_ref, cw_ref, alog_ref, dtb_ref,
                     w_ref, u_ref, kd_ref, qg_ref, aqk_ref, gl_ref, *, n_tiles):
    i = pl.program_id(1)
    prev_ok, next_ok = _halo_ok(i, n_tiles)
    hd, ch, nh = GDN_HEAD_DIM, GDN_CHUNK, GDN_HEADS
    nch = TILE // ch

    def prep(x_ref, p_ref, n_ref, col):
        return _silu(_conv4(x_ref[0], p_ref[0], n_ref[0], cw_ref[:, col:col + MIX], prev_ok, next_ok))

    def l2n(t):
        parts = []
        for h in range(nh):
            th = t[:, h * hd:(h + 1) * hd]
            parts.append(th * lax.rsqrt(jnp.sum(th * th, axis=-1, keepdims=True) + EPS))
        return jnp.concatenate(parts, axis=1)

    q = l2n(prep(q_ref, qp_ref, qn_ref, 0)) * (hd ** -0.5)
    k = l2n(prep(k_ref, kp_ref, kn_ref, MIX))
    v = prep(v_ref, vp_ref, vn_ref, 2 * MIX)

    lane = lax.broadcasted_iota(jnp.int32, (TILE, LANES), 1)
    g = -jnp.exp(alog_ref[...]) * _softplus(a_ref[0] + dtb_ref[...])
    gcum = jnp.where(lane < nh, _chunk_cumsum(g, False), _chunk_cumsum(g, True))
    beta = _sigmoid(b_ref[0])

    ii = lax.broadcasted_iota(jnp.int32, (ch, PACK), 0)
    ll = lax.broadcasted_iota(jnp.int32, (ch, PACK), 1)
    jj = ll % ch
    eye_p = jj == ii
    eye_f = eye_p.astype(F32)
    bd16 = (jj // 16) == (ii // 16)
    bd32 = (jj // 32) == (ii // 32)
    merge_masks = (jnp.logical_and(bd32, jnp.logical_not(bd16)), jnp.logical_not(bd32))
    strict = [jj < ii, jj > ii]
    incl = [jj <= ii, jj >= ii]
    head_p = [(ll // ch) == h for h in range(nh)]
    lw = lax.broadcasted_iota(jnp.int32, (ch, MIX), 1) // hd
    head_w = [lw == h for h in range(nh)]
    lane_row = lax.broadcasted_iota(jnp.int32, (1, LANES), 1)

    def pdot(a, b):
        return jnp.dot(a.astype(BF16), _head_block_diag(b, head_p), preferred_element_type=F32)

    gl_rows = []
    units = []
    for cidx in range(nch):
        rows = slice(cidx * ch, (cidx + 1) * ch)
        kc, qc, vc, gc, bc = k[rows], q[rows], v[rows], gcum[rows], beta[rows]
        kq = _dot_nt(jnp.concatenate([kc, qc], axis=0), _head_block_diag(kc, head_w))
        g_end = jnp.where(lane_row < nh, gc[ch - 1:ch, :], gc[0:1, :])
        gl_rows.append(jnp.exp(g_end))
        for d in range(N_DIR):
            gcols = [gc[:, d * nh + h:d * nh + h + 1] for h in range(nh)]
            bcols = [bc[:, d * nh + h:d * nh + h + 1] for h in range(nh)]
            gcol_p = _spread(gcols, head_p)
            grow_p = jnp.sum(jnp.where(eye_p, gcol_p, 0.0), axis=0, keepdims=True)
            dec = jnp.exp(jnp.where(incl[d], gcol_p - grow_p, 0.0))
            lm = jnp.where(strict[d], _spread(bcols, head_p) * dec * kq[:ch], 0.0)
            aqk = jnp.where(incl[d], dec * kq[ch:], 0.0)
            aqk_ref[0, rows, d * PACK:(d + 1) * PACK] = aqk.astype(BF16)
            units.append((rows, d, lm, kc, qc, vc, gcols, bcols,
                          [g_end[:, d * nh + h:d * nh + h + 1] for h in range(nh)]))
    gl_rows.append(jnp.zeros((8 - nch, LANES), F32))
    gl_ref[0, 0] = jnp.concatenate(gl_rows, axis=0)

    lms = [u[2] for u in units]
    m = [jnp.where(bd16, -lm, 0.0) for lm in lms]
    x = [eye_f + mi for mi in m]
    pw = [pdot(mi, mi) for mi in m]
    for _ in range(2):
        r = [pdot(jnp.concatenate([xi, pi], axis=0), pi) for xi, pi in zip(x, pw)]
        x = [xi + ri[:ch] for xi, ri in zip(x, r)]
        pw = [ri[ch:] for ri in r]
    x = [xi + pdot(xi, pi) for xi, pi in zip(x, pw)]
    for mask in merge_masks:
        y = [pdot(xi, jnp.where(mask, lm, 0.0)) for xi, lm in zip(x, lms)]
        x = [xi - pdot(yi, xi) for xi, yi in zip(x, y)]

    for tinv, (rows, d, _, kc, qc, vc, gcols, bcols, gend_cols) in zip(x, units):
        slab = lambda t, h: t[:, h * hd:(h + 1) * hd]
        gam = [jnp.exp(gcol) for gcol in gcols]
        bgk = jnp.concatenate([(bcols[h] * gam[h]) * slab(kc, h) for h in range(nh)], axis=1)
        bv = jnp.concatenate([bcols[h] * slab(vc, h) for h in range(nh)], axis=1)
        rhs = jnp.concatenate([_head_block_diag(bgk, head_w), _head_block_diag(bv, head_w)], axis=1)
        wu = jnp.dot(tinv.astype(BF16), rhs, preferred_element_type=F32)
        cols = slice(d * MIX, (d + 1) * MIX)
        w_ref[0, rows, cols] = wu[:, :MIX].astype(BF16)
        u_ref[0, rows, cols] = wu[:, MIX:]
        kd_ref[0, rows, cols] = jnp.concatenate(
            [jnp.exp(gend_cols[h] - gcols[h]) * slab(kc, h) for h in range(nh)], axis=1).astype(BF16)
        qg_ref[0, rows, cols] = jnp.concatenate([gam[h] * slab(qc, h) for h in range(nh)], axis=1).astype(BF16)


def _gdn_prep(p, conv_w, alog, dtb):
    b, s, _ = p.shape
    nt = s // TILE
    hpt = TILE // HALO
    nh = s // HALO
    wide = N_DIR * GDN_HEADS * GDN_HEAD_DIM

    def tile_spec(col):
        return pl.BlockSpec((1, TILE, MIX), lambda bi, i: (bi, i, col // MIX))

    def prev_spec(col):
        return pl.BlockSpec((1, HALO, MIX), lambda bi, i: (bi, jnp.maximum(i * hpt - 1, 0), col // MIX))

    def next_spec(col):
        return pl.BlockSpec((1, HALO, MIX), lambda bi, i: (bi, jnp.minimum((i + 1) * hpt, nh - 1), col // MIX))

    in_specs = []
    for col in (C_GDN_Q, C_GDN_K, C_GDN_V):
        in_specs += [tile_spec(col), prev_spec(col), next_spec(col)]
    in_specs += [pl.BlockSpec((1, TILE, LANES), lambda bi, i: (bi, i, C_GDN_B // LANES)),
                 pl.BlockSpec((1, TILE, LANES), lambda bi, i: (bi, i, C_GDN_A // LANES)),
                 pl.BlockSpec((4, 3 * MIX), lambda bi, i: (0, 0)),
                 pl.BlockSpec((1, LANES), lambda bi, i: (0, 0)),
                 pl.BlockSpec((1, LANES), lambda bi, i: (0, 0))]
    wide_spec = pl.BlockSpec((1, TILE, wide), lambda bi, i: (bi, i, 0))
    out_specs = [wide_spec, wide_spec, wide_spec, wide_spec,
                 pl.BlockSpec((1, TILE, N_DIR * PACK), lambda bi, i: (bi, i, 0)),
                 pl.BlockSpec((1, 1, 8, LANES), lambda bi, i: (bi, i, 0, 0))]
    out_shape = [jax.ShapeDtypeStruct((b, s, wide), BF16),
                 jax.ShapeDtypeStruct((b, s, wide), F32),
                 jax.ShapeDtypeStruct((b, s, wide), BF16),
                 jax.ShapeDtypeStruct((b, s, wide), BF16),
                 jax.ShapeDtypeStruct((b, s, N_DIR * PACK), BF16),
                 jax.ShapeDtypeStruct((b, nt, 8, LANES), F32)]
    return pl.pallas_call(
        functools.partial(_gdn_prep_kernel, n_tiles=nt),
        grid=(b, nt),
        in_specs=in_specs,
        out_specs=out_specs,
        out_shape=out_shape,
        compiler_params=_cparams("parallel", "parallel"),
        name="gdn_prep",
    )(p, p, p, p, p, p, p, p, p, p, p, conv_w, alog, dtb)


def _gdn_scan_body(wf, uf, kdf, qgf, af, glf, wb, ub, kdb, qgb, ab, glb, of_ref, ob_ref, state):
    hd, ch, nh = GDN_HEAD_DIM, GDN_CHUNK, GDN_HEADS
    nch = TILE // ch
    lw = lax.broadcasted_iota(jnp.int32, (ch, MIX), 1) // hd
    head_w = [lw == h for h in range(nh)]
    dirs = ((wf, uf, kdf, qgf, af, glf, of_ref), (wb, ub, kdb, qgb, ab, glb, ob_ref))
    for step in range(nch):
        cidx = [step, nch - 1 - step]
        rows = [slice(c * ch, (c + 1) * ch) for c in cidx]
        s_old = {}
        ws = {}
        for d, (w_r, _, _, qg_r, _, _, _) in enumerate(dirs):
            for h in range(nh):
                cols = slice(h * hd, (h + 1) * hd)
                s_old[d, h] = state[d, h]
                lhs = jnp.concatenate([w_r[0, rows[d], cols], qg_r[0, rows[d], cols]], axis=0)
                ws[d, h] = jnp.dot(lhs, s_old[d, h].astype(BF16), preferred_element_type=F32)
            yield
        for d, (_, u_r, kd_r, _, a_r, gl_r, o_r) in enumerate(dirs):
            un = u_r[0, rows[d], :] - jnp.concatenate([ws[d, h][:ch] for h in range(nh)], axis=1)
            inter = jnp.concatenate([ws[d, h][ch:] for h in range(nh)], axis=1)
            intra = jnp.dot(a_r[0, rows[d], :], _head_block_diag(un, head_w), preferred_element_type=F32)
            o_r[0, rows[d], :] = (inter + intra).astype(o_r.dtype)
            unb = un.astype(BF16)
            for h in range(nh):
                cols = slice(h * hd, (h + 1) * hd)
                gl = gl_r[0, 0, cidx[d]:cidx[d] + 1, d * nh + h:d * nh + h + 1]
                state[d, h] = gl * s_old[d, h] + _dot_tn(kd_r[0, rows[d], cols], unb[:, cols])
            yield


def _mirror(j, nt):
    return jnp.where(j == 0, 0, nt - j)


def _lru_body(xf_ref, xfp_ref, xfn_ref, xb_ref, xbp_ref, xbn_ref, cw_ref, cb_ref, wg_ref, bg_ref, lam_ref,
              hf_ref, hb_ref, carry, *, n_tiles):
    j = pl.program_id(1)
    seg_n = 8
    steps = TILE // seg_n
    sub = lax.broadcasted_iota(jnp.int32, (seg_n, LRU_WIDTH), 0)

    def one_dir(d, x_ref, p_ref, n_ref, tile_idx, o_ref):
        prev_ok, next_ok = _halo_ok(tile_idx, n_tiles)
        xt = jnp.swapaxes(x_ref[0].reshape(seg_n, steps, LRU_WIDTH), 0, 1)
        xs = [xt[st] for st in range(steps)]
        hp = p_ref[0] * prev_ok
        hn = n_ref[0] * next_ok
        p1 = jnp.where(sub == 0, hp[7:8], pltpu.roll(xs[steps - 1], 1, 0))
        p2 = jnp.where(sub == 0, hp[6:7], pltpu.roll(xs[steps - 2], 1, 0))
        n1 = jnp.where(sub == seg_n - 1, hn[0:1], pltpu.roll(xs[0], seg_n - 1, 0))
        ext = [p2, p1] + xs + [n1]
        cw = cw_ref[...]
        xh = jnp.concatenate(
            [cw[0:1] * ext[st] + cw[1:2] * ext[st + 1] + cw[2:3] * ext[st + 2] + cw[3:4] * ext[st + 3]
             for st in range(steps)], axis=0) + cb_ref[...]
        yield
        gates = jnp.dot(xh.astype(BF16), wg_ref[:, d * 2 * LRU_WIDTH:(d + 1) * 2 * LRU_WIDTH],
                        preferred_element_type=F32)
        yield
        c_half = (-0.5 * LRU_C) * _softplus(-lam_ref[d:d + 1, :])
        a_blocks, b_blocks = [], []
        blk = TILE // 4
        for r0 in range(0, TILE, blk):
            gb = gates[r0:r0 + blk]
            tr = jnp.tanh(gb[:, :LRU_WIDTH] + bg_ref[2 * d:2 * d + 1, :])
            ti = jnp.tanh(gb[:, LRU_WIDTH:] + bg_ref[2 * d + 1:2 * d + 2, :])
            log_a = c_half * tr + c_half
            ab = jnp.exp(log_a)
            b_blocks.append(jnp.sqrt(-jnp.tanh(log_a) * (ab * ab + 1.0)) * ((ti + 1.0) * xh[r0:r0 + blk]))
            a_blocks.append(ab)
            yield
        a = jnp.concatenate(a_blocks, axis=0)
        bb = jnp.concatenate(b_blocks, axis=0)

        order = list(range(steps)) if d == 0 else list(range(steps - 1, -1, -1))
        piece = lambda t, st: t[st * seg_n:(st + 1) * seg_n]
        a_tot = piece(a, order[0])
        b_tot = piece(bb, order[0])
        for n, st in enumerate(order[1:]):
            b_tot = piece(a, st) * b_tot + piece(bb, st)
            a_tot = piece(a, st) * a_tot
            if n % 8 == 7:
                yield
        s = 1
        while s < seg_n:
            if d == 0:
                keep = sub >= s
                a_sh = jnp.where(keep, pltpu.roll(a_tot, s, 0), 1.0)
                b_sh = jnp.where(keep, pltpu.roll(b_tot, s, 0), 0.0)
            else:
                keep = sub < seg_n - s
                a_sh = jnp.where(keep, pltpu.roll(a_tot, seg_n - s, 0), 1.0)
                b_sh = jnp.where(keep, pltpu.roll(b_tot, seg_n - s, 0), 0.0)
            b_tot = a_tot * b_sh + b_tot
            a_tot = a_tot * a_sh
            s *= 2
        h_in0 = carry[d]
        h_out = a_tot * h_in0 + b_tot
        if d == 0:
            h = jnp.where(sub == 0, h_in0, pltpu.roll(h_out, 1, 0))
            carry[d] = h_out[seg_n - 1:seg_n]
        else:
            h = jnp.where(sub == seg_n - 1, h_in0, pltpu.roll(h_out, seg_n - 1, 0))
            carry[d] = h_out[0:1]
        hs = [None] * steps
        for n, st in enumerate(order):
            h = piece(a, st) * h + piece(bb, st)
            hs[st] = h
            if n % 8 == 7:
                yield
        o_ref[0] = jnp.swapaxes(jnp.stack(hs, axis=0), 0, 1).reshape(TILE, LRU_WIDTH).astype(o_ref.dtype)

    return [one_dir(0, xf_ref, xfp_ref, xfn_ref, j, hf_ref),
            one_dir(1, xb_ref, xbp_ref, xbn_ref, _mirror(j, n_tiles), hb_ref)]


N_SCAN_IN = 12
N_LRU_IN = 11


def _round_robin(gens, weights):
    live = list(zip(gens, weights))
    while live:
        nxt = []
        for g, w in live:
            alive = True
            for _ in range(w):
                try:
                    next(g)
                except StopIteration:
                    alive = False
                    break
            if alive:
                nxt.append((g, w))
        live = nxt


def _sweep_kernel(*refs, n_tiles):
    scan_in, lru_in = refs[:N_SCAN_IN], refs[N_SCAN_IN:N_SCAN_IN + N_LRU_IN]
    of_ref, ob_ref, hf_ref, hb_ref, state, carry = refs[N_SCAN_IN + N_LRU_IN:]

    @pl.when(pl.program_id(1) == 0)
    def _():
        state[...] = jnp.zeros_like(state)
        carry[...] = jnp.zeros_like(carry)

    _round_robin([_gdn_scan_body(*scan_in, of_ref, ob_ref, state)]
                 + _lru_body(*lru_in, hf_ref, hb_ref, carry, n_tiles=n_tiles), (1, 2, 2))


def _sweep(w, u, kd, qg, aqk, gl, p, conv_w, conv_b, wg, bg, lam):
    b, s, _ = u.shape
    nt = s // TILE
    half = GDN_HEADS * GDN_HEAD_DIM
    hpt = TILE // HALO
    nh = s // HALO
    col = C_LRU_X // MIX
    tile_of = (lambda j: j, lambda j: _mirror(j, nt))

    def scan_specs(d):
        idx = lambda bi, j: (bi, tile_of[d](j), d)
        gidx = lambda bi, j: (bi, tile_of[d](j), 0, 0)
        return [pl.BlockSpec((1, TILE, half), idx)] * 4 + [pl.BlockSpec((1, TILE, PACK), idx),
                                                           pl.BlockSpec((1, 1, 8, LANES), gidx)]

    def lru_specs(d):
        t = tile_of[d]
        return [pl.BlockSpec((1, TILE, MIX), lambda bi, j: (bi, t(j), col)),
                pl.BlockSpec((1, HALO, MIX), lambda bi, j: (bi, jnp.maximum(t(j) * hpt - 1, 0), col)),
                pl.BlockSpec((1, HALO, MIX), lambda bi, j: (bi, jnp.minimum((t(j) + 1) * hpt, nh - 1), col))]

    full = lambda shape: pl.BlockSpec(shape, lambda bi, j: tuple(0 for _ in shape))
    out_spec = lambda d: pl.BlockSpec((1, TILE, MIX), lambda bi, j: (bi, tile_of[d](j), 0))
    in_specs = (scan_specs(0) + scan_specs(1) + lru_specs(0) + lru_specs(1)
                + [full((4, LRU_WIDTH)), full((1, LRU_WIDTH)), full((LRU_WIDTH, 4 * LRU_WIDTH)),
                   full((4, LRU_WIDTH)), full((N_DIR, LRU_WIDTH))])
    assert len(in_specs) == N_SCAN_IN + N_LRU_IN
    return pl.pallas_call(
        functools.partial(_sweep_kernel, n_tiles=nt),
        grid=(b, nt),
        in_specs=in_specs,
        out_specs=[out_spec(0), out_spec(1), out_spec(0), out_spec(1)],
        out_shape=[jax.ShapeDtypeStruct((b, s, MIX), BF16)] * 4,
        scratch_shapes=[pltpu.VMEM((N_DIR, GDN_HEADS, GDN_HEAD_DIM, GDN_HEAD_DIM), F32),
                        pltpu.VMEM((N_DIR, 1, LRU_WIDTH), F32)],
        compiler_params=_cparams("parallel", "arbitrary"),
        name="sweep",
    )(w, u, kd, qg, aqk, gl, w, u, kd, qg, aqk, gl, p, p, p, p, p, p, conv_w, conv_b, wg, bg, lam)


def _outproj_kernel(c_ref, x_ref, mod_ref, att_ref, of_ref, ob_ref, gz_ref, hf_ref, hb_ref, lz_ref,
                    gn_ref, w_ref, fg_ref, o_ref, *, final, first_tile):
    hd = GDN_HEAD_DIM
    o = of_ref[0].astype(F32) + ob_ref[0].astype(F32)
    parts = []
    for h in range(GDN_HEADS):
        oh = o[:, h * hd:(h + 1) * hd]
        parts.append(oh * lax.rsqrt(jnp.mean(oh * oh, axis=-1, keepdims=True) + EPS) * gn_ref[...])
    gdn = jnp.concatenate(parts, axis=1) * _silu(gz_ref[0])
    lru = (hf_ref[0].astype(F32) + hb_ref[0].astype(F32)) * _silu(lz_ref[0])
    acc = (jnp.dot(att_ref[0], w_ref[0:MIX, :], preferred_element_type=F32)
           + jnp.dot(gdn.astype(BF16), w_ref[MIX:2 * MIX, :], preferred_element_type=F32)
           + jnp.dot(lru.astype(BF16), w_ref[2 * MIX:3 * MIX, :], preferred_element_type=F32))
    gate = mod_ref[0, 0, 2:3, :]
    xn = _stream_tile(c_ref, x_ref, pl.program_id(1) + first_tile) + gate * acc
    if final:
        ms = jnp.mean(xn * xn, axis=-1, keepdims=True)
        xn = xn * lax.rsqrt(ms + EPS) * fg_ref[...]
    o_ref[0] = xn


def _outproj(c_src, x_src, src_off, modsel, att, of, ob, p, hf, hb, gdn_norm, w_out, final_g, final):
    b, s, _ = p.shape
    d = x_src.shape[2]
    nt = s // TILE
    off = 1 if final else 0
    n_out = nt - off
    tok = lambda col: pl.BlockSpec((1, TILE, MIX), lambda bi, i: (bi, i + off, col))
    return pl.pallas_call(
        functools.partial(_outproj_kernel, final=final, first_tile=off),
        grid=(b, n_out),
        in_specs=_stream_specs(d, src_off, off) + [
                  pl.BlockSpec((1, 1, 8, d), lambda bi, i: (bi, jnp.minimum(i + off, 1), 0, 0)),
                  tok(0), tok(0), tok(0), tok(C_GDN_Z // MIX), tok(0), tok(0), tok(C_LRU_Z // MIX),
                  pl.BlockSpec((1, GDN_HEAD_DIM), lambda bi, i: (0, 0)),
                  pl.BlockSpec((3 * MIX, d), lambda bi, i: (0, 0)),
                  pl.BlockSpec((1, d), lambda bi, i: (0, 0))],
        out_specs=pl.BlockSpec((1, TILE, d), lambda bi, i: (bi, i, 0)),
        out_shape=jax.ShapeDtypeStruct((b, n_out * TILE, d), F32),
        compiler_params=_cparams("parallel", "parallel"),
        name="outproj",
    )(c_src, x_src, modsel, att, of, ob, p, hf, hb, p, gdn_norm, w_out, final_g)


def _rope_tables(s, ctx_len):
    t = np.arange(s - ctx_len)
    n_freq = ATT_HEAD_DIM // 4
    inv = ROPE_BASE ** (-np.arange(n_freq, dtype=np.float64) / n_freq)
    ang = [(t // GRID_W)[:, None] * inv, (t % GRID_W)[:, None] * inv]
    cos = np.concatenate([np.cos(ang[0]), np.cos(ang[0]), np.cos(ang[1]), np.cos(ang[1])], axis=1)
    sin = np.concatenate([-np.sin(ang[0]), np.sin(ang[0]), -np.sin(ang[1]), np.sin(ang[1])], axis=1)
    cos = np.concatenate([np.ones((ctx_len, ATT_HEAD_DIM)), cos], axis=0)
    sin = np.concatenate([np.zeros((ctx_len, ATT_HEAD_DIM)), sin], axis=0)
    return (jnp.asarray(np.tile(cos, (1, 2)), F32), jnp.asarray(np.tile(sin, (1, 2)), F32))


def _permute_w_in(w):
    widths = (512, 128, 128, 512, 512, 512, 512, 8, 8, 512, 512, 512)
    offs = np.concatenate([[0], np.cumsum(widths)])
    names = ("att_q", "att_k", "att_v", "att_z", "gdn_q", "gdn_k", "gdn_v", "gdn_b", "gdn_a", "gdn_z",
             "lru_x", "lru_z")
    col = {n: w[:, offs[i]:offs[i + 1]] for i, n in enumerate(names)}
    pad = jnp.zeros((w.shape[0], LANES - 8), w.dtype)
    return jnp.concatenate(
        [col["att_q"], col["att_z"], col["gdn_q"], col["gdn_k"], col["gdn_v"], col["gdn_z"], col["lru_x"],
         col["lru_z"], col["att_k"], col["att_v"], col["gdn_b"], pad, col["gdn_a"], pad], axis=1)


def _block_diag(w):
    eye = jnp.eye(LRU_BLOCKS, dtype=w.dtype)
    return jnp.einsum("nde,nm->ndme", w, eye).reshape(LRU_WIDTH, LRU_WIDTH)


def _pad_lanes(v):
    return jnp.pad(v.reshape(1, -1), ((0, 0), (0, LANES - v.size)))


def kernel(x, c, ctx, c_ctx, norm_g, w_mod, b_mod, w_in, att_sink, gdn_conv, gdn_a_log, gdn_dt_bias, gdn_norm,
           lru_conv_w, lru_conv_b, lru_w_r, lru_b_r, lru_w_i, lru_b_i, lru_lambda, w_out, final_g):
    b, t, d = x.shape
    ctx_len = ctx.shape[1]
    depth = w_in.shape[0]
    assert ctx_len == TILE and t % TILE == 0 and d == D_MODEL and b == 2
    s = ctx_len + t
    nt = s // TILE

    cos_t, sin_t = _rope_tables(s, ctx_len)
    ct = jnp.pad(jnp.concatenate([c, c_ctx[None]], axis=0).T, ((0, 0), (0, LANES - b - 1)))
    mods = _modulation(ct, w_mod, b_mod)

    c_src, x_src, src_off = ctx, x, 1
    out = None
    for l in range(depth):
        m3 = mods[l, :3].reshape(3, 3, d)
        sel = jnp.stack([jnp.stack([m3[b], m3[bi]], axis=0) for bi in range(b)], axis=0)
        modsel = jnp.pad(sel, ((0, 0), (0, 0), (0, 5), (0, 0)))
        w_perm = _permute_w_in(w_in[l].astype(BF16))
        p = _inproj(c_src, x_src, src_off, modsel, norm_g[l].reshape(1, d), cos_t, sin_t, w_perm)

        att = _attention(p, att_sink[l])

        prep = _gdn_prep(p, gdn_conv[l], _pad_lanes(gdn_a_log[l]), _pad_lanes(gdn_dt_bias[l]))

        wg = jnp.concatenate([_block_diag(lru_w_r[l, 0]), _block_diag(lru_w_i[l, 0]),
                              _block_diag(lru_w_r[l, 1]), _block_diag(lru_w_i[l, 1])], axis=1).astype(BF16)
        bg = 0.5 * jnp.stack([lru_b_r[l, 0], lru_b_i[l, 0], lru_b_r[l, 1], lru_b_i[l, 1]], axis=0)
        o_f, o_b, h_f, h_b = _sweep(*prep, p, 0.5 * lru_conv_w[l], 0.5 * lru_conv_b[l].reshape(1, -1), wg, bg,
                                    lru_lambda[l])

        final = l == depth - 1
        res = _outproj(c_src, x_src, src_off, modsel, att, o_f, o_b, p, h_f, h_b, gdn_norm[l].reshape(1, -1),
                       w_out[l].astype(BF16), final_g.reshape(1, d), final)
        if final:
            out = res
        else:
            c_src, x_src, src_off = res, res, 0
    return out
```

```python
import functools
import math

import numpy as np
import jax
import jax.numpy as jnp
from jax import lax
from jax.experimental import pallas as pl
from jax.experimental.pallas import tpu as pltpu

F32 = jnp.float32
BF16 = jnp.bfloat16

EPS = 1e-6
NEG_INF = -1e30
LOG2E = math.log2(math.e)
GRID_W = 64
ROPE_BASE = 10000.0

D_MODEL = 1024
ATT_HEADS = 8
ATT_KV_HEADS = 2
ATT_HEAD_DIM = 64
ATT_BLOCK = 128
GDN_HEADS = 4
GDN_HEAD_DIM = 128
GDN_CHUNK = 64
LRU_WIDTH = 512
LRU_BLOCKS = 8
LRU_BLOCK_DIM = 64
LRU_C = 8.0
N_DIR = 2
MIX = 512
PACK = GDN_HEADS * GDN_CHUNK

TILE = 256
HALO = 8
LANES = 128
VMEM_LIMIT = 56 * 1024 * 1024

C_ATT_Q, C_ATT_Z, C_GDN_Q, C_GDN_K, C_GDN_V, C_GDN_Z, C_LRU_X, C_LRU_Z = (i * MIX for i in range(8))
C_ATT_K = 8 * MIX
C_ATT_V = C_ATT_K + LANES
C_GDN_B = C_ATT_V + LANES
C_GDN_A = C_GDN_B + LANES
N_IN = C_GDN_A + LANES


def _cparams(*sem):
    return pltpu.CompilerParams(dimension_semantics=sem, vmem_limit_bytes=VMEM_LIMIT)


def _sigmoid(x):
    return 1.0 / (1.0 + jnp.exp(-x))


def _silu(x):
    return x * _sigmoid(x)


def _softplus(x):
    return jnp.maximum(x, 0.0) + jnp.log1p(jnp.exp(-jnp.abs(x)))


def _dot_nt(a, b):
    return lax.dot_general(a.astype(BF16), b.astype(BF16), (((1,), (1,)), ((), ())),
                           preferred_element_type=F32)


def _dot_tn(a, b):
    return lax.dot_general(a.astype(BF16), b.astype(BF16), (((0,), (0,)), ((), ())),
                           preferred_element_type=F32)


def _mod_kernel(ct_ref, w_ref, b_ref, o_ref):
    s = _silu(ct_ref[...])
    w = w_ref[0]
    rid = lax.broadcasted_iota(jnp.int32, (8, w.shape[1]), 0)
    out = jnp.zeros((8, w.shape[1]), F32)
    for r in range(3):
        row = jnp.sum(w * s[:, r:r + 1], axis=0, keepdims=True) + b_ref[0]
        out = jnp.where(rid == r, row, out)
    o_ref[0] = out


def _modulation(ct, w_mod, b_mod):
    depth, d, d3 = w_mod.shape
    return pl.pallas_call(
        _mod_kernel,
        grid=(depth, d3 // d),
        in_specs=[pl.BlockSpec((d, LANES), lambda l, j: (0, 0)),
                  pl.BlockSpec((1, d, d), lambda l, j: (l, 0, j)),
                  pl.BlockSpec((1, 1, d), lambda l, j: (l, 0, j))],
        out_specs=pl.BlockSpec((1, 8, d), lambda l, j: (l, 0, j)),
        out_shape=jax.ShapeDtypeStruct((depth, 8, d3), F32),
        compiler_params=_cparams("parallel", "parallel"),
        name="modulation",
    )(ct, w_mod, b_mod.reshape(depth, 1, d3))


def _rope(x, cos, sin):
    lane = lax.broadcasted_iota(jnp.int32, x.shape, 1)
    first = (lane % 32) < 16
    partner = jnp.where(first, pltpu.roll(x, LANES - 16, 1), pltpu.roll(x, 16, 1))
    return x * cos + partner * sin


def _stream_tile(c_ref, x_ref, tile):
    is_ctx = (jnp.zeros((TILE, 1), jnp.int32) + tile) == 0
    return jnp.where(is_ctx, c_ref[0], x_ref[0])


def _stream_specs(d, off, shift=0):
    return [pl.BlockSpec((1, TILE, d), lambda bi, i: (bi, 0, 0)),
            pl.BlockSpec((1, TILE, d), lambda bi, i: (bi, jnp.maximum(i + shift - off, 1 - off), 0))]


_IN_COLS = []
_src = 0
for _dst, _w in ((C_ATT_Q, MIX), (C_ATT_K, LANES), (C_ATT_V, LANES), (C_ATT_Z, MIX), (C_GDN_Q, MIX), (C_GDN_K, MIX),
                 (C_GDN_V, MIX), (C_GDN_B, N_DIR * GDN_HEADS), (C_GDN_A, N_DIR * GDN_HEADS), (C_GDN_Z, MIX),
                 (C_LRU_X, MIX), (C_LRU_Z, MIX)):
    _IN_COLS.append((_src, _dst, _w))
    _src += _w
IN_WIDTH = _src


def _inproj_kernel(c_ref, x_ref, mod_ref, g_ref, cos_ref, sin_ref, w_ref, o_ref, w_s):
    @pl.when(jnp.logical_and(pl.program_id(0) == 0, pl.program_id(1) == 0))
    def _():
        w_s[:, C_GDN_B:N_IN] = jnp.zeros((w_s.shape[0], N_IN - C_GDN_B), BF16)
        for r0 in range(0, w_s.shape[0], TILE):
            for src, dst, width in _IN_COLS:
                w_s[r0:r0 + TILE, dst:dst + width] = w_ref[0, r0:r0 + TILE, src:src + width].astype(BF16)

    x = _stream_tile(c_ref, x_ref, pl.program_id(1))
    ms = jnp.mean(x * x, axis=-1, keepdims=True)
    y = x * lax.rsqrt(ms + EPS) * g_ref[...]
    shift = mod_ref[0, 0, 0:1, :]
    scale = mod_ref[0, 0, 1:2, :]
    hb = (y * (1.0 + scale) + shift).astype(BF16)
    cos = cos_ref[...]
    sin = sin_ref[...]
    for c0 in range(0, N_IN, MIX):
        res = jnp.dot(hb, w_s[:, c0:c0 + MIX], preferred_element_type=F32)
        if c0 == C_ATT_Q:
            res = jnp.concatenate(
                [_rope(res[:, k:k + LANES], cos, sin) for k in range(0, MIX, LANES)], axis=1)
        if c0 == C_ATT_K:
            res = jnp.concatenate([_rope(res[:, :LANES], cos, sin), res[:, LANES:]], axis=1)
        o_ref[0, :, c0:c0 + MIX] = res


def _inproj(c_src, x_src, off, modsel, g, cos_t, sin_t, w_in, layer):
    b, _, d = x_src.shape
    s = x_src.shape[1] + off * TILE
    nt = s // TILE
    return pl.pallas_call(
        _inproj_kernel,
        grid=(b, nt),
        in_specs=_stream_specs(d, off) + [
                  pl.BlockSpec((1, 1, 8, d), lambda bi, i: (bi, jnp.minimum(i, 1), 0, 0)),
                  pl.BlockSpec((1, d), lambda bi, i: (0, 0)),
                  pl.BlockSpec((TILE, LANES), lambda bi, i: (i, 0)),
                  pl.BlockSpec((TILE, LANES), lambda bi, i: (i, 0)),
                  pl.BlockSpec((1, d, IN_WIDTH), lambda bi, i: (layer, 0, 0), pipeline_mode=pl.Buffered(1))],
        out_specs=pl.BlockSpec((1, TILE, N_IN), lambda bi, i: (bi, i, 0)),
        out_shape=jax.ShapeDtypeStruct((b, s, N_IN), F32),
        scratch_shapes=[pltpu.VMEM((d, N_IN), BF16)],
        compiler_params=_cparams("arbitrary", "arbitrary"),
        name="inproj",
    )(c_src, x_src, modsel, g, cos_t, sin_t, w_in)


def _att_kernel(sink_ref, q_ref, z_ref, kvp_ref, kvc_ref, kvn_ref, kvx_ref, o_ref, *, n_blocks):
    i = pl.program_id(1)
    n_sub = TILE // ATT_BLOCK
    n_band = 3 * ATT_BLOCK
    half = ATT_HEAD_DIM
    group = ATT_HEADS // ATT_KV_HEADS

    kv = jnp.concatenate([kvp_ref[0], kvc_ref[0], kvn_ref[0], kvx_ref[0]], axis=0)
    nk = kv.shape[0]
    k_all, v_all = kv[:, :LANES], kv[:, LANES:]
    k_sw, v_sw = pltpu.roll(k_all, half, 1), pltpu.roll(v_all, half, 1)
    lo_k = lax.broadcasted_iota(jnp.int32, (nk, LANES), 1) < half
    k_dup = [jnp.where(lo_k, k_all, k_sw).astype(BF16), jnp.where(lo_k, k_sw, k_all).astype(BF16)]
    v_dup = [jnp.where(lo_k, v_all, v_sw).astype(BF16), jnp.where(lo_k, v_sw, v_all).astype(BF16)]
    n_loc = (n_sub + 2) * ATT_BLOCK

    def keys(t, sub):
        return jnp.concatenate([t[sub * ATT_BLOCK:sub * ATT_BLOCK + n_band], t[n_loc:]], axis=0)

    r = lax.broadcasted_iota(jnp.int32, (ATT_BLOCK, n_band), 0)
    c = lax.broadcasted_iota(jnp.int32, (ATT_BLOCK, n_band), 1)
    in_win = jnp.abs(r - (c - ATT_BLOCK)) <= ATT_BLOCK
    lo_q = lax.broadcasted_iota(jnp.int32, (ATT_BLOCK, LANES), 1) < half
    outs = [[None] * (ATT_HEADS // 2) for _ in range(n_sub)]

    def band_bias(sub):
        m = i * n_sub + sub
        c_lo = jnp.where(m >= n_sub + 1, 0, jnp.where(m >= n_sub, ATT_BLOCK, n_band))
        c_hi = jnp.where(m < n_sub, 0, jnp.where(m <= n_blocks - 2, n_band, 2 * ATT_BLOCK))
        return jnp.where(jnp.logical_and(in_win, jnp.logical_and(c >= c_lo, c < c_hi)), 0.0, NEG_INF)

    def stream(sub, h, bias):
        q = q_ref[0, sub * ATT_BLOCK:(sub + 1) * ATT_BLOCK, :] * (ATT_HEAD_DIM ** -0.5 * LOG2E)
        lhs = []
        for j in range(h * group, (h + 1) * group):
            q_t = q[:, (j // 2) * LANES:(j // 2 + 1) * LANES]
            lhs.append(jnp.where(lo_q, q_t, 0.0) if j % 2 == 0 else jnp.where(lo_q, 0.0, q_t))
        qs = jnp.concatenate(lhs, axis=0).astype(BF16)
        s = lax.dot_general(qs, keys(k_dup[h], sub), (((1,), (1,)), ((), ())), preferred_element_type=F32)
        yield
        p_blocks, den_blocks = [], []
        for e in range(group):
            sb = s[e * ATT_BLOCK:(e + 1) * ATT_BLOCK]
            sk = sink_ref[h * group + e] * LOG2E
            s_band = sb[:, :n_band] + bias
            s_ctx = sb[:, n_band:]
            mx = jnp.maximum(jnp.maximum(jnp.max(s_band, axis=-1, keepdims=True),
                                         jnp.max(s_ctx, axis=-1, keepdims=True)), sk)
            p_band = jnp.exp2(s_band - mx)
            p_ctx = jnp.exp2(s_ctx - mx)
            den_blocks.append(jnp.sum(p_band, axis=-1, keepdims=True) + jnp.sum(p_ctx, axis=-1, keepdims=True)
                              + jnp.exp2(sk - mx))
            p_blocks.append(jnp.concatenate([p_band, p_ctx], axis=1).astype(BF16))
            yield
        p = jnp.concatenate(p_blocks, axis=0)
        res = (jnp.dot(p, keys(v_dup[h], sub), preferred_element_type=F32)
               / jnp.concatenate(den_blocks, axis=0))
        for t in range(group // 2):
            even = res[(2 * t) * ATT_BLOCK:(2 * t + 1) * ATT_BLOCK]
            odd = res[(2 * t + 1) * ATT_BLOCK:(2 * t + 2) * ATT_BLOCK]
            outs[sub][h * (group // 2) + t] = jnp.where(lo_q, even, odd)
        yield

    for sub in range(n_sub):
        bias = band_bias(sub)
        gens = [stream(sub, h, bias) for h in range(ATT_KV_HEADS)]
        for g in gens:
            next(g)
        for g in gens:
            for _ in g:
                pass
    for sub in range(n_sub):
        rows = slice(sub * ATT_BLOCK, (sub + 1) * ATT_BLOCK)
        o = jnp.concatenate(outs[sub], axis=1)
        o_ref[0, rows, :] = (o * _silu(z_ref[0, rows, :])).astype(o_ref.dtype)


def _attention(p, sink):
    b, s, _ = p.shape
    nb = s // ATT_BLOCK
    n_sub = TILE // ATT_BLOCK
    kvcol = C_ATT_K // (2 * LANES)
    kv_spec = lambda f: pl.BlockSpec((1, ATT_BLOCK, 2 * LANES), f)
    return pl.pallas_call(
        functools.partial(_att_kernel, n_blocks=nb),
        grid=(b, s // TILE),
        in_specs=[pl.BlockSpec(memory_space=pltpu.SMEM),
                  pl.BlockSpec((1, TILE, MIX), lambda bi, i: (bi, i, C_ATT_Q // MIX)),
                  pl.BlockSpec((1, TILE, MIX), lambda bi, i: (bi, i, C_ATT_Z // MIX)),
                  kv_spec(lambda bi, i: (bi, jnp.maximum(i * n_sub - 1, n_sub), kvcol)),
                  pl.BlockSpec((1, TILE, 2 * LANES), lambda bi, i: (bi, i, kvcol)),
                  kv_spec(lambda bi, i: (bi, jnp.minimum((i + 1) * n_sub, nb - 1), kvcol)),
                  pl.BlockSpec((1, TILE, 2 * LANES), lambda bi, i: (bi, 0, kvcol))],
        out_specs=pl.BlockSpec((1, TILE, MIX), lambda bi, i: (bi, i, 0)),
        out_shape=jax.ShapeDtypeStruct((b, s, MIX), BF16),
        compiler_params=_cparams("parallel", "parallel"),
        name="attention",
    )(sink, p, p, p, p, p, p)


def _conv4(x, hp, hn, w, prev_ok, next_ok):
    hp = hp * prev_ok
    hn = hn * next_ok
    ext = jnp.concatenate([hp, x, hn], axis=0)
    n = x.shape[0]
    own = slice(HALO, HALO + n)
    back2, back1, fwd1 = (pltpu.roll(ext, sh, 0) for sh in (2, 1, ext.shape[0] - 1))
    return w[0:1] * back2[own] + w[1:2] * back1[own] + w[2:3] * ext[own] + w[3:4] * fwd1[own]


def _halo_ok(i, n_tiles):
    prev_ok = (i >= 2).astype(F32)
    next_ok = jnp.logical_and(i != 0, i != n_tiles - 1).astype(F32)
    return prev_ok, next_ok


def _chunk_cumsum(g, reverse):
    n = g.shape[0]
    row = lax.broadcasted_iota(jnp.int32, g.shape, 0) % GDN_CHUNK
    s = 1
    while s < GDN_CHUNK:
        if reverse:
            g = g + jnp.where(row < GDN_CHUNK - s, pltpu.roll(g, n - s, 0), 0.0)
        else:
            g = g + jnp.where(row >= s, pltpu.roll(g, s, 0), 0.0)
        s *= 2
    return g


def _spread(cols, masks):
    out = cols[-1]
    for m, col in zip(masks[-2::-1], cols[-2::-1]):
        out = jnp.where(m, col, out)
    return out


def _head_block_diag(x, masks):
    return jnp.concatenate([jnp.where(m, x, 0.0) for m in masks], axis=0).astype(BF16)


def _gdn_prep_kernel(q_ref, qp_ref, qn_ref, k_ref, kp_ref, kn_ref, v_ref, vp_ref, vn_ref,
                     b_ref, a_ref, cw_ref, alog_ref, dtb_ref,
                     w_ref, u_ref, kd_ref, qg_ref, aqk_ref, gl_ref, *, n_tiles):
    i = pl.program_id(1)
    prev_ok, next_ok = _halo_ok(i, n_tiles)
    hd, ch, nh = GDN_HEAD_DIM, GDN_CHUNK, GDN_HEADS
    nch = TILE // ch

    def prep(x_ref, p_ref, n_ref, col):
        xh = _conv4(x_ref[0], p_ref[0], n_ref[0], cw_ref[:, col:col + MIX], prev_ok, next_ok)
        return xh * (1.0 + jnp.tanh(xh))

    def l2n(t):
        parts = []
        for h in range(nh):
            th = t[:, h * hd:(h + 1) * hd]
            parts.append(th * lax.rsqrt(jnp.sum(th * th, axis=-1, keepdims=True) + EPS))
        return jnp.concatenate(parts, axis=1)

    q = l2n(prep(q_ref, qp_ref, qn_ref, 0)) * (hd ** -0.5)
    k = l2n(prep(k_ref, kp_ref, kn_ref, MIX))
    v = prep(v_ref, vp_ref, vn_ref, 2 * MIX)

    lane = lax.broadcasted_iota(jnp.int32, (TILE, LANES), 1)
    g = -jnp.exp(alog_ref[...]) * _softplus(a_ref[0] + dtb_ref[...])
    gcum = jnp.where(lane < nh, _chunk_cumsum(g, False), _chunk_cumsum(g, True))
    beta = _sigmoid(b_ref[0])

    ii = lax.broadcasted_iota(jnp.int32, (ch, PACK), 0)
    ll = lax.broadcasted_iota(jnp.int32, (ch, PACK), 1)
    jj = ll % ch
    eye_p = jj == ii
    eye_f = eye_p.astype(F32)
    bd16 = (jj // 16) == (ii // 16)
    bd32 = (jj // 32) == (ii // 32)
    merge_masks = (jnp.logical_and(bd32, jnp.logical_not(bd16)), jnp.logical_not(bd32))
    strict = [jj < ii, jj > ii]
    incl = [jj <= ii, jj >= ii]
    head_p = [(ll // ch) == h for h in range(nh)]
    lw = lax.broadcasted_iota(jnp.int32, (ch, MIX), 1) // hd
    head_w = [lw == h for h in range(nh)]
    lane_row = lax.broadcasted_iota(jnp.int32, (1, LANES), 1)

    def pdot(a, b):
        return jnp.dot(a.astype(BF16), _head_block_diag(b, head_p), preferred_element_type=F32)

    gl_rows = []
    units = []
    for cidx in range(nch):
        rows = slice(cidx * ch, (cidx + 1) * ch)
        kc, qc, vc, gc, bc = k[rows], q[rows], v[rows], gcum[rows], beta[rows]
        kq = _dot_nt(jnp.concatenate([kc, qc], axis=0), _head_block_diag(kc, head_w))
        g_end = jnp.where(lane_row < nh, gc[ch - 1:ch, :], gc[0:1, :])
        gl_rows.append(jnp.exp(g_end))
        for d in range(N_DIR):
            gcols = [gc[:, d * nh + h:d * nh + h + 1] for h in range(nh)]
            bcols = [bc[:, d * nh + h:d * nh + h + 1] for h in range(nh)]
            gcol_p = _spread(gcols, head_p)
            grow_p = jnp.sum(jnp.where(eye_p, gcol_p, 0.0), axis=0, keepdims=True)
            dec = jnp.exp(jnp.where(incl[d], gcol_p - grow_p, 0.0))
            lm = jnp.where(strict[d], _spread(bcols, head_p) * dec * kq[:ch], 0.0)
            aqk = jnp.where(incl[d], dec * kq[ch:], 0.0)
            aqk_ref[0, rows, d * PACK:(d + 1) * PACK] = aqk.astype(BF16)
            units.append((rows, d, lm, kc, qc, vc, gcols, bcols,
                          [g_end[:, d * nh + h:d * nh + h + 1] for h in range(nh)]))
    gl_rows.append(jnp.zeros((8 - nch, LANES), F32))
    gl_ref[0, 0] = jnp.concatenate(gl_rows, axis=0)

    lms = [u[2] for u in units]
    m = [jnp.where(bd16, -lm, 0.0) for lm in lms]
    x = [eye_f + mi for mi in m]
    pw = [pdot(mi, mi) for mi in m]
    for _ in range(2):
        r = [pdot(jnp.concatenate([xi, pi], axis=0), pi) for xi, pi in zip(x, pw)]
        x = [xi + ri[:ch] for xi, ri in zip(x, r)]
        pw = [ri[ch:] for ri in r]
    x = [xi + pdot(xi, pi) for xi, pi in zip(x, pw)]
    for mask in merge_masks:
        y = [pdot(xi, jnp.where(mask, lm, 0.0)) for xi, lm in zip(x, lms)]
        x = [xi - pdot(yi, xi) for xi, yi in zip(x, y)]

    for tinv, (rows, d, _, kc, qc, vc, gcols, bcols, gend_cols) in zip(x, units):
        slab = lambda t, h: t[:, h * hd:(h + 1) * hd]
        gam = [jnp.exp(gcol) for gcol in gcols]
        bgk = jnp.concatenate([(bcols[h] * gam[h]) * slab(kc, h) for h in range(nh)], axis=1)
        bv = jnp.concatenate([bcols[h] * slab(vc, h) for h in range(nh)], axis=1)
        rhs = jnp.concatenate([_head_block_diag(bgk, head_w), _head_block_diag(bv, head_w)], axis=1)
        wu = jnp.dot(tinv.astype(BF16), rhs, preferred_element_type=F32)
        cols = slice(d * MIX, (d + 1) * MIX)
        w_ref[0, rows, cols] = wu[:, :MIX].astype(BF16)
        u_ref[0, rows, cols] = wu[:, MIX:]
        kd_ref[0, rows, cols] = jnp.concatenate(
            [jnp.exp(gend_cols[h] - gcols[h]) * slab(kc, h) for h in range(nh)], axis=1).astype(BF16)
        qg_ref[0, rows, cols] = jnp.concatenate([gam[h] * slab(qc, h) for h in range(nh)], axis=1).astype(BF16)


def _gdn_prep(p, conv_w, alog, dtb):
    b, s, _ = p.shape
    nt = s // TILE
    hpt = TILE // HALO
    nh = s // HALO
    wide = N_DIR * GDN_HEADS * GDN_HEAD_DIM

    def tile_spec(col):
        return pl.BlockSpec((1, TILE, MIX), lambda bi, i: (bi, i, col // MIX))

    def prev_spec(col):
        return pl.BlockSpec((1, HALO, MIX), lambda bi, i: (bi, jnp.maximum(i * hpt - 1, 0), col // MIX))

    def next_spec(col):
        return pl.BlockSpec((1, HALO, MIX), lambda bi, i: (bi, jnp.minimum((i + 1) * hpt, nh - 1), col // MIX))

    in_specs = []
    for col in (C_GDN_Q, C_GDN_K, C_GDN_V):
        in_specs += [tile_spec(col), prev_spec(col), next_spec(col)]
    in_specs += [pl.BlockSpec((1, TILE, LANES), lambda bi, i: (bi, i, C_GDN_B // LANES)),
                 pl.BlockSpec((1, TILE, LANES), lambda bi, i: (bi, i, C_GDN_A // LANES)),
                 pl.BlockSpec((4, 3 * MIX), lambda bi, i: (0, 0)),
                 pl.BlockSpec((1, LANES), lambda bi, i: (0, 0)),
                 pl.BlockSpec((1, LANES), lambda bi, i: (0, 0))]
    wide_spec = pl.BlockSpec((1, TILE, wide), lambda bi, i: (bi, i, 0))
    out_specs = [wide_spec, wide_spec, wide_spec, wide_spec,
                 pl.BlockSpec((1, TILE, N_DIR * PACK), lambda bi, i: (bi, i, 0)),
                 pl.BlockSpec((1, 1, 8, LANES), lambda bi, i: (bi, i, 0, 0))]
    out_shape = [jax.ShapeDtypeStruct((b, s, wide), BF16),
                 jax.ShapeDtypeStruct((b, s, wide), F32),
                 jax.ShapeDtypeStruct((b, s, wide), BF16),
                 jax.ShapeDtypeStruct((b, s, wide), BF16),
                 jax.ShapeDtypeStruct((b, s, N_DIR * PACK), BF16),
                 jax.ShapeDtypeStruct((b, nt, 8, LANES), F32)]
    return pl.pallas_call(
        functools.partial(_gdn_prep_kernel, n_tiles=nt),
        grid=(b, nt),
        in_specs=in_specs,
        out_specs=out_specs,
        out_shape=out_shape,
        compiler_params=_cparams("parallel", "parallel"),
        name="gdn_prep",
    )(p, p, p, p, p, p, p, p, p, p, p, conv_w, alog, dtb)


def _gdn_scan_body(wf, uf, kdf, qgf, af, glf, wb, ub, kdb, qgb, ab, glb, of_ref, ob_ref, state):
    hd, ch, nh = GDN_HEAD_DIM, GDN_CHUNK, GDN_HEADS
    nch = TILE // ch
    lw = lax.broadcasted_iota(jnp.int32, (ch, MIX), 1) // hd
    head_w = [lw == h for h in range(nh)]
    dirs = ((wf, uf, kdf, qgf, af, glf, of_ref), (wb, ub, kdb, qgb, ab, glb, ob_ref))
    for step in range(nch):
        cidx = [step, nch - 1 - step]
        rows = [slice(c * ch, (c + 1) * ch) for c in cidx]
        s_old = {}
        ws = {}
        for d, (w_r, _, _, qg_r, _, _, _) in enumerate(dirs):
            for h in range(nh):
                cols = slice(h * hd, (h + 1) * hd)
                s_old[d, h] = state[d, h]
                lhs = jnp.concatenate([w_r[0, rows[d], cols], qg_r[0, rows[d], cols]], axis=0)
                ws[d, h] = jnp.dot(lhs, s_old[d, h].astype(BF16), preferred_element_type=F32)
            yield
        for d, (_, u_r, kd_r, _, a_r, gl_r, o_r) in enumerate(dirs):
            un = u_r[0, rows[d], :] - jnp.concatenate([ws[d, h][:ch] for h in range(nh)], axis=1)
            inter = jnp.concatenate([ws[d, h][ch:] for h in range(nh)], axis=1)
            intra = jnp.dot(a_r[0, rows[d], :], _head_block_diag(un, head_w), preferred_element_type=F32)
            o_r[0, rows[d], :] = (inter + intra).astype(o_r.dtype)
            unb = un.astype(BF16)
            for h in range(nh):
                cols = slice(h * hd, (h + 1) * hd)
                gl = gl_r[0, 0, cidx[d]:cidx[d] + 1, d * nh + h:d * nh + h + 1]
                state[d, h] = gl * s_old[d, h] + _dot_tn(kd_r[0, rows[d], cols], unb[:, cols])
            yield


def _mirror(j, nt):
    return jnp.where(j == 0, 0, nt - j)


def _lru_body(xf_ref, xfp_ref, xfn_ref, xb_ref, xbp_ref, xbn_ref, cw_ref, cb_ref, wg_ref, bg_ref, lam_ref,
              hf_ref, hb_ref, carry, *, n_tiles):
    j = pl.program_id(1)
    seg_n = 8
    steps = TILE // seg_n
    sub = lax.broadcasted_iota(jnp.int32, (seg_n, LRU_WIDTH), 0)

    def one_dir(d, x_ref, p_ref, n_ref, tile_idx, o_ref):
        prev_ok, next_ok = _halo_ok(tile_idx, n_tiles)
        xt = jnp.swapaxes(x_ref[0].reshape(seg_n, steps, LRU_WIDTH), 0, 1)
        xs = [xt[st] for st in range(steps)]
        hp = p_ref[0] * prev_ok
        hn = n_ref[0] * next_ok
        p1 = jnp.where(sub == 0, hp[7:8], pltpu.roll(xs[steps - 1], 1, 0))
        p2 = jnp.where(sub == 0, hp[6:7], pltpu.roll(xs[steps - 2], 1, 0))
        n1 = jnp.where(sub == seg_n - 1, hn[0:1], pltpu.roll(xs[0], seg_n - 1, 0))
        ext = [p2, p1] + xs + [n1]
        cw = cw_ref[...]
        xh = jnp.concatenate(
            [cw[0:1] * ext[st] + cw[1:2] * ext[st + 1] + cw[2:3] * ext[st + 2] + cw[3:4] * ext[st + 3]
             for st in range(steps)], axis=0) + cb_ref[...]
        yield
        gates = jnp.dot(xh.astype(BF16), wg_ref[:, d * 2 * LRU_WIDTH:(d + 1) * 2 * LRU_WIDTH],
                        preferred_element_type=F32)
        yield
        c_half = (-0.5 * LRU_C) * _softplus(-lam_ref[d:d + 1, :])
        a_blocks, b_blocks = [], []
        blk = TILE // 4
        for r0 in range(0, TILE, blk):
            gb = gates[r0:r0 + blk]
            tr = jnp.tanh(gb[:, :LRU_WIDTH] + bg_ref[2 * d:2 * d + 1, :])
            ti = jnp.tanh(gb[:, LRU_WIDTH:] + bg_ref[2 * d + 1:2 * d + 2, :])
            log_a = c_half * tr + c_half
            ab = jnp.exp(log_a)
            b_blocks.append(jnp.sqrt(-jnp.tanh(log_a) * (ab * ab + 1.0)) * ((ti + 1.0) * xh[r0:r0 + blk]))
            a_blocks.append(ab)
            yield
        a = jnp.concatenate(a_blocks, axis=0)
        bb = jnp.concatenate(b_blocks, axis=0)

        order = list(range(steps)) if d == 0 else list(range(steps - 1, -1, -1))
        piece = lambda t, st: t[st * seg_n:(st + 1) * seg_n]
        a_tot = piece(a, order[0])
        b_tot = piece(bb, order[0])
        for n, st in enumerate(order[1:]):
            b_tot = piece(a, st) * b_tot + piece(bb, st)
            a_tot = piece(a, st) * a_tot
            if n % 8 == 7:
                yield
        s = 1
        while s < seg_n:
            if d == 0:
                keep = sub >= s
                a_sh = jnp.where(keep, pltpu.roll(a_tot, s, 0), 1.0)
                b_sh = jnp.where(keep, pltpu.roll(b_tot, s, 0), 0.0)
            else:
                keep = sub < seg_n - s
                a_sh = jnp.where(keep, pltpu.roll(a_tot, seg_n - s, 0), 1.0)
                b_sh = jnp.where(keep, pltpu.roll(b_tot, seg_n - s, 0), 0.0)
            b_tot = a_tot * b_sh + b_tot
            a_tot = a_tot * a_sh
            s *= 2
        h_in0 = carry[d]
        h_out = a_tot * h_in0 + b_tot
        if d == 0:
            h = jnp.where(sub == 0, h_in0, pltpu.roll(h_out, 1, 0))
            carry[d] = h_out[seg_n - 1:seg_n]
        else:
            h = jnp.where(sub == seg_n - 1, h_in0, pltpu.roll(h_out, seg_n - 1, 0))
            carry[d] = h_out[0:1]
        hs = [None] * steps
        for n, st in enumerate(order):
            h = piece(a, st) * h + piece(bb, st)
            hs[st] = h
            if n % 8 == 7:
                yield
        o_ref[0] = jnp.swapaxes(jnp.stack(hs, axis=0), 0, 1).reshape(TILE, LRU_WIDTH).astype(o_ref.dtype)

    return [one_dir(0, xf_ref, xfp_ref, xfn_ref, j, hf_ref),
            one_dir(1, xb_ref, xbp_ref, xbn_ref, _mirror(j, n_tiles), hb_ref)]


N_SCAN_IN = 12
N_LRU_IN = 11


def _round_robin(gens, weights):
    live = list(zip(gens, weights))
    while live:
        nxt = []
        for g, w in live:
            alive = True
            for _ in range(w):
                try:
                    next(g)
                except StopIteration:
                    alive = False
                    break
            if alive:
                nxt.append((g, w))
        live = nxt


def _sweep_kernel(*refs, n_tiles):
    scan_in, lru_in = refs[:N_SCAN_IN], refs[N_SCAN_IN:N_SCAN_IN + N_LRU_IN]
    of_ref, ob_ref, hf_ref, hb_ref, state, carry = refs[N_SCAN_IN + N_LRU_IN:]

    @pl.when(pl.program_id(1) == 0)
    def _():
        state[...] = jnp.zeros_like(state)
        carry[...] = jnp.zeros_like(carry)

    _round_robin([_gdn_scan_body(*scan_in, of_ref, ob_ref, state)]
                 + _lru_body(*lru_in, hf_ref, hb_ref, carry, n_tiles=n_tiles), (1, 2, 2))


def _sweep(w, u, kd, qg, aqk, gl, p, conv_w, conv_b, wg, bg, lam):
    b, s, _ = u.shape
    nt = s // TILE
    half = GDN_HEADS * GDN_HEAD_DIM
    hpt = TILE // HALO
    nh = s // HALO
    col = C_LRU_X // MIX
    tile_of = (lambda j: j, lambda j: _mirror(j, nt))

    def scan_specs(d):
        idx = lambda bi, j: (bi, tile_of[d](j), d)
        gidx = lambda bi, j: (bi, tile_of[d](j), 0, 0)
        return [pl.BlockSpec((1, TILE, half), idx)] * 4 + [pl.BlockSpec((1, TILE, PACK), idx),
                                                           pl.BlockSpec((1, 1, 8, LANES), gidx)]

    def lru_specs(d):
        t = tile_of[d]
        return [pl.BlockSpec((1, TILE, MIX), lambda bi, j: (bi, t(j), col)),
                pl.BlockSpec((1, HALO, MIX), lambda bi, j: (bi, jnp.maximum(t(j) * hpt - 1, 0), col)),
                pl.BlockSpec((1, HALO, MIX), lambda bi, j: (bi, jnp.minimum((t(j) + 1) * hpt, nh - 1), col))]

    full = lambda shape: pl.BlockSpec(shape, lambda bi, j: tuple(0 for _ in shape))
    out_spec = lambda d: pl.BlockSpec((1, TILE, MIX), lambda bi, j: (bi, tile_of[d](j), 0))
    in_specs = (scan_specs(0) + scan_specs(1) + lru_specs(0) + lru_specs(1)
                + [full((4, LRU_WIDTH)), full((1, LRU_WIDTH)), full((LRU_WIDTH, 4 * LRU_WIDTH)),
                   full((4, LRU_WIDTH)), full((N_DIR, LRU_WIDTH))])
    assert len(in_specs) == N_SCAN_IN + N_LRU_IN
    return pl.pallas_call(
        functools.partial(_sweep_kernel, n_tiles=nt),
        grid=(b, nt),
        in_specs=in_specs,
        out_specs=[out_spec(0), out_spec(1), out_spec(0), out_spec(1)],
        out_shape=[jax.ShapeDtypeStruct((b, s, MIX), BF16)] * 4,
        scratch_shapes=[pltpu.VMEM((N_DIR, GDN_HEADS, GDN_HEAD_DIM, GDN_HEAD_DIM), F32),
                        pltpu.VMEM((N_DIR, 1, LRU_WIDTH), F32)],
        compiler_params=_cparams("parallel", "arbitrary"),
        name="sweep",
    )(w, u, kd, qg, aqk, gl, w, u, kd, qg, aqk, gl, p, p, p, p, p, p, conv_w, conv_b, wg, bg, lam)


def _outproj_kernel(c_ref, x_ref, mod_ref, att_ref, of_ref, ob_ref, gz_ref, hf_ref, hb_ref, lz_ref,
                    gn_ref, w_ref, fg_ref, o_ref, *, final, first_tile):
    hd = GDN_HEAD_DIM
    o = of_ref[0].astype(F32) + ob_ref[0].astype(F32)
    parts = []
    for h in range(GDN_HEADS):
        oh = o[:, h * hd:(h + 1) * hd]
        parts.append(oh * lax.rsqrt(jnp.mean(oh * oh, axis=-1, keepdims=True) + EPS) * gn_ref[...])
    gdn = jnp.concatenate(parts, axis=1) * _silu(gz_ref[0])
    lru = (hf_ref[0].astype(F32) + hb_ref[0].astype(F32)) * _silu(lz_ref[0])
    acc = (jnp.dot(att_ref[0], w_ref[0:MIX, :], preferred_element_type=F32)
           + jnp.dot(gdn.astype(BF16), w_ref[MIX:2 * MIX, :], preferred_element_type=F32)
           + jnp.dot(lru.astype(BF16), w_ref[2 * MIX:3 * MIX, :], preferred_element_type=F32))
    gate = mod_ref[0, 0, 2:3, :]
    xn = _stream_tile(c_ref, x_ref, pl.program_id(1) + first_tile) + gate * acc
    if final:
        ms = jnp.mean(xn * xn, axis=-1, keepdims=True)
        xn = xn * lax.rsqrt(ms + EPS) * fg_ref[...]
    o_ref[0] = xn


def _outproj(c_src, x_src, src_off, modsel, att, of, ob, p, hf, hb, gdn_norm, w_out, final_g, final):
    b, s, _ = p.shape
    d = x_src.shape[2]
    nt = s // TILE
    off = 1 if final else 0
    n_out = nt - off
    tok = lambda col: pl.BlockSpec((1, TILE, MIX), lambda bi, i: (bi, i + off, col))
    return pl.pallas_call(
        functools.partial(_outproj_kernel, final=final, first_tile=off),
        grid=(b, n_out),
        in_specs=_stream_specs(d, src_off, off) + [
                  pl.BlockSpec((1, 1, 8, d), lambda bi, i: (bi, jnp.minimum(i + off, 1), 0, 0)),
                  tok(0), tok(0), tok(0), tok(C_GDN_Z // MIX), tok(0), tok(0), tok(C_LRU_Z // MIX),
                  pl.BlockSpec((1, GDN_HEAD_DIM), lambda bi, i: (0, 0)),
                  pl.BlockSpec((3 * MIX, d), lambda bi, i: (0, 0)),
                  pl.BlockSpec((1, d), lambda bi, i: (0, 0))],
        out_specs=pl.BlockSpec((1, TILE, d), lambda bi, i: (bi, i, 0)),
        out_shape=jax.ShapeDtypeStruct((b, n_out * TILE, d), F32),
        compiler_params=_cparams("parallel", "parallel"),
        name="outproj",
    )(c_src, x_src, modsel, att, of, ob, p, hf, hb, p, gdn_norm, w_out, final_g)


def _rope_tables(s, ctx_len):
    t = np.arange(s - ctx_len)
    n_freq = ATT_HEAD_DIM // 4
    inv = ROPE_BASE ** (-np.arange(n_freq, dtype=np.float64) / n_freq)
    ang = [(t // GRID_W)[:, None] * inv, (t % GRID_W)[:, None] * inv]
    cos = np.concatenate([np.cos(ang[0]), np.cos(ang[0]), np.cos(ang[1]), np.cos(ang[1])], axis=1)
    sin = np.concatenate([-np.sin(ang[0]), np.sin(ang[0]), -np.sin(ang[1]), np.sin(ang[1])], axis=1)
    cos = np.concatenate([np.ones((ctx_len, ATT_HEAD_DIM)), cos], axis=0)
    sin = np.concatenate([np.zeros((ctx_len, ATT_HEAD_DIM)), sin], axis=0)
    return (jnp.asarray(np.tile(cos, (1, 2)), F32), jnp.asarray(np.tile(sin, (1, 2)), F32))


def _block_diag(w):
    eye = jnp.eye(LRU_BLOCKS, dtype=w.dtype)
    return jnp.einsum("nde,nm->ndme", w, eye).reshape(LRU_WIDTH, LRU_WIDTH)


def _pad_lanes(v):
    return jnp.pad(v.reshape(1, -1), ((0, 0), (0, LANES - v.size)))


def kernel(x, c, ctx, c_ctx, norm_g, w_mod, b_mod, w_in, att_sink, gdn_conv, gdn_a_log, gdn_dt_bias, gdn_norm,
           lru_conv_w, lru_conv_b, lru_w_r, lru_b_r, lru_w_i, lru_b_i, lru_lambda, w_out, final_g):
    b, t, d = x.shape
    ctx_len = ctx.shape[1]
    depth = w_in.shape[0]
    assert ctx_len == TILE and t % TILE == 0 and d == D_MODEL and b == 2
    s = ctx_len + t
    nt = s // TILE

    cos_t, sin_t = _rope_tables(s, ctx_len)
    ct = jnp.pad(jnp.concatenate([c, c_ctx[None]], axis=0).T, ((0, 0), (0, LANES - b - 1)))
    mods = _modulation(ct, w_mod, b_mod)

    c_src, x_src, src_off = ctx, x, 1
    out = None
    for l in range(depth):
        m3 = mods[l, :3].reshape(3, 3, d)
        sel = jnp.stack([jnp.stack([m3[b], m3[bi]], axis=0) for bi in range(b)], axis=0)
        modsel = jnp.pad(sel, ((0, 0), (0, 0), (0, 5), (0, 0)))
        p = _inproj(c_src, x_src, src_off, modsel, norm_g[l].reshape(1, d), cos_t, sin_t, w_in, l)

        att = _attention(p, att_sink[l])

        prep = _gdn_prep(p, 0.5 * gdn_conv[l], _pad_lanes(gdn_a_log[l]), _pad_lanes(gdn_dt_bias[l]))

        wg = jnp.concatenate([_block_diag(lru_w_r[l, 0]), _block_diag(lru_w_i[l, 0]),
                              _block_diag(lru_w_r[l, 1]), _block_diag(lru_w_i[l, 1])], axis=1).astype(BF16)
        bg = 0.5 * jnp.stack([lru_b_r[l, 0], lru_b_i[l, 0], lru_b_r[l, 1], lru_b_i[l, 1]], axis=0)
        o_f, o_b, h_f, h_b = _sweep(*prep, p, 0.5 * lru_conv_w[l], 0.5 * lru_conv_b[l].reshape(1, -1), wg, bg,
                                    lru_lambda[l])

        final = l == depth - 1
        res = _outproj(c_src, x_src, src_off, modsel, att, o_f, o_b, p, h_f, h_b, gdn_norm[l].reshape(1, -1),
                       w_out[l].astype(BF16), final_g.reshape(1, d), final)
        if final:
            out = res
        else:
            c_src, x_src, src_off = res, res, 0
    return out
```

```python
import functools
import math

import numpy as np
import jax
import jax.numpy as jnp
from jax import lax
from jax.experimental import pallas as pl
from jax.experimental.pallas import tpu as pltpu

F32 = jnp.float32
BF16 = jnp.bfloat16

EPS = 1e-6
NEG_INF = -1e30
LOG2E = math.log2(math.e)
GRID_W = 64
ROPE_BASE = 10000.0

D_MODEL = 1024
ATT_HEADS = 8
ATT_KV_HEADS = 2
ATT_HEAD_DIM = 64
ATT_BLOCK = 128
GDN_HEADS = 4
GDN_HEAD_DIM = 128
GDN_CHUNK = 64
LRU_WIDTH = 512
LRU_BLOCKS = 8
LRU_BLOCK_DIM = 64
LRU_C = 8.0
N_DIR = 2
MIX = 512
PACK = GDN_HEADS * GDN_CHUNK

TILE = 256
HALO = 8
LANES = 128
VMEM_LIMIT = 56 * 1024 * 1024

C_ATT_Q, C_ATT_Z, C_GDN_Q, C_GDN_K, C_GDN_V, C_GDN_Z, C_LRU_Z = (i * MIX for i in range(7))
C_ATT_K = 7 * MIX
C_ATT_V = C_ATT_K + LANES
C_GDN_B = C_ATT_V + LANES
C_GDN_A = C_GDN_B + LANES
N_P = C_GDN_A + LANES
C_LRU_X = N_P
N_IN = C_LRU_X + MIX
SEG = 8
STEPS = TILE // SEG


def _cparams(*sem):
    return pltpu.CompilerParams(dimension_semantics=sem, vmem_limit_bytes=VMEM_LIMIT)


def _sigmoid(x):
    return 1.0 / (1.0 + jnp.exp(-x))


def _silu(x):
    return x * _sigmoid(x)


def _softplus(x):
    return jnp.maximum(x, 0.0) + jnp.log1p(jnp.exp(-jnp.abs(x)))


def _dot_nt(a, b):
    return lax.dot_general(a.astype(BF16), b.astype(BF16), (((1,), (1,)), ((), ())),
                           preferred_element_type=F32)


def _dot_tn(a, b):
    return lax.dot_general(a.astype(BF16), b.astype(BF16), (((0,), (0,)), ((), ())),
                           preferred_element_type=F32)


def _mod_kernel(ct_ref, w_ref, b_ref, o_ref):
    s = _silu(ct_ref[...])
    w = w_ref[0]
    rid = lax.broadcasted_iota(jnp.int32, (8, w.shape[1]), 0)
    out = jnp.zeros((8, w.shape[1]), F32)
    for r in range(3):
        row = jnp.sum(w * s[:, r:r + 1], axis=0, keepdims=True) + b_ref[0]
        out = jnp.where(rid == r, row, out)
    o_ref[0] = out


def _modulation(ct, w_mod, b_mod):
    depth, d, d3 = w_mod.shape
    return pl.pallas_call(
        _mod_kernel,
        grid=(depth, d3 // d),
        in_specs=[pl.BlockSpec((d, LANES), lambda l, j: (0, 0)),
                  pl.BlockSpec((1, d, d), lambda l, j: (l, 0, j)),
                  pl.BlockSpec((1, 1, d), lambda l, j: (l, 0, j))],
        out_specs=pl.BlockSpec((1, 8, d), lambda l, j: (l, 0, j)),
        out_shape=jax.ShapeDtypeStruct((depth, 8, d3), F32),
        compiler_params=_cparams("parallel", "parallel"),
        name="modulation",
    )(ct, w_mod, b_mod.reshape(depth, 1, d3))


def _rope(x, cos, sin):
    lane = lax.broadcasted_iota(jnp.int32, x.shape, 1)
    first = (lane % 32) < 16
    partner = jnp.where(first, pltpu.roll(x, LANES - 16, 1), pltpu.roll(x, 16, 1))
    return x * cos + partner * sin


def _stream_tile(c_ref, x_ref, tile):
    is_ctx = (jnp.zeros((TILE, 1), jnp.int32) + tile) == 0
    return jnp.where(is_ctx, c_ref[0], x_ref[0])


def _stream_specs(d, off, shift=0):
    return [pl.BlockSpec((1, TILE, d), lambda bi, i: (bi, 0, 0)),
            pl.BlockSpec((1, TILE, d), lambda bi, i: (bi, jnp.maximum(i + shift - off, 1 - off), 0))]


_IN_COLS = []
_src = 0
for _dst, _w in ((C_ATT_Q, MIX), (C_ATT_K, LANES), (C_ATT_V, LANES), (C_ATT_Z, MIX), (C_GDN_Q, MIX), (C_GDN_K, MIX),
                 (C_GDN_V, MIX), (C_GDN_B, N_DIR * GDN_HEADS), (C_GDN_A, N_DIR * GDN_HEADS), (C_GDN_Z, MIX),
                 (C_LRU_X, MIX), (C_LRU_Z, MIX)):
    _IN_COLS.append((_src, _dst, _w))
    _src += _w
IN_WIDTH = _src


def _scan_order(t):
    return jnp.swapaxes(t.reshape(SEG, STEPS, t.shape[1]), 0, 1).reshape(t.shape)


def _time_order(t):
    return jnp.swapaxes(t.reshape(STEPS, SEG, t.shape[1]), 0, 1).reshape(t.shape)


def _inproj_kernel(c_ref, x_ref, mod_ref, g_ref, cos_ref, sin_ref, w_ref, o_ref, lx_ref, w_s):
    @pl.when(jnp.logical_and(pl.program_id(0) == 0, pl.program_id(1) == 0))
    def _():
        lane = lax.broadcasted_iota(jnp.int32, (TILE, LANES), 1)
        n_gate = N_DIR * GDN_HEADS
        for r0 in range(0, w_s.shape[0], TILE):
            for src, dst, width in _IN_COLS:
                if width >= LANES:
                    w_s[r0:r0 + TILE, dst:dst + width] = w_ref[0, src:src + width, r0:r0 + TILE].T.astype(BF16)
            src_b = _IN_COLS[7][0]
            win = w_ref[0, src_b:src_b + LANES, r0:r0 + TILE].T
            w_s[r0:r0 + TILE, C_GDN_B:C_GDN_B + LANES] = jnp.where(lane < n_gate, win, 0.0).astype(BF16)
            w_s[r0:r0 + TILE, C_GDN_A:C_GDN_A + LANES] = jnp.where(
                lane < n_gate, pltpu.roll(win, LANES - n_gate, 1), 0.0).astype(BF16)

    x = _stream_tile(c_ref, x_ref, pl.program_id(1))
    ms = jnp.mean(x * x, axis=-1, keepdims=True)
    y = x * lax.rsqrt(ms + EPS) * g_ref[...]
    shift = mod_ref[0, 0, 0:1, :]
    scale = mod_ref[0, 0, 1:2, :]
    hb = (y * (1.0 + scale) + shift).astype(BF16)
    cos = cos_ref[...]
    sin = sin_ref[...]
    for c0 in range(0, N_IN, MIX):
        res = jnp.dot(hb, w_s[:, c0:c0 + MIX], preferred_element_type=F32)
        if c0 == C_ATT_Q:
            res = jnp.concatenate(
                [_rope(res[:, k:k + LANES], cos, sin) for k in range(0, MIX, LANES)], axis=1)
        if c0 == C_ATT_K:
            res = jnp.concatenate([_rope(res[:, :LANES], cos, sin), res[:, LANES:]], axis=1)
        if c0 == C_LRU_X:
            lx_ref[0] = _scan_order(res)
        else:
            o_ref[0, :, c0:c0 + MIX] = res


def _inproj(c_src, x_src, off, modsel, g, cos_t, sin_t, w_in, layer):
    b, _, d = x_src.shape
    s = x_src.shape[1] + off * TILE
    nt = s // TILE
    return pl.pallas_call(
        _inproj_kernel,
        grid=(b, nt),
        in_specs=_stream_specs(d, off) + [
                  pl.BlockSpec((1, 1, 8, d), lambda bi, i: (bi, jnp.minimum(i, 1), 0, 0)),
                  pl.BlockSpec((1, d), lambda bi, i: (0, 0)),
                  pl.BlockSpec((TILE, LANES), lambda bi, i: (i, 0)),
                  pl.BlockSpec((TILE, LANES), lambda bi, i: (i, 0)),
                  pl.BlockSpec((1, IN_WIDTH, d), lambda bi, i: (layer, 0, 0), pipeline_mode=pl.Buffered(1))],
        out_specs=[pl.BlockSpec((1, TILE, N_P), lambda bi, i: (bi, i, 0)),
                   pl.BlockSpec((1, TILE, MIX), lambda bi, i: (bi, i, 0))],
        out_shape=[jax.ShapeDtypeStruct((b, s, N_P), F32), jax.ShapeDtypeStruct((b, s, MIX), F32)],
        scratch_shapes=[pltpu.VMEM((d, N_IN), BF16)],
        compiler_params=_cparams("arbitrary", "arbitrary"),
        name="inproj",
    )(c_src, x_src, modsel, g, cos_t, sin_t, w_in)


def _att_kernel(sink_ref, q_ref, z_ref, kvp_ref, kvc_ref, kvn_ref, kvx_ref, o_ref, *, n_blocks):
    i = pl.program_id(1)
    n_sub = TILE // ATT_BLOCK
    n_band = 3 * ATT_BLOCK
    half = ATT_HEAD_DIM
    group = ATT_HEADS // ATT_KV_HEADS

    kv = jnp.concatenate([kvp_ref[0], kvc_ref[0], kvn_ref[0], kvx_ref[0]], axis=0)
    nk = kv.shape[0]
    k_all, v_all = kv[:, :LANES], kv[:, LANES:]
    k_sw, v_sw = pltpu.roll(k_all, half, 1), pltpu.roll(v_all, half, 1)
    lo_k = lax.broadcasted_iota(jnp.int32, (nk, LANES), 1) < half
    k_dup = [jnp.where(lo_k, k_all, k_sw).astype(BF16), jnp.where(lo_k, k_sw, k_all).astype(BF16)]
    v_dup = [jnp.where(lo_k, v_all, v_sw).astype(BF16), jnp.where(lo_k, v_sw, v_all).astype(BF16)]
    n_loc = (n_sub + 2) * ATT_BLOCK

    def keys(t, sub):
        return jnp.concatenate([t[sub * ATT_BLOCK:sub * ATT_BLOCK + n_band], t[n_loc:]], axis=0)

    r = lax.broadcasted_iota(jnp.int32, (ATT_BLOCK, n_band), 0)
    c = lax.broadcasted_iota(jnp.int32, (ATT_BLOCK, n_band), 1)
    in_win = jnp.abs(r - (c - ATT_BLOCK)) <= ATT_BLOCK
    lo_q = lax.broadcasted_iota(jnp.int32, (ATT_BLOCK, LANES), 1) < half
    outs = [[None] * (ATT_HEADS // 2) for _ in range(n_sub)]

    def band_bias(sub):
        m = i * n_sub + sub
        c_lo = jnp.where(m >= n_sub + 1, 0, jnp.where(m >= n_sub, ATT_BLOCK, n_band))
        c_hi = jnp.where(m < n_sub, 0, jnp.where(m <= n_blocks - 2, n_band, 2 * ATT_BLOCK))
        return jnp.where(jnp.logical_and(in_win, jnp.logical_and(c >= c_lo, c < c_hi)), 0.0, NEG_INF)

    def stream(sub, h, bias):
        q = q_ref[0, sub * ATT_BLOCK:(sub + 1) * ATT_BLOCK, :] * (ATT_HEAD_DIM ** -0.5 * LOG2E)
        lhs = []
        for j in range(h * group, (h + 1) * group):
            q_t = q[:, (j // 2) * LANES:(j // 2 + 1) * LANES]
            lhs.append(jnp.where(lo_q, q_t, 0.0) if j % 2 == 0 else jnp.where(lo_q, 0.0, q_t))
        qs = jnp.concatenate(lhs, axis=0).astype(BF16)
        s = lax.dot_general(qs, keys(k_dup[h], sub), (((1,), (1,)), ((), ())), preferred_element_type=F32)
        yield
        p_blocks, den_blocks = [], []
        for e in range(group):
            sb = s[e * ATT_BLOCK:(e + 1) * ATT_BLOCK]
            sk = sink_ref[h * group + e] * LOG2E
            s_band = sb[:, :n_band] + bias
            s_ctx = sb[:, n_band:]
            mx = jnp.maximum(jnp.maximum(jnp.max(s_band, axis=-1, keepdims=True),
                                         jnp.max(s_ctx, axis=-1, keepdims=True)), sk)
            p_band = jnp.exp2(s_band - mx)
            p_ctx = jnp.exp2(s_ctx - mx)
            den_blocks.append(jnp.sum(p_band, axis=-1, keepdims=True) + jnp.sum(p_ctx, axis=-1, keepdims=True)
                              + jnp.exp2(sk - mx))
            p_blocks.append(jnp.concatenate([p_band, p_ctx], axis=1).astype(BF16))
            yield
        p = jnp.concatenate(p_blocks, axis=0)
        res = (jnp.dot(p, keys(v_dup[h], sub), preferred_element_type=F32)
               / jnp.concatenate(den_blocks, axis=0))
        for t in range(group // 2):
            even = res[(2 * t) * ATT_BLOCK:(2 * t + 1) * ATT_BLOCK]
            odd = res[(2 * t + 1) * ATT_BLOCK:(2 * t + 2) * ATT_BLOCK]
            outs[sub][h * (group // 2) + t] = jnp.where(lo_q, even, odd)
        yield

    for sub in range(n_sub):
        bias = band_bias(sub)
        gens = [stream(sub, h, bias) for h in range(ATT_KV_HEADS)]
        for g in gens:
            next(g)
        for g in gens:
            for _ in g:
                pass
    for sub in range(n_sub):
        rows = slice(sub * ATT_BLOCK, (sub + 1) * ATT_BLOCK)
        o = jnp.concatenate(outs[sub], axis=1)
        o_ref[0, rows, :] = (o * _silu(z_ref[0, rows, :])).astype(o_ref.dtype)


def _attention(p, sink):
    b, s, _ = p.shape
    nb = s // ATT_BLOCK
    n_sub = TILE // ATT_BLOCK
    kvcol = C_ATT_K // (2 * LANES)
    kv_spec = lambda f: pl.BlockSpec((1, ATT_BLOCK, 2 * LANES), f)
    return pl.pallas_call(
        functools.partial(_att_kernel, n_blocks=nb),
        grid=(b, s // TILE),
        in_specs=[pl.BlockSpec(memory_space=pltpu.SMEM),
                  pl.BlockSpec((1, TILE, MIX), lambda bi, i: (bi, i, C_ATT_Q // MIX)),
                  pl.BlockSpec((1, TILE, MIX), lambda bi, i: (bi, i, C_ATT_Z // MIX)),
                  kv_spec(lambda bi, i: (bi, jnp.maximum(i * n_sub - 1, n_sub), kvcol)),
                  pl.BlockSpec((1, TILE, 2 * LANES), lambda bi, i: (bi, i, kvcol)),
                  kv_spec(lambda bi, i: (bi, jnp.minimum((i + 1) * n_sub, nb - 1), kvcol)),
                  pl.BlockSpec((1, TILE, 2 * LANES), lambda bi, i: (bi, 0, kvcol))],
        out_specs=pl.BlockSpec((1, TILE, MIX), lambda bi, i: (bi, i, 0)),
        out_shape=jax.ShapeDtypeStruct((b, s, MIX), BF16),
        compiler_params=_cparams("parallel", "parallel"),
        name="attention",
    )(sink, p, p, p, p, p, p)


def _conv4(x, hp, hn, w, prev_ok, next_ok):
    hp = hp * prev_ok
    hn = hn * next_ok
    ext = jnp.concatenate([hp, x, hn], axis=0)
    n = x.shape[0]
    own = slice(HALO, HALO + n)
    back2, back1, fwd1 = (pltpu.roll(ext, sh, 0) for sh in (2, 1, ext.shape[0] - 1))
    return w[0:1] * back2[own] + w[1:2] * back1[own] + w[2:3] * ext[own] + w[3:4] * fwd1[own]


def _halo_ok(i, n_tiles):
    prev_ok = (i >= 2).astype(F32)
    next_ok = jnp.logical_and(i != 0, i != n_tiles - 1).astype(F32)
    return prev_ok, next_ok


def _chunk_cumsum(g, reverse):
    n = g.shape[0]
    row = lax.broadcasted_iota(jnp.int32, g.shape, 0) % GDN_CHUNK
    s = 1
    while s < GDN_CHUNK:
        if reverse:
            g = g + jnp.where(row < GDN_CHUNK - s, pltpu.roll(g, n - s, 0), 0.0)
        else:
            g = g + jnp.where(row >= s, pltpu.roll(g, s, 0), 0.0)
        s *= 2
    return g


def _spread(cols, masks):
    out = cols[-1]
    for m, col in zip(masks[-2::-1], cols[-2::-1]):
        out = jnp.where(m, col, out)
    return out


def _head_block_diag(x, masks):
    return jnp.concatenate([jnp.where(m, x, 0.0) for m in masks], axis=0).astype(BF16)


def _gdn_prep_kernel(q_ref, qp_ref, qn_ref, k_ref, kp_ref, kn_ref, v_ref, vp_ref, vn_ref,
                     b_ref, a_ref, cw_ref, alog_ref, dtb_ref,
                     w_ref, u_ref, kd_ref, qg_ref, aqk_ref, gl_ref, *, n_tiles):
    i = pl.program_id(1)
    prev_ok, next_ok = _halo_ok(i, n_tiles)
    hd, ch, nh = GDN_HEAD_DIM, GDN_CHUNK, GDN_HEADS
    nch = TILE // ch

    def prep(x_ref, p_ref, n_ref, col):
        xh = _conv4(x_ref[0], p_ref[0], n_ref[0], cw_ref[:, col:col + MIX], prev_ok, next_ok)
        return xh * (1.0 + jnp.tanh(xh))

    def l2n(t):
        parts = []
        for h in range(nh):
            th = t[:, h * hd:(h + 1) * hd]
            parts.append(th * lax.rsqrt(jnp.sum(th * th, axis=-1, keepdims=True) + EPS))
        return jnp.concatenate(parts, axis=1)

    q = l2n(prep(q_ref, qp_ref, qn_ref, 0)) * (hd ** -0.5)
    k = l2n(prep(k_ref, kp_ref, kn_ref, MIX))
    v = prep(v_ref, vp_ref, vn_ref, 2 * MIX)

    lane = lax.broadcasted_iota(jnp.int32, (TILE, LANES), 1)
    g = -jnp.exp(alog_ref[...]) * _softplus(a_ref[0] + dtb_ref[...])
    gcum = jnp.where(lane < nh, _chunk_cumsum(g, False), _chunk_cumsum(g, True))
    beta = _sigmoid(b_ref[0])

    ii = lax.broadcasted_iota(jnp.int32, (ch, PACK), 0)
    ll = lax.broadcasted_iota(jnp.int32, (ch, PACK), 1)
    jj = ll % ch
    eye_p = jj == ii
    eye_f = eye_p.astype(F32)
    bd16 = (jj // 16) == (ii // 16)
    bd32 = (jj // 32) == (ii // 32)
    merge_masks = (jnp.logical_and(bd32, jnp.logical_not(bd16)), jnp.logical_not(bd32))
    strict = [jj < ii, jj > ii]
    incl = [jj <= ii, jj >= ii]
    head_p = [(ll // ch) == h for h in range(nh)]
    lw = lax.broadcasted_iota(jnp.int32, (ch, MIX), 1) // hd
    head_w = [lw == h for h in range(nh)]
    lane_row = lax.broadcasted_iota(jnp.int32, (1, LANES), 1)

    def pdot(a, b):
        return jnp.dot(a.astype(BF16), _head_block_diag(b, head_p), preferred_element_type=F32)

    gl_rows = []
    units = []
    for cidx in range(nch):
        rows = slice(cidx * ch, (cidx + 1) * ch)
        kc, qc, vc, gc, bc = k[rows], q[rows], v[rows], gcum[rows], beta[rows]
        kq = _dot_nt(jnp.concatenate([kc, qc], axis=0), _head_block_diag(kc, head_w))
        g_end = jnp.where(lane_row < nh, gc[ch - 1:ch, :], gc[0:1, :])
        gl_rows.append(jnp.exp(g_end))
        for d in range(N_DIR):
            gcols = [gc[:, d * nh + h:d * nh + h + 1] for h in range(nh)]
            bcols = [bc[:, d * nh + h:d * nh + h + 1] for h in range(nh)]
            gcol_p = _spread(gcols, head_p)
            grow_p = jnp.sum(jnp.where(eye_p, gcol_p, 0.0), axis=0, keepdims=True)
            dec = jnp.exp(jnp.where(incl[d], gcol_p - grow_p, 0.0))
            lm = jnp.where(strict[d], _spread(bcols, head_p) * dec * kq[:ch], 0.0)
            aqk = jnp.where(incl[d], dec * kq[ch:], 0.0)
            aqk_ref[0, rows, d * PACK:(d + 1) * PACK] = aqk.astype(BF16)
            units.append((rows, d, lm, kc, qc, vc, gcols, bcols,
                          [g_end[:, d * nh + h:d * nh + h + 1] for h in range(nh)]))
    gl_rows.append(jnp.zeros((8 - nch, LANES), F32))
    gl_ref[0, 0] = jnp.concatenate(gl_rows, axis=0)

    lms = [u[2] for u in units]
    m = [jnp.where(bd16, -lm, 0.0) for lm in lms]
    x = [eye_f + mi for mi in m]
    pw = [pdot(mi, mi) for mi in m]
    for _ in range(2):
        r = [pdot(jnp.concatenate([xi, pi], axis=0), pi) for xi, pi in zip(x, pw)]
        x = [xi + ri[:ch] for xi, ri in zip(x, r)]
        pw = [ri[ch:] for ri in r]
    x = [xi + pdot(xi, pi) for xi, pi in zip(x, pw)]
    for mask in merge_masks:
        y = [pdot(xi, jnp.where(mask, lm, 0.0)) for xi, lm in zip(x, lms)]
        x = [xi - pdot(yi, xi) for xi, yi in zip(x, y)]

    for tinv, (rows, d, _, kc, qc, vc, gcols, bcols, gend_cols) in zip(x, units):
        slab = lambda t, h: t[:, h * hd:(h + 1) * hd]
        gam = [jnp.exp(gcol) for gcol in gcols]
        bgk = jnp.concatenate([(bcols[h] * gam[h]) * slab(kc, h) for h in range(nh)], axis=1)
        bv = jnp.concatenate([bcols[h] * slab(vc, h) for h in range(nh)], axis=1)
        rhs = jnp.concatenate([_head_block_diag(bgk, head_w), _head_block_diag(bv, head_w)], axis=1)
        wu = jnp.dot(tinv.astype(BF16), rhs, preferred_element_type=F32)
        cols = slice(d * MIX, (d + 1) * MIX)
        w_ref[0, rows, cols] = wu[:, :MIX].astype(BF16)
        u_ref[0, rows, cols] = wu[:, MIX:]
        kd_ref[0, rows, cols] = jnp.concatenate(
            [jnp.exp(gend_cols[h] - gcols[h]) * slab(kc, h) for h in range(nh)], axis=1).astype(BF16)
        qg_ref[0, rows, cols] = jnp.concatenate([gam[h] * slab(qc, h) for h in range(nh)], axis=1).astype(BF16)


def _gdn_prep(p, conv_w, alog, dtb):
    b, s, _ = p.shape
    nt = s // TILE
    hpt = TILE // HALO
    nh = s // HALO
    wide = N_DIR * GDN_HEADS * GDN_HEAD_DIM

    def tile_spec(col):
        return pl.BlockSpec((1, TILE, MIX), lambda bi, i: (bi, i, col // MIX))

    def prev_spec(col):
        return pl.BlockSpec((1, HALO, MIX), lambda bi, i: (bi, jnp.maximum(i * hpt - 1, 0), col // MIX))

    def next_spec(col):
        return pl.BlockSpec((1, HALO, MIX), lambda bi, i: (bi, jnp.minimum((i + 1) * hpt, nh - 1), col // MIX))

    in_specs = []
    for col in (C_GDN_Q, C_GDN_K, C_GDN_V):
        in_specs += [tile_spec(col), prev_spec(col), next_spec(col)]
    in_specs += [pl.BlockSpec((1, TILE, LANES), lambda bi, i: (bi, i, C_GDN_B // LANES)),
                 pl.BlockSpec((1, TILE, LANES), lambda bi, i: (bi, i, C_GDN_A // LANES)),
                 pl.BlockSpec((4, 3 * MIX), lambda bi, i: (0, 0)),
                 pl.BlockSpec((1, LANES), lambda bi, i: (0, 0)),
                 pl.BlockSpec((1, LANES), lambda bi, i: (0, 0))]
    wide_spec = pl.BlockSpec((1, TILE, wide), lambda bi, i: (bi, i, 0))
    out_specs = [wide_spec, wide_spec, wide_spec, wide_spec,
                 pl.BlockSpec((1, TILE, N_DIR * PACK), lambda bi, i: (bi, i, 0)),
                 pl.BlockSpec((1, 1, 8, LANES), lambda bi, i: (bi, i, 0, 0))]
    out_shape = [jax.ShapeDtypeStruct((b, s, wide), BF16),
                 jax.ShapeDtypeStruct((b, s, wide), F32),
                 jax.ShapeDtypeStruct((b, s, wide), BF16),
                 jax.ShapeDtypeStruct((b, s, wide), BF16),
                 jax.ShapeDtypeStruct((b, s, N_DIR * PACK), BF16),
                 jax.ShapeDtypeStruct((b, nt, 8, LANES), F32)]
    return pl.pallas_call(
        functools.partial(_gdn_prep_kernel, n_tiles=nt),
        grid=(b, nt),
        in_specs=in_specs,
        out_specs=out_specs,
        out_shape=out_shape,
        compiler_params=_cparams("parallel", "parallel"),
        name="gdn_prep",
    )(p, p, p, p, p, p, p, p, p, p, p, conv_w, alog, dtb)


def _gdn_scan_body(wf, uf, kdf, qgf, af, glf, wb, ub, kdb, qgb, ab, glb, of_ref, ob_ref, state):
    hd, ch, nh = GDN_HEAD_DIM, GDN_CHUNK, GDN_HEADS
    nch = TILE // ch
    lw = lax.broadcasted_iota(jnp.int32, (ch, MIX), 1) // hd
    head_w = [lw == h for h in range(nh)]
    dirs = ((wf, uf, kdf, qgf, af, glf, of_ref), (wb, ub, kdb, qgb, ab, glb, ob_ref))
    for step in range(nch):
        cidx = [step, nch - 1 - step]
        rows = [slice(c * ch, (c + 1) * ch) for c in cidx]
        s_old = {}
        ws = {}
        for d, (w_r, _, _, qg_r, _, _, _) in enumerate(dirs):
            for h in range(nh):
                cols = slice(h * hd, (h + 1) * hd)
                s_old[d, h] = state[d, h]
                lhs = jnp.concatenate([w_r[0, rows[d], cols], qg_r[0, rows[d], cols]], axis=0)
                ws[d, h] = jnp.dot(lhs, s_old[d, h].astype(BF16), preferred_element_type=F32)
            yield
        for d, (_, u_r, kd_r, _, a_r, gl_r, o_r) in enumerate(dirs):
            un = u_r[0, rows[d], :] - jnp.concatenate([ws[d, h][:ch] for h in range(nh)], axis=1)
            inter = jnp.concatenate([ws[d, h][ch:] for h in range(nh)], axis=1)
            intra = jnp.dot(a_r[0, rows[d], :], _head_block_diag(un, head_w), preferred_element_type=F32)
            o_r[0, rows[d], :] = (inter + intra).astype(o_r.dtype)
            unb = un.astype(BF16)
            for h in range(nh):
                cols = slice(h * hd, (h + 1) * hd)
                gl = gl_r[0, 0, cidx[d]:cidx[d] + 1, d * nh + h:d * nh + h + 1]
                state[d, h] = gl * s_old[d, h] + _dot_tn(kd_r[0, rows[d], cols], unb[:, cols])
            yield


def _mirror(j, nt):
    return jnp.where(j == 0, 0, nt - j)


def _lru_body(xf_ref, xfp2_ref, xfp_ref, xfn_ref, xb_ref, xbp2_ref, xbp_ref, xbn_ref, cw_ref, cb_ref, wg_ref, bg_ref,
              lam_ref, hf_ref, hb_ref, carry, *, n_tiles):
    j = pl.program_id(1)
    seg_n = SEG
    steps = STEPS
    sub = lax.broadcasted_iota(jnp.int32, (seg_n, LRU_WIDTH), 0)

    def one_dir(d, x_ref, p2_ref, p_ref, n_ref, tile_idx, o_ref):
        prev_ok, next_ok = _halo_ok(tile_idx, n_tiles)
        xs = [x_ref[0, st * seg_n:(st + 1) * seg_n, :] for st in range(steps)]
        last = seg_n - 1
        p1 = jnp.where(sub == 0, p_ref[0, last:last + 1, :] * prev_ok, pltpu.roll(xs[steps - 1], 1, 0))
        p2 = jnp.where(sub == 0, p2_ref[0, last:last + 1, :] * prev_ok, pltpu.roll(xs[steps - 2], 1, 0))
        n1 = jnp.where(sub == last, n_ref[0, 0:1, :] * next_ok, pltpu.roll(xs[0], last, 0))
        ext = [p2, p1] + xs + [n1]
        cw = cw_ref[...]
        xh = jnp.concatenate(
            [cw[0:1] * ext[st] + cw[1:2] * ext[st + 1] + cw[2:3] * ext[st + 2] + cw[3:4] * ext[st + 3]
             for st in range(steps)], axis=0) + cb_ref[...]
        yield
        gates = jnp.dot(xh.astype(BF16), wg_ref[:, d * 2 * LRU_WIDTH:(d + 1) * 2 * LRU_WIDTH],
                        preferred_element_type=F32)
        yield
        c_half = (-0.5 * LRU_C) * _softplus(-lam_ref[d:d + 1, :])
        a_blocks, b_blocks = [], []
        blk = TILE // 4
        for r0 in range(0, TILE, blk):
            gb = gates[r0:r0 + blk]
            tr = jnp.tanh(gb[:, :LRU_WIDTH] + bg_ref[2 * d:2 * d + 1, :])
            ti = jnp.tanh(gb[:, LRU_WIDTH:] + bg_ref[2 * d + 1:2 * d + 2, :])
            log_a = c_half * tr + c_half
            ab = jnp.exp(log_a)
            b_blocks.append(jnp.sqrt(-jnp.tanh(log_a) * (ab * ab + 1.0)) * ((ti + 1.0) * xh[r0:r0 + blk]))
            a_blocks.append(ab)
            yield
        a = jnp.concatenate(a_blocks, axis=0)
        bb = jnp.concatenate(b_blocks, axis=0)

        order = list(range(steps)) if d == 0 else list(range(steps - 1, -1, -1))
        piece = lambda t, st: t[st * seg_n:(st + 1) * seg_n]
        a_tot = piece(a, order[0])
        b_tot = piece(bb, order[0])
        for n, st in enumerate(order[1:]):
            b_tot = piece(a, st) * b_tot + piece(bb, st)
            a_tot = piece(a, st) * a_tot
            if n % 8 == 7:
                yield
        s = 1
        while s < seg_n:
            if d == 0:
                keep = sub >= s
                a_sh = jnp.where(keep, pltpu.roll(a_tot, s, 0), 1.0)
                b_sh = jnp.where(keep, pltpu.roll(b_tot, s, 0), 0.0)
            else:
                keep = sub < seg_n - s
                a_sh = jnp.where(keep, pltpu.roll(a_tot, seg_n - s, 0), 1.0)
                b_sh = jnp.where(keep, pltpu.roll(b_tot, seg_n - s, 0), 0.0)
            b_tot = a_tot * b_sh + b_tot
            a_tot = a_tot * a_sh
            s *= 2
        h_in0 = carry[d]
        h_out = a_tot * h_in0 + b_tot
        if d == 0:
            h = jnp.where(sub == 0, h_in0, pltpu.roll(h_out, 1, 0))
            carry[d] = h_out[seg_n - 1:seg_n]
        else:
            h = jnp.where(sub == seg_n - 1, h_in0, pltpu.roll(h_out, seg_n - 1, 0))
            carry[d] = h_out[0:1]
        hs = [None] * steps
        for n, st in enumerate(order):
            h = piece(a, st) * h + piece(bb, st)
            hs[st] = h
            if n % 8 == 7:
                yield
        o_ref[0] = jnp.concatenate(hs, axis=0).astype(o_ref.dtype)

    return [one_dir(0, xf_ref, xfp2_ref, xfp_ref, xfn_ref, j, hf_ref),
            one_dir(1, xb_ref, xbp2_ref, xbp_ref, xbn_ref, _mirror(j, n_tiles), hb_ref)]


N_SCAN_IN = 12
N_LRU_IN = 13


def _round_robin(gens, weights):
    live = list(zip(gens, weights))
    while live:
        nxt = []
        for g, w in live:
            alive = True
            for _ in range(w):
                try:
                    next(g)
                except StopIteration:
                    alive = False
                    break
            if alive:
                nxt.append((g, w))
        live = nxt


def _sweep_kernel(*refs, n_tiles):
    scan_in, lru_in = refs[:N_SCAN_IN], refs[N_SCAN_IN:N_SCAN_IN + N_LRU_IN]
    of_ref, ob_ref, hf_ref, hb_ref, state, carry = refs[N_SCAN_IN + N_LRU_IN:]

    @pl.when(pl.program_id(1) == 0)
    def _():
        state[...] = jnp.zeros_like(state)
        carry[...] = jnp.zeros_like(carry)

    _round_robin([_gdn_scan_body(*scan_in, of_ref, ob_ref, state)]
                 + _lru_body(*lru_in, hf_ref, hb_ref, carry, n_tiles=n_tiles), (1, 2, 2))


def _sweep(w, u, kd, qg, aqk, gl, lx, conv_w, conv_b, wg, bg, lam):
    b, s, _ = u.shape
    nt = s // TILE
    half = GDN_HEADS * GDN_HEAD_DIM
    hpt = TILE // HALO
    nh = s // HALO
    tile_of = (lambda j: j, lambda j: _mirror(j, nt))

    def scan_specs(d):
        idx = lambda bi, j: (bi, tile_of[d](j), d)
        gidx = lambda bi, j: (bi, tile_of[d](j), 0, 0)
        return [pl.BlockSpec((1, TILE, half), idx)] * 4 + [pl.BlockSpec((1, TILE, PACK), idx),
                                                           pl.BlockSpec((1, 1, 8, LANES), gidx)]

    def lru_specs(d):
        t = tile_of[d]
        return [pl.BlockSpec((1, TILE, MIX), lambda bi, j: (bi, t(j), 0)),
                pl.BlockSpec((1, HALO, MIX), lambda bi, j: (bi, jnp.maximum(t(j) * hpt - 2, 0), 0)),
                pl.BlockSpec((1, HALO, MIX), lambda bi, j: (bi, jnp.maximum(t(j) * hpt - 1, 0), 0)),
                pl.BlockSpec((1, HALO, MIX), lambda bi, j: (bi, jnp.minimum((t(j) + 1) * hpt, nh - 1), 0))]

    full = lambda shape: pl.BlockSpec(shape, lambda bi, j: tuple(0 for _ in shape))
    out_spec = lambda d: pl.BlockSpec((1, TILE, MIX), lambda bi, j: (bi, tile_of[d](j), 0))
    in_specs = (scan_specs(0) + scan_specs(1) + lru_specs(0) + lru_specs(1)
                + [full((4, LRU_WIDTH)), full((1, LRU_WIDTH)), full((LRU_WIDTH, 4 * LRU_WIDTH)),
                   full((4, LRU_WIDTH)), full((N_DIR, LRU_WIDTH))])
    assert len(in_specs) == N_SCAN_IN + N_LRU_IN
    return pl.pallas_call(
        functools.partial(_sweep_kernel, n_tiles=nt),
        grid=(b, nt),
        in_specs=in_specs,
        out_specs=[out_spec(0), out_spec(1), out_spec(0), out_spec(1)],
        out_shape=[jax.ShapeDtypeStruct((b, s, MIX), BF16)] * 4,
        scratch_shapes=[pltpu.VMEM((N_DIR, GDN_HEADS, GDN_HEAD_DIM, GDN_HEAD_DIM), F32),
                        pltpu.VMEM((N_DIR, 1, LRU_WIDTH), F32)],
        compiler_params=_cparams("parallel", "arbitrary"),
        name="sweep",
    )(w, u, kd, qg, aqk, gl, w, u, kd, qg, aqk, gl, *([lx] * 8), conv_w, conv_b, wg, bg, lam)


def _outproj_kernel(c_ref, x_ref, mod_ref, att_ref, of_ref, ob_ref, gz_ref, hf_ref, hb_ref, lz_ref,
                    gn_ref, w_ref, fg_ref, o_ref, *, final, first_tile):
    hd = GDN_HEAD_DIM
    o = of_ref[0].astype(F32) + ob_ref[0].astype(F32)
    parts = []
    for h in range(GDN_HEADS):
        oh = o[:, h * hd:(h + 1) * hd]
        parts.append(oh * lax.rsqrt(jnp.mean(oh * oh, axis=-1, keepdims=True) + EPS) * gn_ref[...])
    gdn = jnp.concatenate(parts, axis=1) * _silu(gz_ref[0])
    lru = _time_order(hf_ref[0].astype(F32) + hb_ref[0].astype(F32)) * _silu(lz_ref[0])
    acc = (jnp.dot(att_ref[0], w_ref[0:MIX, :], preferred_element_type=F32)
           + jnp.dot(gdn.astype(BF16), w_ref[MIX:2 * MIX, :], preferred_element_type=F32)
           + jnp.dot(lru.astype(BF16), w_ref[2 * MIX:3 * MIX, :], preferred_element_type=F32))
    gate = mod_ref[0, 0, 2:3, :]
    xn = _stream_tile(c_ref, x_ref, pl.program_id(1) + first_tile) + gate * acc
    if final:
        ms = jnp.mean(xn * xn, axis=-1, keepdims=True)
        xn = xn * lax.rsqrt(ms + EPS) * fg_ref[...]
    o_ref[0] = xn


def _outproj(c_src, x_src, src_off, modsel, att, of, ob, p, hf, hb, gdn_norm, w_out, final_g, final):
    b, s, _ = p.shape
    d = x_src.shape[2]
    nt = s // TILE
    off = 1 if final else 0
    n_out = nt - off
    tok = lambda col: pl.BlockSpec((1, TILE, MIX), lambda bi, i: (bi, i + off, col))
    return pl.pallas_call(
        functools.partial(_outproj_kernel, final=final, first_tile=off),
        grid=(b, n_out),
        in_specs=_stream_specs(d, src_off, off) + [
                  pl.BlockSpec((1, 1, 8, d), lambda bi, i: (bi, jnp.minimum(i + off, 1), 0, 0)),
                  tok(0), tok(0), tok(0), tok(C_GDN_Z // MIX), tok(0), tok(0), tok(C_LRU_Z // MIX),
                  pl.BlockSpec((1, GDN_HEAD_DIM), lambda bi, i: (0, 0)),
                  pl.BlockSpec((3 * MIX, d), lambda bi, i: (0, 0)),
                  pl.BlockSpec((1, d), lambda bi, i: (0, 0))],
        out_specs=pl.BlockSpec((1, TILE, d), lambda bi, i: (bi, i, 0)),
        out_shape=jax.ShapeDtypeStruct((b, n_out * TILE, d), F32),
        compiler_params=_cparams("parallel", "parallel"),
        name="outproj",
    )(c_src, x_src, modsel, att, of, ob, p, hf, hb, p, gdn_norm, w_out, final_g)


def _rope_tables(s, ctx_len):
    t = np.arange(s - ctx_len)
    n_freq = ATT_HEAD_DIM // 4
    inv = ROPE_BASE ** (-np.arange(n_freq, dtype=np.float64) / n_freq)
    ang = [(t // GRID_W)[:, None] * inv, (t % GRID_W)[:, None] * inv]
    cos = np.concatenate([np.cos(ang[0]), np.cos(ang[0]), np.cos(ang[1]), np.cos(ang[1])], axis=1)
    sin = np.concatenate([-np.sin(ang[0]), np.sin(ang[0]), -np.sin(ang[1]), np.sin(ang[1])], axis=1)
    cos = np.concatenate([np.ones((ctx_len, ATT_HEAD_DIM)), cos], axis=0)
    sin = np.concatenate([np.zeros((ctx_len, ATT_HEAD_DIM)), sin], axis=0)
    return (jnp.asarray(np.tile(cos, (1, 2)), F32), jnp.asarray(np.tile(sin, (1, 2)), F32))


def _block_diag(w):
    eye = jnp.eye(LRU_BLOCKS, dtype=w.dtype)
    return jnp.einsum("nde,nm->ndme", w, eye).reshape(LRU_WIDTH, LRU_WIDTH)


def _pad_lanes(v):
    return jnp.pad(v.reshape(1, -1), ((0, 0), (0, LANES - v.size)))


def kernel(x, c, ctx, c_ctx, norm_g, w_mod, b_mod, w_in, att_sink, gdn_conv, gdn_a_log, gdn_dt_bias, gdn_norm,
           lru_conv_w, lru_conv_b, lru_w_r, lru_b_r, lru_w_i, lru_b_i, lru_lambda, w_out, final_g):
    b, t, d = x.shape
    ctx_len = ctx.shape[1]
    depth = w_in.shape[0]
    assert ctx_len == TILE and t % TILE == 0 and d == D_MODEL and b == 2
    s = ctx_len + t
    nt = s // TILE

    cos_t, sin_t = _rope_tables(s, ctx_len)
    ct = jnp.pad(jnp.concatenate([c, c_ctx[None]], axis=0).T, ((0, 0), (0, LANES - b - 1)))
    mods = _modulation(ct, w_mod, b_mod)

    w_in_t = jnp.swapaxes(w_in, 1, 2)
    c_src, x_src, src_off = ctx, x, 1
    out = None
    for l in range(depth):
        m3 = mods[l, :3].reshape(3, 3, d)
        sel = jnp.stack([jnp.stack([m3[b], m3[bi]], axis=0) for bi in range(b)], axis=0)
        modsel = jnp.pad(sel, ((0, 0), (0, 0), (0, 5), (0, 0)))
        p, lx = _inproj(c_src, x_src, src_off, modsel, norm_g[l].reshape(1, d), cos_t, sin_t, w_in_t, l)

        att = _attention(p, att_sink[l])

        prep = _gdn_prep(p, 0.5 * gdn_conv[l], _pad_lanes(gdn_a_log[l]), _pad_lanes(gdn_dt_bias[l]))

        wg = jnp.concatenate([_block_diag(lru_w_r[l, 0]), _block_diag(lru_w_i[l, 0]),
                              _block_diag(lru_w_r[l, 1]), _block_diag(lru_w_i[l, 1])], axis=1).astype(BF16)
        bg = 0.5 * jnp.stack([lru_b_r[l, 0], lru_b_i[l, 0], lru_b_r[l, 1], lru_b_i[l, 1]], axis=0)
        o_f, o_b, h_f, h_b = _sweep(*prep, lx, 0.5 * lru_conv_w[l], 0.5 * lru_conv_b[l].reshape(1, -1), wg, bg,
                                    lru_lambda[l])

        final = l == depth - 1
        res = _outproj(c_src, x_src, src_off, modsel, att, o_f, o_b, p, h_f, h_b, gdn_norm[l].reshape(1, -1),
                       w_out[l].astype(BF16), final_g.reshape(1, d), final)
        if final:
            out = res
        else:
            c_src, x_src, src_off = res, res, 0
    return out
```

```python
import functools
import math

import numpy as np
import jax
import jax.numpy as jnp
from jax import lax
from jax.experimental import pallas as pl
from jax.experimental.pallas import tpu as pltpu

F32 = jnp.float32
BF16 = jnp.bfloat16

EPS = 1e-6
NEG_INF = -1e30
LOG2E = math.log2(math.e)
GRID_W = 64
ROPE_BASE = 10000.0

D_MODEL = 1024
ATT_HEADS = 8
ATT_KV_HEADS = 2
ATT_HEAD_DIM = 64
ATT_BLOCK = 128
GDN_HEADS = 4
GDN_HEAD_DIM = 128
GDN_CHUNK = 64
LRU_WIDTH = 512
LRU_BLOCKS = 8
LRU_BLOCK_DIM = 64
LRU_C = 8.0
N_DIR = 2
MIX = 512
PACK = GDN_HEADS * GDN_CHUNK

TILE = 256
HALO = 8
LANES = 128
VMEM_LIMIT = 56 * 1024 * 1024

C_ATT_Q, C_ATT_Z, C_GDN_Z, C_LRU_Z = (i * MIX for i in range(4))
C_ATT_K = 4 * MIX
C_ATT_V = C_ATT_K + LANES
C_GDN_B = C_ATT_V + LANES
C_GDN_A = C_GDN_B + LANES
N_P = C_GDN_A + LANES
C_GDN_Q, C_GDN_K, C_GDN_V, C_LRU_X = (N_P + i * MIX for i in range(4))
N_IN = C_LRU_X + MIX
SEG = 8
STEPS = TILE // SEG


def _cparams(*sem):
    return pltpu.CompilerParams(dimension_semantics=sem, vmem_limit_bytes=VMEM_LIMIT)


def _sigmoid(x):
    return 1.0 / (1.0 + jnp.exp(-x))


def _silu(x):
    return x * _sigmoid(x)


def _softplus(x):
    return jnp.maximum(x, 0.0) + jnp.log1p(jnp.exp(-jnp.abs(x)))


def _dot_nt(a, b):
    return lax.dot_general(a.astype(BF16), b.astype(BF16), (((1,), (1,)), ((), ())),
                           preferred_element_type=F32)


def _dot_tn(a, b):
    return lax.dot_general(a.astype(BF16), b.astype(BF16), (((0,), (0,)), ((), ())),
                           preferred_element_type=F32)


def _mod_kernel(ct_ref, w_ref, b_ref, o_ref):
    s = _silu(ct_ref[...])
    w = w_ref[0]
    rid = lax.broadcasted_iota(jnp.int32, (8, w.shape[1]), 0)
    out = jnp.zeros((8, w.shape[1]), F32)
    for r in range(3):
        row = jnp.sum(w * s[:, r:r + 1], axis=0, keepdims=True) + b_ref[0]
        out = jnp.where(rid == r, row, out)
    o_ref[0] = out


def _modulation(ct, w_mod, b_mod):
    depth, d, d3 = w_mod.shape
    return pl.pallas_call(
        _mod_kernel,
        grid=(depth, d3 // d),
        in_specs=[pl.BlockSpec((d, LANES), lambda l, j: (0, 0)),
                  pl.BlockSpec((1, d, d), lambda l, j: (l, 0, j)),
                  pl.BlockSpec((1, 1, d), lambda l, j: (l, 0, j))],
        out_specs=pl.BlockSpec((1, 8, d), lambda l, j: (l, 0, j)),
        out_shape=jax.ShapeDtypeStruct((depth, 8, d3), F32),
        compiler_params=_cparams("parallel", "parallel"),
        name="modulation",
    )(ct, w_mod, b_mod.reshape(depth, 1, d3))


def _rope(x, cos, sin):
    lane = lax.broadcasted_iota(jnp.int32, x.shape, 1)
    first = (lane % 32) < 16
    partner = jnp.where(first, pltpu.roll(x, LANES - 16, 1), pltpu.roll(x, 16, 1))
    return x * cos + partner * sin


def _stream_tile(c_ref, x_ref, tile):
    is_ctx = (jnp.zeros((TILE, 1), jnp.int32) + tile) == 0
    return jnp.where(is_ctx, c_ref[0], x_ref[0])


def _stream_specs(d, off, shift=0):
    return [pl.BlockSpec((1, TILE, d), lambda bi, i: (bi, 0, 0)),
            pl.BlockSpec((1, TILE, d), lambda bi, i: (bi, jnp.maximum(i + shift - off, 1 - off), 0))]


def _conv4(ext, w):
    n = ext.shape[0] - 2 * HALO
    own = slice(HALO, HALO + n)
    back2, back1, fwd1 = (pltpu.roll(ext, sh, 0) for sh in (2, 1, ext.shape[0] - 1))
    return w[0:1] * back2[own] + w[1:2] * back1[own] + w[2:3] * ext[own] + w[3:4] * fwd1[own]


def _halo_ok(i, n_tiles):
    prev_ok = (i >= 2).astype(F32)
    next_ok = jnp.logical_and(i != 0, i != n_tiles - 1).astype(F32)
    return prev_ok, next_ok


_IN_COLS = []
_src = 0
for _dst, _w in ((C_ATT_Q, MIX), (C_ATT_K, LANES), (C_ATT_V, LANES), (C_ATT_Z, MIX), (C_GDN_Q, MIX), (C_GDN_K, MIX),
                 (C_GDN_V, MIX), (C_GDN_B, N_DIR * GDN_HEADS), (C_GDN_A, N_DIR * GDN_HEADS), (C_GDN_Z, MIX),
                 (C_LRU_X, MIX), (C_LRU_Z, MIX)):
    _IN_COLS.append((_src, _dst, _w))
    _src += _w
IN_WIDTH = _src


def _scan_order(t):
    return jnp.swapaxes(t.reshape(SEG, STEPS, t.shape[1]), 0, 1).reshape(t.shape)


def _time_order(t):
    return jnp.swapaxes(t.reshape(STEPS, SEG, t.shape[1]), 0, 1).reshape(t.shape)


def _l2norm_heads(t):
    hd = GDN_HEAD_DIM
    parts = []
    for h in range(GDN_HEADS):
        th = t[:, h * hd:(h + 1) * hd]
        parts.append(th * lax.rsqrt(jnp.sum(th * th, axis=-1, keepdims=True) + EPS))
    return jnp.concatenate(parts, axis=1)


def _inproj_kernel(c_ref, x_ref, xp_ref, xn_ref, mod_ref, g_ref, cos_ref, sin_ref, cw_ref, w_ref,
                   o_ref, lx_ref, gq_ref, gk_ref, gv_ref, w_s, *, n_tiles):
    @pl.when(jnp.logical_and(pl.program_id(0) == 0, pl.program_id(1) == 0))
    def _():
        lane = lax.broadcasted_iota(jnp.int32, (TILE, LANES), 1)
        n_gate = N_DIR * GDN_HEADS
        for r0 in range(0, w_s.shape[0], TILE):
            for src, dst, width in _IN_COLS:
                if width >= LANES:
                    w_s[r0:r0 + TILE, dst:dst + width] = w_ref[0, src:src + width, r0:r0 + TILE].T.astype(BF16)
            src_b = _IN_COLS[7][0]
            win = w_ref[0, src_b:src_b + LANES, r0:r0 + TILE].T
            w_s[r0:r0 + TILE, C_GDN_B:C_GDN_B + LANES] = jnp.where(lane < n_gate, win, 0.0).astype(BF16)
            w_s[r0:r0 + TILE, C_GDN_A:C_GDN_A + LANES] = jnp.where(
                lane < n_gate, pltpu.roll(win, LANES - n_gate, 1), 0.0).astype(BF16)

    i = pl.program_id(1)
    prev_ok, next_ok = _halo_ok(i, n_tiles)
    x = jnp.concatenate([xp_ref[0], _stream_tile(c_ref, x_ref, i), xn_ref[0]], axis=0)
    ms = jnp.mean(x * x, axis=-1, keepdims=True)
    y = x * lax.rsqrt(ms + EPS) * g_ref[...]
    shift = mod_ref[0, 0, 0:1, :]
    scale = mod_ref[0, 0, 1:2, :]
    h_ext = y * (1.0 + scale) + shift
    hb = h_ext[HALO:HALO + TILE].astype(BF16)
    hb_ext = h_ext.astype(BF16)
    cos = cos_ref[...]
    sin = sin_ref[...]

    def gdn_group(c0, col, o_r, normalise, scale_by):
        res = jnp.dot(hb_ext, w_s[:, c0:c0 + MIX], preferred_element_type=F32)
        yield
        ext = jnp.concatenate([res[:HALO] * prev_ok, res[HALO:HALO + TILE], res[HALO + TILE:] * next_ok], axis=0)
        xh = _conv4(ext, cw_ref[:, col:col + MIX])
        act = xh * (1.0 + jnp.tanh(xh))
        if normalise:
            act = _l2norm_heads(act)
        o_r[0] = act * scale_by if scale_by != 1.0 else act
        yield

    def plain_group(c0):
        res = jnp.dot(hb, w_s[:, c0:c0 + MIX], preferred_element_type=F32)
        yield
        if c0 == C_ATT_Q:
            res = jnp.concatenate(
                [_rope(res[:, k:k + LANES], cos, sin) for k in range(0, MIX, LANES)], axis=1)
        if c0 == C_ATT_K:
            res = jnp.concatenate([_rope(res[:, :LANES], cos, sin), res[:, LANES:]], axis=1)
        if c0 == C_LRU_X:
            lx_ref[0] = _scan_order(res)
        else:
            o_ref[0, :, c0:c0 + MIX] = res
        yield

    groups = [gdn_group(C_GDN_Q, 0, gq_ref, True, GDN_HEAD_DIM ** -0.5),
              gdn_group(C_GDN_K, MIX, gk_ref, True, 1.0),
              gdn_group(C_GDN_V, 2 * MIX, gv_ref, False, 1.0)]
    groups += [plain_group(c0) for c0 in list(range(0, N_P, MIX)) + [C_LRU_X]]
    next(groups[0])
    for k, g in enumerate(groups):
        if k + 1 < len(groups):
            next(groups[k + 1])
        next(g)


def _inproj(c_src, x_src, off, modsel, g, cos_t, sin_t, conv_w, w_in, layer):
    b, _, d = x_src.shape
    s = x_src.shape[1] + off * TILE
    nt = s // TILE
    hpt = TILE // HALO
    n_halo = x_src.shape[1] // HALO
    halo_specs = [
        pl.BlockSpec((1, HALO, d), lambda bi, i: (bi, jnp.clip((i - off) * hpt - 1, 0, n_halo - 1), 0)),
        pl.BlockSpec((1, HALO, d), lambda bi, i: (bi, jnp.clip((i - off + 1) * hpt, 0, n_halo - 1), 0))]
    tok = pl.BlockSpec((1, TILE, MIX), lambda bi, i: (bi, i, 0))
    return pl.pallas_call(
        functools.partial(_inproj_kernel, n_tiles=nt),
        grid=(b, nt),
        in_specs=_stream_specs(d, off) + halo_specs + [
                  pl.BlockSpec((1, 1, 8, d), lambda bi, i: (bi, jnp.minimum(i, 1), 0, 0)),
                  pl.BlockSpec((1, d), lambda bi, i: (0, 0)),
                  pl.BlockSpec((TILE, LANES), lambda bi, i: (i, 0)),
                  pl.BlockSpec((TILE, LANES), lambda bi, i: (i, 0)),
                  pl.BlockSpec((4, 3 * MIX), lambda bi, i: (0, 0)),
                  pl.BlockSpec((1, IN_WIDTH, d), lambda bi, i: (layer, 0, 0), pipeline_mode=pl.Buffered(1))],
        out_specs=[pl.BlockSpec((1, TILE, N_P), lambda bi, i: (bi, i, 0)), tok, tok, tok, tok],
        out_shape=[jax.ShapeDtypeStruct((b, s, N_P), F32)] + [jax.ShapeDtypeStruct((b, s, MIX), F32)] * 4,
        scratch_shapes=[pltpu.VMEM((d, N_IN), BF16)],
        compiler_params=_cparams("arbitrary", "arbitrary"),
        name="inproj",
    )(c_src, x_src, x_src, x_src, modsel, g, cos_t, sin_t, conv_w, w_in)


def _att_kernel(sink_ref, q_ref, z_ref, kvp_ref, kvc_ref, kvn_ref, kvx_ref, o_ref, *, n_blocks):
    i = pl.program_id(1)
    n_sub = TILE // ATT_BLOCK
    n_band = 3 * ATT_BLOCK
    half = ATT_HEAD_DIM
    group = ATT_HEADS // ATT_KV_HEADS

    kv = jnp.concatenate([kvp_ref[0], kvc_ref[0], kvn_ref[0], kvx_ref[0]], axis=0)
    nk = kv.shape[0]
    k_all, v_all = kv[:, :LANES], kv[:, LANES:]
    k_sw, v_sw = pltpu.roll(k_all, half, 1), pltpu.roll(v_all, half, 1)
    lo_k = lax.broadcasted_iota(jnp.int32, (nk, LANES), 1) < half
    k_dup = [jnp.where(lo_k, k_all, k_sw).astype(BF16), jnp.where(lo_k, k_sw, k_all).astype(BF16)]
    v_dup = [jnp.where(lo_k, v_all, v_sw).astype(BF16), jnp.where(lo_k, v_sw, v_all).astype(BF16)]
    n_loc = (n_sub + 2) * ATT_BLOCK

    def keys(t, sub):
        return jnp.concatenate([t[sub * ATT_BLOCK:sub * ATT_BLOCK + n_band], t[n_loc:]], axis=0)

    r = lax.broadcasted_iota(jnp.int32, (ATT_BLOCK, n_band), 0)
    c = lax.broadcasted_iota(jnp.int32, (ATT_BLOCK, n_band), 1)
    in_win = jnp.abs(r - (c - ATT_BLOCK)) <= ATT_BLOCK
    lo_q = lax.broadcasted_iota(jnp.int32, (ATT_BLOCK, LANES), 1) < half
    outs = [[None] * (ATT_HEADS // 2) for _ in range(n_sub)]

    def band_bias(sub):
        m = i * n_sub + sub
        c_lo = jnp.where(m >= n_sub + 1, 0, jnp.where(m >= n_sub, ATT_BLOCK, n_band))
        c_hi = jnp.where(m < n_sub, 0, jnp.where(m <= n_blocks - 2, n_band, 2 * ATT_BLOCK))
        return jnp.where(jnp.logical_and(in_win, jnp.logical_and(c >= c_lo, c < c_hi)), 0.0, NEG_INF)

    def stream(sub, h, bias):
        q = q_ref[0, sub * ATT_BLOCK:(sub + 1) * ATT_BLOCK, :] * (ATT_HEAD_DIM ** -0.5 * LOG2E)
        lhs = []
        for j in range(h * group, (h + 1) * group):
            q_t = q[:, (j // 2) * LANES:(j // 2 + 1) * LANES]
            lhs.append(jnp.where(lo_q, q_t, 0.0) if j % 2 == 0 else jnp.where(lo_q, 0.0, q_t))
        qs = jnp.concatenate(lhs, axis=0).astype(BF16)
        s = lax.dot_general(qs, keys(k_dup[h], sub), (((1,), (1,)), ((), ())), preferred_element_type=F32)
        yield
        p_blocks, den_blocks = [], []
        for e in range(group):
            sb = s[e * ATT_BLOCK:(e + 1) * ATT_BLOCK]
            sk = sink_ref[h * group + e] * LOG2E
            s_band = sb[:, :n_band] + bias
            s_ctx = sb[:, n_band:]
            mx = jnp.maximum(jnp.maximum(jnp.max(s_band, axis=-1, keepdims=True),
                                         jnp.max(s_ctx, axis=-1, keepdims=True)), sk)
            p_band = jnp.exp2(s_band - mx)
            p_ctx = jnp.exp2(s_ctx - mx)
            den_blocks.append(jnp.sum(p_band, axis=-1, keepdims=True) + jnp.sum(p_ctx, axis=-1, keepdims=True)
                              + jnp.exp2(sk - mx))
            p_blocks.append(jnp.concatenate([p_band, p_ctx], axis=1).astype(BF16))
            yield
        p = jnp.concatenate(p_blocks, axis=0)
        res = (jnp.dot(p, keys(v_dup[h], sub), preferred_element_type=F32)
               / jnp.concatenate(den_blocks, axis=0))
        for t in range(group // 2):
            even = res[(2 * t) * ATT_BLOCK:(2 * t + 1) * ATT_BLOCK]
            odd = res[(2 * t + 1) * ATT_BLOCK:(2 * t + 2) * ATT_BLOCK]
            outs[sub][h * (group // 2) + t] = jnp.where(lo_q, even, odd)
        yield

    for sub in range(n_sub):
        bias = band_bias(sub)
        gens = [stream(sub, h, bias) for h in range(ATT_KV_HEADS)]
        for g in gens:
            next(g)
        for g in gens:
            for _ in g:
                pass
    for sub in range(n_sub):
        rows = slice(sub * ATT_BLOCK, (sub + 1) * ATT_BLOCK)
        o = jnp.concatenate(outs[sub], axis=1)
        o_ref[0, rows, :] = (o * _silu(z_ref[0, rows, :])).astype(o_ref.dtype)


def _attention(p, sink):
    b, s, _ = p.shape
    nb = s // ATT_BLOCK
    n_sub = TILE // ATT_BLOCK
    kvcol = C_ATT_K // (2 * LANES)
    kv_spec = lambda f: pl.BlockSpec((1, ATT_BLOCK, 2 * LANES), f)
    return pl.pallas_call(
        functools.partial(_att_kernel, n_blocks=nb),
        grid=(b, s // TILE),
        in_specs=[pl.BlockSpec(memory_space=pltpu.SMEM),
                  pl.BlockSpec((1, TILE, MIX), lambda bi, i: (bi, i, C_ATT_Q // MIX)),
                  pl.BlockSpec((1, TILE, MIX), lambda bi, i: (bi, i, C_ATT_Z // MIX)),
                  kv_spec(lambda bi, i: (bi, jnp.maximum(i * n_sub - 1, n_sub), kvcol)),
                  pl.BlockSpec((1, TILE, 2 * LANES), lambda bi, i: (bi, i, kvcol)),
                  kv_spec(lambda bi, i: (bi, jnp.minimum((i + 1) * n_sub, nb - 1), kvcol)),
                  pl.BlockSpec((1, TILE, 2 * LANES), lambda bi, i: (bi, 0, kvcol))],
        out_specs=pl.BlockSpec((1, TILE, MIX), lambda bi, i: (bi, i, 0)),
        out_shape=jax.ShapeDtypeStruct((b, s, MIX), BF16),
        compiler_params=_cparams("parallel", "parallel"),
        name="attention",
    )(sink, p, p, p, p, p, p)


def _chunk_cumsum(g, reverse):
    n = g.shape[0]
    row = lax.broadcasted_iota(jnp.int32, g.shape, 0) % GDN_CHUNK
    s = 1
    while s < GDN_CHUNK:
        if reverse:
            g = g + jnp.where(row < GDN_CHUNK - s, pltpu.roll(g, n - s, 0), 0.0)
        else:
            g = g + jnp.where(row >= s, pltpu.roll(g, s, 0), 0.0)
        s *= 2
    return g


def _spread(cols, masks):
    out = cols[-1]
    for m, col in zip(masks[-2::-1], cols[-2::-1]):
        out = jnp.where(m, col, out)
    return out


def _head_block_diag(x, masks):
    return jnp.concatenate([jnp.where(m, x, 0.0) for m in masks], axis=0).astype(BF16)


def _gdn_prep_kernel(q_ref, k_ref, v_ref, b_ref, a_ref, alog_ref, dtb_ref,
                     w_ref, u_ref, kd_ref, qg_ref, aqk_ref, gl_ref):
    hd, ch, nh = GDN_HEAD_DIM, GDN_CHUNK, GDN_HEADS
    nch = TILE // ch
    q, k, v = q_ref[0], k_ref[0], v_ref[0]

    lane = lax.broadcasted_iota(jnp.int32, (TILE, LANES), 1)
    g = -jnp.exp(alog_ref[...]) * _softplus(a_ref[0] + dtb_ref[...])
    gcum = jnp.where(lane < nh, _chunk_cumsum(g, False), _chunk_cumsum(g, True))
    beta = _sigmoid(b_ref[0])

    ii = lax.broadcasted_iota(jnp.int32, (ch, PACK), 0)
    ll = lax.broadcasted_iota(jnp.int32, (ch, PACK), 1)
    jj = ll % ch
    eye_p = jj == ii
    eye_f = eye_p.astype(F32)
    bd16 = (jj // 16) == (ii // 16)
    bd32 = (jj // 32) == (ii // 32)
    merge_masks = (jnp.logical_and(bd32, jnp.logical_not(bd16)), jnp.logical_not(bd32))
    strict = [jj < ii, jj > ii]
    incl = [jj <= ii, jj >= ii]
    head_p = [(ll // ch) == h for h in range(nh)]
    lw = lax.broadcasted_iota(jnp.int32, (ch, MIX), 1) // hd
    head_w = [lw == h for h in range(nh)]
    lane_row = lax.broadcasted_iota(jnp.int32, (1, LANES), 1)

    def pdot(a, b):
        return jnp.dot(a.astype(BF16), _head_block_diag(b, head_p), preferred_element_type=F32)

    gl_rows = []
    units = []
    for cidx in range(nch):
        rows = slice(cidx * ch, (cidx + 1) * ch)
        kc, qc, vc, gc, bc = k[rows], q[rows], v[rows], gcum[rows], beta[rows]
        kq = _dot_nt(jnp.concatenate([kc, qc], axis=0), _head_block_diag(kc, head_w))
        g_end = jnp.where(lane_row < nh, gc[ch - 1:ch, :], gc[0:1, :])
        gl_rows.append(jnp.exp(g_end))
        for d in range(N_DIR):
            gcols = [gc[:, d * nh + h:d * nh + h + 1] for h in range(nh)]
            bcols = [bc[:, d * nh + h:d * nh + h + 1] for h in range(nh)]
            gcol_p = _spread(gcols, head_p)
            grow_p = jnp.sum(jnp.where(eye_p, gcol_p, 0.0), axis=0, keepdims=True)
            dec = jnp.exp(jnp.where(incl[d], gcol_p - grow_p, 0.0))
            lm = jnp.where(strict[d], _spread(bcols, head_p) * dec * kq[:ch], 0.0)
            aqk = jnp.where(incl[d], dec * kq[ch:], 0.0)
            aqk_ref[0, rows, d * PACK:(d + 1) * PACK] = aqk.astype(BF16)
            units.append((rows, d, lm, kc, qc, vc, gcols, bcols,
                          [g_end[:, d * nh + h:d * nh + h + 1] for h in range(nh)]))
    gl_rows.append(jnp.zeros((8 - nch, LANES), F32))
    gl_ref[0, 0] = jnp.concatenate(gl_rows, axis=0)

    lms = [u[2] for u in units]
    m = [jnp.where(bd16, -lm, 0.0) for lm in lms]
    x = [eye_f + mi for mi in m]
    pw = [pdot(mi, mi) for mi in m]
    for _ in range(2):
        r = [pdot(jnp.concatenate([xi, pi], axis=0), pi) for xi, pi in zip(x, pw)]
        x = [xi + ri[:ch] for xi, ri in zip(x, r)]
        pw = [ri[ch:] for ri in r]
    x = [xi + pdot(xi, pi) for xi, pi in zip(x, pw)]
    for mask in merge_masks:
        y = [pdot(xi, jnp.where(mask, lm, 0.0)) for xi, lm in zip(x, lms)]
        x = [xi - pdot(yi, xi) for xi, yi in zip(x, y)]

    for tinv, (rows, d, _, kc, qc, vc, gcols, bcols, gend_cols) in zip(x, units):
        slab = lambda t, h: t[:, h * hd:(h + 1) * hd]
        gam = [jnp.exp(gcol) for gcol in gcols]
        bgk = jnp.concatenate([(bcols[h] * gam[h]) * slab(kc, h) for h in range(nh)], axis=1)
        bv = jnp.concatenate([bcols[h] * slab(vc, h) for h in range(nh)], axis=1)
        rhs = jnp.concatenate([_head_block_diag(bgk, head_w), _head_block_diag(bv, head_w)], axis=1)
        wu = jnp.dot(tinv.astype(BF16), rhs, preferred_element_type=F32)
        cols = slice(d * MIX, (d + 1) * MIX)
        w_ref[0, rows, cols] = wu[:, :MIX].astype(BF16)
        u_ref[0, rows, cols] = wu[:, MIX:]
        kd_ref[0, rows, cols] = jnp.concatenate(
            [jnp.exp(gend_cols[h] - gcols[h]) * slab(kc, h) for h in range(nh)], axis=1).astype(BF16)
        qg_ref[0, rows, cols] = jnp.concatenate([gam[h] * slab(qc, h) for h in range(nh)], axis=1).astype(BF16)


def _gdn_prep(p, gq, gk, gv, alog, dtb):
    b, s, _ = p.shape
    nt = s // TILE
    wide = N_DIR * GDN_HEADS * GDN_HEAD_DIM
    in_specs = [pl.BlockSpec((1, TILE, MIX), lambda bi, i: (bi, i, 0))] * 3
    in_specs += [pl.BlockSpec((1, TILE, LANES), lambda bi, i: (bi, i, C_GDN_B // LANES)),
                 pl.BlockSpec((1, TILE, LANES), lambda bi, i: (bi, i, C_GDN_A // LANES)),
                 pl.BlockSpec((1, LANES), lambda bi, i: (0, 0)),
                 pl.BlockSpec((1, LANES), lambda bi, i: (0, 0))]
    wide_spec = pl.BlockSpec((1, TILE, wide), lambda bi, i: (bi, i, 0))
    out_specs = [wide_spec, wide_spec, wide_spec, wide_spec,
                 pl.BlockSpec((1, TILE, N_DIR * PACK), lambda bi, i: (bi, i, 0)),
                 pl.BlockSpec((1, 1, 8, LANES), lambda bi, i: (bi, i, 0, 0))]
    out_shape = [jax.ShapeDtypeStruct((b, s, wide), BF16),
                 jax.ShapeDtypeStruct((b, s, wide), F32),
                 jax.ShapeDtypeStruct((b, s, wide), BF16),
                 jax.ShapeDtypeStruct((b, s, wide), BF16),
                 jax.ShapeDtypeStruct((b, s, N_DIR * PACK), BF16),
                 jax.ShapeDtypeStruct((b, nt, 8, LANES), F32)]
    return pl.pallas_call(
        _gdn_prep_kernel,
        grid=(b, nt),
        in_specs=in_specs,
        out_specs=out_specs,
        out_shape=out_shape,
        compiler_params=_cparams("parallel", "parallel"),
        name="gdn_prep",
    )(gq, gk, gv, p, p, alog, dtb)


def _gdn_scan_body(wf, uf, kdf, qgf, af, glf, wb, ub, kdb, qgb, ab, glb, of_ref, ob_ref, state):
    hd, ch, nh = GDN_HEAD_DIM, GDN_CHUNK, GDN_HEADS
    nch = TILE // ch
    lw = lax.broadcasted_iota(jnp.int32, (ch, MIX), 1) // hd
    head_w = [lw == h for h in range(nh)]
    dirs = ((wf, uf, kdf, qgf, af, glf, of_ref), (wb, ub, kdb, qgb, ab, glb, ob_ref))
    for step in range(nch):
        cidx = [step, nch - 1 - step]
        rows = [slice(c * ch, (c + 1) * ch) for c in cidx]
        s_old = {}
        ws = {}
        for d, (w_r, _, _, qg_r, _, _, _) in enumerate(dirs):
            for h in range(nh):
                cols = slice(h * hd, (h + 1) * hd)
                s_old[d, h] = state[d, h]
                lhs = jnp.concatenate([w_r[0, rows[d], cols], qg_r[0, rows[d], cols]], axis=0)
                ws[d, h] = jnp.dot(lhs, s_old[d, h].astype(BF16), preferred_element_type=F32)
            yield
        for d, (_, u_r, kd_r, _, a_r, gl_r, o_r) in enumerate(dirs):
            un = u_r[0, rows[d], :] - jnp.concatenate([ws[d, h][:ch] for h in range(nh)], axis=1)
            inter = jnp.concatenate([ws[d, h][ch:] for h in range(nh)], axis=1)
            intra = jnp.dot(a_r[0, rows[d], :], _head_block_diag(un, head_w), preferred_element_type=F32)
            o_r[0, rows[d], :] = (inter + intra).astype(o_r.dtype)
            unb = un.astype(BF16)
            for h in range(nh):
                cols = slice(h * hd, (h + 1) * hd)
                gl = gl_r[0, 0, cidx[d]:cidx[d] + 1, d * nh + h:d * nh + h + 1]
                state[d, h] = gl * s_old[d, h] + _dot_tn(kd_r[0, rows[d], cols], unb[:, cols])
            yield


def _mirror(j, nt):
    return jnp.where(j == 0, 0, nt - j)


def _lru_body(xf_ref, xfp2_ref, xfp_ref, xfn_ref, xb_ref, xbp2_ref, xbp_ref, xbn_ref, cw_ref, cb_ref, wg_ref, bg_ref,
              lam_ref, hf_ref, hb_ref, carry, *, n_tiles):
    j = pl.program_id(1)
    seg_n = SEG
    steps = STEPS
    sub = lax.broadcasted_iota(jnp.int32, (seg_n, LRU_WIDTH), 0)

    def one_dir(d, x_ref, p2_ref, p_ref, n_ref, tile_idx, o_ref):
        prev_ok, next_ok = _halo_ok(tile_idx, n_tiles)
        xs = [x_ref[0, st * seg_n:(st + 1) * seg_n, :] for st in range(steps)]
        last = seg_n - 1
        p1 = jnp.where(sub == 0, p_ref[0, last:last + 1, :] * prev_ok, pltpu.roll(xs[steps - 1], 1, 0))
        p2 = jnp.where(sub == 0, p2_ref[0, last:last + 1, :] * prev_ok, pltpu.roll(xs[steps - 2], 1, 0))
        n1 = jnp.where(sub == last, n_ref[0, 0:1, :] * next_ok, pltpu.roll(xs[0], last, 0))
        ext = [p2, p1] + xs + [n1]
        cw = cw_ref[...]
        xh = jnp.concatenate(
            [cw[0:1] * ext[st] + cw[1:2] * ext[st + 1] + cw[2:3] * ext[st + 2] + cw[3:4] * ext[st + 3]
             for st in range(steps)], axis=0) + cb_ref[...]
        yield
        gates = jnp.dot(xh.astype(BF16), wg_ref[:, d * 2 * LRU_WIDTH:(d + 1) * 2 * LRU_WIDTH],
                        preferred_element_type=F32)
        yield
        c_half = (-0.5 * LRU_C) * _softplus(-lam_ref[d:d + 1, :])
        a_blocks, b_blocks = [], []
        blk = TILE // 4
        for r0 in range(0, TILE, blk):
            gb = gates[r0:r0 + blk]
            tr = jnp.tanh(gb[:, :LRU_WIDTH] + bg_ref[2 * d:2 * d + 1, :])
            ti = jnp.tanh(gb[:, LRU_WIDTH:] + bg_ref[2 * d + 1:2 * d + 2, :])
            log_a = c_half * tr + c_half
            ab = jnp.exp(log_a)
            b_blocks.append(jnp.sqrt(-jnp.tanh(log_a) * (ab * ab + 1.0)) * ((ti + 1.0) * xh[r0:r0 + blk]))
            a_blocks.append(ab)
            yield
        a = jnp.concatenate(a_blocks, axis=0)
        bb = jnp.concatenate(b_blocks, axis=0)

        order = list(range(steps)) if d == 0 else list(range(steps - 1, -1, -1))
        piece = lambda t, st: t[st * seg_n:(st + 1) * seg_n]
        a_tot = piece(a, order[0])
        b_tot = piece(bb, order[0])
        for n, st in enumerate(order[1:]):
            b_tot = piece(a, st) * b_tot + piece(bb, st)
            a_tot = piece(a, st) * a_tot
            if n % 8 == 7:
                yield
        s = 1
        while s < seg_n:
            if d == 0:
                keep = sub >= s
                a_sh = jnp.where(keep, pltpu.roll(a_tot, s, 0), 1.0)
                b_sh = jnp.where(keep, pltpu.roll(b_tot, s, 0), 0.0)
            else:
                keep = sub < seg_n - s
                a_sh = jnp.where(keep, pltpu.roll(a_tot, seg_n - s, 0), 1.0)
                b_sh = jnp.where(keep, pltpu.roll(b_tot, seg_n - s, 0), 0.0)
            b_tot = a_tot * b_sh + b_tot
            a_tot = a_tot * a_sh
            s *= 2
        h_in0 = carry[d]
        h_out = a_tot * h_in0 + b_tot
        if d == 0:
            h = jnp.where(sub == 0, h_in0, pltpu.roll(h_out, 1, 0))
            carry[d] = h_out[seg_n - 1:seg_n]
        else:
            h = jnp.where(sub == seg_n - 1, h_in0, pltpu.roll(h_out, seg_n - 1, 0))
            carry[d] = h_out[0:1]
        hs = [None] * steps
        for n, st in enumerate(order):
            h = piece(a, st) * h + piece(bb, st)
            hs[st] = h
            if n % 8 == 7:
                yield
        o_ref[0] = jnp.concatenate(hs, axis=0).astype(o_ref.dtype)

    return [one_dir(0, xf_ref, xfp2_ref, xfp_ref, xfn_ref, j, hf_ref),
            one_dir(1, xb_ref, xbp2_ref, xbp_ref, xbn_ref, _mirror(j, n_tiles), hb_ref)]


N_SCAN_IN = 12
N_LRU_IN = 13


def _round_robin(gens, weights):
    live = list(zip(gens, weights))
    while live:
        nxt = []
        for g, w in live:
            alive = True
            for _ in range(w):
                try:
                    next(g)
                except StopIteration:
                    alive = False
                    break
            if alive:
                nxt.append((g, w))
        live = nxt


def _sweep_kernel(*refs, n_tiles):
    scan_in, lru_in = refs[:N_SCAN_IN], refs[N_SCAN_IN:N_SCAN_IN + N_LRU_IN]
    of_ref, ob_ref, hf_ref, hb_ref, state, carry = refs[N_SCAN_IN + N_LRU_IN:]

    @pl.when(pl.program_id(1) == 0)
    def _():
        state[...] = jnp.zeros_like(state)
        carry[...] = jnp.zeros_like(carry)

    _round_robin([_gdn_scan_body(*scan_in, of_ref, ob_ref, state)]
                 + _lru_body(*lru_in, hf_ref, hb_ref, carry, n_tiles=n_tiles), (1, 2, 2))


def _sweep(w, u, kd, qg, aqk, gl, lx, conv_w, conv_b, wg, bg, lam):
    b, s, _ = u.shape
    nt = s // TILE
    half = GDN_HEADS * GDN_HEAD_DIM
    hpt = TILE // HALO
    nh = s // HALO
    tile_of = (lambda j: j, lambda j: _mirror(j, nt))

    def scan_specs(d):
        idx = lambda bi, j: (bi, tile_of[d](j), d)
        gidx = lambda bi, j: (bi, tile_of[d](j), 0, 0)
        return [pl.BlockSpec((1, TILE, half), idx)] * 4 + [pl.BlockSpec((1, TILE, PACK), idx),
                                                           pl.BlockSpec((1, 1, 8, LANES), gidx)]

    def lru_specs(d):
        t = tile_of[d]
        return [pl.BlockSpec((1, TILE, MIX), lambda bi, j: (bi, t(j), 0)),
                pl.BlockSpec((1, HALO, MIX), lambda bi, j: (bi, jnp.maximum(t(j) * hpt - 2, 0), 0)),
                pl.BlockSpec((1, HALO, MIX), lambda bi, j: (bi, jnp.maximum(t(j) * hpt - 1, 0), 0)),
                pl.BlockSpec((1, HALO, MIX), lambda bi, j: (bi, jnp.minimum((t(j) + 1) * hpt, nh - 1), 0))]

    full = lambda shape: pl.BlockSpec(shape, lambda bi, j: tuple(0 for _ in shape))
    out_spec = lambda d: pl.BlockSpec((1, TILE, MIX), lambda bi, j: (bi, tile_of[d](j), 0))
    in_specs = (scan_specs(0) + scan_specs(1) + lru_specs(0) + lru_specs(1)
                + [full((4, LRU_WIDTH)), full((1, LRU_WIDTH)), full((LRU_WIDTH, 4 * LRU_WIDTH)),
                   full((4, LRU_WIDTH)), full((N_DIR, LRU_WIDTH))])
    assert len(in_specs) == N_SCAN_IN + N_LRU_IN
    return pl.pallas_call(
        functools.partial(_sweep_kernel, n_tiles=nt),
        grid=(b, nt),
        in_specs=in_specs,
        out_specs=[out_spec(0), out_spec(1), out_spec(0), out_spec(1)],
        out_shape=[jax.ShapeDtypeStruct((b, s, MIX), BF16)] * 4,
        scratch_shapes=[pltpu.VMEM((N_DIR, GDN_HEADS, GDN_HEAD_DIM, GDN_HEAD_DIM), F32),
                        pltpu.VMEM((N_DIR, 1, LRU_WIDTH), F32)],
        compiler_params=_cparams("parallel", "arbitrary"),
        name="sweep",
    )(w, u, kd, qg, aqk, gl, w, u, kd, qg, aqk, gl, *([lx] * 8), conv_w, conv_b, wg, bg, lam)


def _outproj_kernel(c_ref, x_ref, mod_ref, att_ref, of_ref, ob_ref, gz_ref, hf_ref, hb_ref, lz_ref,
                    gn_ref, w_ref, fg_ref, o_ref, *, final, first_tile):
    hd = GDN_HEAD_DIM
    o = of_ref[0].astype(F32) + ob_ref[0].astype(F32)
    parts = []
    for h in range(GDN_HEADS):
        oh = o[:, h * hd:(h + 1) * hd]
        parts.append(oh * lax.rsqrt(jnp.mean(oh * oh, axis=-1, keepdims=True) + EPS) * gn_ref[...])
    gdn = jnp.concatenate(parts, axis=1) * _silu(gz_ref[0])
    lru = _time_order(hf_ref[0].astype(F32) + hb_ref[0].astype(F32)) * _silu(lz_ref[0])
    acc = (jnp.dot(att_ref[0], w_ref[0:MIX, :], preferred_element_type=F32)
           + jnp.dot(gdn.astype(BF16), w_ref[MIX:2 * MIX, :], preferred_element_type=F32)
           + jnp.dot(lru.astype(BF16), w_ref[2 * MIX:3 * MIX, :], preferred_element_type=F32))
    gate = mod_ref[0, 0, 2:3, :]
    xn = _stream_tile(c_ref, x_ref, pl.program_id(1) + first_tile) + gate * acc
    if final:
        ms = jnp.mean(xn * xn, axis=-1, keepdims=True)
        xn = xn * lax.rsqrt(ms + EPS) * fg_ref[...]
    o_ref[0] = xn


def _outproj(c_src, x_src, src_off, modsel, att, of, ob, p, hf, hb, gdn_norm, w_out, final_g, final):
    b, s, _ = p.shape
    d = x_src.shape[2]
    nt = s // TILE
    off = 1 if final else 0
    n_out = nt - off
    tok = lambda col: pl.BlockSpec((1, TILE, MIX), lambda bi, i: (bi, i + off, col))
    return pl.pallas_call(
        functools.partial(_outproj_kernel, final=final, first_tile=off),
        grid=(b, n_out),
        in_specs=_stream_specs(d, src_off, off) + [
                  pl.BlockSpec((1, 1, 8, d), lambda bi, i: (bi, jnp.minimum(i + off, 1), 0, 0)),
                  tok(0), tok(0), tok(0), tok(C_GDN_Z // MIX), tok(0), tok(0), tok(C_LRU_Z // MIX),
                  pl.BlockSpec((1, GDN_HEAD_DIM), lambda bi, i: (0, 0)),
                  pl.BlockSpec((3 * MIX, d), lambda bi, i: (0, 0)),
                  pl.BlockSpec((1, d), lambda bi, i: (0, 0))],
        out_specs=pl.BlockSpec((1, TILE, d), lambda bi, i: (bi, i, 0)),
        out_shape=jax.ShapeDtypeStruct((b, n_out * TILE, d), F32),
        compiler_params=_cparams("parallel", "parallel"),
        name="outproj",
    )(c_src, x_src, modsel, att, of, ob, p, hf, hb, p, gdn_norm, w_out, final_g)


def _rope_tables(s, ctx_len):
    t = np.arange(s - ctx_len)
    n_freq = ATT_HEAD_DIM // 4
    inv = ROPE_BASE ** (-np.arange(n_freq, dtype=np.float64) / n_freq)
    ang = [(t // GRID_W)[:, None] * inv, (t % GRID_W)[:, None] * inv]
    cos = np.concatenate([np.cos(ang[0]), np.cos(ang[0]), np.cos(ang[1]), np.cos(ang[1])], axis=1)
    sin = np.concatenate([-np.sin(ang[0]), np.sin(ang[0]), -np.sin(ang[1]), np.sin(ang[1])], axis=1)
    cos = np.concatenate([np.ones((ctx_len, ATT_HEAD_DIM)), cos], axis=0)
    sin = np.concatenate([np.zeros((ctx_len, ATT_HEAD_DIM)), sin], axis=0)
    return (jnp.asarray(np.tile(cos, (1, 2)), F32), jnp.asarray(np.tile(sin, (1, 2)), F32))


def _block_diag(w):
    eye = jnp.eye(LRU_BLOCKS, dtype=w.dtype)
    return jnp.einsum("nde,nm->ndme", w, eye).reshape(LRU_WIDTH, LRU_WIDTH)


def _pad_lanes(v):
    return jnp.pad(v.reshape(1, -1), ((0, 0), (0, LANES - v.size)))


def kernel(x, c, ctx, c_ctx, norm_g, w_mod, b_mod, w_in, att_sink, gdn_conv, gdn_a_log, gdn_dt_bias, gdn_norm,
           lru_conv_w, lru_conv_b, lru_w_r, lru_b_r, lru_w_i, lru_b_i, lru_lambda, w_out, final_g):
    b, t, d = x.shape
    ctx_len = ctx.shape[1]
    depth = w_in.shape[0]
    assert ctx_len == TILE and t % TILE == 0 and d == D_MODEL and b == 2
    s = ctx_len + t
    nt = s // TILE

    cos_t, sin_t = _rope_tables(s, ctx_len)
    ct = jnp.pad(jnp.concatenate([c, c_ctx[None]], axis=0).T, ((0, 0), (0, LANES - b - 1)))
    mods = _modulation(ct, w_mod, b_mod)

    w_in_t = jnp.swapaxes(w_in, 1, 2)
    c_src, x_src, src_off = ctx, x, 1
    out = None
    for l in range(depth):
        m3 = mods[l, :3].reshape(3, 3, d)
        sel = jnp.stack([jnp.stack([m3[b], m3[bi]], axis=0) for bi in range(b)], axis=0)
        modsel = jnp.pad(sel, ((0, 0), (0, 0), (0, 5), (0, 0)))
        p, lx, gq, gk, gv = _inproj(c_src, x_src, src_off, modsel, norm_g[l].reshape(1, d), cos_t, sin_t,
                                    0.5 * gdn_conv[l], w_in_t, l)

        att = _attention(p, att_sink[l])

        prep = _gdn_prep(p, gq, gk, gv, _pad_lanes(gdn_a_log[l]), _pad_lanes(gdn_dt_bias[l]))

        wg = jnp.concatenate([_block_diag(lru_w_r[l, 0]), _block_diag(lru_w_i[l, 0]),
                              _block_diag(lru_w_r[l, 1]), _block_diag(lru_w_i[l, 1])], axis=1).astype(BF16)
        bg = 0.5 * jnp.stack([lru_b_r[l, 0], lru_b_i[l, 0], lru_b_r[l, 1], lru_b_i[l, 1]], axis=0)
        o_f, o_b, h_f, h_b = _sweep(*prep, lx, 0.5 * lru_conv_w[l], 0.5 * lru_conv_b[l].reshape(1, -1), wg, bg,
                                    lru_lambda[l])

        final = l == depth - 1
        res = _outproj(c_src, x_src, src_off, modsel, att, o_f, o_b, p, h_f, h_b, gdn_norm[l].reshape(1, -1),
                       w_out[l].astype(BF16), final_g.reshape(1, d), final)
        if final:
            out = res
        else:
            c_src, x_src, src_off = res, res, 0
    return out
```

```python
import functools
import math

import numpy as np
import jax
import jax.numpy as jnp
from jax import lax
from jax.experimental import pallas as pl
from jax.experimental.pallas import tpu as pltpu

F32 = jnp.float32
BF16 = jnp.bfloat16

EPS = 1e-6
NEG_INF = -1e30
LOG2E = math.log2(math.e)
GRID_W = 64
ROPE_BASE = 10000.0

D_MODEL = 1024
ATT_HEADS = 8
ATT_KV_HEADS = 2
ATT_HEAD_DIM = 64
ATT_BLOCK = 128
GDN_HEADS = 4
GDN_HEAD_DIM = 128
GDN_CHUNK = 64
LRU_WIDTH = 512
LRU_BLOCKS = 8
LRU_BLOCK_DIM = 64
LRU_C = 8.0
N_DIR = 2
MIX = 512
PACK = GDN_HEADS * GDN_CHUNK

TILE = 256
HALO = 8
LANES = 128
VMEM_LIMIT = 56 * 1024 * 1024

C_ATT_Q, C_GDN_Q, C_GDN_K, C_GDN_V = (i * MIX for i in range(4))
C_ATT_K = 4 * MIX
C_ATT_V = C_ATT_K + LANES
C_GDN_B = C_ATT_V + LANES
C_GDN_A = C_GDN_B + LANES
N_P = C_GDN_A + LANES
C_ATT_Z, C_GDN_Z, C_LRU_Z = (N_P + i * MIX for i in range(3))
N_Z = 3 * MIX
C_LRU_X = N_P + N_Z
N_IN = C_LRU_X + MIX
SEG = 8
STEPS = TILE // SEG


def _cparams(*sem):
    return pltpu.CompilerParams(dimension_semantics=sem, vmem_limit_bytes=VMEM_LIMIT)


def _sigmoid(x):
    return 1.0 / (1.0 + jnp.exp(-x))


def _silu(x):
    return x * _sigmoid(x)


def _softplus(x):
    return jnp.maximum(x, 0.0) + jnp.log1p(jnp.exp(-jnp.abs(x)))


def _dot_nt(a, b):
    return lax.dot_general(a.astype(BF16), b.astype(BF16), (((1,), (1,)), ((), ())),
                           preferred_element_type=F32)


def _dot_tn(a, b):
    return lax.dot_general(a.astype(BF16), b.astype(BF16), (((0,), (0,)), ((), ())),
                           preferred_element_type=F32)


def _mod_kernel(ct_ref, w_ref, b_ref, o_ref):
    s = _silu(ct_ref[...])
    w = w_ref[0]
    rid = lax.broadcasted_iota(jnp.int32, (8, w.shape[1]), 0)
    out = jnp.zeros((8, w.shape[1]), F32)
    for r in range(3):
        row = jnp.sum(w * s[:, r:r + 1], axis=0, keepdims=True) + b_ref[0]
        out = jnp.where(rid == r, row, out)
    o_ref[0] = out


def _modulation(ct, w_mod, b_mod):
    depth, d, d3 = w_mod.shape
    return pl.pallas_call(
        _mod_kernel,
        grid=(depth, d3 // d),
        in_specs=[pl.BlockSpec((d, LANES), lambda l, j: (0, 0)),
                  pl.BlockSpec((1, d, d), lambda l, j: (l, 0, j)),
                  pl.BlockSpec((1, 1, d), lambda l, j: (l, 0, j))],
        out_specs=pl.BlockSpec((1, 8, d), lambda l, j: (l, 0, j)),
        out_shape=jax.ShapeDtypeStruct((depth, 8, d3), F32),
        compiler_params=_cparams("parallel", "parallel"),
        name="modulation",
    )(ct, w_mod, b_mod.reshape(depth, 1, d3))


def _rope(x, cos, sin):
    lane = lax.broadcasted_iota(jnp.int32, x.shape, 1)
    first = (lane % 32) < 16
    partner = jnp.where(first, pltpu.roll(x, LANES - 16, 1), pltpu.roll(x, 16, 1))
    return x * cos + partner * sin


def _stream_tile(c_ref, x_ref, tile):
    is_ctx = (jnp.zeros((TILE, 1), jnp.int32) + tile) == 0
    return jnp.where(is_ctx, c_ref[0], x_ref[0])


def _stream_specs(d, off, shift=0):
    return [pl.BlockSpec((1, TILE, d), lambda bi, i: (bi, 0, 0)),
            pl.BlockSpec((1, TILE, d), lambda bi, i: (bi, jnp.maximum(i + shift - off, 1 - off), 0))]


_IN_COLS = []
_src = 0
for _dst, _w in ((C_ATT_Q, MIX), (C_ATT_K, LANES), (C_ATT_V, LANES), (C_ATT_Z, MIX), (C_GDN_Q, MIX), (C_GDN_K, MIX),
                 (C_GDN_V, MIX), (C_GDN_B, N_DIR * GDN_HEADS), (C_GDN_A, N_DIR * GDN_HEADS), (C_GDN_Z, MIX),
                 (C_LRU_X, MIX), (C_LRU_Z, MIX)):
    _IN_COLS.append((_src, _dst, _w))
    _src += _w
IN_WIDTH = _src


def _scan_order(t):
    return jnp.swapaxes(t.reshape(SEG, STEPS, t.shape[1]), 0, 1).reshape(t.shape)


def _time_order(t):
    return jnp.swapaxes(t.reshape(STEPS, SEG, t.shape[1]), 0, 1).reshape(t.shape)


def _inproj_kernel(c_ref, x_ref, mod_ref, g_ref, cos_ref, sin_ref, w_ref, o_ref, z_ref, lx_ref, w_s):
    @pl.when(jnp.logical_and(pl.program_id(0) == 0, pl.program_id(1) == 0))
    def _():
        lane = lax.broadcasted_iota(jnp.int32, (TILE, LANES), 1)
        n_gate = N_DIR * GDN_HEADS
        for r0 in range(0, w_s.shape[0], TILE):
            for src, dst, width in _IN_COLS:
                if width >= LANES:
                    w_s[r0:r0 + TILE, dst:dst + width] = w_ref[0, src:src + width, r0:r0 + TILE].T.astype(BF16)
            src_b = _IN_COLS[7][0]
            win = w_ref[0, src_b:src_b + LANES, r0:r0 + TILE].T
            w_s[r0:r0 + TILE, C_GDN_B:C_GDN_B + LANES] = jnp.where(lane < n_gate, win, 0.0).astype(BF16)
            w_s[r0:r0 + TILE, C_GDN_A:C_GDN_A + LANES] = jnp.where(
                lane < n_gate, pltpu.roll(win, LANES - n_gate, 1), 0.0).astype(BF16)

    x = _stream_tile(c_ref, x_ref, pl.program_id(1))
    ms = jnp.mean(x * x, axis=-1, keepdims=True)
    y = x * lax.rsqrt(ms + EPS) * g_ref[...]
    shift = mod_ref[0, 0, 0:1, :]
    scale = mod_ref[0, 0, 1:2, :]
    hb = (y * (1.0 + scale) + shift).astype(BF16)
    cos = cos_ref[...]
    sin = sin_ref[...]
    for c0 in range(0, N_IN, MIX):
        res = jnp.dot(hb, w_s[:, c0:c0 + MIX], preferred_element_type=F32)
        if c0 == C_ATT_Q:
            res = jnp.concatenate(
                [_rope(res[:, k:k + LANES], cos, sin) for k in range(0, MIX, LANES)], axis=1)
        if c0 == C_ATT_K:
            res = jnp.concatenate([_rope(res[:, :LANES], cos, sin), res[:, LANES:]], axis=1)
        if c0 == C_LRU_X:
            lx_ref[0] = _scan_order(res)
        elif c0 >= N_P:
            z_ref[0, :, c0 - N_P:c0 - N_P + MIX] = res.astype(z_ref.dtype)
        else:
            o_ref[0, :, c0:c0 + MIX] = res


def _inproj(c_src, x_src, off, modsel, g, cos_t, sin_t, w_in, layer):
    b, _, d = x_src.shape
    s = x_src.shape[1] + off * TILE
    nt = s // TILE
    return pl.pallas_call(
        _inproj_kernel,
        grid=(b, nt),
        in_specs=_stream_specs(d, off) + [
                  pl.BlockSpec((1, 1, 8, d), lambda bi, i: (bi, jnp.minimum(i, 1), 0, 0)),
                  pl.BlockSpec((1, d), lambda bi, i: (0, 0)),
                  pl.BlockSpec((TILE, LANES), lambda bi, i: (i, 0)),
                  pl.BlockSpec((TILE, LANES), lambda bi, i: (i, 0)),
                  pl.BlockSpec((1, IN_WIDTH, d), lambda bi, i: (layer, 0, 0), pipeline_mode=pl.Buffered(1))],
        out_specs=[pl.BlockSpec((1, TILE, N_P), lambda bi, i: (bi, i, 0)),
                   pl.BlockSpec((1, TILE, N_Z), lambda bi, i: (bi, i, 0)),
                   pl.BlockSpec((1, TILE, MIX), lambda bi, i: (bi, i, 0))],
        out_shape=[jax.ShapeDtypeStruct((b, s, N_P), F32), jax.ShapeDtypeStruct((b, s, N_Z), BF16),
                   jax.ShapeDtypeStruct((b, s, MIX), F32)],
        scratch_shapes=[pltpu.VMEM((d, N_IN), BF16)],
        compiler_params=_cparams("arbitrary", "arbitrary"),
        name="inproj",
    )(c_src, x_src, modsel, g, cos_t, sin_t, w_in)


def _att_kernel(sink_ref, q_ref, z_ref, kvp_ref, kvc_ref, kvn_ref, kvx_ref, o_ref, *, n_blocks):
    i = pl.program_id(1)
    n_sub = TILE // ATT_BLOCK
    n_band = 3 * ATT_BLOCK
    half = ATT_HEAD_DIM
    group = ATT_HEADS // ATT_KV_HEADS

    kv = jnp.concatenate([kvp_ref[0], kvc_ref[0], kvn_ref[0], kvx_ref[0]], axis=0)
    nk = kv.shape[0]
    k_all, v_all = kv[:, :LANES], kv[:, LANES:]
    k_sw, v_sw = pltpu.roll(k_all, half, 1), pltpu.roll(v_all, half, 1)
    lo_k = lax.broadcasted_iota(jnp.int32, (nk, LANES), 1) < half
    k_dup = [jnp.where(lo_k, k_all, k_sw).astype(BF16), jnp.where(lo_k, k_sw, k_all).astype(BF16)]
    v_dup = [jnp.where(lo_k, v_all, v_sw).astype(BF16), jnp.where(lo_k, v_sw, v_all).astype(BF16)]
    n_loc = (n_sub + 2) * ATT_BLOCK

    def keys(t, sub):
        return jnp.concatenate([t[sub * ATT_BLOCK:sub * ATT_BLOCK + n_band], t[n_loc:]], axis=0)

    r = lax.broadcasted_iota(jnp.int32, (ATT_BLOCK, n_band), 0)
    c = lax.broadcasted_iota(jnp.int32, (ATT_BLOCK, n_band), 1)
    in_win = jnp.abs(r - (c - ATT_BLOCK)) <= ATT_BLOCK
    lo_q = lax.broadcasted_iota(jnp.int32, (ATT_BLOCK, LANES), 1) < half
    outs = [[None] * (ATT_HEADS // 2) for _ in range(n_sub)]

    def band_bias(sub):
        m = i * n_sub + sub
        c_lo = jnp.where(m >= n_sub + 1, 0, jnp.where(m >= n_sub, ATT_BLOCK, n_band))
        c_hi = jnp.where(m < n_sub, 0, jnp.where(m <= n_blocks - 2, n_band, 2 * ATT_BLOCK))
        return jnp.where(jnp.logical_and(in_win, jnp.logical_and(c >= c_lo, c < c_hi)), 0.0, NEG_INF)

    def stream(sub, h, bias):
        q = q_ref[0, sub * ATT_BLOCK:(sub + 1) * ATT_BLOCK, :] * (ATT_HEAD_DIM ** -0.5 * LOG2E)
        lhs = []
        for j in range(h * group, (h + 1) * group):
            q_t = q[:, (j // 2) * LANES:(j // 2 + 1) * LANES]
            lhs.append(jnp.where(lo_q, q_t, 0.0) if j % 2 == 0 else jnp.where(lo_q, 0.0, q_t))
        qs = jnp.concatenate(lhs, axis=0).astype(BF16)
        s = lax.dot_general(qs, keys(k_dup[h], sub), (((1,), (1,)), ((), ())), preferred_element_type=F32)
        yield
        p_blocks, den_blocks = [], []
        for e in range(group):
            sb = s[e * ATT_BLOCK:(e + 1) * ATT_BLOCK]
            sk = sink_ref[h * group + e] * LOG2E
            s_band = sb[:, :n_band] + bias
            s_ctx = sb[:, n_band:]
            mx = jnp.maximum(jnp.maximum(jnp.max(s_band, axis=-1, keepdims=True),
                                         jnp.max(s_ctx, axis=-1, keepdims=True)), sk)
            p_band = jnp.exp2(s_band - mx)
            p_ctx = jnp.exp2(s_ctx - mx)
            den_blocks.append(jnp.sum(p_band, axis=-1, keepdims=True) + jnp.sum(p_ctx, axis=-1, keepdims=True)
                              + jnp.exp2(sk - mx))
            p_blocks.append(jnp.concatenate([p_band, p_ctx], axis=1).astype(BF16))
            yield
        p = jnp.concatenate(p_blocks, axis=0)
        res = (jnp.dot(p, keys(v_dup[h], sub), preferred_element_type=F32)
               / jnp.concatenate(den_blocks, axis=0))
        for t in range(group // 2):
            even = res[(2 * t) * ATT_BLOCK:(2 * t + 1) * ATT_BLOCK]
            odd = res[(2 * t + 1) * ATT_BLOCK:(2 * t + 2) * ATT_BLOCK]
            outs[sub][h * (group // 2) + t] = jnp.where(lo_q, even, odd)
        yield

    for sub in range(n_sub):
        bias = band_bias(sub)
        gens = [stream(sub, h, bias) for h in range(ATT_KV_HEADS)]
        for g in gens:
            next(g)
        for g in gens:
            for _ in g:
                pass
    for sub in range(n_sub):
        rows = slice(sub * ATT_BLOCK, (sub + 1) * ATT_BLOCK)
        o = jnp.concatenate(outs[sub], axis=1)
        o_ref[0, rows, :] = (o * _silu(z_ref[0, rows, :].astype(F32))).astype(o_ref.dtype)


def _attention(p, pz, sink):
    b, s, _ = p.shape
    nb = s // ATT_BLOCK
    n_sub = TILE // ATT_BLOCK
    kvcol = C_ATT_K // (2 * LANES)
    kv_spec = lambda f: pl.BlockSpec((1, ATT_BLOCK, 2 * LANES), f)
    return pl.pallas_call(
        functools.partial(_att_kernel, n_blocks=nb),
        grid=(b, s // TILE),
        in_specs=[pl.BlockSpec(memory_space=pltpu.SMEM),
                  pl.BlockSpec((1, TILE, MIX), lambda bi, i: (bi, i, C_ATT_Q // MIX)),
                  pl.BlockSpec((1, TILE, MIX), lambda bi, i: (bi, i, (C_ATT_Z - N_P) // MIX)),
                  kv_spec(lambda bi, i: (bi, jnp.maximum(i * n_sub - 1, n_sub), kvcol)),
                  pl.BlockSpec((1, TILE, 2 * LANES), lambda bi, i: (bi, i, kvcol)),
                  kv_spec(lambda bi, i: (bi, jnp.minimum((i + 1) * n_sub, nb - 1), kvcol)),
                  pl.BlockSpec((1, TILE, 2 * LANES), lambda bi, i: (bi, 0, kvcol))],
        out_specs=pl.BlockSpec((1, TILE, MIX), lambda bi, i: (bi, i, 0)),
        out_shape=jax.ShapeDtypeStruct((b, s, MIX), BF16),
        compiler_params=_cparams("parallel", "parallel"),
        name="attention",
    )(sink, p, pz, p, p, p, p)


def _conv4(x, hp, hn, w, prev_ok, next_ok):
    hp = hp * prev_ok
    hn = hn * next_ok
    ext = jnp.concatenate([hp, x, hn], axis=0)
    n = x.shape[0]
    own = slice(HALO, HALO + n)
    back2, back1, fwd1 = (pltpu.roll(ext, sh, 0) for sh in (2, 1, ext.shape[0] - 1))
    return w[0:1] * back2[own] + w[1:2] * back1[own] + w[2:3] * ext[own] + w[3:4] * fwd1[own]


def _halo_ok(i, n_tiles):
    prev_ok = (i >= 2).astype(F32)
    next_ok = jnp.logical_and(i != 0, i != n_tiles - 1).astype(F32)
    return prev_ok, next_ok


def _chunk_cumsum(g, reverse):
    n = g.shape[0]
    row = lax.broadcasted_iota(jnp.int32, g.shape, 0) % GDN_CHUNK
    s = 1
    while s < GDN_CHUNK:
        if reverse:
            g = g + jnp.where(row < GDN_CHUNK - s, pltpu.roll(g, n - s, 0), 0.0)
        else:
            g = g + jnp.where(row >= s, pltpu.roll(g, s, 0), 0.0)
        s *= 2
    return g


def _spread(cols, masks):
    out = cols[-1]
    for m, col in zip(masks[-2::-1], cols[-2::-1]):
        out = jnp.where(m, col, out)
    return out


def _head_block_diag(x, masks):
    return jnp.concatenate([jnp.where(m, x, 0.0) for m in masks], axis=0).astype(BF16)


def _gdn_prep_kernel(q_ref, qp_ref, qn_ref, k_ref, kp_ref, kn_ref, v_ref, vp_ref, vn_ref,
                     b_ref, a_ref, cw_ref, alog_ref, dtb_ref,
                     w_ref, u_ref, kd_ref, qg_ref, aqk_ref, gl_ref, *, n_tiles):
    i = pl.program_id(1)
    prev_ok, next_ok = _halo_ok(i, n_tiles)
    hd, ch, nh = GDN_HEAD_DIM, GDN_CHUNK, GDN_HEADS
    nch = TILE // ch

    def prep(x_ref, p_ref, n_ref, col):
        xh = _conv4(x_ref[0], p_ref[0], n_ref[0], cw_ref[:, col:col + MIX], prev_ok, next_ok)
        return xh * (1.0 + jnp.tanh(xh))

    def l2n(t):
        parts = []
        for h in range(nh):
            th = t[:, h * hd:(h + 1) * hd]
            parts.append(th * lax.rsqrt(jnp.sum(th * th, axis=-1, keepdims=True) + EPS))
        return jnp.concatenate(parts, axis=1)

    q = l2n(prep(q_ref, qp_ref, qn_ref, 0)) * (hd ** -0.5)
    k = l2n(prep(k_ref, kp_ref, kn_ref, MIX))
    v = prep(v_ref, vp_ref, vn_ref, 2 * MIX)

    lane = lax.broadcasted_iota(jnp.int32, (TILE, LANES), 1)
    g = -jnp.exp(alog_ref[...]) * _softplus(a_ref[0] + dtb_ref[...])
    gcum = jnp.where(lane < nh, _chunk_cumsum(g, False), _chunk_cumsum(g, True))
    beta = _sigmoid(b_ref[0])

    ii = lax.broadcasted_iota(jnp.int32, (ch, PACK), 0)
    ll = lax.broadcasted_iota(jnp.int32, (ch, PACK), 1)
    jj = ll % ch
    eye_p = jj == ii
    eye_f = eye_p.astype(F32)
    bd16 = (jj // 16) == (ii // 16)
    bd32 = (jj // 32) == (ii // 32)
    merge_masks = (jnp.logical_and(bd32, jnp.logical_not(bd16)), jnp.logical_not(bd32))
    strict = [jj < ii, jj > ii]
    incl = [jj <= ii, jj >= ii]
    head_p = [(ll // ch) == h for h in range(nh)]
    lw = lax.broadcasted_iota(jnp.int32, (ch, MIX), 1) // hd
    head_w = [lw == h for h in range(nh)]
    lane_row = lax.broadcasted_iota(jnp.int32, (1, LANES), 1)

    def pdot(a, b):
        return jnp.dot(a.astype(BF16), _head_block_diag(b, head_p), preferred_element_type=F32)

    gl_rows = []
    units = []
    for cidx in range(nch):
        rows = slice(cidx * ch, (cidx + 1) * ch)
        kc, qc, vc, gc, bc = k[rows], q[rows], v[rows], gcum[rows], beta[rows]
        kq = _dot_nt(jnp.concatenate([kc, qc], axis=0), _head_block_diag(kc, head_w))
        g_end = jnp.where(lane_row < nh, gc[ch - 1:ch, :], gc[0:1, :])
        gl_rows.append(jnp.exp(g_end))
        for d in range(N_DIR):
            gcols = [gc[:, d * nh + h:d * nh + h + 1] for h in range(nh)]
            bcols = [bc[:, d * nh + h:d * nh + h + 1] for h in range(nh)]
            gcol_p = _spread(gcols, head_p)
            grow_p = jnp.sum(jnp.where(eye_p, gcol_p, 0.0), axis=0, keepdims=True)
            dec = jnp.exp(jnp.where(incl[d], gcol_p - grow_p, 0.0))
            lm = jnp.where(strict[d], _spread(bcols, head_p) * dec * kq[:ch], 0.0)
            aqk = jnp.where(incl[d], dec * kq[ch:], 0.0)
            aqk_ref[0, rows, d * PACK:(d + 1) * PACK] = aqk.astype(BF16)
            units.append((rows, d, lm, kc, qc, vc, gcols, bcols,
                          [g_end[:, d * nh + h:d * nh + h + 1] for h in range(nh)]))
    gl_rows.append(jnp.zeros((8 - nch, LANES), F32))
    gl_ref[0, 0] = jnp.concatenate(gl_rows, axis=0)

    lms = [u[2] for u in units]
    m = [jnp.where(bd16, -lm, 0.0) for lm in lms]
    x = [eye_f + mi for mi in m]
    pw = [pdot(mi, mi) for mi in m]
    for _ in range(2):
        r = [pdot(jnp.concatenate([xi, pi], axis=0), pi) for xi, pi in zip(x, pw)]
        x = [xi + ri[:ch] for xi, ri in zip(x, r)]
        pw = [ri[ch:] for ri in r]
    x = [xi + pdot(xi, pi) for xi, pi in zip(x, pw)]
    for mask in merge_masks:
        y = [pdot(xi, jnp.where(mask, lm, 0.0)) for xi, lm in zip(x, lms)]
        x = [xi - pdot(yi, xi) for xi, yi in zip(x, y)]

    for tinv, (rows, d, _, kc, qc, vc, gcols, bcols, gend_cols) in zip(x, units):
        slab = lambda t, h: t[:, h * hd:(h + 1) * hd]
        gam = [jnp.exp(gcol) for gcol in gcols]
        bgk = jnp.concatenate([(bcols[h] * gam[h]) * slab(kc, h) for h in range(nh)], axis=1)
        bv = jnp.concatenate([bcols[h] * slab(vc, h) for h in range(nh)], axis=1)
        rhs = jnp.concatenate([_head_block_diag(bgk, head_w), _head_block_diag(bv, head_w)], axis=1)
        wu = jnp.dot(tinv.astype(BF16), rhs, preferred_element_type=F32)
        cols = slice(d * MIX, (d + 1) * MIX)
        w_ref[0, rows, cols] = wu[:, :MIX].astype(BF16)
        u_ref[0, rows, cols] = wu[:, MIX:]
        kd_ref[0, rows, cols] = jnp.concatenate(
            [jnp.exp(gend_cols[h] - gcols[h]) * slab(kc, h) for h in range(nh)], axis=1).astype(BF16)
        qg_ref[0, rows, cols] = jnp.concatenate([gam[h] * slab(qc, h) for h in range(nh)], axis=1).astype(BF16)


def _gdn_prep(p, conv_w, alog, dtb):
    b, s, _ = p.shape
    nt = s // TILE
    hpt = TILE // HALO
    nh = s // HALO
    wide = N_DIR * GDN_HEADS * GDN_HEAD_DIM

    def tile_spec(col):
        return pl.BlockSpec((1, TILE, MIX), lambda bi, i: (bi, i, col // MIX))

    def prev_spec(col):
        return pl.BlockSpec((1, HALO, MIX), lambda bi, i: (bi, jnp.maximum(i * hpt - 1, 0), col // MIX))

    def next_spec(col):
        return pl.BlockSpec((1, HALO, MIX), lambda bi, i: (bi, jnp.minimum((i + 1) * hpt, nh - 1), col // MIX))

    in_specs = []
    for col in (C_GDN_Q, C_GDN_K, C_GDN_V):
        in_specs += [tile_spec(col), prev_spec(col), next_spec(col)]
    in_specs += [pl.BlockSpec((1, TILE, LANES), lambda bi, i: (bi, i, C_GDN_B // LANES)),
                 pl.BlockSpec((1, TILE, LANES), lambda bi, i: (bi, i, C_GDN_A // LANES)),
                 pl.BlockSpec((4, 3 * MIX), lambda bi, i: (0, 0)),
                 pl.BlockSpec((1, LANES), lambda bi, i: (0, 0)),
                 pl.BlockSpec((1, LANES), lambda bi, i: (0, 0))]
    wide_spec = pl.BlockSpec((1, TILE, wide), lambda bi, i: (bi, i, 0))
    out_specs = [wide_spec, wide_spec, wide_spec, wide_spec,
                 pl.BlockSpec((1, TILE, N_DIR * PACK), lambda bi, i: (bi, i, 0)),
                 pl.BlockSpec((1, 1, 8, LANES), lambda bi, i: (bi, i, 0, 0))]
    out_shape = [jax.ShapeDtypeStruct((b, s, wide), BF16),
                 jax.ShapeDtypeStruct((b, s, wide), F32),
                 jax.ShapeDtypeStruct((b, s, wide), BF16),
                 jax.ShapeDtypeStruct((b, s, wide), BF16),
                 jax.ShapeDtypeStruct((b, s, N_DIR * PACK), BF16),
                 jax.ShapeDtypeStruct((b, nt, 8, LANES), F32)]
    return pl.pallas_call(
        functools.partial(_gdn_prep_kernel, n_tiles=nt),
        grid=(b, nt),
        in_specs=in_specs,
        out_specs=out_specs,
        out_shape=out_shape,
        compiler_params=_cparams("parallel", "parallel"),
        name="gdn_prep",
    )(p, p, p, p, p, p, p, p, p, p, p, conv_w, alog, dtb)


def _gdn_scan_body(wf, uf, kdf, qgf, af, glf, wb, ub, kdb, qgb, ab, glb, of_ref, ob_ref, state):
    hd, ch, nh = GDN_HEAD_DIM, GDN_CHUNK, GDN_HEADS
    nch = TILE // ch
    lw = lax.broadcasted_iota(jnp.int32, (ch, MIX), 1) // hd
    head_w = [lw == h for h in range(nh)]
    dirs = ((wf, uf, kdf, qgf, af, glf, of_ref), (wb, ub, kdb, qgb, ab, glb, ob_ref))
    for step in range(nch):
        cidx = [step, nch - 1 - step]
        rows = [slice(c * ch, (c + 1) * ch) for c in cidx]
        s_old = {}
        ws = {}
        for d, (w_r, _, _, qg_r, _, _, _) in enumerate(dirs):
            for h in range(nh):
                cols = slice(h * hd, (h + 1) * hd)
                s_old[d, h] = state[d, h]
                lhs = jnp.concatenate([w_r[0, rows[d], cols], qg_r[0, rows[d], cols]], axis=0)
                ws[d, h] = jnp.dot(lhs, s_old[d, h].astype(BF16), preferred_element_type=F32)
            yield
        for d, (_, u_r, kd_r, _, a_r, gl_r, o_r) in enumerate(dirs):
            un = u_r[0, rows[d], :] - jnp.concatenate([ws[d, h][:ch] for h in range(nh)], axis=1)
            inter = jnp.concatenate([ws[d, h][ch:] for h in range(nh)], axis=1)
            intra = jnp.dot(a_r[0, rows[d], :], _head_block_diag(un, head_w), preferred_element_type=F32)
            o_r[0, rows[d], :] = (inter + intra).astype(o_r.dtype)
            unb = un.astype(BF16)
            for h in range(nh):
                cols = slice(h * hd, (h + 1) * hd)
                gl = gl_r[0, 0, cidx[d]:cidx[d] + 1, d * nh + h:d * nh + h + 1]
                state[d, h] = gl * s_old[d, h] + _dot_tn(kd_r[0, rows[d], cols], unb[:, cols])
            yield


def _mirror(j, nt):
    return jnp.where(j == 0, 0, nt - j)


def _lru_body(xf_ref, xfp2_ref, xfp_ref, xfn_ref, xb_ref, xbp2_ref, xbp_ref, xbn_ref, cw_ref, cb_ref, wg_ref, bg_ref,
              lam_ref, hf_ref, hb_ref, carry, *, n_tiles):
    j = pl.program_id(1)
    seg_n = SEG
    steps = STEPS
    sub = lax.broadcasted_iota(jnp.int32, (seg_n, LRU_WIDTH), 0)

    def one_dir(d, x_ref, p2_ref, p_ref, n_ref, tile_idx, o_ref):
        prev_ok, next_ok = _halo_ok(tile_idx, n_tiles)
        xs = [x_ref[0, st * seg_n:(st + 1) * seg_n, :] for st in range(steps)]
        last = seg_n - 1
        p1 = jnp.where(sub == 0, p_ref[0, last:last + 1, :] * prev_ok, pltpu.roll(xs[steps - 1], 1, 0))
        p2 = jnp.where(sub == 0, p2_ref[0, last:last + 1, :] * prev_ok, pltpu.roll(xs[steps - 2], 1, 0))
        n1 = jnp.where(sub == last, n_ref[0, 0:1, :] * next_ok, pltpu.roll(xs[0], last, 0))
        ext = [p2, p1] + xs + [n1]
        cw = cw_ref[...]
        xh = jnp.concatenate(
            [cw[0:1] * ext[st] + cw[1:2] * ext[st + 1] + cw[2:3] * ext[st + 2] + cw[3:4] * ext[st + 3]
             for st in range(steps)], axis=0) + cb_ref[...]
        yield
        gates = jnp.dot(xh.astype(BF16), wg_ref[:, d * 2 * LRU_WIDTH:(d + 1) * 2 * LRU_WIDTH],
                        preferred_element_type=F32)
        yield
        c_half = (-0.5 * LRU_C) * _softplus(-lam_ref[d:d + 1, :])
        a_blocks, b_blocks = [], []
        blk = TILE // 4
        for r0 in range(0, TILE, blk):
            gb = gates[r0:r0 + blk]
            tr = jnp.tanh(gb[:, :LRU_WIDTH] + bg_ref[2 * d:2 * d + 1, :])
            ti = jnp.tanh(gb[:, LRU_WIDTH:] + bg_ref[2 * d + 1:2 * d + 2, :])
            log_a = c_half * tr + c_half
            ab = jnp.exp(log_a)
            b_blocks.append(jnp.sqrt(-jnp.tanh(log_a) * (ab * ab + 1.0)) * ((ti + 1.0) * xh[r0:r0 + blk]))
            a_blocks.append(ab)
            yield
        a = jnp.concatenate(a_blocks, axis=0)
        bb = jnp.concatenate(b_blocks, axis=0)

        order = list(range(steps)) if d == 0 else list(range(steps - 1, -1, -1))
        piece = lambda t, st: t[st * seg_n:(st + 1) * seg_n]
        a_tot = piece(a, order[0])
        b_tot = piece(bb, order[0])
        for n, st in enumerate(order[1:]):
            b_tot = piece(a, st) * b_tot + piece(bb, st)
            a_tot = piece(a, st) * a_tot
            if n % 8 == 7:
                yield
        s = 1
        while s < seg_n:
            if d == 0:
                keep = sub >= s
                a_sh = jnp.where(keep, pltpu.roll(a_tot, s, 0), 1.0)
                b_sh = jnp.where(keep, pltpu.roll(b_tot, s, 0), 0.0)
            else:
                keep = sub < seg_n - s
                a_sh = jnp.where(keep, pltpu.roll(a_tot, seg_n - s, 0), 1.0)
                b_sh = jnp.where(keep, pltpu.roll(b_tot, seg_n - s, 0), 0.0)
            b_tot = a_tot * b_sh + b_tot
            a_tot = a_tot * a_sh
            s *= 2
        h_in0 = carry[d]
        h_out = a_tot * h_in0 + b_tot
        if d == 0:
            h = jnp.where(sub == 0, h_in0, pltpu.roll(h_out, 1, 0))
            carry[d] = h_out[seg_n - 1:seg_n]
        else:
            h = jnp.where(sub == seg_n - 1, h_in0, pltpu.roll(h_out, seg_n - 1, 0))
            carry[d] = h_out[0:1]
        hs = [None] * steps
        for n, st in enumerate(order):
            h = piece(a, st) * h + piece(bb, st)
            hs[st] = h
            if n % 8 == 7:
                yield
        o_ref[0] = jnp.concatenate(hs, axis=0).astype(o_ref.dtype)

    return [one_dir(0, xf_ref, xfp2_ref, xfp_ref, xfn_ref, j, hf_ref),
            one_dir(1, xb_ref, xbp2_ref, xbp_ref, xbn_ref, _mirror(j, n_tiles), hb_ref)]


N_SCAN_IN = 12
N_LRU_IN = 13


def _round_robin(gens, weights):
    live = list(zip(gens, weights))
    while live:
        nxt = []
        for g, w in live:
            alive = True
            for _ in range(w):
                try:
                    next(g)
                except StopIteration:
                    alive = False
                    break
            if alive:
                nxt.append((g, w))
        live = nxt


def _sweep_kernel(*refs, n_tiles):
    scan_in, lru_in = refs[:N_SCAN_IN], refs[N_SCAN_IN:N_SCAN_IN + N_LRU_IN]
    of_ref, ob_ref, hf_ref, hb_ref, state, carry = refs[N_SCAN_IN + N_LRU_IN:]

    @pl.when(pl.program_id(1) == 0)
    def _():
        state[...] = jnp.zeros_like(state)
        carry[...] = jnp.zeros_like(carry)

    _round_robin([_gdn_scan_body(*scan_in, of_ref, ob_ref, state)]
                 + _lru_body(*lru_in, hf_ref, hb_ref, carry, n_tiles=n_tiles), (1, 2, 2))


def _sweep(w, u, kd, qg, aqk, gl, lx, conv_w, conv_b, wg, bg, lam):
    b, s, _ = u.shape
    nt = s // TILE
    half = GDN_HEADS * GDN_HEAD_DIM
    hpt = TILE // HALO
    nh = s // HALO
    tile_of = (lambda j: j, lambda j: _mirror(j, nt))

    def scan_specs(d):
        idx = lambda bi, j: (bi, tile_of[d](j), d)
        gidx = lambda bi, j: (bi, tile_of[d](j), 0, 0)
        return [pl.BlockSpec((1, TILE, half), idx)] * 4 + [pl.BlockSpec((1, TILE, PACK), idx),
                                                           pl.BlockSpec((1, 1, 8, LANES), gidx)]

    def lru_specs(d):
        t = tile_of[d]
        return [pl.BlockSpec((1, TILE, MIX), lambda bi, j: (bi, t(j), 0)),
                pl.BlockSpec((1, HALO, MIX), lambda bi, j: (bi, jnp.maximum(t(j) * hpt - 2, 0), 0)),
                pl.BlockSpec((1, HALO, MIX), lambda bi, j: (bi, jnp.maximum(t(j) * hpt - 1, 0), 0)),
                pl.BlockSpec((1, HALO, MIX), lambda bi, j: (bi, jnp.minimum((t(j) + 1) * hpt, nh - 1), 0))]

    full = lambda shape: pl.BlockSpec(shape, lambda bi, j: tuple(0 for _ in shape))
    out_spec = lambda d: pl.BlockSpec((1, TILE, MIX), lambda bi, j: (bi, tile_of[d](j), 0))
    in_specs = (scan_specs(0) + scan_specs(1) + lru_specs(0) + lru_specs(1)
                + [full((4, LRU_WIDTH)), full((1, LRU_WIDTH)), full((LRU_WIDTH, 4 * LRU_WIDTH)),
                   full((4, LRU_WIDTH)), full((N_DIR, LRU_WIDTH))])
    assert len(in_specs) == N_SCAN_IN + N_LRU_IN
    return pl.pallas_call(
        functools.partial(_sweep_kernel, n_tiles=nt),
        grid=(b, nt),
        in_specs=in_specs,
        out_specs=[out_spec(0), out_spec(1), out_spec(0), out_spec(1)],
        out_shape=[jax.ShapeDtypeStruct((b, s, MIX), BF16)] * 4,
        scratch_shapes=[pltpu.VMEM((N_DIR, GDN_HEADS, GDN_HEAD_DIM, GDN_HEAD_DIM), F32),
                        pltpu.VMEM((N_DIR, 1, LRU_WIDTH), F32)],
        compiler_params=_cparams("parallel", "arbitrary"),
        name="sweep",
    )(w, u, kd, qg, aqk, gl, w, u, kd, qg, aqk, gl, *([lx] * 8), conv_w, conv_b, wg, bg, lam)


def _outproj_kernel(c_ref, x_ref, mod_ref, att_ref, of_ref, ob_ref, gz_ref, hf_ref, hb_ref, lz_ref,
                    gn_ref, w_ref, fg_ref, o_ref, *, final, first_tile):
    hd = GDN_HEAD_DIM
    o = of_ref[0].astype(F32) + ob_ref[0].astype(F32)
    parts = []
    for h in range(GDN_HEADS):
        oh = o[:, h * hd:(h + 1) * hd]
        parts.append(oh * lax.rsqrt(jnp.mean(oh * oh, axis=-1, keepdims=True) + EPS) * gn_ref[...])
    gdn = jnp.concatenate(parts, axis=1) * _silu(gz_ref[0].astype(F32))
    lru = _time_order(hf_ref[0].astype(F32) + hb_ref[0].astype(F32)) * _silu(lz_ref[0].astype(F32))
    acc = (jnp.dot(att_ref[0], w_ref[0:MIX, :], preferred_element_type=F32)
           + jnp.dot(gdn.astype(BF16), w_ref[MIX:2 * MIX, :], preferred_element_type=F32)
           + jnp.dot(lru.astype(BF16), w_ref[2 * MIX:3 * MIX, :], preferred_element_type=F32))
    gate = mod_ref[0, 0, 2:3, :]
    xn = _stream_tile(c_ref, x_ref, pl.program_id(1) + first_tile) + gate * acc
    if final:
        ms = jnp.mean(xn * xn, axis=-1, keepdims=True)
        xn = xn * lax.rsqrt(ms + EPS) * fg_ref[...]
    o_ref[0] = xn


def _outproj(c_src, x_src, src_off, modsel, att, of, ob, pz, hf, hb, gdn_norm, w_out, final_g, final):
    b, s, _ = pz.shape
    d = x_src.shape[2]
    nt = s // TILE
    off = 1 if final else 0
    n_out = nt - off
    tok = lambda col: pl.BlockSpec((1, TILE, MIX), lambda bi, i: (bi, i + off, col))
    return pl.pallas_call(
        functools.partial(_outproj_kernel, final=final, first_tile=off),
        grid=(b, n_out),
        in_specs=_stream_specs(d, src_off, off) + [
                  pl.BlockSpec((1, 1, 8, d), lambda bi, i: (bi, jnp.minimum(i + off, 1), 0, 0)),
                  tok(0), tok(0), tok(0), tok((C_GDN_Z - N_P) // MIX), tok(0), tok(0), tok((C_LRU_Z - N_P) // MIX),
                  pl.BlockSpec((1, GDN_HEAD_DIM), lambda bi, i: (0, 0)),
                  pl.BlockSpec((3 * MIX, d), lambda bi, i: (0, 0)),
                  pl.BlockSpec((1, d), lambda bi, i: (0, 0))],
        out_specs=pl.BlockSpec((1, TILE, d), lambda bi, i: (bi, i, 0)),
        out_shape=jax.ShapeDtypeStruct((b, n_out * TILE, d), F32),
        compiler_params=_cparams("parallel", "parallel"),
        name="outproj",
    )(c_src, x_src, modsel, att, of, ob, pz, hf, hb, pz, gdn_norm, w_out, final_g)


def _rope_tables(s, ctx_len):
    t = np.arange(s - ctx_len)
    n_freq = ATT_HEAD_DIM // 4
    inv = ROPE_BASE ** (-np.arange(n_freq, dtype=np.float64) / n_freq)
    ang = [(t // GRID_W)[:, None] * inv, (t % GRID_W)[:, None] * inv]
    cos = np.concatenate([np.cos(ang[0]), np.cos(ang[0]), np.cos(ang[1]), np.cos(ang[1])], axis=1)
    sin = np.concatenate([-np.sin(ang[0]), np.sin(ang[0]), -np.sin(ang[1]), np.sin(ang[1])], axis=1)
    cos = np.concatenate([np.ones((ctx_len, ATT_HEAD_DIM)), cos], axis=0)
    sin = np.concatenate([np.zeros((ctx_len, ATT_HEAD_DIM)), sin], axis=0)
    return (jnp.asarray(np.tile(cos, (1, 2)), F32), jnp.asarray(np.tile(sin, (1, 2)), F32))


def _block_diag(w):
    eye = jnp.eye(LRU_BLOCKS, dtype=w.dtype)
    return jnp.einsum("nde,nm->ndme", w, eye).reshape(LRU_WIDTH, LRU_WIDTH)


def _pad_lanes(v):
    return jnp.pad(v.reshape(1, -1), ((0, 0), (0, LANES - v.size)))


def kernel(x, c, ctx, c_ctx, norm_g, w_mod, b_mod, w_in, att_sink, gdn_conv, gdn_a_log, gdn_dt_bias, gdn_norm,
           lru_conv_w, lru_conv_b, lru_w_r, lru_b_r, lru_w_i, lru_b_i, lru_lambda, w_out, final_g):
    b, t, d = x.shape
    ctx_len = ctx.shape[1]
    depth = w_in.shape[0]
    assert ctx_len == TILE and t % TILE == 0 and d == D_MODEL and b == 2
    s = ctx_len + t
    nt = s // TILE

    cos_t, sin_t = _rope_tables(s, ctx_len)
    ct = jnp.pad(jnp.concatenate([c, c_ctx[None]], axis=0).T, ((0, 0), (0, LANES - b - 1)))
    mods = _modulation(ct, w_mod, b_mod)

    w_in_t = jnp.swapaxes(w_in, 1, 2)
    c_src, x_src, src_off = ctx, x, 1
    out = None
    for l in range(depth):
        m3 = mods[l, :3].reshape(3, 3, d)
        sel = jnp.stack([jnp.stack([m3[b], m3[bi]], axis=0) for bi in range(b)], axis=0)
        modsel = jnp.pad(sel, ((0, 0), (0, 0), (0, 5), (0, 0)))
        p, pz, lx = _inproj(c_src, x_src, src_off, modsel, norm_g[l].reshape(1, d), cos_t, sin_t, w_in_t, l)

        att = _attention(p, pz, att_sink[l])

        prep = _gdn_prep(p, 0.5 * gdn_conv[l], _pad_lanes(gdn_a_log[l]), _pad_lanes(gdn_dt_bias[l]))

        wg = jnp.concatenate([_block_diag(lru_w_r[l, 0]), _block_diag(lru_w_i[l, 0]),
                              _block_diag(lru_w_r[l, 1]), _block_diag(lru_w_i[l, 1])], axis=1).astype(BF16)
        bg = 0.5 * jnp.stack([lru_b_r[l, 0], lru_b_i[l, 0], lru_b_r[l, 1], lru_b_i[l, 1]], axis=0)
        o_f, o_b, h_f, h_b = _sweep(*prep, lx, 0.5 * lru_conv_w[l], 0.5 * lru_conv_b[l].reshape(1, -1), wg, bg,
                                    lru_lambda[l])

        final = l == depth - 1
        res = _outproj(c_src, x_src, src_off, modsel, att, o_f, o_b, pz, h_f, h_b, gdn_norm[l].reshape(1, -1),
                       w_out[l].astype(BF16), final_g.reshape(1, d), final)
        if final:
            out = res
        else:
            c_src, x_src, src_off = res, res, 0
    return out
```

```python
import functools
import math

import numpy as np
import jax
import jax.numpy as jnp
from jax import lax
from jax.experimental import pallas as pl
from jax.experimental.pallas import tpu as pltpu

F32 = jnp.float32
BF16 = jnp.bfloat16

EPS = 1e-6
NEG_INF = -1e30
LOG2E = math.log2(math.e)
GRID_W = 64
ROPE_BASE = 10000.0

D_MODEL = 1024
ATT_HEADS = 8
ATT_KV_HEADS = 2
ATT_HEAD_DIM = 64
ATT_BLOCK = 128
GDN_HEADS = 4
GDN_HEAD_DIM = 128
GDN_CHUNK = 64
LRU_WIDTH = 512
LRU_BLOCKS = 8
LRU_BLOCK_DIM = 64
LRU_C = 8.0
N_DIR = 2
MIX = 512
PACK = GDN_HEADS * GDN_CHUNK

TILE = 256
HALO = 8
LANES = 128
VMEM_LIMIT = 56 * 1024 * 1024

C_ATT_Q, C_GDN_Q, C_GDN_K, C_GDN_V = (i * MIX for i in range(4))
C_ATT_K = 4 * MIX
C_ATT_V = C_ATT_K + LANES
C_GDN_B = C_ATT_V + LANES
C_GDN_A = C_GDN_B + LANES
N_P = C_GDN_A + LANES
C_ATT_Z, C_GDN_Z, C_LRU_Z = (N_P + i * MIX for i in range(3))
N_Z = 3 * MIX
C_LRU_X = N_P + N_Z
N_IN = C_LRU_X + MIX
SEG = 8
STEPS = TILE // SEG


def _cparams(*sem):
    return pltpu.CompilerParams(dimension_semantics=sem, vmem_limit_bytes=VMEM_LIMIT)


def _sigmoid(x):
    return 1.0 / (1.0 + jnp.exp(-x))


def _silu(x):
    return x * _sigmoid(x)


def _softplus(x):
    return jnp.maximum(x, 0.0) + jnp.log1p(jnp.exp(-jnp.abs(x)))


def _dot_nt(a, b):
    return lax.dot_general(a.astype(BF16), b.astype(BF16), (((1,), (1,)), ((), ())),
                           preferred_element_type=F32)


def _dot_tn(a, b):
    return lax.dot_general(a.astype(BF16), b.astype(BF16), (((0,), (0,)), ((), ())),
                           preferred_element_type=F32)


def _mod_kernel(ct_ref, w_ref, b_ref, o_ref):
    s = _silu(ct_ref[...])
    w = w_ref[0]
    rid = lax.broadcasted_iota(jnp.int32, (8, w.shape[1]), 0)
    out = jnp.zeros((8, w.shape[1]), F32)
    for r in range(3):
        row = jnp.sum(w * s[:, r:r + 1], axis=0, keepdims=True) + b_ref[0]
        out = jnp.where(rid == r, row, out)
    o_ref[0] = out


def _modulation(ct, w_mod, b_mod):
    depth, d, d3 = w_mod.shape
    return pl.pallas_call(
        _mod_kernel,
        grid=(depth, d3 // d),
        in_specs=[pl.BlockSpec((d, LANES), lambda l, j: (0, 0)),
                  pl.BlockSpec((1, d, d), lambda l, j: (l, 0, j)),
                  pl.BlockSpec((1, 1, d), lambda l, j: (l, 0, j))],
        out_specs=pl.BlockSpec((1, 8, d), lambda l, j: (l, 0, j)),
        out_shape=jax.ShapeDtypeStruct((depth, 8, d3), F32),
        compiler_params=_cparams("parallel", "parallel"),
        name="modulation",
    )(ct, w_mod, b_mod.reshape(depth, 1, d3))


def _rope(x, cos, sin):
    lane = lax.broadcasted_iota(jnp.int32, x.shape, 1)
    first = (lane % 32) < 16
    partner = jnp.where(first, pltpu.roll(x, LANES - 16, 1), pltpu.roll(x, 16, 1))
    return x * cos + partner * sin


def _stream_tile(c_ref, x_ref, tile):
    is_ctx = (jnp.zeros((TILE, 1), jnp.int32) + tile) == 0
    return jnp.where(is_ctx, c_ref[0], x_ref[0])


def _stream_specs(d, off, shift=0):
    return [pl.BlockSpec((1, TILE, d), lambda bi, i: (bi, 0, 0)),
            pl.BlockSpec((1, TILE, d), lambda bi, i: (bi, jnp.maximum(i + shift - off, 1 - off), 0))]


_IN_COLS = []
_src = 0
for _dst, _w in ((C_ATT_Q, MIX), (C_ATT_K, LANES), (C_ATT_V, LANES), (C_ATT_Z, MIX), (C_GDN_Q, MIX), (C_GDN_K, MIX),
                 (C_GDN_V, MIX), (C_GDN_B, N_DIR * GDN_HEADS), (C_GDN_A, N_DIR * GDN_HEADS), (C_GDN_Z, MIX),
                 (C_LRU_X, MIX), (C_LRU_Z, MIX)):
    _IN_COLS.append((_src, _dst, _w))
    _src += _w
IN_WIDTH = _src


def _scan_order(t):
    return jnp.swapaxes(t.reshape(SEG, STEPS, t.shape[1]), 0, 1).reshape(t.shape)


def _time_order(t):
    return jnp.swapaxes(t.reshape(STEPS, SEG, t.shape[1]), 0, 1).reshape(t.shape)


def _inproj_kernel(c_ref, x_ref, mod_ref, g_ref, cos_ref, sin_ref, w_ref, o_ref, z_ref, lx_ref, w_s):
    @pl.when(jnp.logical_and(pl.program_id(0) == 0, pl.program_id(1) == 0))
    def _():
        lane = lax.broadcasted_iota(jnp.int32, (TILE, LANES), 1)
        n_gate = N_DIR * GDN_HEADS
        for r0 in range(0, w_s.shape[0], TILE):
            for src, dst, width in _IN_COLS:
                if width >= LANES:
                    w_s[r0:r0 + TILE, dst:dst + width] = w_ref[0, src:src + width, r0:r0 + TILE].T.astype(BF16)
            src_b = _IN_COLS[7][0]
            win = w_ref[0, src_b:src_b + LANES, r0:r0 + TILE].T
            w_s[r0:r0 + TILE, C_GDN_B:C_GDN_B + LANES] = jnp.where(lane < n_gate, win, 0.0).astype(BF16)
            w_s[r0:r0 + TILE, C_GDN_A:C_GDN_A + LANES] = jnp.where(
                lane < n_gate, pltpu.roll(win, LANES - n_gate, 1), 0.0).astype(BF16)

    x = _stream_tile(c_ref, x_ref, pl.program_id(1))
    ms = jnp.mean(x * x, axis=-1, keepdims=True)
    y = x * lax.rsqrt(ms + EPS) * g_ref[...]
    shift = mod_ref[0, 0, 0:1, :]
    scale = mod_ref[0, 0, 1:2, :]
    hb = (y * (1.0 + scale) + shift).astype(BF16)
    cos = cos_ref[...]
    sin = sin_ref[...]
    for c0 in range(0, N_IN, MIX):
        res = jnp.dot(hb, w_s[:, c0:c0 + MIX], preferred_element_type=F32)
        if c0 == C_ATT_Q:
            res = jnp.concatenate(
                [_rope(res[:, k:k + LANES], cos, sin) for k in range(0, MIX, LANES)], axis=1)
        if c0 == C_ATT_K:
            res = jnp.concatenate([_rope(res[:, :LANES], cos, sin), res[:, LANES:]], axis=1)
        if c0 == C_LRU_X:
            lx_ref[0] = _scan_order(res)
        elif c0 >= N_P:
            z_ref[0, :, c0 - N_P:c0 - N_P + MIX] = res.astype(z_ref.dtype)
        else:
            o_ref[0, :, c0:c0 + MIX] = res


def _inproj(c_src, x_src, off, modsel, g, cos_t, sin_t, w_in, layer):
    b, _, d = x_src.shape
    s = x_src.shape[1] + off * TILE
    nt = s // TILE
    return pl.pallas_call(
        _inproj_kernel,
        grid=(b, nt),
        in_specs=_stream_specs(d, off) + [
                  pl.BlockSpec((1, 1, 8, d), lambda bi, i: (bi, jnp.minimum(i, 1), 0, 0)),
                  pl.BlockSpec((1, d), lambda bi, i: (0, 0)),
                  pl.BlockSpec((TILE, LANES), lambda bi, i: (i, 0)),
                  pl.BlockSpec((TILE, LANES), lambda bi, i: (i, 0)),
                  pl.BlockSpec((1, IN_WIDTH, d), lambda bi, i: (layer, 0, 0), pipeline_mode=pl.Buffered(1))],
        out_specs=[pl.BlockSpec((1, TILE, N_P), lambda bi, i: (bi, i, 0)),
                   pl.BlockSpec((1, TILE, N_Z), lambda bi, i: (bi, i, 0)),
                   pl.BlockSpec((1, TILE, MIX), lambda bi, i: (bi, i, 0))],
        out_shape=[jax.ShapeDtypeStruct((b, s, N_P), F32), jax.ShapeDtypeStruct((b, s, N_Z), BF16),
                   jax.ShapeDtypeStruct((b, s, MIX), F32)],
        scratch_shapes=[pltpu.VMEM((d, N_IN), BF16)],
        compiler_params=_cparams("arbitrary", "arbitrary"),
        name="inproj",
    )(c_src, x_src, modsel, g, cos_t, sin_t, w_in)


def _att_kernel(sink_ref, q_ref, z_ref, kvp_ref, kvc_ref, kvn_ref, kvx_ref, o_ref, *, n_blocks):
    i = pl.program_id(1)
    n_sub = TILE // ATT_BLOCK
    n_band = 3 * ATT_BLOCK
    half = ATT_HEAD_DIM
    group = ATT_HEADS // ATT_KV_HEADS

    kv = jnp.concatenate([kvp_ref[0], kvc_ref[0], kvn_ref[0], kvx_ref[0]], axis=0)
    nk = kv.shape[0]
    k_all, v_all = kv[:, :LANES], kv[:, LANES:]
    k_sw, v_sw = pltpu.roll(k_all, half, 1), pltpu.roll(v_all, half, 1)
    lo_k = lax.broadcasted_iota(jnp.int32, (nk, LANES), 1) < half
    k_dup = [jnp.where(lo_k, k_all, k_sw).astype(BF16), jnp.where(lo_k, k_sw, k_all).astype(BF16)]
    v_dup = [jnp.where(lo_k, v_all, v_sw).astype(BF16), jnp.where(lo_k, v_sw, v_all).astype(BF16)]
    n_loc = (n_sub + 2) * ATT_BLOCK

    def keys(t, sub):
        return jnp.concatenate([t[sub * ATT_BLOCK:sub * ATT_BLOCK + n_band], t[n_loc:]], axis=0)

    r = lax.broadcasted_iota(jnp.int32, (ATT_BLOCK, n_band), 0)
    c = lax.broadcasted_iota(jnp.int32, (ATT_BLOCK, n_band), 1)
    in_win = jnp.abs(r - (c - ATT_BLOCK)) <= ATT_BLOCK
    lo_q = lax.broadcasted_iota(jnp.int32, (ATT_BLOCK, LANES), 1) < half
    outs = [[None] * (ATT_HEADS // 2) for _ in range(n_sub)]

    def band_bias(sub):
        m = i * n_sub + sub
        c_lo = jnp.where(m >= n_sub + 1, 0, jnp.where(m >= n_sub, ATT_BLOCK, n_band))
        c_hi = jnp.where(m < n_sub, 0, jnp.where(m <= n_blocks - 2, n_band, 2 * ATT_BLOCK))
        return jnp.where(jnp.logical_and(in_win, jnp.logical_and(c >= c_lo, c < c_hi)), 0.0, NEG_INF)

    def stream(sub, h, bias):
        q = q_ref[0, sub * ATT_BLOCK:(sub + 1) * ATT_BLOCK, :] * (ATT_HEAD_DIM ** -0.5 * LOG2E)
        lhs = []
        for j in range(h * group, (h + 1) * group):
            q_t = q[:, (j // 2) * LANES:(j // 2 + 1) * LANES]
            lhs.append(jnp.where(lo_q, q_t, 0.0) if j % 2 == 0 else jnp.where(lo_q, 0.0, q_t))
        qs = jnp.concatenate(lhs, axis=0).astype(BF16)
        s = lax.dot_general(qs, keys(k_dup[h], sub), (((1,), (1,)), ((), ())), preferred_element_type=F32)
        yield
        p_blocks, den_blocks = [], []
        for e in range(group):
            sb = s[e * ATT_BLOCK:(e + 1) * ATT_BLOCK]
            sk = sink_ref[h * group + e] * LOG2E
            s_band = sb[:, :n_band] + bias
            s_ctx = sb[:, n_band:]
            mx = jnp.maximum(jnp.maximum(jnp.max(s_band, axis=-1, keepdims=True),
                                         jnp.max(s_ctx, axis=-1, keepdims=True)), sk)
            p_band = jnp.exp2(s_band - mx)
            p_ctx = jnp.exp2(s_ctx - mx)
            den_blocks.append(jnp.sum(p_band, axis=-1, keepdims=True) + jnp.sum(p_ctx, axis=-1, keepdims=True)
                              + jnp.exp2(sk - mx))
            p_blocks.append(jnp.concatenate([p_band, p_ctx], axis=1).astype(BF16))
            yield
        p = jnp.concatenate(p_blocks, axis=0)
        res = (jnp.dot(p, keys(v_dup[h], sub), preferred_element_type=F32)
               / jnp.concatenate(den_blocks, axis=0))
        for t in range(group // 2):
            even = res[(2 * t) * ATT_BLOCK:(2 * t + 1) * ATT_BLOCK]
            odd = res[(2 * t + 1) * ATT_BLOCK:(2 * t + 2) * ATT_BLOCK]
            outs[sub][h * (group // 2) + t] = jnp.where(lo_q, even, odd)
        yield

    for sub in range(n_sub):
        bias = band_bias(sub)
        gens = [stream(sub, h, bias) for h in range(ATT_KV_HEADS)]
        for g in gens:
            next(g)
        for g in gens:
            for _ in g:
                pass
    for sub in range(n_sub):
        rows = slice(sub * ATT_BLOCK, (sub + 1) * ATT_BLOCK)
        o = jnp.concatenate(outs[sub], axis=1)
        o_ref[0, rows, :] = (o * _silu(z_ref[0, rows, :].astype(F32))).astype(o_ref.dtype)


def _attention(p, pz, sink):
    b, s, _ = p.shape
    nb = s // ATT_BLOCK
    n_sub = TILE // ATT_BLOCK
    kvcol = C_ATT_K // (2 * LANES)
    kv_spec = lambda f: pl.BlockSpec((1, ATT_BLOCK, 2 * LANES), f)
    return pl.pallas_call(
        functools.partial(_att_kernel, n_blocks=nb),
        grid=(b, s // TILE),
        in_specs=[pl.BlockSpec(memory_space=pltpu.SMEM),
                  pl.BlockSpec((1, TILE, MIX), lambda bi, i: (bi, i, C_ATT_Q // MIX)),
                  pl.BlockSpec((1, TILE, MIX), lambda bi, i: (bi, i, (C_ATT_Z - N_P) // MIX)),
                  kv_spec(lambda bi, i: (bi, jnp.maximum(i * n_sub - 1, n_sub), kvcol)),
                  pl.BlockSpec((1, TILE, 2 * LANES), lambda bi, i: (bi, i, kvcol)),
                  kv_spec(lambda bi, i: (bi, jnp.minimum((i + 1) * n_sub, nb - 1), kvcol)),
                  pl.BlockSpec((1, TILE, 2 * LANES), lambda bi, i: (bi, 0, kvcol))],
        out_specs=pl.BlockSpec((1, TILE, MIX), lambda bi, i: (bi, i, 0)),
        out_shape=jax.ShapeDtypeStruct((b, s, MIX), BF16),
        compiler_params=_cparams("parallel", "parallel"),
        name="attention",
    )(sink, p, pz, p, p, p, p)


def _conv4(x, hp, hn, w, prev_ok, next_ok):
    hp = hp * prev_ok
    hn = hn * next_ok
    ext = jnp.concatenate([hp, x, hn], axis=0)
    n = x.shape[0]
    own = slice(HALO, HALO + n)
    back2, back1, fwd1 = (pltpu.roll(ext, sh, 0) for sh in (2, 1, ext.shape[0] - 1))
    return w[0:1] * back2[own] + w[1:2] * back1[own] + w[2:3] * ext[own] + w[3:4] * fwd1[own]


def _halo_ok(i, n_tiles):
    prev_ok = (i >= 2).astype(F32)
    next_ok = jnp.logical_and(i != 0, i != n_tiles - 1).astype(F32)
    return prev_ok, next_ok


def _chunk_cumsum(g, reverse):
    n = g.shape[0]
    row = lax.broadcasted_iota(jnp.int32, g.shape, 0) % GDN_CHUNK
    s = 1
    while s < GDN_CHUNK:
        if reverse:
            g = g + jnp.where(row < GDN_CHUNK - s, pltpu.roll(g, n - s, 0), 0.0)
        else:
            g = g + jnp.where(row >= s, pltpu.roll(g, s, 0), 0.0)
        s *= 2
    return g


def _spread(cols, masks):
    out = cols[-1]
    for m, col in zip(masks[-2::-1], cols[-2::-1]):
        out = jnp.where(m, col, out)
    return out


def _head_block_diag(x, masks):
    return jnp.concatenate([jnp.where(m, x, 0.0) for m in masks], axis=0).astype(BF16)


def _gdn_prep_kernel(q_ref, qp_ref, qn_ref, k_ref, kp_ref, kn_ref, v_ref, vp_ref, vn_ref,
                     b_ref, a_ref, cw_ref, alog_ref, dtb_ref,
                     w_ref, u_ref, kd_ref, qg_ref, aqk_ref, gl_ref, *, n_tiles):
    i = pl.program_id(1)
    prev_ok, next_ok = _halo_ok(i, n_tiles)
    hd, ch, nh = GDN_HEAD_DIM, GDN_CHUNK, GDN_HEADS
    nch = TILE // ch

    def prep(x_ref, p_ref, n_ref, col):
        xh = _conv4(x_ref[0], p_ref[0], n_ref[0], cw_ref[:, col:col + MIX], prev_ok, next_ok)
        return xh * (1.0 + jnp.tanh(xh))

    def l2n(t):
        parts = []
        for h in range(nh):
            th = t[:, h * hd:(h + 1) * hd]
            parts.append(th * lax.rsqrt(jnp.sum(th * th, axis=-1, keepdims=True) + EPS))
        return jnp.concatenate(parts, axis=1)

    q = l2n(prep(q_ref, qp_ref, qn_ref, 0)) * (hd ** -0.5)
    k = l2n(prep(k_ref, kp_ref, kn_ref, MIX))
    v = prep(v_ref, vp_ref, vn_ref, 2 * MIX)

    lane = lax.broadcasted_iota(jnp.int32, (TILE, LANES), 1)
    g = -jnp.exp(alog_ref[...]) * _softplus(a_ref[0] + dtb_ref[...])
    gcum = jnp.where(lane < nh, _chunk_cumsum(g, False), _chunk_cumsum(g, True))
    beta = _sigmoid(b_ref[0])

    ii = lax.broadcasted_iota(jnp.int32, (ch, PACK), 0)
    ll = lax.broadcasted_iota(jnp.int32, (ch, PACK), 1)
    jj = ll % ch
    eye_p = jj == ii
    eye_f = eye_p.astype(F32)
    bd16 = (jj // 16) == (ii // 16)
    bd32 = (jj // 32) == (ii // 32)
    merge_masks = (jnp.logical_and(bd32, jnp.logical_not(bd16)), jnp.logical_not(bd32))
    strict = [jj < ii, jj > ii]
    incl = [jj <= ii, jj >= ii]
    head_p = [(ll // ch) == h for h in range(nh)]
    lw = lax.broadcasted_iota(jnp.int32, (ch, MIX), 1) // hd
    head_w = [lw == h for h in range(nh)]
    lane_row = lax.broadcasted_iota(jnp.int32, (1, LANES), 1)

    def pdot(a, b):
        return jnp.dot(a.astype(BF16), _head_block_diag(b, head_p), preferred_element_type=F32)

    gl_rows = []
    units = []
    for cidx in range(nch):
        rows = slice(cidx * ch, (cidx + 1) * ch)
        kc, qc, vc, gc, bc = k[rows], q[rows], v[rows], gcum[rows], beta[rows]
        kq = _dot_nt(jnp.concatenate([kc, qc], axis=0), _head_block_diag(kc, head_w))
        g_end = jnp.where(lane_row < nh, gc[ch - 1:ch, :], gc[0:1, :])
        gl_rows.append(jnp.exp(g_end))
        for d in range(N_DIR):
            gcols = [gc[:, d * nh + h:d * nh + h + 1] for h in range(nh)]
            bcols = [bc[:, d * nh + h:d * nh + h + 1] for h in range(nh)]
            gcol_p = _spread(gcols, head_p)
            grow_p = jnp.sum(jnp.where(eye_p, gcol_p, 0.0), axis=0, keepdims=True)
            dec = jnp.exp(jnp.where(incl[d], gcol_p - grow_p, 0.0))
            lm = jnp.where(strict[d], _spread(bcols, head_p) * dec * kq[:ch], 0.0)
            aqk = jnp.where(incl[d], dec * kq[ch:], 0.0)
            aqk_ref[0, rows, d * PACK:(d + 1) * PACK] = aqk.astype(BF16)
            units.append((rows, d, lm, kc, qc, vc, gcols, bcols,
                          [g_end[:, d * nh + h:d * nh + h + 1] for h in range(nh)]))
    gl_rows.append(jnp.zeros((8 - nch, LANES), F32))
    gl_ref[0, 0] = jnp.concatenate(gl_rows, axis=0)

    lms = [u[2] for u in units]
    m = [jnp.where(bd16, -lm, 0.0) for lm in lms]
    x = [eye_f + mi for mi in m]
    pw = [pdot(mi, mi) for mi in m]
    for _ in range(2):
        r = [pdot(jnp.concatenate([xi, pi], axis=0), pi) for xi, pi in zip(x, pw)]
        x = [xi + ri[:ch] for xi, ri in zip(x, r)]
        pw = [ri[ch:] for ri in r]
    x = [xi + pdot(xi, pi) for xi, pi in zip(x, pw)]
    for mask in merge_masks:
        y = [pdot(xi, jnp.where(mask, lm, 0.0)) for xi, lm in zip(x, lms)]
        x = [xi - pdot(yi, xi) for xi, yi in zip(x, y)]

    for tinv, (rows, d, _, kc, qc, vc, gcols, bcols, gend_cols) in zip(x, units):
        slab = lambda t, h: t[:, h * hd:(h + 1) * hd]
        gam = [jnp.exp(gcol) for gcol in gcols]
        bgk = jnp.concatenate([(bcols[h] * gam[h]) * slab(kc, h) for h in range(nh)], axis=1)
        bv = jnp.concatenate([bcols[h] * slab(vc, h) for h in range(nh)], axis=1)
        rhs = jnp.concatenate([_head_block_diag(bgk, head_w), _head_block_diag(bv, head_w)], axis=1)
        wu = jnp.dot(tinv.astype(BF16), rhs, preferred_element_type=F32)
        cols = slice(d * MIX, (d + 1) * MIX)
        w_ref[0, rows, cols] = wu[:, :MIX].astype(BF16)
        u_ref[0, rows, cols] = wu[:, MIX:]
        kd_ref[0, rows, cols] = jnp.concatenate(
            [jnp.exp(gend_cols[h] - gcols[h]) * slab(kc, h) for h in range(nh)], axis=1).astype(BF16)
        qg_ref[0, rows, cols] = jnp.concatenate([gam[h] * slab(qc, h) for h in range(nh)], axis=1).astype(BF16)


def _gdn_prep(p, conv_w, alog, dtb):
    b, s, _ = p.shape
    nt = s // TILE
    hpt = TILE // HALO
    nh = s // HALO
    wide = N_DIR * GDN_HEADS * GDN_HEAD_DIM

    def tile_spec(col):
        return pl.BlockSpec((1, TILE, MIX), lambda bi, i: (bi, i, col // MIX))

    def prev_spec(col):
        return pl.BlockSpec((1, HALO, MIX), lambda bi, i: (bi, jnp.maximum(i * hpt - 1, 0), col // MIX))

    def next_spec(col):
        return pl.BlockSpec((1, HALO, MIX), lambda bi, i: (bi, jnp.minimum((i + 1) * hpt, nh - 1), col // MIX))

    in_specs = []
    for col in (C_GDN_Q, C_GDN_K, C_GDN_V):
        in_specs += [tile_spec(col), prev_spec(col), next_spec(col)]
    in_specs += [pl.BlockSpec((1, TILE, LANES), lambda bi, i: (bi, i, C_GDN_B // LANES)),
                 pl.BlockSpec((1, TILE, LANES), lambda bi, i: (bi, i, C_GDN_A // LANES)),
                 pl.BlockSpec((4, 3 * MIX), lambda bi, i: (0, 0)),
                 pl.BlockSpec((1, LANES), lambda bi, i: (0, 0)),
                 pl.BlockSpec((1, LANES), lambda bi, i: (0, 0))]
    wide_spec = pl.BlockSpec((1, TILE, wide), lambda bi, i: (bi, i, 0))
    out_specs = [wide_spec, wide_spec, wide_spec, wide_spec,
                 pl.BlockSpec((1, TILE, N_DIR * PACK), lambda bi, i: (bi, i, 0)),
                 pl.BlockSpec((1, 1, 8, LANES), lambda bi, i: (bi, i, 0, 0))]
    out_shape = [jax.ShapeDtypeStruct((b, s, wide), BF16),
                 jax.ShapeDtypeStruct((b, s, wide), F32),
                 jax.ShapeDtypeStruct((b, s, wide), BF16),
                 jax.ShapeDtypeStruct((b, s, wide), BF16),
                 jax.ShapeDtypeStruct((b, s, N_DIR * PACK), BF16),
                 jax.ShapeDtypeStruct((b, nt, 8, LANES), F32)]
    return pl.pallas_call(
        functools.partial(_gdn_prep_kernel, n_tiles=nt),
        grid=(b, nt),
        in_specs=in_specs,
        out_specs=out_specs,
        out_shape=out_shape,
        compiler_params=_cparams("parallel", "parallel"),
        name="gdn_prep",
    )(p, p, p, p, p, p, p, p, p, p, p, conv_w, alog, dtb)


def _gdn_scan_body(bi, wf, uf, kdf, qgf, af, glf, wb, ub, kdb, qgb, ab, glb, of_ref, ob_ref, state):
    hd, ch, nh = GDN_HEAD_DIM, GDN_CHUNK, GDN_HEADS
    nch = TILE // ch
    lw = lax.broadcasted_iota(jnp.int32, (ch, MIX), 1) // hd
    head_w = [lw == h for h in range(nh)]
    dirs = ((wf, uf, kdf, qgf, af, glf, of_ref), (wb, ub, kdb, qgb, ab, glb, ob_ref))
    for step in range(nch):
        cidx = [step, nch - 1 - step]
        rows = [slice(c * ch, (c + 1) * ch) for c in cidx]
        s_old = {}
        ws = {}
        for d, (w_r, _, _, qg_r, _, _, _) in enumerate(dirs):
            for h in range(nh):
                cols = slice(h * hd, (h + 1) * hd)
                s_old[d, h] = state[bi, d, h]
                lhs = jnp.concatenate([w_r[bi, rows[d], cols], qg_r[bi, rows[d], cols]], axis=0)
                ws[d, h] = jnp.dot(lhs, s_old[d, h].astype(BF16), preferred_element_type=F32)
            yield
        for d, (_, u_r, kd_r, _, a_r, gl_r, o_r) in enumerate(dirs):
            un = u_r[bi, rows[d], :] - jnp.concatenate([ws[d, h][:ch] for h in range(nh)], axis=1)
            inter = jnp.concatenate([ws[d, h][ch:] for h in range(nh)], axis=1)
            intra = jnp.dot(a_r[bi, rows[d], :], _head_block_diag(un, head_w), preferred_element_type=F32)
            o_r[bi, rows[d], :] = (inter + intra).astype(o_r.dtype)
            unb = un.astype(BF16)
            for h in range(nh):
                cols = slice(h * hd, (h + 1) * hd)
                gl = gl_r[bi, 0, cidx[d]:cidx[d] + 1, d * nh + h:d * nh + h + 1]
                state[bi, d, h] = gl * s_old[d, h] + _dot_tn(kd_r[bi, rows[d], cols], unb[:, cols])
            yield


def _mirror(j, nt):
    return jnp.where(j == 0, 0, nt - j)


def _lru_body(bi, xf_ref, xfp2_ref, xfp_ref, xfn_ref, xb_ref, xbp2_ref, xbp_ref, xbn_ref, cw_ref, cb_ref, wg_ref, bg_ref,
              lam_ref, hf_ref, hb_ref, carry, *, n_tiles):
    j = pl.program_id(0)
    seg_n = SEG
    steps = STEPS
    sub = lax.broadcasted_iota(jnp.int32, (seg_n, LRU_WIDTH), 0)

    def one_dir(d, x_ref, p2_ref, p_ref, n_ref, tile_idx, o_ref):
        prev_ok, next_ok = _halo_ok(tile_idx, n_tiles)
        xs = [x_ref[bi, st * seg_n:(st + 1) * seg_n, :] for st in range(steps)]
        last = seg_n - 1
        p1 = jnp.where(sub == 0, p_ref[bi, last:last + 1, :] * prev_ok, pltpu.roll(xs[steps - 1], 1, 0))
        p2 = jnp.where(sub == 0, p2_ref[bi, last:last + 1, :] * prev_ok, pltpu.roll(xs[steps - 2], 1, 0))
        n1 = jnp.where(sub == last, n_ref[bi, 0:1, :] * next_ok, pltpu.roll(xs[0], last, 0))
        ext = [p2, p1] + xs + [n1]
        cw = cw_ref[...]
        xh = jnp.concatenate(
            [cw[0:1] * ext[st] + cw[1:2] * ext[st + 1] + cw[2:3] * ext[st + 2] + cw[3:4] * ext[st + 3]
             for st in range(steps)], axis=0) + cb_ref[...]
        yield
        gates = jnp.dot(xh.astype(BF16), wg_ref[:, d * 2 * LRU_WIDTH:(d + 1) * 2 * LRU_WIDTH],
                        preferred_element_type=F32)
        yield
        c_half = (-0.5 * LRU_C) * _softplus(-lam_ref[d:d + 1, :])
        a_blocks, b_blocks = [], []
        blk = TILE // 4
        for r0 in range(0, TILE, blk):
            gb = gates[r0:r0 + blk]
            tr = jnp.tanh(gb[:, :LRU_WIDTH] + bg_ref[2 * d:2 * d + 1, :])
            ti = jnp.tanh(gb[:, LRU_WIDTH:] + bg_ref[2 * d + 1:2 * d + 2, :])
            log_a = c_half * tr + c_half
            ab = jnp.exp(log_a)
            b_blocks.append(jnp.sqrt(-jnp.tanh(log_a) * (ab * ab + 1.0)) * ((ti + 1.0) * xh[r0:r0 + blk]))
            a_blocks.append(ab)
            yield
        a = jnp.concatenate(a_blocks, axis=0)
        bb = jnp.concatenate(b_blocks, axis=0)

        order = list(range(steps)) if d == 0 else list(range(steps - 1, -1, -1))
        piece = lambda t, st: t[st * seg_n:(st + 1) * seg_n]
        a_tot = piece(a, order[0])
        b_tot = piece(bb, order[0])
        for n, st in enumerate(order[1:]):
            b_tot = piece(a, st) * b_tot + piece(bb, st)
            a_tot = piece(a, st) * a_tot
            if n % 8 == 7:
                yield
        s = 1
        while s < seg_n:
            if d == 0:
                keep = sub >= s
                a_sh = jnp.where(keep, pltpu.roll(a_tot, s, 0), 1.0)
                b_sh = jnp.where(keep, pltpu.roll(b_tot, s, 0), 0.0)
            else:
                keep = sub < seg_n - s
                a_sh = jnp.where(keep, pltpu.roll(a_tot, seg_n - s, 0), 1.0)
                b_sh = jnp.where(keep, pltpu.roll(b_tot, seg_n - s, 0), 0.0)
            b_tot = a_tot * b_sh + b_tot
            a_tot = a_tot * a_sh
            s *= 2
        h_in0 = carry[bi, d]
        h_out = a_tot * h_in0 + b_tot
        if d == 0:
            h = jnp.where(sub == 0, h_in0, pltpu.roll(h_out, 1, 0))
            carry[bi, d] = h_out[seg_n - 1:seg_n]
        else:
            h = jnp.where(sub == seg_n - 1, h_in0, pltpu.roll(h_out, seg_n - 1, 0))
            carry[bi, d] = h_out[0:1]
        hs = [None] * steps
        for n, st in enumerate(order):
            h = piece(a, st) * h + piece(bb, st)
            hs[st] = h
            if n % 8 == 7:
                yield
        o_ref[bi] = jnp.concatenate(hs, axis=0).astype(o_ref.dtype)

    return [one_dir(0, xf_ref, xfp2_ref, xfp_ref, xfn_ref, j, hf_ref),
            one_dir(1, xb_ref, xbp2_ref, xbp_ref, xbn_ref, _mirror(j, n_tiles), hb_ref)]


N_SCAN_IN = 12
N_LRU_IN = 13


def _round_robin(gens, weights):
    live = list(zip(gens, weights))
    while live:
        nxt = []
        for g, w in live:
            alive = True
            for _ in range(w):
                try:
                    next(g)
                except StopIteration:
                    alive = False
                    break
            if alive:
                nxt.append((g, w))
        live = nxt


def _sweep_kernel(*refs, n_tiles, n_batch):
    scan_in, lru_in = refs[:N_SCAN_IN], refs[N_SCAN_IN:N_SCAN_IN + N_LRU_IN]
    of_ref, ob_ref, hf_ref, hb_ref, state, carry = refs[N_SCAN_IN + N_LRU_IN:]

    @pl.when(pl.program_id(0) == 0)
    def _():
        state[...] = jnp.zeros_like(state)
        carry[...] = jnp.zeros_like(carry)

    scans = [_gdn_scan_body(bi, *scan_in, of_ref, ob_ref, state) for bi in range(n_batch)]
    lrus = [g for bi in range(n_batch) for g in _lru_body(bi, *lru_in, hf_ref, hb_ref, carry, n_tiles=n_tiles)]
    _round_robin(scans + lrus, (1,) * len(scans) + (2,) * len(lrus))


def _sweep(w, u, kd, qg, aqk, gl, lx, conv_w, conv_b, wg, bg, lam):
    b, s, _ = u.shape
    nt = s // TILE
    half = GDN_HEADS * GDN_HEAD_DIM
    hpt = TILE // HALO
    nh = s // HALO
    tile_of = (lambda j: j, lambda j: _mirror(j, nt))

    def scan_specs(d):
        idx = lambda j: (0, tile_of[d](j), d)
        gidx = lambda j: (0, tile_of[d](j), 0, 0)
        return [pl.BlockSpec((b, TILE, half), idx)] * 4 + [pl.BlockSpec((b, TILE, PACK), idx),
                                                           pl.BlockSpec((b, 1, 8, LANES), gidx)]

    def lru_specs(d):
        t = tile_of[d]
        return [pl.BlockSpec((b, TILE, MIX), lambda j: (0, t(j), 0)),
                pl.BlockSpec((b, HALO, MIX), lambda j: (0, jnp.maximum(t(j) * hpt - 2, 0), 0)),
                pl.BlockSpec((b, HALO, MIX), lambda j: (0, jnp.maximum(t(j) * hpt - 1, 0), 0)),
                pl.BlockSpec((b, HALO, MIX), lambda j: (0, jnp.minimum((t(j) + 1) * hpt, nh - 1), 0))]

    full = lambda shape: pl.BlockSpec(shape, lambda j: tuple(0 for _ in shape))
    out_spec = lambda d: pl.BlockSpec((b, TILE, MIX), lambda j: (0, tile_of[d](j), 0))
    in_specs = (scan_specs(0) + scan_specs(1) + lru_specs(0) + lru_specs(1)
                + [full((4, LRU_WIDTH)), full((1, LRU_WIDTH)), full((LRU_WIDTH, 4 * LRU_WIDTH)),
                   full((4, LRU_WIDTH)), full((N_DIR, LRU_WIDTH))])
    assert len(in_specs) == N_SCAN_IN + N_LRU_IN
    return pl.pallas_call(
        functools.partial(_sweep_kernel, n_tiles=nt, n_batch=b),
        grid=(nt,),
        in_specs=in_specs,
        out_specs=[out_spec(0), out_spec(1), out_spec(0), out_spec(1)],
        out_shape=[jax.ShapeDtypeStruct((b, s, MIX), BF16)] * 4,
        scratch_shapes=[pltpu.VMEM((b, N_DIR, GDN_HEADS, GDN_HEAD_DIM, GDN_HEAD_DIM), F32),
                        pltpu.VMEM((b, N_DIR, 1, LRU_WIDTH), F32)],
        compiler_params=_cparams("arbitrary"),
        name="sweep",
    )(w, u, kd, qg, aqk, gl, w, u, kd, qg, aqk, gl, *([lx] * 8), conv_w, conv_b, wg, bg, lam)


def _outproj_kernel(c_ref, x_ref, mod_ref, att_ref, of_ref, ob_ref, gz_ref, hf_ref, hb_ref, lz_ref,
                    gn_ref, w_ref, fg_ref, o_ref, *, final, first_tile):
    hd = GDN_HEAD_DIM
    o = of_ref[0].astype(F32) + ob_ref[0].astype(F32)
    parts = []
    for h in range(GDN_HEADS):
        oh = o[:, h * hd:(h + 1) * hd]
        parts.append(oh * lax.rsqrt(jnp.mean(oh * oh, axis=-1, keepdims=True) + EPS) * gn_ref[...])
    gdn = jnp.concatenate(parts, axis=1) * _silu(gz_ref[0].astype(F32))
    lru = _time_order(hf_ref[0].astype(F32) + hb_ref[0].astype(F32)) * _silu(lz_ref[0].astype(F32))
    acc = (jnp.dot(att_ref[0], w_ref[0:MIX, :], preferred_element_type=F32)
           + jnp.dot(gdn.astype(BF16), w_ref[MIX:2 * MIX, :], preferred_element_type=F32)
           + jnp.dot(lru.astype(BF16), w_ref[2 * MIX:3 * MIX, :], preferred_element_type=F32))
    gate = mod_ref[0, 0, 2:3, :]
    xn = _stream_tile(c_ref, x_ref, pl.program_id(1) + first_tile) + gate * acc
    if final:
        ms = jnp.mean(xn * xn, axis=-1, keepdims=True)
        xn = xn * lax.rsqrt(ms + EPS) * fg_ref[...]
    o_ref[0] = xn


def _outproj(c_src, x_src, src_off, modsel, att, of, ob, pz, hf, hb, gdn_norm, w_out, final_g, final):
    b, s, _ = pz.shape
    d = x_src.shape[2]
    nt = s // TILE
    off = 1 if final else 0
    n_out = nt - off
    tok = lambda col: pl.BlockSpec((1, TILE, MIX), lambda bi, i: (bi, i + off, col))
    return pl.pallas_call(
        functools.partial(_outproj_kernel, final=final, first_tile=off),
        grid=(b, n_out),
        in_specs=_stream_specs(d, src_off, off) + [
                  pl.BlockSpec((1, 1, 8, d), lambda bi, i: (bi, jnp.minimum(i + off, 1), 0, 0)),
                  tok(0), tok(0), tok(0), tok((C_GDN_Z - N_P) // MIX), tok(0), tok(0), tok((C_LRU_Z - N_P) // MIX),
                  pl.BlockSpec((1, GDN_HEAD_DIM), lambda bi, i: (0, 0)),
                  pl.BlockSpec((3 * MIX, d), lambda bi, i: (0, 0)),
                  pl.BlockSpec((1, d), lambda bi, i: (0, 0))],
        out_specs=pl.BlockSpec((1, TILE, d), lambda bi, i: (bi, i, 0)),
        out_shape=jax.ShapeDtypeStruct((b, n_out * TILE, d), F32),
        compiler_params=_cparams("parallel", "parallel"),
        name="outproj",
    )(c_src, x_src, modsel, att, of, ob, pz, hf, hb, pz, gdn_norm, w_out, final_g)


def _rope_tables(s, ctx_len):
    t = np.arange(s - ctx_len)
    n_freq = ATT_HEAD_DIM // 4
    inv = ROPE_BASE ** (-np.arange(n_freq, dtype=np.float64) / n_freq)
    ang = [(t // GRID_W)[:, None] * inv, (t % GRID_W)[:, None] * inv]
    cos = np.concatenate([np.cos(ang[0]), np.cos(ang[0]), np.cos(ang[1]), np.cos(ang[1])], axis=1)
    sin = np.concatenate([-np.sin(ang[0]), np.sin(ang[0]), -np.sin(ang[1]), np.sin(ang[1])], axis=1)
    cos = np.concatenate([np.ones((ctx_len, ATT_HEAD_DIM)), cos], axis=0)
    sin = np.concatenate([np.zeros((ctx_len, ATT_HEAD_DIM)), sin], axis=0)
    return (jnp.asarray(np.tile(cos, (1, 2)), F32), jnp.asarray(np.tile(sin, (1, 2)), F32))


def _block_diag(w):
    eye = jnp.eye(LRU_BLOCKS, dtype=w.dtype)
    return jnp.einsum("nde,nm->ndme", w, eye).reshape(LRU_WIDTH, LRU_WIDTH)


def _pad_lanes(v):
    return jnp.pad(v.reshape(1, -1), ((0, 0), (0, LANES - v.size)))


def kernel(x, c, ctx, c_ctx, norm_g, w_mod, b_mod, w_in, att_sink, gdn_conv, gdn_a_log, gdn_dt_bias, gdn_norm,
           lru_conv_w, lru_conv_b, lru_w_r, lru_b_r, lru_w_i, lru_b_i, lru_lambda, w_out, final_g):
    b, t, d = x.shape
    ctx_len = ctx.shape[1]
    depth = w_in.shape[0]
    assert ctx_len == TILE and t % TILE == 0 and d == D_MODEL and b == 2
    s = ctx_len + t
    nt = s // TILE

    cos_t, sin_t = _rope_tables(s, ctx_len)
    ct = jnp.pad(jnp.concatenate([c, c_ctx[None]], axis=0).T, ((0, 0), (0, LANES - b - 1)))
    mods = _modulation(ct, w_mod, b_mod)

    w_in_t = jnp.swapaxes(w_in, 1, 2)
    c_src, x_src, src_off = ctx, x, 1
    out = None
    for l in range(depth):
        m3 = mods[l, :3].reshape(3, 3, d)
        sel = jnp.stack([jnp.stack([m3[b], m3[bi]], axis=0) for bi in range(b)], axis=0)
        modsel = jnp.pad(sel, ((0, 0), (0, 0), (0, 5), (0, 0)))
        p, pz, lx = _inproj(c_src, x_src, src_off, modsel, norm_g[l].reshape(1, d), cos_t, sin_t, w_in_t, l)

        att = _attention(p, pz, att_sink[l])

        prep = _gdn_prep(p, 0.5 * gdn_conv[l], _pad_lanes(gdn_a_log[l]), _pad_lanes(gdn_dt_bias[l]))

        wg = jnp.concatenate([_block_diag(lru_w_r[l, 0]), _block_diag(lru_w_i[l, 0]),
                              _block_diag(lru_w_r[l, 1]), _block_diag(lru_w_i[l, 1])], axis=1).astype(BF16)
        bg = 0.5 * jnp.stack([lru_b_r[l, 0], lru_b_i[l, 0], lru_b_r[l, 1], lru_b_i[l, 1]], axis=0)
        o_f, o_b, h_f, h_b = _sweep(*prep, lx, 0.5 * lru_conv_w[l], 0.5 * lru_conv_b[l].reshape(1, -1), wg, bg,
                                    lru_lambda[l])

        final = l == depth - 1
        res = _outproj(c_src, x_src, src_off, modsel, att, o_f, o_b, pz, h_f, h_b, gdn_norm[l].reshape(1, -1),
                       w_out[l].astype(BF16), final_g.reshape(1, d), final)
        if final:
            out = res
        else:
            c_src, x_src, src_off = res, res, 0
    return out
```

```python
import functools
import math

import numpy as np
import jax
import jax.numpy as jnp
from jax import lax
from jax.experimental import pallas as pl
from jax.experimental.pallas import tpu as pltpu

F32 = jnp.float32
BF16 = jnp.bfloat16

EPS = 1e-6
NEG_INF = -1e30
LOG2E = math.log2(math.e)
GRID_W = 64
ROPE_BASE = 10000.0

D_MODEL = 1024
ATT_HEADS = 8
ATT_KV_HEADS = 2
ATT_HEAD_DIM = 64
ATT_BLOCK = 128
GDN_HEADS = 4
GDN_HEAD_DIM = 128
GDN_CHUNK = 64
LRU_WIDTH = 512
LRU_BLOCKS = 8
LRU_BLOCK_DIM = 64
LRU_C = 8.0
N_DIR = 2
MIX = 512
PACK = GDN_HEADS * GDN_CHUNK

TILE = 256
HALO = 8
LANES = 128
VMEM_LIMIT = 56 * 1024 * 1024

C_ATT_Q, C_GDN_Q, C_GDN_K, C_GDN_V = (i * MIX for i in range(4))
C_ATT_K = 4 * MIX
C_ATT_V = C_ATT_K + LANES
C_GDN_B = C_ATT_V + LANES
C_GDN_A = C_GDN_B + LANES
N_P = C_GDN_A + LANES
C_ATT_Z, C_GDN_Z, C_LRU_Z = (N_P + i * MIX for i in range(3))
N_Z = 3 * MIX
C_LRU_X = N_P + N_Z
N_IN = C_LRU_X + MIX
SEG = 8
STEPS = TILE // SEG


def _cparams(*sem):
    return pltpu.CompilerParams(dimension_semantics=sem, vmem_limit_bytes=VMEM_LIMIT)


def _sigmoid(x):
    return 1.0 / (1.0 + jnp.exp(-x))


def _silu(x):
    return x * _sigmoid(x)


def _softplus(x):
    return jnp.maximum(x, 0.0) + jnp.log1p(jnp.exp(-jnp.abs(x)))


def _dot_nt(a, b):
    return lax.dot_general(a.astype(BF16), b.astype(BF16), (((1,), (1,)), ((), ())),
                           preferred_element_type=F32)


def _dot_tn(a, b):
    return lax.dot_general(a.astype(BF16), b.astype(BF16), (((0,), (0,)), ((), ())),
                           preferred_element_type=F32)


def _mod_kernel(ct_ref, w_ref, b_ref, o_ref):
    s = _silu(ct_ref[...])
    w = w_ref[0]
    rid = lax.broadcasted_iota(jnp.int32, (8, w.shape[1]), 0)
    out = jnp.zeros((8, w.shape[1]), F32)
    for r in range(3):
        row = jnp.sum(w * s[:, r:r + 1], axis=0, keepdims=True) + b_ref[0]
        out = jnp.where(rid == r, row, out)
    o_ref[0] = out


def _modulation(ct, w_mod, b_mod):
    depth, d, d3 = w_mod.shape
    return pl.pallas_call(
        _mod_kernel,
        grid=(depth, d3 // d),
        in_specs=[pl.BlockSpec((d, LANES), lambda l, j: (0, 0)),
                  pl.BlockSpec((1, d, d), lambda l, j: (l, 0, j)),
                  pl.BlockSpec((1, 1, d), lambda l, j: (l, 0, j))],
        out_specs=pl.BlockSpec((1, 8, d), lambda l, j: (l, 0, j)),
        out_shape=jax.ShapeDtypeStruct((depth, 8, d3), F32),
        compiler_params=_cparams("parallel", "parallel"),
        name="modulation",
    )(ct, w_mod, b_mod.reshape(depth, 1, d3))


def _rope(x, cos, sin):
    lane = lax.broadcasted_iota(jnp.int32, x.shape, 1)
    first = (lane % 32) < 16
    partner = jnp.where(first, pltpu.roll(x, LANES - 16, 1), pltpu.roll(x, 16, 1))
    return x * cos + partner * sin


def _stream_tile(c_ref, x_ref, tile):
    is_ctx = (jnp.zeros((TILE, 1), jnp.int32) + tile) == 0
    return jnp.where(is_ctx, c_ref[0], x_ref[0])


def _stream_specs(d, off, shift=0):
    return [pl.BlockSpec((1, TILE, d), lambda bi, i: (bi, 0, 0)),
            pl.BlockSpec((1, TILE, d), lambda bi, i: (bi, jnp.maximum(i + shift - off, 1 - off), 0))]


_IN_COLS = []
_src = 0
for _dst, _w in ((C_ATT_Q, MIX), (C_ATT_K, LANES), (C_ATT_V, LANES), (C_ATT_Z, MIX), (C_GDN_Q, MIX), (C_GDN_K, MIX),
                 (C_GDN_V, MIX), (C_GDN_B, N_DIR * GDN_HEADS), (C_GDN_A, N_DIR * GDN_HEADS), (C_GDN_Z, MIX),
                 (C_LRU_X, MIX), (C_LRU_Z, MIX)):
    _IN_COLS.append((_src, _dst, _w))
    _src += _w
IN_WIDTH = _src


def _scan_order(t):
    return jnp.swapaxes(t.reshape(SEG, STEPS, t.shape[1]), 0, 1).reshape(t.shape)


def _time_order(t):
    return jnp.swapaxes(t.reshape(STEPS, SEG, t.shape[1]), 0, 1).reshape(t.shape)


def _inproj_kernel(c_ref, x_ref, mod_ref, g_ref, cos_ref, sin_ref, w_ref, o_ref, z_ref, lx_ref, w_s):
    @pl.when(jnp.logical_and(pl.program_id(0) == 0, pl.program_id(1) == 0))
    def _():
        lane = lax.broadcasted_iota(jnp.int32, (TILE, LANES), 1)
        n_gate = N_DIR * GDN_HEADS
        for r0 in range(0, w_s.shape[0], TILE):
            for src, dst, width in _IN_COLS:
                if width >= LANES:
                    w_s[r0:r0 + TILE, dst:dst + width] = w_ref[0, src:src + width, r0:r0 + TILE].T.astype(BF16)
            src_b = _IN_COLS[7][0]
            win = w_ref[0, src_b:src_b + LANES, r0:r0 + TILE].T
            w_s[r0:r0 + TILE, C_GDN_B:C_GDN_B + LANES] = jnp.where(lane < n_gate, win, 0.0).astype(BF16)
            w_s[r0:r0 + TILE, C_GDN_A:C_GDN_A + LANES] = jnp.where(
                lane < n_gate, pltpu.roll(win, LANES - n_gate, 1), 0.0).astype(BF16)

    x = _stream_tile(c_ref, x_ref, pl.program_id(1))
    ms = jnp.mean(x * x, axis=-1, keepdims=True)
    y = x * lax.rsqrt(ms + EPS) * g_ref[...]
    shift = mod_ref[0, 0, 0:1, :]
    scale = mod_ref[0, 0, 1:2, :]
    hb = (y * (1.0 + scale) + shift).astype(BF16)
    cos = cos_ref[...]
    sin = sin_ref[...]
    for c0 in range(0, N_IN, MIX):
        res = jnp.dot(hb, w_s[:, c0:c0 + MIX], preferred_element_type=F32)
        if c0 == C_ATT_Q:
            res = jnp.concatenate(
                [_rope(res[:, k:k + LANES], cos, sin) for k in range(0, MIX, LANES)], axis=1)
        if c0 == C_ATT_K:
            res = jnp.concatenate([_rope(res[:, :LANES], cos, sin), res[:, LANES:]], axis=1)
        if c0 == C_LRU_X:
            lx_ref[0] = _scan_order(res)
        elif c0 >= N_P:
            z_ref[0, :, c0 - N_P:c0 - N_P + MIX] = res.astype(z_ref.dtype)
        else:
            o_ref[0, :, c0:c0 + MIX] = res


def _inproj(c_src, x_src, off, modsel, g, cos_t, sin_t, w_in, layer):
    b, _, d = x_src.shape
    s = x_src.shape[1] + off * TILE
    nt = s // TILE
    return pl.pallas_call(
        _inproj_kernel,
        grid=(b, nt),
        in_specs=_stream_specs(d, off) + [
                  pl.BlockSpec((1, 1, 8, d), lambda bi, i: (bi, jnp.minimum(i, 1), 0, 0)),
                  pl.BlockSpec((1, d), lambda bi, i: (0, 0)),
                  pl.BlockSpec((TILE, LANES), lambda bi, i: (i, 0)),
                  pl.BlockSpec((TILE, LANES), lambda bi, i: (i, 0)),
                  pl.BlockSpec((1, IN_WIDTH, d), lambda bi, i: (layer, 0, 0), pipeline_mode=pl.Buffered(1))],
        out_specs=[pl.BlockSpec((1, TILE, N_P), lambda bi, i: (bi, i, 0)),
                   pl.BlockSpec((1, TILE, N_Z), lambda bi, i: (bi, i, 0)),
                   pl.BlockSpec((1, TILE, MIX), lambda bi, i: (bi, i, 0))],
        out_shape=[jax.ShapeDtypeStruct((b, s, N_P), F32), jax.ShapeDtypeStruct((b, s, N_Z), BF16),
                   jax.ShapeDtypeStruct((b, s, MIX), F32)],
        scratch_shapes=[pltpu.VMEM((d, N_IN), BF16)],
        compiler_params=_cparams("arbitrary", "arbitrary"),
        name="inproj",
    )(c_src, x_src, modsel, g, cos_t, sin_t, w_in)


def _att_kernel(sink_ref, q_ref, z_ref, kvp_ref, kvc_ref, kvn_ref, kvx_ref, o_ref, *, n_blocks):
    i = pl.program_id(1)
    n_sub = TILE // ATT_BLOCK
    n_band = 3 * ATT_BLOCK
    half = ATT_HEAD_DIM
    group = ATT_HEADS // ATT_KV_HEADS

    kv = jnp.concatenate([kvp_ref[0], kvc_ref[0], kvn_ref[0], kvx_ref[0]], axis=0)
    nk = kv.shape[0]
    k_all, v_all = kv[:, :LANES], kv[:, LANES:]
    k_sw, v_sw = pltpu.roll(k_all, half, 1), pltpu.roll(v_all, half, 1)
    lo_k = lax.broadcasted_iota(jnp.int32, (nk, LANES), 1) < half
    k_dup = [jnp.where(lo_k, k_all, k_sw).astype(BF16), jnp.where(lo_k, k_sw, k_all).astype(BF16)]
    v_dup = [jnp.where(lo_k, v_all, v_sw).astype(BF16), jnp.where(lo_k, v_sw, v_all).astype(BF16)]
    n_loc = (n_sub + 2) * ATT_BLOCK

    def keys(t, sub):
        return jnp.concatenate([t[sub * ATT_BLOCK:sub * ATT_BLOCK + n_band], t[n_loc:]], axis=0)

    r = lax.broadcasted_iota(jnp.int32, (ATT_BLOCK, n_band), 0)
    c = lax.broadcasted_iota(jnp.int32, (ATT_BLOCK, n_band), 1)
    in_win = jnp.abs(r - (c - ATT_BLOCK)) <= ATT_BLOCK
    lo_q = lax.broadcasted_iota(jnp.int32, (ATT_BLOCK, LANES), 1) < half
    outs = [[None] * (ATT_HEADS // 2) for _ in range(n_sub)]

    def band_bias(sub):
        m = i * n_sub + sub
        c_lo = jnp.where(m >= n_sub + 1, 0, jnp.where(m >= n_sub, ATT_BLOCK, n_band))
        c_hi = jnp.where(m < n_sub, 0, jnp.where(m <= n_blocks - 2, n_band, 2 * ATT_BLOCK))
        return jnp.where(jnp.logical_and(in_win, jnp.logical_and(c >= c_lo, c < c_hi)), 0.0, NEG_INF)

    def stream(sub, h, bias):
        q = q_ref[0, sub * ATT_BLOCK:(sub + 1) * ATT_BLOCK, :] * (ATT_HEAD_DIM ** -0.5 * LOG2E)
        lhs = []
        for j in range(h * group, (h + 1) * group):
            q_t = q[:, (j // 2) * LANES:(j // 2 + 1) * LANES]
            lhs.append(jnp.where(lo_q, q_t, 0.0) if j % 2 == 0 else jnp.where(lo_q, 0.0, q_t))
        qs = jnp.concatenate(lhs, axis=0).astype(BF16)
        s = lax.dot_general(qs, keys(k_dup[h], sub), (((1,), (1,)), ((), ())), preferred_element_type=F32)
        yield
        p_blocks, den_blocks = [], []
        for e in range(group):
            sb = s[e * ATT_BLOCK:(e + 1) * ATT_BLOCK]
            sk = sink_ref[h * group + e] * LOG2E
            s_band = sb[:, :n_band] + bias
            s_ctx = sb[:, n_band:]
            mx = jnp.maximum(jnp.maximum(jnp.max(s_band, axis=-1, keepdims=True),
                                         jnp.max(s_ctx, axis=-1, keepdims=True)), sk)
            p_band = jnp.exp2(s_band - mx)
            p_ctx = jnp.exp2(s_ctx - mx)
            den_blocks.append(jnp.sum(p_band, axis=-1, keepdims=True) + jnp.sum(p_ctx, axis=-1, keepdims=True)
                              + jnp.exp2(sk - mx))
            p_blocks.append(jnp.concatenate([p_band, p_ctx], axis=1).astype(BF16))
            yield
        p = jnp.concatenate(p_blocks, axis=0)
        res = (jnp.dot(p, keys(v_dup[h], sub), preferred_element_type=F32)
               / jnp.concatenate(den_blocks, axis=0))
        for t in range(group // 2):
            even = res[(2 * t) * ATT_BLOCK:(2 * t + 1) * ATT_BLOCK]
            odd = res[(2 * t + 1) * ATT_BLOCK:(2 * t + 2) * ATT_BLOCK]
            outs[sub][h * (group // 2) + t] = jnp.where(lo_q, even, odd)
        yield

    for sub in range(n_sub):
        bias = band_bias(sub)
        gens = [stream(sub, h, bias) for h in range(ATT_KV_HEADS)]
        for g in gens:
            next(g)
        for g in gens:
            for _ in g:
                pass
    for sub in range(n_sub):
        rows = slice(sub * ATT_BLOCK, (sub + 1) * ATT_BLOCK)
        o = jnp.concatenate(outs[sub], axis=1)
        o_ref[0, rows, :] = (o * _silu(z_ref[0, rows, :].astype(F32))).astype(o_ref.dtype)


def _attention(p, pz, sink):
    b, s, _ = p.shape
    nb = s // ATT_BLOCK
    n_sub = TILE // ATT_BLOCK
    kvcol = C_ATT_K // (2 * LANES)
    kv_spec = lambda f: pl.BlockSpec((1, ATT_BLOCK, 2 * LANES), f)
    return pl.pallas_call(
        functools.partial(_att_kernel, n_blocks=nb),
        grid=(b, s // TILE),
        in_specs=[pl.BlockSpec(memory_space=pltpu.SMEM),
                  pl.BlockSpec((1, TILE, MIX), lambda bi, i: (bi, i, C_ATT_Q // MIX)),
                  pl.BlockSpec((1, TILE, MIX), lambda bi, i: (bi, i, (C_ATT_Z - N_P) // MIX)),
                  kv_spec(lambda bi, i: (bi, jnp.maximum(i * n_sub - 1, n_sub), kvcol)),
                  pl.BlockSpec((1, TILE, 2 * LANES), lambda bi, i: (bi, i, kvcol)),
                  kv_spec(lambda bi, i: (bi, jnp.minimum((i + 1) * n_sub, nb - 1), kvcol)),
                  pl.BlockSpec((1, TILE, 2 * LANES), lambda bi, i: (bi, 0, kvcol))],
        out_specs=pl.BlockSpec((1, TILE, MIX), lambda bi, i: (bi, i, 0)),
        out_shape=jax.ShapeDtypeStruct((b, s, MIX), BF16),
        compiler_params=_cparams("parallel", "parallel"),
        name="attention",
    )(sink, p, pz, p, p, p, p)


def _conv4(x, hp, hn, w, prev_ok, next_ok):
    hp = hp * prev_ok
    hn = hn * next_ok
    ext = jnp.concatenate([hp, x, hn], axis=0)
    n = x.shape[0]
    own = slice(HALO, HALO + n)
    back2, back1, fwd1 = (pltpu.roll(ext, sh, 0) for sh in (2, 1, ext.shape[0] - 1))
    return w[0:1] * back2[own] + w[1:2] * back1[own] + w[2:3] * ext[own] + w[3:4] * fwd1[own]


def _halo_ok(i, n_tiles):
    prev_ok = (i >= 2).astype(F32)
    next_ok = jnp.logical_and(i != 0, i != n_tiles - 1).astype(F32)
    return prev_ok, next_ok


def _chunk_cumsum(g, reverse):
    n = g.shape[0]
    row = lax.broadcasted_iota(jnp.int32, g.shape, 0) % GDN_CHUNK
    s = 1
    while s < GDN_CHUNK:
        if reverse:
            g = g + jnp.where(row < GDN_CHUNK - s, pltpu.roll(g, n - s, 0), 0.0)
        else:
            g = g + jnp.where(row >= s, pltpu.roll(g, s, 0), 0.0)
        s *= 2
    return g


def _spread(cols, masks):
    out = cols[-1]
    for m, col in zip(masks[-2::-1], cols[-2::-1]):
        out = jnp.where(m, col, out)
    return out


def _head_block_diag(x, masks):
    return jnp.concatenate([jnp.where(m, x, 0.0) for m in masks], axis=0).astype(BF16)


def _gdn_prep_kernel(q_ref, qp_ref, qn_ref, k_ref, kp_ref, kn_ref, v_ref, vp_ref, vn_ref,
                     b_ref, a_ref, cw_ref, alog_ref, dtb_ref,
                     w_ref, u_ref, kd_ref, qg_ref, aqk_ref, gl_ref, *, n_tiles):
    i = pl.program_id(1)
    prev_ok, next_ok = _halo_ok(i, n_tiles)
    hd, ch, nh = GDN_HEAD_DIM, GDN_CHUNK, GDN_HEADS
    nch = TILE // ch

    def prep(x_ref, p_ref, n_ref, col):
        xh = _conv4(x_ref[0], p_ref[0], n_ref[0], cw_ref[:, col:col + MIX], prev_ok, next_ok)
        return xh * (1.0 + jnp.tanh(xh))

    def l2n(t):
        parts = []
        for h in range(nh):
            th = t[:, h * hd:(h + 1) * hd]
            parts.append(th * lax.rsqrt(jnp.sum(th * th, axis=-1, keepdims=True) + EPS))
        return jnp.concatenate(parts, axis=1)

    q = l2n(prep(q_ref, qp_ref, qn_ref, 0)) * (hd ** -0.5)
    k = l2n(prep(k_ref, kp_ref, kn_ref, MIX))
    v = prep(v_ref, vp_ref, vn_ref, 2 * MIX)

    lane = lax.broadcasted_iota(jnp.int32, (TILE, LANES), 1)
    g = -jnp.exp(alog_ref[...]) * _softplus(a_ref[0] + dtb_ref[...])
    gcum = jnp.where(lane < nh, _chunk_cumsum(g, False), _chunk_cumsum(g, True))
    beta = _sigmoid(b_ref[0])

    hp = 2
    pk = hp * ch
    ii = lax.broadcasted_iota(jnp.int32, (ch, pk), 0)
    ll = lax.broadcasted_iota(jnp.int32, (ch, pk), 1)
    jj = ll % ch
    eye_p = jj == ii
    eye_f = eye_p.astype(F32)
    bd16 = (jj // 16) == (ii // 16)
    bd32 = (jj // 32) == (ii // 32)
    merge_masks = (jnp.logical_and(bd32, jnp.logical_not(bd16)), jnp.logical_not(bd32))
    strict = [jj < ii, jj > ii]
    incl = [jj <= ii, jj >= ii]
    head_p = [(ll // ch) == h for h in range(hp)]
    lw = lax.broadcasted_iota(jnp.int32, (ch, hp * hd), 1) // hd
    head_w = [lw == h for h in range(hp)]
    lane_row = lax.broadcasted_iota(jnp.int32, (1, LANES), 1)

    def pdot(a, b):
        return jnp.dot(a.astype(BF16), _head_block_diag(b, head_p), preferred_element_type=F32)

    gl_rows = []
    units = []
    for cidx in range(nch):
        rows = slice(cidx * ch, (cidx + 1) * ch)
        gc, bc = gcum[rows], beta[rows]
        g_end = jnp.where(lane_row < nh, gc[ch - 1:ch, :], gc[0:1, :])
        gl_rows.append(jnp.exp(g_end))
        for pair in range(nh // hp):
            lanes = slice(pair * hp * hd, (pair + 1) * hp * hd)
            kc, qc, vc = k[rows, lanes], q[rows, lanes], v[rows, lanes]
            kq = _dot_nt(jnp.concatenate([kc, qc], axis=0), _head_block_diag(kc, head_w))
            for d in range(N_DIR):
                c0 = d * nh + pair * hp
                gcols = [gc[:, c0 + h:c0 + h + 1] for h in range(hp)]
                bcols = [bc[:, c0 + h:c0 + h + 1] for h in range(hp)]
                gcol_p = _spread(gcols, head_p)
                grow_p = jnp.sum(jnp.where(eye_p, gcol_p, 0.0), axis=0, keepdims=True)
                dec = jnp.exp(jnp.where(incl[d], gcol_p - grow_p, 0.0))
                lm = jnp.where(strict[d], _spread(bcols, head_p) * dec * kq[:ch], 0.0)
                aqk = jnp.where(incl[d], dec * kq[ch:], 0.0)
                aqk_ref[0, rows, d * PACK + pair * pk:d * PACK + (pair + 1) * pk] = aqk.astype(BF16)
                units.append((rows, d, pair, lm, kc, qc, vc, gcols, bcols,
                              [g_end[:, c0 + h:c0 + h + 1] for h in range(hp)]))
    gl_rows.append(jnp.zeros((8 - nch, LANES), F32))
    gl_ref[0, 0] = jnp.concatenate(gl_rows, axis=0)

    lms = [u[3] for u in units]
    m = [jnp.where(bd16, -lm, 0.0) for lm in lms]
    x = [eye_f + mi for mi in m]
    pw = [pdot(mi, mi) for mi in m]
    for _ in range(2):
        r = [pdot(jnp.concatenate([xi, pi], axis=0), pi) for xi, pi in zip(x, pw)]
        x = [xi + ri[:ch] for xi, ri in zip(x, r)]
        pw = [ri[ch:] for ri in r]
    x = [xi + pdot(xi, pi) for xi, pi in zip(x, pw)]
    for mask in merge_masks:
        y = [pdot(xi, jnp.where(mask, lm, 0.0)) for xi, lm in zip(x, lms)]
        x = [xi - pdot(yi, xi) for xi, yi in zip(x, y)]

    for tinv, (rows, d, pair, _, kc, qc, vc, gcols, bcols, gend_cols) in zip(x, units):
        slab = lambda t, h: t[:, h * hd:(h + 1) * hd]
        gam = [jnp.exp(gcol) for gcol in gcols]
        bgk = jnp.concatenate([(bcols[h] * gam[h]) * slab(kc, h) for h in range(hp)], axis=1)
        bv = jnp.concatenate([bcols[h] * slab(vc, h) for h in range(hp)], axis=1)
        rhs = jnp.concatenate([_head_block_diag(bgk, head_w), _head_block_diag(bv, head_w)], axis=1)
        wu = jnp.dot(tinv.astype(BF16), rhs, preferred_element_type=F32)
        half = hp * hd
        cols = slice(d * MIX + pair * half, d * MIX + (pair + 1) * half)
        w_ref[0, rows, cols] = wu[:, :half].astype(BF16)
        u_ref[0, rows, cols] = wu[:, half:]
        kd_ref[0, rows, cols] = jnp.concatenate(
            [jnp.exp(gend_cols[h] - gcols[h]) * slab(kc, h) for h in range(hp)], axis=1).astype(BF16)
        qg_ref[0, rows, cols] = jnp.concatenate([gam[h] * slab(qc, h) for h in range(hp)], axis=1).astype(BF16)


def _gdn_prep(p, conv_w, alog, dtb):
    b, s, _ = p.shape
    nt = s // TILE
    hpt = TILE // HALO
    nh = s // HALO
    wide = N_DIR * GDN_HEADS * GDN_HEAD_DIM

    def tile_spec(col):
        return pl.BlockSpec((1, TILE, MIX), lambda bi, i: (bi, i, col // MIX))

    def prev_spec(col):
        return pl.BlockSpec((1, HALO, MIX), lambda bi, i: (bi, jnp.maximum(i * hpt - 1, 0), col // MIX))

    def next_spec(col):
        return pl.BlockSpec((1, HALO, MIX), lambda bi, i: (bi, jnp.minimum((i + 1) * hpt, nh - 1), col // MIX))

    in_specs = []
    for col in (C_GDN_Q, C_GDN_K, C_GDN_V):
        in_specs += [tile_spec(col), prev_spec(col), next_spec(col)]
    in_specs += [pl.BlockSpec((1, TILE, LANES), lambda bi, i: (bi, i, C_GDN_B // LANES)),
                 pl.BlockSpec((1, TILE, LANES), lambda bi, i: (bi, i, C_GDN_A // LANES)),
                 pl.BlockSpec((4, 3 * MIX), lambda bi, i: (0, 0)),
                 pl.BlockSpec((1, LANES), lambda bi, i: (0, 0)),
                 pl.BlockSpec((1, LANES), lambda bi, i: (0, 0))]
    wide_spec = pl.BlockSpec((1, TILE, wide), lambda bi, i: (bi, i, 0))
    out_specs = [wide_spec, wide_spec, wide_spec, wide_spec,
                 pl.BlockSpec((1, TILE, N_DIR * PACK), lambda bi, i: (bi, i, 0)),
                 pl.BlockSpec((1, 1, 8, LANES), lambda bi, i: (bi, i, 0, 0))]
    out_shape = [jax.ShapeDtypeStruct((b, s, wide), BF16),
                 jax.ShapeDtypeStruct((b, s, wide), F32),
                 jax.ShapeDtypeStruct((b, s, wide), BF16),
                 jax.ShapeDtypeStruct((b, s, wide), BF16),
                 jax.ShapeDtypeStruct((b, s, N_DIR * PACK), BF16),
                 jax.ShapeDtypeStruct((b, nt, 8, LANES), F32)]
    return pl.pallas_call(
        functools.partial(_gdn_prep_kernel, n_tiles=nt),
        grid=(b, nt),
        in_specs=in_specs,
        out_specs=out_specs,
        out_shape=out_shape,
        compiler_params=_cparams("parallel", "parallel"),
        name="gdn_prep",
    )(p, p, p, p, p, p, p, p, p, p, p, conv_w, alog, dtb)


def _gdn_scan_body(bi, wf, uf, kdf, qgf, af, glf, wb, ub, kdb, qgb, ab, glb, of_ref, ob_ref, state):
    hd, ch, nh = GDN_HEAD_DIM, GDN_CHUNK, GDN_HEADS
    nch = TILE // ch
    lw = lax.broadcasted_iota(jnp.int32, (ch, MIX), 1) // hd
    head_w = [lw == h for h in range(nh)]
    dirs = ((wf, uf, kdf, qgf, af, glf, of_ref), (wb, ub, kdb, qgb, ab, glb, ob_ref))
    for step in range(nch):
        cidx = [step, nch - 1 - step]
        rows = [slice(c * ch, (c + 1) * ch) for c in cidx]
        s_old = {}
        ws = {}
        for d, (w_r, _, _, qg_r, _, _, _) in enumerate(dirs):
            for h in range(nh):
                cols = slice(h * hd, (h + 1) * hd)
                s_old[d, h] = state[bi, d, h]
                lhs = jnp.concatenate([w_r[bi, rows[d], cols], qg_r[bi, rows[d], cols]], axis=0)
                ws[d, h] = jnp.dot(lhs, s_old[d, h].astype(BF16), preferred_element_type=F32)
            yield
        for d, (_, u_r, kd_r, _, a_r, gl_r, o_r) in enumerate(dirs):
            un = u_r[bi, rows[d], :] - jnp.concatenate([ws[d, h][:ch] for h in range(nh)], axis=1)
            inter = jnp.concatenate([ws[d, h][ch:] for h in range(nh)], axis=1)
            intra = jnp.dot(a_r[bi, rows[d], :], _head_block_diag(un, head_w), preferred_element_type=F32)
            o_r[bi, rows[d], :] = (inter + intra).astype(o_r.dtype)
            unb = un.astype(BF16)
            for h in range(nh):
                cols = slice(h * hd, (h + 1) * hd)
                gl = gl_r[bi, 0, cidx[d]:cidx[d] + 1, d * nh + h:d * nh + h + 1]
                state[bi, d, h] = gl * s_old[d, h] + _dot_tn(kd_r[bi, rows[d], cols], unb[:, cols])
            yield


def _mirror(j, nt):
    return jnp.where(j == 0, 0, nt - j)


def _lru_body(bi, xf_ref, xfp2_ref, xfp_ref, xfn_ref, xb_ref, xbp2_ref, xbp_ref, xbn_ref, cw_ref, cb_ref, wg_ref, bg_ref,
              lam_ref, hf_ref, hb_ref, carry, *, n_tiles):
    j = pl.program_id(0)
    seg_n = SEG
    steps = STEPS
    sub = lax.broadcasted_iota(jnp.int32, (seg_n, LRU_WIDTH), 0)

    def one_dir(d, x_ref, p2_ref, p_ref, n_ref, tile_idx, o_ref):
        prev_ok, next_ok = _halo_ok(tile_idx, n_tiles)
        xs = [x_ref[bi, st * seg_n:(st + 1) * seg_n, :] for st in range(steps)]
        last = seg_n - 1
        p1 = jnp.where(sub == 0, p_ref[bi, last:last + 1, :] * prev_ok, pltpu.roll(xs[steps - 1], 1, 0))
        p2 = jnp.where(sub == 0, p2_ref[bi, last:last + 1, :] * prev_ok, pltpu.roll(xs[steps - 2], 1, 0))
        n1 = jnp.where(sub == last, n_ref[bi, 0:1, :] * next_ok, pltpu.roll(xs[0], last, 0))
        ext = [p2, p1] + xs + [n1]
        cw = cw_ref[...]
        xh = jnp.concatenate(
            [cw[0:1] * ext[st] + cw[1:2] * ext[st + 1] + cw[2:3] * ext[st + 2] + cw[3:4] * ext[st + 3]
             for st in range(steps)], axis=0) + cb_ref[...]
        yield
        gates = jnp.dot(xh.astype(BF16), wg_ref[:, d * 2 * LRU_WIDTH:(d + 1) * 2 * LRU_WIDTH],
                        preferred_element_type=F32)
        yield
        c_half = (-0.5 * LRU_C) * _softplus(-lam_ref[d:d + 1, :])
        a_blocks, b_blocks = [], []
        blk = TILE // 4
        for r0 in range(0, TILE, blk):
            gb = gates[r0:r0 + blk]
            tr = jnp.tanh(gb[:, :LRU_WIDTH] + bg_ref[2 * d:2 * d + 1, :])
            ti = jnp.tanh(gb[:, LRU_WIDTH:] + bg_ref[2 * d + 1:2 * d + 2, :])
            log_a = c_half * tr + c_half
            ab = jnp.exp(log_a)
            b_blocks.append(jnp.sqrt(-jnp.tanh(log_a) * (ab * ab + 1.0)) * ((ti + 1.0) * xh[r0:r0 + blk]))
            a_blocks.append(ab)
            yield
        a = jnp.concatenate(a_blocks, axis=0)
        bb = jnp.concatenate(b_blocks, axis=0)

        order = list(range(steps)) if d == 0 else list(range(steps - 1, -1, -1))
        piece = lambda t, st: t[st * seg_n:(st + 1) * seg_n]
        a_tot = piece(a, order[0])
        b_tot = piece(bb, order[0])
        for n, st in enumerate(order[1:]):
            b_tot = piece(a, st) * b_tot + piece(bb, st)
            a_tot = piece(a, st) * a_tot
            if n % 8 == 7:
                yield
        s = 1
        while s < seg_n:
            if d == 0:
                keep = sub >= s
                a_sh = jnp.where(keep, pltpu.roll(a_tot, s, 0), 1.0)
                b_sh = jnp.where(keep, pltpu.roll(b_tot, s, 0), 0.0)
            else:
                keep = sub < seg_n - s
                a_sh = jnp.where(keep, pltpu.roll(a_tot, seg_n - s, 0), 1.0)
                b_sh = jnp.where(keep, pltpu.roll(b_tot, seg_n - s, 0), 0.0)
            b_tot = a_tot * b_sh + b_tot
            a_tot = a_tot * a_sh
            s *= 2
        h_in0 = carry[bi, d]
        h_out = a_tot * h_in0 + b_tot
        if d == 0:
            h = jnp.where(sub == 0, h_in0, pltpu.roll(h_out, 1, 0))
            carry[bi, d] = h_out[seg_n - 1:seg_n]
        else:
            h = jnp.where(sub == seg_n - 1, h_in0, pltpu.roll(h_out, seg_n - 1, 0))
            carry[bi, d] = h_out[0:1]
        hs = [None] * steps
        for n, st in enumerate(order):
            h = piece(a, st) * h + piece(bb, st)
            hs[st] = h
            if n % 8 == 7:
                yield
        o_ref[bi] = jnp.concatenate(hs, axis=0).astype(o_ref.dtype)

    return [one_dir(0, xf_ref, xfp2_ref, xfp_ref, xfn_ref, j, hf_ref),
            one_dir(1, xb_ref, xbp2_ref, xbp_ref, xbn_ref, _mirror(j, n_tiles), hb_ref)]


N_SCAN_IN = 12
N_LRU_IN = 13


def _round_robin(gens, weights):
    live = list(zip(gens, weights))
    while live:
        nxt = []
        for g, w in live:
            alive = True
            for _ in range(w):
                try:
                    next(g)
                except StopIteration:
                    alive = False
                    break
            if alive:
                nxt.append((g, w))
        live = nxt


def _sweep_kernel(*refs, n_tiles, n_batch):
    scan_in, lru_in = refs[:N_SCAN_IN], refs[N_SCAN_IN:N_SCAN_IN + N_LRU_IN]
    of_ref, ob_ref, hf_ref, hb_ref, state, carry = refs[N_SCAN_IN + N_LRU_IN:]

    @pl.when(pl.program_id(0) == 0)
    def _():
        state[...] = jnp.zeros_like(state)
        carry[...] = jnp.zeros_like(carry)

    scans = [_gdn_scan_body(bi, *scan_in, of_ref, ob_ref, state) for bi in range(n_batch)]
    lrus = [g for bi in range(n_batch) for g in _lru_body(bi, *lru_in, hf_ref, hb_ref, carry, n_tiles=n_tiles)]
    _round_robin(scans + lrus, (1,) * len(scans) + (2,) * len(lrus))


def _sweep(w, u, kd, qg, aqk, gl, lx, conv_w, conv_b, wg, bg, lam):
    b, s, _ = u.shape
    nt = s // TILE
    half = GDN_HEADS * GDN_HEAD_DIM
    hpt = TILE // HALO
    nh = s // HALO
    tile_of = (lambda j: j, lambda j: _mirror(j, nt))

    def scan_specs(d):
        idx = lambda j: (0, tile_of[d](j), d)
        gidx = lambda j: (0, tile_of[d](j), 0, 0)
        return [pl.BlockSpec((b, TILE, half), idx)] * 4 + [pl.BlockSpec((b, TILE, PACK), idx),
                                                           pl.BlockSpec((b, 1, 8, LANES), gidx)]

    def lru_specs(d):
        t = tile_of[d]
        return [pl.BlockSpec((b, TILE, MIX), lambda j: (0, t(j), 0)),
                pl.BlockSpec((b, HALO, MIX), lambda j: (0, jnp.maximum(t(j) * hpt - 2, 0), 0)),
                pl.BlockSpec((b, HALO, MIX), lambda j: (0, jnp.maximum(t(j) * hpt - 1, 0), 0)),
                pl.BlockSpec((b, HALO, MIX), lambda j: (0, jnp.minimum((t(j) + 1) * hpt, nh - 1), 0))]

    full = lambda shape: pl.BlockSpec(shape, lambda j: tuple(0 for _ in shape))
    out_spec = lambda d: pl.BlockSpec((b, TILE, MIX), lambda j: (0, tile_of[d](j), 0))
    in_specs = (scan_specs(0) + scan_specs(1) + lru_specs(0) + lru_specs(1)
                + [full((4, LRU_WIDTH)), full((1, LRU_WIDTH)), full((LRU_WIDTH, 4 * LRU_WIDTH)),
                   full((4, LRU_WIDTH)), full((N_DIR, LRU_WIDTH))])
    assert len(in_specs) == N_SCAN_IN + N_LRU_IN
    return pl.pallas_call(
        functools.partial(_sweep_kernel, n_tiles=nt, n_batch=b),
        grid=(nt,),
        in_specs=in_specs,
        out_specs=[out_spec(0), out_spec(1), out_spec(0), out_spec(1)],
        out_shape=[jax.ShapeDtypeStruct((b, s, MIX), BF16)] * 4,
        scratch_shapes=[pltpu.VMEM((b, N_DIR, GDN_HEADS, GDN_HEAD_DIM, GDN_HEAD_DIM), F32),
                        pltpu.VMEM((b, N_DIR, 1, LRU_WIDTH), F32)],
        compiler_params=_cparams("arbitrary"),
        name="sweep",
    )(w, u, kd, qg, aqk, gl, w, u, kd, qg, aqk, gl, *([lx] * 8), conv_w, conv_b, wg, bg, lam)


def _outproj_kernel(c_ref, x_ref, mod_ref, att_ref, of_ref, ob_ref, gz_ref, hf_ref, hb_ref, lz_ref,
                    gn_ref, w_ref, fg_ref, o_ref, *, final, first_tile):
    hd = GDN_HEAD_DIM
    o = of_ref[0].astype(F32) + ob_ref[0].astype(F32)
    parts = []
    for h in range(GDN_HEADS):
        oh = o[:, h * hd:(h + 1) * hd]
        parts.append(oh * lax.rsqrt(jnp.mean(oh * oh, axis=-1, keepdims=True) + EPS) * gn_ref[...])
    gdn = jnp.concatenate(parts, axis=1) * _silu(gz_ref[0].astype(F32))
    lru = _time_order(hf_ref[0].astype(F32) + hb_ref[0].astype(F32)) * _silu(lz_ref[0].astype(F32))
    acc = (jnp.dot(att_ref[0], w_ref[0:MIX, :], preferred_element_type=F32)
           + jnp.dot(gdn.astype(BF16), w_ref[MIX:2 * MIX, :], preferred_element_type=F32)
           + jnp.dot(lru.astype(BF16), w_ref[2 * MIX:3 * MIX, :], preferred_element_type=F32))
    gate = mod_ref[0, 0, 2:3, :]
    xn = _stream_tile(c_ref, x_ref, pl.program_id(1) + first_tile) + gate * acc
    if final:
        ms = jnp.mean(xn * xn, axis=-1, keepdims=True)
        xn = xn * lax.rsqrt(ms + EPS) * fg_ref[...]
    o_ref[0] = xn


def _outproj(c_src, x_src, src_off, modsel, att, of, ob, pz, hf, hb, gdn_norm, w_out, final_g, final):
    b, s, _ = pz.shape
    d = x_src.shape[2]
    nt = s // TILE
    off = 1 if final else 0
    n_out = nt - off
    tok = lambda col: pl.BlockSpec((1, TILE, MIX), lambda bi, i: (bi, i + off, col))
    return pl.pallas_call(
        functools.partial(_outproj_kernel, final=final, first_tile=off),
        grid=(b, n_out),
        in_specs=_stream_specs(d, src_off, off) + [
                  pl.BlockSpec((1, 1, 8, d), lambda bi, i: (bi, jnp.minimum(i + off, 1), 0, 0)),
                  tok(0), tok(0), tok(0), tok((C_GDN_Z - N_P) // MIX), tok(0), tok(0), tok((C_LRU_Z - N_P) // MIX),
                  pl.BlockSpec((1, GDN_HEAD_DIM), lambda bi, i: (0, 0)),
                  pl.BlockSpec((3 * MIX, d), lambda bi, i: (0, 0)),
                  pl.BlockSpec((1, d), lambda bi, i: (0, 0))],
        out_specs=pl.BlockSpec((1, TILE, d), lambda bi, i: (bi, i, 0)),
        out_shape=jax.ShapeDtypeStruct((b, n_out * TILE, d), F32),
        compiler_params=_cparams("parallel", "parallel"),
        name="outproj",
    )(c_src, x_src, modsel, att, of, ob, pz, hf, hb, pz, gdn_norm, w_out, final_g)


def _rope_tables(s, ctx_len):
    t = np.arange(s - ctx_len)
    n_freq = ATT_HEAD_DIM // 4
    inv = ROPE_BASE ** (-np.arange(n_freq, dtype=np.float64) / n_freq)
    ang = [(t // GRID_W)[:, None] * inv, (t % GRID_W)[:, None] * inv]
    cos = np.concatenate([np.cos(ang[0]), np.cos(ang[0]), np.cos(ang[1]), np.cos(ang[1])], axis=1)
    sin = np.concatenate([-np.sin(ang[0]), np.sin(ang[0]), -np.sin(ang[1]), np.sin(ang[1])], axis=1)
    cos = np.concatenate([np.ones((ctx_len, ATT_HEAD_DIM)), cos], axis=0)
    sin = np.concatenate([np.zeros((ctx_len, ATT_HEAD_DIM)), sin], axis=0)
    return (jnp.asarray(np.tile(cos, (1, 2)), F32), jnp.asarray(np.tile(sin, (1, 2)), F32))


def _block_diag(w):
    eye = jnp.eye(LRU_BLOCKS, dtype=w.dtype)
    return jnp.einsum("nde,nm->ndme", w, eye).reshape(LRU_WIDTH, LRU_WIDTH)


def _pad_lanes(v):
    return jnp.pad(v.reshape(1, -1), ((0, 0), (0, LANES - v.size)))


def kernel(x, c, ctx, c_ctx, norm_g, w_mod, b_mod, w_in, att_sink, gdn_conv, gdn_a_log, gdn_dt_bias, gdn_norm,
           lru_conv_w, lru_conv_b, lru_w_r, lru_b_r, lru_w_i, lru_b_i, lru_lambda, w_out, final_g):
    b, t, d = x.shape
    ctx_len = ctx.shape[1]
    depth = w_in.shape[0]
    assert ctx_len == TILE and t % TILE == 0 and d == D_MODEL and b == 2
    s = ctx_len + t
    nt = s // TILE

    cos_t, sin_t = _rope_tables(s, ctx_len)
    ct = jnp.pad(jnp.concatenate([c, c_ctx[None]], axis=0).T, ((0, 0), (0, LANES - b - 1)))
    mods = _modulation(ct, w_mod, b_mod)

    w_in_t = jnp.swapaxes(w_in, 1, 2)
    c_src, x_src, src_off = ctx, x, 1
    out = None
    for l in range(depth):
        m3 = mods[l, :3].reshape(3, 3, d)
        sel = jnp.stack([jnp.stack([m3[b], m3[bi]], axis=0) for bi in range(b)], axis=0)
        modsel = jnp.pad(sel, ((0, 0), (0, 0), (0, 5), (0, 0)))
        p, pz, lx = _inproj(c_src, x_src, src_off, modsel, norm_g[l].reshape(1, d), cos_t, sin_t, w_in_t, l)

        att = _attention(p, pz, att_sink[l])

        prep = _gdn_prep(p, 0.5 * gdn_conv[l], _pad_lanes(gdn_a_log[l]), _pad_lanes(gdn_dt_bias[l]))

        wg = jnp.concatenate([_block_diag(lru_w_r[l, 0]), _block_diag(lru_w_i[l, 0]),
                              _block_diag(lru_w_r[l, 1]), _block_diag(lru_w_i[l, 1])], axis=1).astype(BF16)
        bg = 0.5 * jnp.stack([lru_b_r[l, 0], lru_b_i[l, 0], lru_b_r[l, 1], lru_b_i[l, 1]], axis=0)
        o_f, o_b, h_f, h_b = _sweep(*prep, lx, 0.5 * lru_conv_w[l], 0.5 * lru_conv_b[l].reshape(1, -1), wg, bg,
                                    lru_lambda[l])

        final = l == depth - 1
        res = _outproj(c_src, x_src, src_off, modsel, att, o_f, o_b, pz, h_f, h_b, gdn_norm[l].reshape(1, -1),
                       w_out[l].astype(BF16), final_g.reshape(1, d), final)
        if final:
            out = res
        else:
            c_src, x_src, src_off = res, res, 0
    return out
```

```python
import functools
import math

import numpy as np
import jax
import jax.numpy as jnp
from jax import lax
from jax.experimental import pallas as pl
from jax.experimental.pallas import tpu as pltpu

F32 = jnp.float32
BF16 = jnp.bfloat16

EPS = 1e-6
NEG_INF = -1e30
LOG2E = math.log2(math.e)
GRID_W = 64
ROPE_BASE = 10000.0

D_MODEL = 1024
ATT_HEADS = 8
ATT_KV_HEADS = 2
ATT_HEAD_DIM = 64
ATT_BLOCK = 128
GDN_HEADS = 4
GDN_HEAD_DIM = 128
GDN_CHUNK = 64
LRU_WIDTH = 512
LRU_BLOCKS = 8
LRU_BLOCK_DIM = 64
LRU_C = 8.0
N_DIR = 2
MIX = 512
PACK = GDN_HEADS * GDN_CHUNK

TILE = 256
HALO = 8
LANES = 128
VMEM_LIMIT = 56 * 1024 * 1024

C_ATT_Q, C_GDN_Q, C_GDN_K, C_GDN_V = (i * MIX for i in range(4))
C_ATT_K = 4 * MIX
C_ATT_V = C_ATT_K + LANES
C_GDN_B = C_ATT_V + LANES
C_GDN_A = C_GDN_B + LANES
N_P = C_GDN_A + LANES
C_ATT_Z, C_GDN_Z, C_LRU_Z = (N_P + i * MIX for i in range(3))
N_Z = 3 * MIX
C_LRU_X = N_P + N_Z
N_IN = C_LRU_X + MIX
SEG = 8
STEPS = TILE // SEG


def _cparams(*sem):
    return pltpu.CompilerParams(dimension_semantics=sem, vmem_limit_bytes=VMEM_LIMIT)


def _sigmoid(x):
    return 1.0 / (1.0 + jnp.exp(-x))


def _silu(x):
    return x * _sigmoid(x)


def _softplus(x):
    return jnp.maximum(x, 0.0) + jnp.log1p(jnp.exp(-jnp.abs(x)))


def _dot_nt(a, b):
    return lax.dot_general(a.astype(BF16), b.astype(BF16), (((1,), (1,)), ((), ())),
                           preferred_element_type=F32)


def _dot_tn(a, b):
    return lax.dot_general(a.astype(BF16), b.astype(BF16), (((0,), (0,)), ((), ())),
                           preferred_element_type=F32)


def _mod_kernel(ct_ref, w_ref, b_ref, o_ref):
    s = _silu(ct_ref[...])
    w = w_ref[0]
    rid = lax.broadcasted_iota(jnp.int32, (8, w.shape[1]), 0)
    out = jnp.zeros((8, w.shape[1]), F32)
    for r in range(3):
        row = jnp.sum(w * s[:, r:r + 1], axis=0, keepdims=True) + b_ref[0]
        out = jnp.where(rid == r, row, out)
    o_ref[0] = out


def _modulation(ct, w_mod, b_mod):
    depth, d, d3 = w_mod.shape
    return pl.pallas_call(
        _mod_kernel,
        grid=(depth, d3 // d),
        in_specs=[pl.BlockSpec((d, LANES), lambda l, j: (0, 0)),
                  pl.BlockSpec((1, d, d), lambda l, j: (l, 0, j)),
                  pl.BlockSpec((1, 1, d), lambda l, j: (l, 0, j))],
        out_specs=pl.BlockSpec((1, 8, d), lambda l, j: (l, 0, j)),
        out_shape=jax.ShapeDtypeStruct((depth, 8, d3), F32),
        compiler_params=_cparams("parallel", "parallel"),
        name="modulation",
    )(ct, w_mod, b_mod.reshape(depth, 1, d3))


def _rope(x, cos, sin):
    lane = lax.broadcasted_iota(jnp.int32, x.shape, 1)
    first = (lane % 32) < 16
    partner = jnp.where(first, pltpu.roll(x, LANES - 16, 1), pltpu.roll(x, 16, 1))
    return x * cos + partner * sin


def _stream_tile(c_ref, x_ref, tile):
    is_ctx = (jnp.zeros((TILE, 1), jnp.int32) + tile) == 0
    return jnp.where(is_ctx, c_ref[0], x_ref[0])


def _stream_specs(d, off, shift=0):
    return [pl.BlockSpec((1, TILE, d), lambda bi, i: (bi, 0, 0)),
            pl.BlockSpec((1, TILE, d), lambda bi, i: (bi, jnp.maximum(i + shift - off, 1 - off), 0))]


_IN_COLS = []
_src = 0
for _dst, _w in ((C_ATT_Q, MIX), (C_ATT_K, LANES), (C_ATT_V, LANES), (C_ATT_Z, MIX), (C_GDN_Q, MIX), (C_GDN_K, MIX),
                 (C_GDN_V, MIX), (C_GDN_B, N_DIR * GDN_HEADS), (C_GDN_A, N_DIR * GDN_HEADS), (C_GDN_Z, MIX),
                 (C_LRU_X, MIX), (C_LRU_Z, MIX)):
    _IN_COLS.append((_src, _dst, _w))
    _src += _w
IN_WIDTH = _src


def _scan_order(t):
    return jnp.swapaxes(t.reshape(SEG, STEPS, t.shape[1]), 0, 1).reshape(t.shape)


def _time_order(t):
    return jnp.swapaxes(t.reshape(STEPS, SEG, t.shape[1]), 0, 1).reshape(t.shape)


def _inproj_kernel(c_ref, x_ref, mod_ref, g_ref, cos_ref, sin_ref, w_ref, o_ref, z_ref, lx_ref, w_s):
    @pl.when(jnp.logical_and(pl.program_id(0) == 0, pl.program_id(1) == 0))
    def _():
        lane = lax.broadcasted_iota(jnp.int32, (TILE, LANES), 1)
        n_gate = N_DIR * GDN_HEADS
        for r0 in range(0, w_s.shape[0], TILE):
            for src, dst, width in _IN_COLS:
                if width >= LANES:
                    w_s[r0:r0 + TILE, dst:dst + width] = w_ref[0, src:src + width, r0:r0 + TILE].T.astype(BF16)
            src_b = _IN_COLS[7][0]
            win = w_ref[0, src_b:src_b + LANES, r0:r0 + TILE].T
            w_s[r0:r0 + TILE, C_GDN_B:C_GDN_B + LANES] = jnp.where(lane < n_gate, win, 0.0).astype(BF16)
            w_s[r0:r0 + TILE, C_GDN_A:C_GDN_A + LANES] = jnp.where(
                lane < n_gate, pltpu.roll(win, LANES - n_gate, 1), 0.0).astype(BF16)

    x = _stream_tile(c_ref, x_ref, pl.program_id(1))
    ms = jnp.mean(x * x, axis=-1, keepdims=True)
    y = x * lax.rsqrt(ms + EPS) * g_ref[...]
    shift = mod_ref[0, 0, 0:1, :]
    scale = mod_ref[0, 0, 1:2, :]
    hb = (y * (1.0 + scale) + shift).astype(BF16)
    cos = cos_ref[...]
    sin = sin_ref[...]
    for c0 in range(0, N_IN, MIX):
        res = jnp.dot(hb, w_s[:, c0:c0 + MIX], preferred_element_type=F32)
        if c0 == C_ATT_Q:
            res = jnp.concatenate(
                [_rope(res[:, k:k + LANES], cos, sin) for k in range(0, MIX, LANES)], axis=1)
        if c0 == C_ATT_K:
            res = jnp.concatenate([_rope(res[:, :LANES], cos, sin), res[:, LANES:]], axis=1)
        if c0 == C_LRU_X:
            lx_ref[0] = _scan_order(res)
        elif c0 >= N_P:
            z_ref[0, :, c0 - N_P:c0 - N_P + MIX] = res.astype(z_ref.dtype)
        else:
            o_ref[0, :, c0:c0 + MIX] = res


def _inproj(c_src, x_src, off, modsel, g, cos_t, sin_t, w_in, layer):
    b, _, d = x_src.shape
    s = x_src.shape[1] + off * TILE
    nt = s // TILE
    return pl.pallas_call(
        _inproj_kernel,
        grid=(b, nt),
        in_specs=_stream_specs(d, off) + [
                  pl.BlockSpec((1, 1, 8, d), lambda bi, i: (bi, jnp.minimum(i, 1), 0, 0)),
                  pl.BlockSpec((1, d), lambda bi, i: (0, 0)),
                  pl.BlockSpec((TILE, LANES), lambda bi, i: (i, 0)),
                  pl.BlockSpec((TILE, LANES), lambda bi, i: (i, 0)),
                  pl.BlockSpec((1, IN_WIDTH, d), lambda bi, i: (layer, 0, 0), pipeline_mode=pl.Buffered(1))],
        out_specs=[pl.BlockSpec((1, TILE, N_P), lambda bi, i: (bi, i, 0)),
                   pl.BlockSpec((1, TILE, N_Z), lambda bi, i: (bi, i, 0)),
                   pl.BlockSpec((1, TILE, MIX), lambda bi, i: (bi, i, 0))],
        out_shape=[jax.ShapeDtypeStruct((b, s, N_P), F32), jax.ShapeDtypeStruct((b, s, N_Z), BF16),
                   jax.ShapeDtypeStruct((b, s, MIX), F32)],
        scratch_shapes=[pltpu.VMEM((d, N_IN), BF16)],
        compiler_params=_cparams("arbitrary", "arbitrary"),
        name="inproj",
    )(c_src, x_src, modsel, g, cos_t, sin_t, w_in)


def _att_kernel(sink_ref, q_ref, z_ref, kvp_ref, kvc_ref, kvn_ref, kvx_ref, o_ref, *, n_blocks):
    i = pl.program_id(1)
    n_sub = TILE // ATT_BLOCK
    n_band = 3 * ATT_BLOCK
    half = ATT_HEAD_DIM
    group = ATT_HEADS // ATT_KV_HEADS

    kv = jnp.concatenate([kvp_ref[0], kvc_ref[0], kvn_ref[0], kvx_ref[0]], axis=0)
    nk = kv.shape[0]
    k_all, v_all = kv[:, :LANES], kv[:, LANES:]
    k_sw, v_sw = pltpu.roll(k_all, half, 1), pltpu.roll(v_all, half, 1)
    lo_k = lax.broadcasted_iota(jnp.int32, (nk, LANES), 1) < half
    k_dup = [jnp.where(lo_k, k_all, k_sw).astype(BF16), jnp.where(lo_k, k_sw, k_all).astype(BF16)]
    v_dup = [jnp.where(lo_k, v_all, v_sw).astype(BF16), jnp.where(lo_k, v_sw, v_all).astype(BF16)]
    n_loc = (n_sub + 2) * ATT_BLOCK

    def keys(t, sub):
        return jnp.concatenate([t[sub * ATT_BLOCK:sub * ATT_BLOCK + n_band], t[n_loc:]], axis=0)

    r = lax.broadcasted_iota(jnp.int32, (ATT_BLOCK, n_band), 0)
    c = lax.broadcasted_iota(jnp.int32, (ATT_BLOCK, n_band), 1)
    in_win = jnp.abs(r - (c - ATT_BLOCK)) <= ATT_BLOCK
    lo_q = lax.broadcasted_iota(jnp.int32, (ATT_BLOCK, LANES), 1) < half
    outs = [[None] * (ATT_HEADS // 2) for _ in range(n_sub)]

    def band_bias(sub):
        m = i * n_sub + sub
        c_lo = jnp.where(m >= n_sub + 1, 0, jnp.where(m >= n_sub, ATT_BLOCK, n_band))
        c_hi = jnp.where(m < n_sub, 0, jnp.where(m <= n_blocks - 2, n_band, 2 * ATT_BLOCK))
        return jnp.where(jnp.logical_and(in_win, jnp.logical_and(c >= c_lo, c < c_hi)), 0.0, NEG_INF)

    def stream(sub, h, bias):
        q = q_ref[0, sub * ATT_BLOCK:(sub + 1) * ATT_BLOCK, :] * (ATT_HEAD_DIM ** -0.5 * LOG2E)
        lhs = []
        for j in range(h * group, (h + 1) * group):
            q_t = q[:, (j // 2) * LANES:(j // 2 + 1) * LANES]
            lhs.append(jnp.where(lo_q, q_t, 0.0) if j % 2 == 0 else jnp.where(lo_q, 0.0, q_t))
        qs = jnp.concatenate(lhs, axis=0).astype(BF16)
        s = lax.dot_general(qs, keys(k_dup[h], sub), (((1,), (1,)), ((), ())), preferred_element_type=F32)
        yield
        p_blocks, den_blocks = [], []
        for e in range(group):
            sb = s[e * ATT_BLOCK:(e + 1) * ATT_BLOCK]
            sk = sink_ref[h * group + e] * LOG2E
            s_band = sb[:, :n_band] + bias
            s_ctx = sb[:, n_band:]
            mx = jnp.maximum(jnp.maximum(jnp.max(s_band, axis=-1, keepdims=True),
                                         jnp.max(s_ctx, axis=-1, keepdims=True)), sk)
            p_band = jnp.exp2(s_band - mx)
            p_ctx = jnp.exp2(s_ctx - mx)
            den_blocks.append(jnp.sum(p_band, axis=-1, keepdims=True) + jnp.sum(p_ctx, axis=-1, keepdims=True)
                              + jnp.exp2(sk - mx))
            p_blocks.append(jnp.concatenate([p_band, p_ctx], axis=1).astype(BF16))
            yield
        p = jnp.concatenate(p_blocks, axis=0)
        res = (jnp.dot(p, keys(v_dup[h], sub), preferred_element_type=F32)
               / jnp.concatenate(den_blocks, axis=0))
        for t in range(group // 2):
            even = res[(2 * t) * ATT_BLOCK:(2 * t + 1) * ATT_BLOCK]
            odd = res[(2 * t + 1) * ATT_BLOCK:(2 * t + 2) * ATT_BLOCK]
            outs[sub][h * (group // 2) + t] = jnp.where(lo_q, even, odd)
        yield

    for sub in range(n_sub):
        bias = band_bias(sub)
        gens = [stream(sub, h, bias) for h in range(ATT_KV_HEADS)]
        for g in gens:
            next(g)
        for g in gens:
            for _ in g:
                pass
    for sub in range(n_sub):
        rows = slice(sub * ATT_BLOCK, (sub + 1) * ATT_BLOCK)
        o = jnp.concatenate(outs[sub], axis=1)
        o_ref[0, rows, :] = (o * _silu(z_ref[0, rows, :].astype(F32))).astype(o_ref.dtype)


def _attention(p, pz, sink):
    b, s, _ = p.shape
    nb = s // ATT_BLOCK
    n_sub = TILE // ATT_BLOCK
    kvcol = C_ATT_K // (2 * LANES)
    kv_spec = lambda f: pl.BlockSpec((1, ATT_BLOCK, 2 * LANES), f)
    return pl.pallas_call(
        functools.partial(_att_kernel, n_blocks=nb),
        grid=(b, s // TILE),
        in_specs=[pl.BlockSpec(memory_space=pltpu.SMEM),
                  pl.BlockSpec((1, TILE, MIX), lambda bi, i: (bi, i, C_ATT_Q // MIX)),
                  pl.BlockSpec((1, TILE, MIX), lambda bi, i: (bi, i, (C_ATT_Z - N_P) // MIX)),
                  kv_spec(lambda bi, i: (bi, jnp.maximum(i * n_sub - 1, n_sub), kvcol)),
                  pl.BlockSpec((1, TILE, 2 * LANES), lambda bi, i: (bi, i, kvcol)),
                  kv_spec(lambda bi, i: (bi, jnp.minimum((i + 1) * n_sub, nb - 1), kvcol)),
                  pl.BlockSpec((1, TILE, 2 * LANES), lambda bi, i: (bi, 0, kvcol))],
        out_specs=pl.BlockSpec((1, TILE, MIX), lambda bi, i: (bi, i, 0)),
        out_shape=jax.ShapeDtypeStruct((b, s, MIX), BF16),
        compiler_params=_cparams("parallel", "parallel"),
        name="attention",
    )(sink, p, pz, p, p, p, p)


def _conv4(x, hp, hn, w, prev_ok, next_ok):
    hp = hp * prev_ok
    hn = hn * next_ok
    ext = jnp.concatenate([hp, x, hn], axis=0)
    n = x.shape[0]
    own = slice(HALO, HALO + n)
    back2, back1, fwd1 = (pltpu.roll(ext, sh, 0) for sh in (2, 1, ext.shape[0] - 1))
    return w[0:1] * back2[own] + w[1:2] * back1[own] + w[2:3] * ext[own] + w[3:4] * fwd1[own]


def _halo_ok(i, n_tiles):
    prev_ok = (i >= 2).astype(F32)
    next_ok = jnp.logical_and(i != 0, i != n_tiles - 1).astype(F32)
    return prev_ok, next_ok


def _chunk_cumsum(g, reverse):
    n = g.shape[0]
    row = lax.broadcasted_iota(jnp.int32, g.shape, 0) % GDN_CHUNK
    s = 1
    while s < GDN_CHUNK:
        if reverse:
            g = g + jnp.where(row < GDN_CHUNK - s, pltpu.roll(g, n - s, 0), 0.0)
        else:
            g = g + jnp.where(row >= s, pltpu.roll(g, s, 0), 0.0)
        s *= 2
    return g


def _spread(cols, masks):
    out = cols[-1]
    for m, col in zip(masks[-2::-1], cols[-2::-1]):
        out = jnp.where(m, col, out)
    return out


def _head_block_diag(x, masks):
    return jnp.concatenate([jnp.where(m, x, 0.0) for m in masks], axis=0).astype(BF16)


def _gdn_prep_kernel(q_ref, qp_ref, qn_ref, k_ref, kp_ref, kn_ref, v_ref, vp_ref, vn_ref,
                     b_ref, a_ref, cw_ref, alog_ref, dtb_ref,
                     w_ref, u_ref, kd_ref, qg_ref, aqk_ref, gl_ref, *, n_tiles, n_batch):
    i = pl.program_id(0)
    prev_ok, next_ok = _halo_ok(i, n_tiles)
    hd, ch, nh = GDN_HEAD_DIM, GDN_CHUNK, GDN_HEADS
    nch = TILE // ch

    def prep(bi, x_ref, p_ref, n_ref, col):
        xh = _conv4(x_ref[bi], p_ref[bi], n_ref[bi], cw_ref[:, col:col + MIX], prev_ok, next_ok)
        return xh * (1.0 + jnp.tanh(xh))

    def l2n(t):
        parts = []
        for h in range(nh):
            th = t[:, h * hd:(h + 1) * hd]
            parts.append(th * lax.rsqrt(jnp.sum(th * th, axis=-1, keepdims=True) + EPS))
        return jnp.concatenate(parts, axis=1)

    hp = 2
    pk = hp * ch
    ii = lax.broadcasted_iota(jnp.int32, (ch, pk), 0)
    ll = lax.broadcasted_iota(jnp.int32, (ch, pk), 1)
    jj = ll % ch
    eye_p = jj == ii
    eye_f = eye_p.astype(F32)
    bd16 = (jj // 16) == (ii // 16)
    bd32 = (jj // 32) == (ii // 32)
    merge_masks = (jnp.logical_and(bd32, jnp.logical_not(bd16)), jnp.logical_not(bd32))
    strict = [jj < ii, jj > ii]
    incl = [jj <= ii, jj >= ii]
    head_p = [(ll // ch) == h for h in range(hp)]
    lw = lax.broadcasted_iota(jnp.int32, (ch, hp * hd), 1) // hd
    head_w = [lw == h for h in range(hp)]
    lane_row = lax.broadcasted_iota(jnp.int32, (1, LANES), 1)

    def pdot(a, b):
        return jnp.dot(a.astype(BF16), _head_block_diag(b, head_p), preferred_element_type=F32)

    def build(bi, units):
        q = l2n(prep(bi, q_ref, qp_ref, qn_ref, 0)) * (hd ** -0.5)
        yield
        k = l2n(prep(bi, k_ref, kp_ref, kn_ref, MIX))
        yield
        v = prep(bi, v_ref, vp_ref, vn_ref, 2 * MIX)

        yield
        lane = lax.broadcasted_iota(jnp.int32, (TILE, LANES), 1)
        g = -jnp.exp(alog_ref[...]) * _softplus(a_ref[bi] + dtb_ref[...])
        gcum = jnp.where(lane < nh, _chunk_cumsum(g, False), _chunk_cumsum(g, True))
        beta = _sigmoid(b_ref[bi])

        yield
        gl_rows = []
        for cidx in range(nch):
            rows = slice(cidx * ch, (cidx + 1) * ch)
            gc, bc = gcum[rows], beta[rows]
            g_end = jnp.where(lane_row < nh, gc[ch - 1:ch, :], gc[0:1, :])
            gl_rows.append(jnp.exp(g_end))
            for pair in range(nh // hp):
                lanes = slice(pair * hp * hd, (pair + 1) * hp * hd)
                kc, qc, vc = k[rows, lanes], q[rows, lanes], v[rows, lanes]
                kq = _dot_nt(jnp.concatenate([kc, qc], axis=0), _head_block_diag(kc, head_w))
                for d in range(N_DIR):
                    c0 = d * nh + pair * hp
                    gcols = [gc[:, c0 + h:c0 + h + 1] for h in range(hp)]
                    bcols = [bc[:, c0 + h:c0 + h + 1] for h in range(hp)]
                    gcol_p = _spread(gcols, head_p)
                    grow_p = jnp.sum(jnp.where(eye_p, gcol_p, 0.0), axis=0, keepdims=True)
                    dec = jnp.exp(jnp.where(incl[d], gcol_p - grow_p, 0.0))
                    lm = jnp.where(strict[d], _spread(bcols, head_p) * dec * kq[:ch], 0.0)
                    aqk = jnp.where(incl[d], dec * kq[ch:], 0.0)
                    aqk_ref[bi, rows, d * PACK + pair * pk:d * PACK + (pair + 1) * pk] = aqk.astype(BF16)
                    units.append((bi, rows, d, pair, lm, kc, qc, vc, gcols, bcols,
                                  [g_end[:, c0 + h:c0 + h + 1] for h in range(hp)]))
            yield
        gl_rows.append(jnp.zeros((8 - nch, LANES), F32))
        gl_ref[bi, 0] = jnp.concatenate(gl_rows, axis=0)


    def solve(units):
        lms = [u[4] for u in units]
        m = [jnp.where(bd16, -lm, 0.0) for lm in lms]
        x = [eye_f + mi for mi in m]
        pw = [pdot(mi, mi) for mi in m]
        yield
        for _ in range(2):
            r = [pdot(jnp.concatenate([xi, pi], axis=0), pi) for xi, pi in zip(x, pw)]
            x = [xi + ri[:ch] for xi, ri in zip(x, r)]
            pw = [ri[ch:] for ri in r]
            yield
        x = [xi + pdot(xi, pi) for xi, pi in zip(x, pw)]
        yield
        for mask in merge_masks:
            y = [pdot(xi, jnp.where(mask, lm, 0.0)) for xi, lm in zip(x, lms)]
            yield
            x = [xi - pdot(yi, xi) for xi, yi in zip(x, y)]
            yield

        for tinv, (bi, rows, d, pair, _, kc, qc, vc, gcols, bcols, gend_cols) in zip(x, units):
            slab = lambda t, h: t[:, h * hd:(h + 1) * hd]
            gam = [jnp.exp(gcol) for gcol in gcols]
            bgk = jnp.concatenate([(bcols[h] * gam[h]) * slab(kc, h) for h in range(hp)], axis=1)
            bv = jnp.concatenate([bcols[h] * slab(vc, h) for h in range(hp)], axis=1)
            rhs = jnp.concatenate([_head_block_diag(bgk, head_w), _head_block_diag(bv, head_w)], axis=1)
            wu = jnp.dot(tinv.astype(BF16), rhs, preferred_element_type=F32)
            half = hp * hd
            cols = slice(d * MIX + pair * half, d * MIX + (pair + 1) * half)
            w_ref[bi, rows, cols] = wu[:, :half].astype(BF16)
            u_ref[bi, rows, cols] = wu[:, half:]
            kd_ref[bi, rows, cols] = jnp.concatenate(
                [jnp.exp(gend_cols[h] - gcols[h]) * slab(kc, h) for h in range(hp)], axis=1).astype(BF16)
            qg_ref[bi, rows, cols] = jnp.concatenate([gam[h] * slab(qc, h) for h in range(hp)], axis=1).astype(BF16)
            yield

    units = [[] for _ in range(n_batch)]
    for _ in build(0, units[0]):
        pass
    for bi in range(n_batch):
        stage = [solve(units[bi])]
        if bi + 1 < n_batch:
            stage.append(build(bi + 1, units[bi + 1]))
        _round_robin(stage, (1,) * len(stage))


def _gdn_prep(p, conv_w, alog, dtb):
    b, s, _ = p.shape
    nt = s // TILE
    hpt = TILE // HALO
    nh = s // HALO
    wide = N_DIR * GDN_HEADS * GDN_HEAD_DIM

    def tile_spec(col):
        return pl.BlockSpec((b, TILE, MIX), lambda i: (0, i, col // MIX))

    def prev_spec(col):
        return pl.BlockSpec((b, HALO, MIX), lambda i: (0, jnp.maximum(i * hpt - 1, 0), col // MIX))

    def next_spec(col):
        return pl.BlockSpec((b, HALO, MIX), lambda i: (0, jnp.minimum((i + 1) * hpt, nh - 1), col // MIX))

    in_specs = []
    for col in (C_GDN_Q, C_GDN_K, C_GDN_V):
        in_specs += [tile_spec(col), prev_spec(col), next_spec(col)]
    in_specs += [pl.BlockSpec((b, TILE, LANES), lambda i: (0, i, C_GDN_B // LANES)),
                 pl.BlockSpec((b, TILE, LANES), lambda i: (0, i, C_GDN_A // LANES)),
                 pl.BlockSpec((4, 3 * MIX), lambda i: (0, 0)),
                 pl.BlockSpec((1, LANES), lambda i: (0, 0)),
                 pl.BlockSpec((1, LANES), lambda i: (0, 0))]
    wide_spec = pl.BlockSpec((b, TILE, wide), lambda i: (0, i, 0))
    out_specs = [wide_spec, wide_spec, wide_spec, wide_spec,
                 pl.BlockSpec((b, TILE, N_DIR * PACK), lambda i: (0, i, 0)),
                 pl.BlockSpec((b, 1, 8, LANES), lambda i: (0, i, 0, 0))]
    out_shape = [jax.ShapeDtypeStruct((b, s, wide), BF16),
                 jax.ShapeDtypeStruct((b, s, wide), F32),
                 jax.ShapeDtypeStruct((b, s, wide), BF16),
                 jax.ShapeDtypeStruct((b, s, wide), BF16),
                 jax.ShapeDtypeStruct((b, s, N_DIR * PACK), BF16),
                 jax.ShapeDtypeStruct((b, nt, 8, LANES), F32)]
    return pl.pallas_call(
        functools.partial(_gdn_prep_kernel, n_tiles=nt, n_batch=b),
        grid=(nt,),
        in_specs=in_specs,
        out_specs=out_specs,
        out_shape=out_shape,
        compiler_params=_cparams("parallel"),
        name="gdn_prep",
    )(p, p, p, p, p, p, p, p, p, p, p, conv_w, alog, dtb)


def _gdn_scan_body(bi, wf, uf, kdf, qgf, af, glf, wb, ub, kdb, qgb, ab, glb, of_ref, ob_ref, state):
    hd, ch, nh = GDN_HEAD_DIM, GDN_CHUNK, GDN_HEADS
    nch = TILE // ch
    hp = 2
    lw = lax.broadcasted_iota(jnp.int32, (ch, hp * hd), 1) // hd
    head_w = [lw == h for h in range(hp)]
    dirs = ((wf, uf, kdf, qgf, af, glf, of_ref), (wb, ub, kdb, qgb, ab, glb, ob_ref))
    for step in range(nch):
        cidx = [step, nch - 1 - step]
        rows = [slice(c * ch, (c + 1) * ch) for c in cidx]
        s_old = {}
        ws = {}
        for d, (w_r, _, _, qg_r, _, _, _) in enumerate(dirs):
            for h in range(nh):
                cols = slice(h * hd, (h + 1) * hd)
                s_old[d, h] = state[bi, d, h]
                lhs = jnp.concatenate([w_r[bi, rows[d], cols], qg_r[bi, rows[d], cols]], axis=0)
                ws[d, h] = jnp.dot(lhs, s_old[d, h].astype(BF16), preferred_element_type=F32)
            yield
        for d, (_, u_r, kd_r, _, a_r, gl_r, o_r) in enumerate(dirs):
            un = u_r[bi, rows[d], :] - jnp.concatenate([ws[d, h][:ch] for h in range(nh)], axis=1)
            inter = jnp.concatenate([ws[d, h][ch:] for h in range(nh)], axis=1)
            intra = jnp.concatenate(
                [jnp.dot(a_r[bi, rows[d], g * hp * ch:(g + 1) * hp * ch],
                         _head_block_diag(un[:, g * hp * hd:(g + 1) * hp * hd], head_w),
                         preferred_element_type=F32) for g in range(nh // hp)], axis=1)
            o_r[bi, rows[d], :] = (inter + intra).astype(o_r.dtype)
            unb = un.astype(BF16)
            for h in range(nh):
                cols = slice(h * hd, (h + 1) * hd)
                gl = gl_r[bi, 0, cidx[d]:cidx[d] + 1, d * nh + h:d * nh + h + 1]
                state[bi, d, h] = gl * s_old[d, h] + _dot_tn(kd_r[bi, rows[d], cols], unb[:, cols])
            yield


def _mirror(j, nt):
    return jnp.where(j == 0, 0, nt - j)


def _lru_body(bi, xf_ref, xfp2_ref, xfp_ref, xfn_ref, xb_ref, xbp2_ref, xbp_ref, xbn_ref, cw_ref, cb_ref, wg_ref, bg_ref,
              lam_ref, hf_ref, hb_ref, carry, *, n_tiles):
    j = pl.program_id(0)
    seg_n = SEG
    steps = STEPS
    sub = lax.broadcasted_iota(jnp.int32, (seg_n, LRU_WIDTH), 0)

    def one_dir(d, x_ref, p2_ref, p_ref, n_ref, tile_idx, o_ref):
        prev_ok, next_ok = _halo_ok(tile_idx, n_tiles)
        xs = [x_ref[bi, st * seg_n:(st + 1) * seg_n, :] for st in range(steps)]
        last = seg_n - 1
        p1 = jnp.where(sub == 0, p_ref[bi, last:last + 1, :] * prev_ok, pltpu.roll(xs[steps - 1], 1, 0))
        p2 = jnp.where(sub == 0, p2_ref[bi, last:last + 1, :] * prev_ok, pltpu.roll(xs[steps - 2], 1, 0))
        n1 = jnp.where(sub == last, n_ref[bi, 0:1, :] * next_ok, pltpu.roll(xs[0], last, 0))
        ext = [p2, p1] + xs + [n1]
        cw = cw_ref[...]
        xh = jnp.concatenate(
            [cw[0:1] * ext[st] + cw[1:2] * ext[st + 1] + cw[2:3] * ext[st + 2] + cw[3:4] * ext[st + 3]
             for st in range(steps)], axis=0) + cb_ref[...]
        yield
        gates = jnp.dot(xh.astype(BF16), wg_ref[:, d * 2 * LRU_WIDTH:(d + 1) * 2 * LRU_WIDTH],
                        preferred_element_type=F32)
        yield
        c_half = (-0.5 * LRU_C) * _softplus(-lam_ref[d:d + 1, :])
        a_blocks, b_blocks = [], []
        blk = TILE // 4
        for r0 in range(0, TILE, blk):
            gb = gates[r0:r0 + blk]
            tr = jnp.tanh(gb[:, :LRU_WIDTH] + bg_ref[2 * d:2 * d + 1, :])
            ti = jnp.tanh(gb[:, LRU_WIDTH:] + bg_ref[2 * d + 1:2 * d + 2, :])
            log_a = c_half * tr + c_half
            ab = jnp.exp(log_a)
            b_blocks.append(jnp.sqrt(-jnp.tanh(log_a) * (ab * ab + 1.0)) * ((ti + 1.0) * xh[r0:r0 + blk]))
            a_blocks.append(ab)
            yield
        a = jnp.concatenate(a_blocks, axis=0)
        bb = jnp.concatenate(b_blocks, axis=0)

        order = list(range(steps)) if d == 0 else list(range(steps - 1, -1, -1))
        piece = lambda t, st: t[st * seg_n:(st + 1) * seg_n]
        a_tot = piece(a, order[0])
        b_tot = piece(bb, order[0])
        for n, st in enumerate(order[1:]):
            b_tot = piece(a, st) * b_tot + piece(bb, st)
            a_tot = piece(a, st) * a_tot
            if n % 8 == 7:
                yield
        s = 1
        while s < seg_n:
            if d == 0:
                keep = sub >= s
                a_sh = jnp.where(keep, pltpu.roll(a_tot, s, 0), 1.0)
                b_sh = jnp.where(keep, pltpu.roll(b_tot, s, 0), 0.0)
            else:
                keep = sub < seg_n - s
                a_sh = jnp.where(keep, pltpu.roll(a_tot, seg_n - s, 0), 1.0)
                b_sh = jnp.where(keep, pltpu.roll(b_tot, seg_n - s, 0), 0.0)
            b_tot = a_tot * b_sh + b_tot
            a_tot = a_tot * a_sh
            s *= 2
        h_in0 = carry[bi, d]
        h_out = a_tot * h_in0 + b_tot
        if d == 0:
            h = jnp.where(sub == 0, h_in0, pltpu.roll(h_out, 1, 0))
            carry[bi, d] = h_out[seg_n - 1:seg_n]
        else:
            h = jnp.where(sub == seg_n - 1, h_in0, pltpu.roll(h_out, seg_n - 1, 0))
            carry[bi, d] = h_out[0:1]
        hs = [None] * steps
        for n, st in enumerate(order):
            h = piece(a, st) * h + piece(bb, st)
            hs[st] = h
            if n % 8 == 7:
                yield
        o_ref[bi] = jnp.concatenate(hs, axis=0).astype(o_ref.dtype)

    return [one_dir(0, xf_ref, xfp2_ref, xfp_ref, xfn_ref, j, hf_ref),
            one_dir(1, xb_ref, xbp2_ref, xbp_ref, xbn_ref, _mirror(j, n_tiles), hb_ref)]


N_SCAN_IN = 12
N_LRU_IN = 13


def _round_robin(gens, weights):
    live = list(zip(gens, weights))
    while live:
        nxt = []
        for g, w in live:
            alive = True
            for _ in range(w):
                try:
                    next(g)
                except StopIteration:
                    alive = False
                    break
            if alive:
                nxt.append((g, w))
        live = nxt


def _sweep_kernel(*refs, n_tiles, n_batch):
    scan_in, lru_in = refs[:N_SCAN_IN], refs[N_SCAN_IN:N_SCAN_IN + N_LRU_IN]
    of_ref, ob_ref, hf_ref, hb_ref, state, carry = refs[N_SCAN_IN + N_LRU_IN:]

    @pl.when(pl.program_id(0) == 0)
    def _():
        state[...] = jnp.zeros_like(state)
        carry[...] = jnp.zeros_like(carry)

    scans = [_gdn_scan_body(bi, *scan_in, of_ref, ob_ref, state) for bi in range(n_batch)]
    lrus = [g for bi in range(n_batch) for g in _lru_body(bi, *lru_in, hf_ref, hb_ref, carry, n_tiles=n_tiles)]
    _round_robin(scans + lrus, (1,) * len(scans) + (2,) * len(lrus))


def _sweep(w, u, kd, qg, aqk, gl, lx, conv_w, conv_b, wg, bg, lam):
    b, s, _ = u.shape
    nt = s // TILE
    half = GDN_HEADS * GDN_HEAD_DIM
    hpt = TILE // HALO
    nh = s // HALO
    tile_of = (lambda j: j, lambda j: _mirror(j, nt))

    def scan_specs(d):
        idx = lambda j: (0, tile_of[d](j), d)
        gidx = lambda j: (0, tile_of[d](j), 0, 0)
        return [pl.BlockSpec((b, TILE, half), idx)] * 4 + [pl.BlockSpec((b, TILE, PACK), idx),
                                                           pl.BlockSpec((b, 1, 8, LANES), gidx)]

    def lru_specs(d):
        t = tile_of[d]
        return [pl.BlockSpec((b, TILE, MIX), lambda j: (0, t(j), 0)),
                pl.BlockSpec((b, HALO, MIX), lambda j: (0, jnp.maximum(t(j) * hpt - 2, 0), 0)),
                pl.BlockSpec((b, HALO, MIX), lambda j: (0, jnp.maximum(t(j) * hpt - 1, 0), 0)),
                pl.BlockSpec((b, HALO, MIX), lambda j: (0, jnp.minimum((t(j) + 1) * hpt, nh - 1), 0))]

    full = lambda shape: pl.BlockSpec(shape, lambda j: tuple(0 for _ in shape))
    out_spec = lambda d: pl.BlockSpec((b, TILE, MIX), lambda j: (0, tile_of[d](j), 0))
    in_specs = (scan_specs(0) + scan_specs(1) + lru_specs(0) + lru_specs(1)
                + [full((4, LRU_WIDTH)), full((1, LRU_WIDTH)), full((LRU_WIDTH, 4 * LRU_WIDTH)),
                   full((4, LRU_WIDTH)), full((N_DIR, LRU_WIDTH))])
    assert len(in_specs) == N_SCAN_IN + N_LRU_IN
    return pl.pallas_call(
        functools.partial(_sweep_kernel, n_tiles=nt, n_batch=b),
        grid=(nt,),
        in_specs=in_specs,
        out_specs=[out_spec(0), out_spec(1), out_spec(0), out_spec(1)],
        out_shape=[jax.ShapeDtypeStruct((b, s, MIX), BF16)] * 4,
        scratch_shapes=[pltpu.VMEM((b, N_DIR, GDN_HEADS, GDN_HEAD_DIM, GDN_HEAD_DIM), F32),
                        pltpu.VMEM((b, N_DIR, 1, LRU_WIDTH), F32)],
        compiler_params=_cparams("arbitrary"),
        name="sweep",
    )(w, u, kd, qg, aqk, gl, w, u, kd, qg, aqk, gl, *([lx] * 8), conv_w, conv_b, wg, bg, lam)


def _outproj_kernel(c_ref, x_ref, mod_ref, att_ref, of_ref, ob_ref, gz_ref, hf_ref, hb_ref, lz_ref,
                    gn_ref, w_ref, fg_ref, o_ref, *, final, first_tile):
    hd = GDN_HEAD_DIM
    o = of_ref[0].astype(F32) + ob_ref[0].astype(F32)
    parts = []
    for h in range(GDN_HEADS):
        oh = o[:, h * hd:(h + 1) * hd]
        parts.append(oh * lax.rsqrt(jnp.mean(oh * oh, axis=-1, keepdims=True) + EPS) * gn_ref[...])
    gdn = jnp.concatenate(parts, axis=1) * _silu(gz_ref[0].astype(F32))
    lru = _time_order(hf_ref[0].astype(F32) + hb_ref[0].astype(F32)) * _silu(lz_ref[0].astype(F32))
    acc = (jnp.dot(att_ref[0], w_ref[0:MIX, :], preferred_element_type=F32)
           + jnp.dot(gdn.astype(BF16), w_ref[MIX:2 * MIX, :], preferred_element_type=F32)
           + jnp.dot(lru.astype(BF16), w_ref[2 * MIX:3 * MIX, :], preferred_element_type=F32))
    gate = mod_ref[0, 0, 2:3, :]
    xn = _stream_tile(c_ref, x_ref, pl.program_id(1) + first_tile) + gate * acc
    if final:
        ms = jnp.mean(xn * xn, axis=-1, keepdims=True)
        xn = xn * lax.rsqrt(ms + EPS) * fg_ref[...]
    o_ref[0] = xn


def _outproj(c_src, x_src, src_off, modsel, att, of, ob, pz, hf, hb, gdn_norm, w_out, final_g, final):
    b, s, _ = pz.shape
    d = x_src.shape[2]
    nt = s // TILE
    off = 1 if final else 0
    n_out = nt - off
    tok = lambda col: pl.BlockSpec((1, TILE, MIX), lambda bi, i: (bi, i + off, col))
    return pl.pallas_call(
        functools.partial(_outproj_kernel, final=final, first_tile=off),
        grid=(b, n_out),
        in_specs=_stream_specs(d, src_off, off) + [
                  pl.BlockSpec((1, 1, 8, d), lambda bi, i: (bi, jnp.minimum(i + off, 1), 0, 0)),
                  tok(0), tok(0), tok(0), tok((C_GDN_Z - N_P) // MIX), tok(0), tok(0), tok((C_LRU_Z - N_P) // MIX),
                  pl.BlockSpec((1, GDN_HEAD_DIM), lambda bi, i: (0, 0)),
                  pl.BlockSpec((3 * MIX, d), lambda bi, i: (0, 0)),
                  pl.BlockSpec((1, d), lambda bi, i: (0, 0))],
        out_specs=pl.BlockSpec((1, TILE, d), lambda bi, i: (bi, i, 0)),
        out_shape=jax.ShapeDtypeStruct((b, n_out * TILE, d), F32),
        compiler_params=_cparams("parallel", "parallel"),
        name="outproj",
    )(c_src, x_src, modsel, att, of, ob, pz, hf, hb, pz, gdn_norm, w_out, final_g)


def _rope_tables(s, ctx_len):
    t = np.arange(s - ctx_len)
    n_freq = ATT_HEAD_DIM // 4
    inv = ROPE_BASE ** (-np.arange(n_freq, dtype=np.float64) / n_freq)
    ang = [(t // GRID_W)[:, None] * inv, (t % GRID_W)[:, None] * inv]
    cos = np.concatenate([np.cos(ang[0]), np.cos(ang[0]), np.cos(ang[1]), np.cos(ang[1])], axis=1)
    sin = np.concatenate([-np.sin(ang[0]), np.sin(ang[0]), -np.sin(ang[1]), np.sin(ang[1])], axis=1)
    cos = np.concatenate([np.ones((ctx_len, ATT_HEAD_DIM)), cos], axis=0)
    sin = np.concatenate([np.zeros((ctx_len, ATT_HEAD_DIM)), sin], axis=0)
    return (jnp.asarray(np.tile(cos, (1, 2)), F32), jnp.asarray(np.tile(sin, (1, 2)), F32))


def _block_diag(w):
    eye = jnp.eye(LRU_BLOCKS, dtype=w.dtype)
    return jnp.einsum("nde,nm->ndme", w, eye).reshape(LRU_WIDTH, LRU_WIDTH)


def _pad_lanes(v):
    return jnp.pad(v.reshape(1, -1), ((0, 0), (0, LANES - v.size)))


def kernel(x, c, ctx, c_ctx, norm_g, w_mod, b_mod, w_in, att_sink, gdn_conv, gdn_a_log, gdn_dt_bias, gdn_norm,
           lru_conv_w, lru_conv_b, lru_w_r, lru_b_r, lru_w_i, lru_b_i, lru_lambda, w_out, final_g):
    b, t, d = x.shape
    ctx_len = ctx.shape[1]
    depth = w_in.shape[0]
    assert ctx_len == TILE and t % TILE == 0 and d == D_MODEL and b == 2
    s = ctx_len + t
    nt = s // TILE

    cos_t, sin_t = _rope_tables(s, ctx_len)
    ct = jnp.pad(jnp.concatenate([c, c_ctx[None]], axis=0).T, ((0, 0), (0, LANES - b - 1)))
    mods = _modulation(ct, w_mod, b_mod)

    w_in_t = jnp.swapaxes(w_in, 1, 2)
    c_src, x_src, src_off = ctx, x, 1
    out = None
    for l in range(depth):
        m3 = mods[l, :3].reshape(3, 3, d)
        sel = jnp.stack([jnp.stack([m3[b], m3[bi]], axis=0) for bi in range(b)], axis=0)
        modsel = jnp.pad(sel, ((0, 0), (0, 0), (0, 5), (0, 0)))
        p, pz, lx = _inproj(c_src, x_src, src_off, modsel, norm_g[l].reshape(1, d), cos_t, sin_t, w_in_t, l)

        att = _attention(p, pz, att_sink[l])

        prep = _gdn_prep(p, 0.5 * gdn_conv[l], _pad_lanes(gdn_a_log[l]), _pad_lanes(gdn_dt_bias[l]))

        wg = jnp.concatenate([_block_diag(lru_w_r[l, 0]), _block_diag(lru_w_i[l, 0]),
                              _block_diag(lru_w_r[l, 1]), _block_diag(lru_w_i[l, 1])], axis=1).astype(BF16)
        bg = 0.5 * jnp.stack([lru_b_r[l, 0], lru_b_i[l, 0], lru_b_r[l, 1], lru_b_i[l, 1]], axis=0)
        o_f, o_b, h_f, h_b = _sweep(*prep, lx, 0.5 * lru_conv_w[l], 0.5 * lru_conv_b[l].reshape(1, -1), wg, bg,
                                    lru_lambda[l])

        final = l == depth - 1
        res = _outproj(c_src, x_src, src_off, modsel, att, o_f, o_b, pz, h_f, h_b, gdn_norm[l].reshape(1, -1),
                       w_out[l].astype(BF16), final_g.reshape(1, d), final)
        if final:
            out = res
        else:
            c_src, x_src, src_off = res, res, 0
    return out
```

```python
import functools
import math

import numpy as np
import jax
import jax.numpy as jnp
from jax import lax
from jax.experimental import pallas as pl
from jax.experimental.pallas import tpu as pltpu

F32 = jnp.float32
BF16 = jnp.bfloat16

EPS = 1e-6
NEG_INF = -1e30
LOG2E = math.log2(math.e)
GRID_W = 64
ROPE_BASE = 10000.0

D_MODEL = 1024
ATT_HEADS = 8
ATT_KV_HEADS = 2
ATT_HEAD_DIM = 64
ATT_BLOCK = 128
GDN_HEADS = 4
GDN_HEAD_DIM = 128
GDN_CHUNK = 64
LRU_WIDTH = 512
LRU_BLOCKS = 8
LRU_BLOCK_DIM = 64
LRU_C = 8.0
N_DIR = 2
MIX = 512
PACK = GDN_HEADS * GDN_CHUNK

TILE = 256
HALO = 8
LANES = 128
VMEM_LIMIT = 56 * 1024 * 1024

C_ATT_Q, C_GDN_Q, C_GDN_K, C_GDN_V = (i * MIX for i in range(4))
C_ATT_K = 4 * MIX
C_ATT_V = C_ATT_K + LANES
C_GDN_B = C_ATT_V + LANES
C_GDN_A = C_GDN_B + LANES
N_P = C_GDN_A + LANES
C_ATT_Z, C_GDN_Z, C_LRU_Z = (N_P + i * MIX for i in range(3))
N_Z = 3 * MIX
C_LRU_X = N_P + N_Z
N_IN = C_LRU_X + MIX
SEG = 8
STEPS = TILE // SEG


def _cparams(*sem):
    return pltpu.CompilerParams(dimension_semantics=sem, vmem_limit_bytes=VMEM_LIMIT)


def _sigmoid(x):
    return 1.0 / (1.0 + jnp.exp(-x))


def _silu(x):
    return x * _sigmoid(x)


def _softplus(x):
    return jnp.maximum(x, 0.0) + jnp.log1p(jnp.exp(-jnp.abs(x)))


def _dot_nt(a, b):
    return lax.dot_general(a.astype(BF16), b.astype(BF16), (((1,), (1,)), ((), ())),
                           preferred_element_type=F32)


def _dot_tn(a, b):
    return lax.dot_general(a.astype(BF16), b.astype(BF16), (((0,), (0,)), ((), ())),
                           preferred_element_type=F32)


def _mod_kernel(ct_ref, w_ref, b_ref, o_ref):
    s = _silu(ct_ref[...])
    w = w_ref[0]
    rid = lax.broadcasted_iota(jnp.int32, (8, w.shape[1]), 0)
    out = jnp.zeros((8, w.shape[1]), F32)
    for r in range(3):
        row = jnp.sum(w * s[:, r:r + 1], axis=0, keepdims=True) + b_ref[0]
        out = jnp.where(rid == r, row, out)
    o_ref[0] = out


def _modulation(ct, w_mod, b_mod):
    depth, d, d3 = w_mod.shape
    return pl.pallas_call(
        _mod_kernel,
        grid=(depth, d3 // d),
        in_specs=[pl.BlockSpec((d, LANES), lambda l, j: (0, 0)),
                  pl.BlockSpec((1, d, d), lambda l, j: (l, 0, j)),
                  pl.BlockSpec((1, 1, d), lambda l, j: (l, 0, j))],
        out_specs=pl.BlockSpec((1, 8, d), lambda l, j: (l, 0, j)),
        out_shape=jax.ShapeDtypeStruct((depth, 8, d3), F32),
        compiler_params=_cparams("parallel", "parallel"),
        name="modulation",
    )(ct, w_mod, b_mod.reshape(depth, 1, d3))


def _rope(x, cos, sin):
    lane = lax.broadcasted_iota(jnp.int32, x.shape, 1)
    first = (lane % 32) < 16
    partner = jnp.where(first, pltpu.roll(x, LANES - 16, 1), pltpu.roll(x, 16, 1))
    return x * cos + partner * sin


def _stream_tile(c_ref, x_ref, tile, bi=0):
    is_ctx = (jnp.zeros((TILE, 1), jnp.int32) + tile) == 0
    return jnp.where(is_ctx, c_ref[bi], x_ref[bi])


def _stream_specs(d, off, shift=0):
    return [pl.BlockSpec((1, TILE, d), lambda bi, i: (bi, 0, 0)),
            pl.BlockSpec((1, TILE, d), lambda bi, i: (bi, jnp.maximum(i + shift - off, 1 - off), 0))]


_IN_COLS = []
_src = 0
for _dst, _w in ((C_ATT_Q, MIX), (C_ATT_K, LANES), (C_ATT_V, LANES), (C_ATT_Z, MIX), (C_GDN_Q, MIX), (C_GDN_K, MIX),
                 (C_GDN_V, MIX), (C_GDN_B, N_DIR * GDN_HEADS), (C_GDN_A, N_DIR * GDN_HEADS), (C_GDN_Z, MIX),
                 (C_LRU_X, MIX), (C_LRU_Z, MIX)):
    _IN_COLS.append((_src, _dst, _w))
    _src += _w
IN_WIDTH = _src


def _scan_order(t):
    return jnp.swapaxes(t.reshape(SEG, STEPS, t.shape[1]), 0, 1).reshape(t.shape)


def _time_order(t):
    return jnp.swapaxes(t.reshape(STEPS, SEG, t.shape[1]), 0, 1).reshape(t.shape)


def _inproj_kernel(c_ref, x_ref, mod_ref, g_ref, cos_ref, sin_ref, w_ref, o_ref, z_ref, lx_ref, w_s, *, n_batch):
    @pl.when(pl.program_id(0) == 0)
    def _():
        lane = lax.broadcasted_iota(jnp.int32, (TILE, LANES), 1)
        n_gate = N_DIR * GDN_HEADS
        for r0 in range(0, w_s.shape[0], TILE):
            for src, dst, width in _IN_COLS:
                if width >= LANES:
                    w_s[r0:r0 + TILE, dst:dst + width] = w_ref[0, src:src + width, r0:r0 + TILE].T.astype(BF16)
            src_b = _IN_COLS[7][0]
            win = w_ref[0, src_b:src_b + LANES, r0:r0 + TILE].T
            w_s[r0:r0 + TILE, C_GDN_B:C_GDN_B + LANES] = jnp.where(lane < n_gate, win, 0.0).astype(BF16)
            w_s[r0:r0 + TILE, C_GDN_A:C_GDN_A + LANES] = jnp.where(
                lane < n_gate, pltpu.roll(win, LANES - n_gate, 1), 0.0).astype(BF16)

    hs = []
    for bi in range(n_batch):
        x = _stream_tile(c_ref, x_ref, pl.program_id(0), bi)
        ms = jnp.mean(x * x, axis=-1, keepdims=True)
        y = x * lax.rsqrt(ms + EPS) * g_ref[...]
        shift = mod_ref[bi, 0, 0:1, :]
        scale = mod_ref[bi, 0, 1:2, :]
        hs.append((y * (1.0 + scale) + shift).astype(BF16))
    hb = jnp.concatenate(hs, axis=0)
    cos = cos_ref[...]
    sin = sin_ref[...]
    for c0 in range(0, N_IN, MIX):
        both = jnp.dot(hb, w_s[:, c0:c0 + MIX], preferred_element_type=F32)
        for bi in range(n_batch):
            res = both[bi * TILE:(bi + 1) * TILE]
            if c0 == C_ATT_Q:
                res = jnp.concatenate(
                    [_rope(res[:, k:k + LANES], cos, sin) for k in range(0, MIX, LANES)], axis=1)
            if c0 == C_ATT_K:
                res = jnp.concatenate([_rope(res[:, :LANES], cos, sin), res[:, LANES:]], axis=1)
            if c0 == C_LRU_X:
                lx_ref[bi] = _scan_order(res)
            elif c0 >= N_P:
                z_ref[bi, :, c0 - N_P:c0 - N_P + MIX] = res.astype(z_ref.dtype)
            else:
                o_ref[bi, :, c0:c0 + MIX] = res


def _inproj(c_src, x_src, off, modsel, g, cos_t, sin_t, w_in, layer):
    b, _, d = x_src.shape
    s = x_src.shape[1] + off * TILE
    nt = s // TILE
    return pl.pallas_call(
        functools.partial(_inproj_kernel, n_batch=b),
        grid=(nt,),
        in_specs=[pl.BlockSpec((b, TILE, d), lambda i: (0, 0, 0)),
                  pl.BlockSpec((b, TILE, d), lambda i: (0, jnp.maximum(i - off, 1 - off), 0)),
                  pl.BlockSpec((b, 1, 8, d), lambda i: (0, jnp.minimum(i, 1), 0, 0)),
                  pl.BlockSpec((1, d), lambda i: (0, 0)),
                  pl.BlockSpec((TILE, LANES), lambda i: (i, 0)),
                  pl.BlockSpec((TILE, LANES), lambda i: (i, 0)),
                  pl.BlockSpec((1, IN_WIDTH, d), lambda i: (layer, 0, 0), pipeline_mode=pl.Buffered(1))],
        out_specs=[pl.BlockSpec((b, TILE, N_P), lambda i: (0, i, 0)),
                   pl.BlockSpec((b, TILE, N_Z), lambda i: (0, i, 0)),
                   pl.BlockSpec((b, TILE, MIX), lambda i: (0, i, 0))],
        out_shape=[jax.ShapeDtypeStruct((b, s, N_P), F32), jax.ShapeDtypeStruct((b, s, N_Z), BF16),
                   jax.ShapeDtypeStruct((b, s, MIX), F32)],
        scratch_shapes=[pltpu.VMEM((d, N_IN), BF16)],
        compiler_params=_cparams("arbitrary"),
        name="inproj",
    )(c_src, x_src, modsel, g, cos_t, sin_t, w_in)


def _att_kernel(sink_ref, q_ref, z_ref, kvp_ref, kvc_ref, kvn_ref, kvx_ref, o_ref, *, n_blocks):
    i = pl.program_id(1)
    n_sub = TILE // ATT_BLOCK
    n_band = 3 * ATT_BLOCK
    half = ATT_HEAD_DIM
    group = ATT_HEADS // ATT_KV_HEADS

    kv = jnp.concatenate([kvp_ref[0], kvc_ref[0], kvn_ref[0], kvx_ref[0]], axis=0)
    nk = kv.shape[0]
    k_all, v_all = kv[:, :LANES], kv[:, LANES:]
    k_sw, v_sw = pltpu.roll(k_all, half, 1), pltpu.roll(v_all, half, 1)
    lo_k = lax.broadcasted_iota(jnp.int32, (nk, LANES), 1) < half
    k_dup = [jnp.where(lo_k, k_all, k_sw).astype(BF16), jnp.where(lo_k, k_sw, k_all).astype(BF16)]
    v_dup = [jnp.where(lo_k, v_all, v_sw).astype(BF16), jnp.where(lo_k, v_sw, v_all).astype(BF16)]
    n_loc = (n_sub + 2) * ATT_BLOCK

    def keys(t, sub):
        return jnp.concatenate([t[sub * ATT_BLOCK:sub * ATT_BLOCK + n_band], t[n_loc:]], axis=0)

    r = lax.broadcasted_iota(jnp.int32, (ATT_BLOCK, n_band), 0)
    c = lax.broadcasted_iota(jnp.int32, (ATT_BLOCK, n_band), 1)
    in_win = jnp.abs(r - (c - ATT_BLOCK)) <= ATT_BLOCK
    lo_q = lax.broadcasted_iota(jnp.int32, (ATT_BLOCK, LANES), 1) < half
    outs = [[None] * (ATT_HEADS // 2) for _ in range(n_sub)]

    def band_bias(sub):
        m = i * n_sub + sub
        c_lo = jnp.where(m >= n_sub + 1, 0, jnp.where(m >= n_sub, ATT_BLOCK, n_band))
        c_hi = jnp.where(m < n_sub, 0, jnp.where(m <= n_blocks - 2, n_band, 2 * ATT_BLOCK))
        return jnp.where(jnp.logical_and(in_win, jnp.logical_and(c >= c_lo, c < c_hi)), 0.0, NEG_INF)

    def stream(sub, h, bias):
        q = q_ref[0, sub * ATT_BLOCK:(sub + 1) * ATT_BLOCK, :] * (ATT_HEAD_DIM ** -0.5 * LOG2E)
        lhs = []
        for j in range(h * group, (h + 1) * group):
            q_t = q[:, (j // 2) * LANES:(j // 2 + 1) * LANES]
            lhs.append(jnp.where(lo_q, q_t, 0.0) if j % 2 == 0 else jnp.where(lo_q, 0.0, q_t))
        qs = jnp.concatenate(lhs, axis=0).astype(BF16)
        s = lax.dot_general(qs, keys(k_dup[h], sub), (((1,), (1,)), ((), ())), preferred_element_type=F32)
        yield
        p_blocks, den_blocks = [], []
        for e in range(group):
            sb = s[e * ATT_BLOCK:(e + 1) * ATT_BLOCK]
            sk = sink_ref[h * group + e] * LOG2E
            s_band = sb[:, :n_band] + bias
            s_ctx = sb[:, n_band:]
            mx = jnp.maximum(jnp.maximum(jnp.max(s_band, axis=-1, keepdims=True),
                                         jnp.max(s_ctx, axis=-1, keepdims=True)), sk)
            p_band = jnp.exp2(s_band - mx)
            p_ctx = jnp.exp2(s_ctx - mx)
            den_blocks.append(jnp.sum(p_band, axis=-1, keepdims=True) + jnp.sum(p_ctx, axis=-1, keepdims=True)
                              + jnp.exp2(sk - mx))
            p_blocks.append(jnp.concatenate([p_band, p_ctx], axis=1).astype(BF16))
            yield
        p = jnp.concatenate(p_blocks, axis=0)
        res = (jnp.dot(p, keys(v_dup[h], sub), preferred_element_type=F32)
               / jnp.concatenate(den_blocks, axis=0))
        for t in range(group // 2):
            even = res[(2 * t) * ATT_BLOCK:(2 * t + 1) * ATT_BLOCK]
            odd = res[(2 * t + 1) * ATT_BLOCK:(2 * t + 2) * ATT_BLOCK]
            outs[sub][h * (group // 2) + t] = jnp.where(lo_q, even, odd)
        yield

    for sub in range(n_sub):
        bias = band_bias(sub)
        gens = [stream(sub, h, bias) for h in range(ATT_KV_HEADS)]
        for g in gens:
            next(g)
        for g in gens:
            for _ in g:
                pass
    for sub in range(n_sub):
        rows = slice(sub * ATT_BLOCK, (sub + 1) * ATT_BLOCK)
        o = jnp.concatenate(outs[sub], axis=1)
        o_ref[0, rows, :] = (o * _silu(z_ref[0, rows, :].astype(F32))).astype(o_ref.dtype)


def _attention(p, pz, sink):
    b, s, _ = p.shape
    nb = s // ATT_BLOCK
    n_sub = TILE // ATT_BLOCK
    kvcol = C_ATT_K // (2 * LANES)
    kv_spec = lambda f: pl.BlockSpec((1, ATT_BLOCK, 2 * LANES), f)
    return pl.pallas_call(
        functools.partial(_att_kernel, n_blocks=nb),
        grid=(b, s // TILE),
        in_specs=[pl.BlockSpec(memory_space=pltpu.SMEM),
                  pl.BlockSpec((1, TILE, MIX), lambda bi, i: (bi, i, C_ATT_Q // MIX)),
                  pl.BlockSpec((1, TILE, MIX), lambda bi, i: (bi, i, (C_ATT_Z - N_P) // MIX)),
                  kv_spec(lambda bi, i: (bi, jnp.maximum(i * n_sub - 1, n_sub), kvcol)),
                  pl.BlockSpec((1, TILE, 2 * LANES), lambda bi, i: (bi, i, kvcol)),
                  kv_spec(lambda bi, i: (bi, jnp.minimum((i + 1) * n_sub, nb - 1), kvcol)),
                  pl.BlockSpec((1, TILE, 2 * LANES), lambda bi, i: (bi, 0, kvcol))],
        out_specs=pl.BlockSpec((1, TILE, MIX), lambda bi, i: (bi, i, 0)),
        out_shape=jax.ShapeDtypeStruct((b, s, MIX), BF16),
        compiler_params=_cparams("parallel", "parallel"),
        name="attention",
    )(sink, p, pz, p, p, p, p)


def _conv4(x, hp, hn, w, prev_ok, next_ok):
    hp = hp * prev_ok
    hn = hn * next_ok
    ext = jnp.concatenate([hp, x, hn], axis=0)
    n = x.shape[0]
    own = slice(HALO, HALO + n)
    back2, back1, fwd1 = (pltpu.roll(ext, sh, 0) for sh in (2, 1, ext.shape[0] - 1))
    return w[0:1] * back2[own] + w[1:2] * back1[own] + w[2:3] * ext[own] + w[3:4] * fwd1[own]


def _halo_ok(i, n_tiles):
    prev_ok = (i >= 2).astype(F32)
    next_ok = jnp.logical_and(i != 0, i != n_tiles - 1).astype(F32)
    return prev_ok, next_ok


def _chunk_cumsum(g, reverse):
    n = g.shape[0]
    row = lax.broadcasted_iota(jnp.int32, g.shape, 0) % GDN_CHUNK
    s = 1
    while s < GDN_CHUNK:
        if reverse:
            g = g + jnp.where(row < GDN_CHUNK - s, pltpu.roll(g, n - s, 0), 0.0)
        else:
            g = g + jnp.where(row >= s, pltpu.roll(g, s, 0), 0.0)
        s *= 2
    return g


def _spread(cols, masks):
    out = cols[-1]
    for m, col in zip(masks[-2::-1], cols[-2::-1]):
        out = jnp.where(m, col, out)
    return out


def _head_block_diag(x, masks):
    return jnp.concatenate([jnp.where(m, x, 0.0) for m in masks], axis=0).astype(BF16)


def _gdn_prep_kernel(q_ref, qp_ref, qn_ref, k_ref, kp_ref, kn_ref, v_ref, vp_ref, vn_ref,
                     b_ref, a_ref, cw_ref, alog_ref, dtb_ref,
                     w_ref, u_ref, kd_ref, qg_ref, aqk_ref, gl_ref, *, n_tiles, n_batch):
    i = pl.program_id(0)
    prev_ok, next_ok = _halo_ok(i, n_tiles)
    hd, ch, nh = GDN_HEAD_DIM, GDN_CHUNK, GDN_HEADS
    nch = TILE // ch

    def prep(bi, x_ref, p_ref, n_ref, col):
        xh = _conv4(x_ref[bi], p_ref[bi], n_ref[bi], cw_ref[:, col:col + MIX], prev_ok, next_ok)
        return xh * (1.0 + jnp.tanh(xh))

    def l2n(t):
        parts = []
        for h in range(nh):
            th = t[:, h * hd:(h + 1) * hd]
            parts.append(th * lax.rsqrt(jnp.sum(th * th, axis=-1, keepdims=True) + EPS))
        return jnp.concatenate(parts, axis=1)

    hp = 2
    pk = hp * ch
    ii = lax.broadcasted_iota(jnp.int32, (ch, pk), 0)
    ll = lax.broadcasted_iota(jnp.int32, (ch, pk), 1)
    jj = ll % ch
    eye_p = jj == ii
    eye_f = eye_p.astype(F32)
    bd16 = (jj // 16) == (ii // 16)
    bd32 = (jj // 32) == (ii // 32)
    merge_masks = (jnp.logical_and(bd32, jnp.logical_not(bd16)), jnp.logical_not(bd32))
    strict = [jj < ii, jj > ii]
    incl = [jj <= ii, jj >= ii]
    head_p = [(ll // ch) == h for h in range(hp)]
    lw = lax.broadcasted_iota(jnp.int32, (ch, hp * hd), 1) // hd
    head_w = [lw == h for h in range(hp)]
    lane_row = lax.broadcasted_iota(jnp.int32, (1, LANES), 1)

    def pdot(a, b):
        return jnp.dot(a.astype(BF16), _head_block_diag(b, head_p), preferred_element_type=F32)

    def build(bi, units):
        q = l2n(prep(bi, q_ref, qp_ref, qn_ref, 0)) * (hd ** -0.5)
        yield
        k = l2n(prep(bi, k_ref, kp_ref, kn_ref, MIX))
        yield
        v = prep(bi, v_ref, vp_ref, vn_ref, 2 * MIX)

        yield
        lane = lax.broadcasted_iota(jnp.int32, (TILE, LANES), 1)
        g = -jnp.exp(alog_ref[...]) * _softplus(a_ref[bi] + dtb_ref[...])
        gcum = jnp.where(lane < nh, _chunk_cumsum(g, False), _chunk_cumsum(g, True))
        beta = _sigmoid(b_ref[bi])

        yield
        gl_rows = []
        for cidx in range(nch):
            rows = slice(cidx * ch, (cidx + 1) * ch)
            gc, bc = gcum[rows], beta[rows]
            g_end = jnp.where(lane_row < nh, gc[ch - 1:ch, :], gc[0:1, :])
            gl_rows.append(jnp.exp(g_end))
            for pair in range(nh // hp):
                lanes = slice(pair * hp * hd, (pair + 1) * hp * hd)
                kc, qc, vc = k[rows, lanes], q[rows, lanes], v[rows, lanes]
                kq = _dot_nt(jnp.concatenate([kc, qc], axis=0), _head_block_diag(kc, head_w))
                for d in range(N_DIR):
                    c0 = d * nh + pair * hp
                    gcols = [gc[:, c0 + h:c0 + h + 1] for h in range(hp)]
                    bcols = [bc[:, c0 + h:c0 + h + 1] for h in range(hp)]
                    gcol_p = _spread(gcols, head_p)
                    grow_p = jnp.sum(jnp.where(eye_p, gcol_p, 0.0), axis=0, keepdims=True)
                    dec = jnp.exp(jnp.where(incl[d], gcol_p - grow_p, 0.0))
                    lm = jnp.where(strict[d], _spread(bcols, head_p) * dec * kq[:ch], 0.0)
                    aqk = jnp.where(incl[d], dec * kq[ch:], 0.0)
                    aqk_ref[bi, rows, d * PACK + pair * pk:d * PACK + (pair + 1) * pk] = aqk.astype(BF16)
                    units.append((bi, rows, d, pair, lm, kc, qc, vc, gcols, bcols,
                                  [g_end[:, c0 + h:c0 + h + 1] for h in range(hp)]))
            yield
        gl_rows.append(jnp.zeros((8 - nch, LANES), F32))
        gl_ref[bi, 0] = jnp.concatenate(gl_rows, axis=0)


    def solve(units):
        lms = [u[4] for u in units]
        m = [jnp.where(bd16, -lm, 0.0) for lm in lms]
        x = [eye_f + mi for mi in m]
        pw = [pdot(mi, mi) for mi in m]
        yield
        for _ in range(2):
            r = [pdot(jnp.concatenate([xi, pi], axis=0), pi) for xi, pi in zip(x, pw)]
            x = [xi + ri[:ch] for xi, ri in zip(x, r)]
            pw = [ri[ch:] for ri in r]
            yield
        x = [xi + pdot(xi, pi) for xi, pi in zip(x, pw)]
        yield
        for mask in merge_masks:
            y = [pdot(xi, jnp.where(mask, lm, 0.0)) for xi, lm in zip(x, lms)]
            yield
            x = [xi - pdot(yi, xi) for xi, yi in zip(x, y)]
            yield

        for tinv, (bi, rows, d, pair, _, kc, qc, vc, gcols, bcols, gend_cols) in zip(x, units):
            slab = lambda t, h: t[:, h * hd:(h + 1) * hd]
            gam = [jnp.exp(gcol) for gcol in gcols]
            bgk = jnp.concatenate([(bcols[h] * gam[h]) * slab(kc, h) for h in range(hp)], axis=1)
            bv = jnp.concatenate([bcols[h] * slab(vc, h) for h in range(hp)], axis=1)
            rhs = jnp.concatenate([_head_block_diag(bgk, head_w), _head_block_diag(bv, head_w)], axis=1)
            wu = jnp.dot(tinv.astype(BF16), rhs, preferred_element_type=F32)
            half = hp * hd
            cols = slice(d * MIX + pair * half, d * MIX + (pair + 1) * half)
            w_ref[bi, rows, cols] = wu[:, :half].astype(BF16)
            u_ref[bi, rows, cols] = wu[:, half:]
            kd_ref[bi, rows, cols] = jnp.concatenate(
                [jnp.exp(gend_cols[h] - gcols[h]) * slab(kc, h) for h in range(hp)], axis=1).astype(BF16)
            qg_ref[bi, rows, cols] = jnp.concatenate([gam[h] * slab(qc, h) for h in range(hp)], axis=1).astype(BF16)
            yield

    units = [[] for _ in range(n_batch)]
    for _ in build(0, units[0]):
        pass
    for bi in range(n_batch):
        stage = [solve(units[bi])]
        if bi + 1 < n_batch:
            stage.append(build(bi + 1, units[bi + 1]))
        _round_robin(stage, (1,) * len(stage))


def _gdn_prep(p, conv_w, alog, dtb):
    b, s, _ = p.shape
    nt = s // TILE
    hpt = TILE // HALO
    nh = s // HALO
    wide = N_DIR * GDN_HEADS * GDN_HEAD_DIM

    def tile_spec(col):
        return pl.BlockSpec((b, TILE, MIX), lambda i: (0, i, col // MIX))

    def prev_spec(col):
        return pl.BlockSpec((b, HALO, MIX), lambda i: (0, jnp.maximum(i * hpt - 1, 0), col // MIX))

    def next_spec(col):
        return pl.BlockSpec((b, HALO, MIX), lambda i: (0, jnp.minimum((i + 1) * hpt, nh - 1), col // MIX))

    in_specs = []
    for col in (C_GDN_Q, C_GDN_K, C_GDN_V):
        in_specs += [tile_spec(col), prev_spec(col), next_spec(col)]
    in_specs += [pl.BlockSpec((b, TILE, LANES), lambda i: (0, i, C_GDN_B // LANES)),
                 pl.BlockSpec((b, TILE, LANES), lambda i: (0, i, C_GDN_A // LANES)),
                 pl.BlockSpec((4, 3 * MIX), lambda i: (0, 0)),
                 pl.BlockSpec((1, LANES), lambda i: (0, 0)),
                 pl.BlockSpec((1, LANES), lambda i: (0, 0))]
    wide_spec = pl.BlockSpec((b, TILE, wide), lambda i: (0, i, 0))
    out_specs = [wide_spec, wide_spec, wide_spec, wide_spec,
                 pl.BlockSpec((b, TILE, N_DIR * PACK), lambda i: (0, i, 0)),
                 pl.BlockSpec((b, 1, 8, LANES), lambda i: (0, i, 0, 0))]
    out_shape = [jax.ShapeDtypeStruct((b, s, wide), BF16),
                 jax.ShapeDtypeStruct((b, s, wide), F32),
                 jax.ShapeDtypeStruct((b, s, wide), BF16),
                 jax.ShapeDtypeStruct((b, s, wide), BF16),
                 jax.ShapeDtypeStruct((b, s, N_DIR * PACK), BF16),
                 jax.ShapeDtypeStruct((b, nt, 8, LANES), F32)]
    return pl.pallas_call(
        functools.partial(_gdn_prep_kernel, n_tiles=nt, n_batch=b),
        grid=(nt,),
        in_specs=in_specs,
        out_specs=out_specs,
        out_shape=out_shape,
        compiler_params=_cparams("parallel"),
        name="gdn_prep",
    )(p, p, p, p, p, p, p, p, p, p, p, conv_w, alog, dtb)


def _gdn_scan_body(bi, wf, uf, kdf, qgf, af, glf, wb, ub, kdb, qgb, ab, glb, of_ref, ob_ref, state):
    hd, ch, nh = GDN_HEAD_DIM, GDN_CHUNK, GDN_HEADS
    nch = TILE // ch
    hp = 2
    lw = lax.broadcasted_iota(jnp.int32, (ch, hp * hd), 1) // hd
    head_w = [lw == h for h in range(hp)]
    dirs = ((wf, uf, kdf, qgf, af, glf, of_ref), (wb, ub, kdb, qgb, ab, glb, ob_ref))
    for step in range(nch):
        cidx = [step, nch - 1 - step]
        rows = [slice(c * ch, (c + 1) * ch) for c in cidx]
        s_old = {}
        ws = {}
        for d, (w_r, _, _, qg_r, _, _, _) in enumerate(dirs):
            for h in range(nh):
                cols = slice(h * hd, (h + 1) * hd)
                s_old[d, h] = state[bi, d, h]
                lhs = jnp.concatenate([w_r[bi, rows[d], cols], qg_r[bi, rows[d], cols]], axis=0)
                ws[d, h] = jnp.dot(lhs, s_old[d, h].astype(BF16), preferred_element_type=F32)
            yield
        for d, (_, u_r, kd_r, _, a_r, gl_r, o_r) in enumerate(dirs):
            un = u_r[bi, rows[d], :] - jnp.concatenate([ws[d, h][:ch] for h in range(nh)], axis=1)
            inter = jnp.concatenate([ws[d, h][ch:] for h in range(nh)], axis=1)
            intra = jnp.concatenate(
                [jnp.dot(a_r[bi, rows[d], g * hp * ch:(g + 1) * hp * ch],
                         _head_block_diag(un[:, g * hp * hd:(g + 1) * hp * hd], head_w),
                         preferred_element_type=F32) for g in range(nh // hp)], axis=1)
            o_r[bi, rows[d], :] = (inter + intra).astype(o_r.dtype)
            unb = un.astype(BF16)
            for h in range(nh):
                cols = slice(h * hd, (h + 1) * hd)
                gl = gl_r[bi, 0, cidx[d]:cidx[d] + 1, d * nh + h:d * nh + h + 1]
                state[bi, d, h] = gl * s_old[d, h] + _dot_tn(kd_r[bi, rows[d], cols], unb[:, cols])
            yield


def _mirror(j, nt):
    return jnp.where(j == 0, 0, nt - j)


def _lru_body(bi, xf_ref, xfp2_ref, xfp_ref, xfn_ref, xb_ref, xbp2_ref, xbp_ref, xbn_ref, cw_ref, cb_ref, wg_ref, bg_ref,
              lam_ref, hf_ref, hb_ref, carry, *, n_tiles):
    j = pl.program_id(0)
    seg_n = SEG
    steps = STEPS
    sub = lax.broadcasted_iota(jnp.int32, (seg_n, LRU_WIDTH), 0)

    def one_dir(d, x_ref, p2_ref, p_ref, n_ref, tile_idx, o_ref):
        prev_ok, next_ok = _halo_ok(tile_idx, n_tiles)
        xs = [x_ref[bi, st * seg_n:(st + 1) * seg_n, :] for st in range(steps)]
        last = seg_n - 1
        p1 = jnp.where(sub == 0, p_ref[bi, last:last + 1, :] * prev_ok, pltpu.roll(xs[steps - 1], 1, 0))
        p2 = jnp.where(sub == 0, p2_ref[bi, last:last + 1, :] * prev_ok, pltpu.roll(xs[steps - 2], 1, 0))
        n1 = jnp.where(sub == last, n_ref[bi, 0:1, :] * next_ok, pltpu.roll(xs[0], last, 0))
        ext = [p2, p1] + xs + [n1]
        cw = cw_ref[...]
        xh = jnp.concatenate(
            [cw[0:1] * ext[st] + cw[1:2] * ext[st + 1] + cw[2:3] * ext[st + 2] + cw[3:4] * ext[st + 3]
             for st in range(steps)], axis=0) + cb_ref[...]
        yield
        gates = jnp.dot(xh.astype(BF16), wg_ref[:, d * 2 * LRU_WIDTH:(d + 1) * 2 * LRU_WIDTH],
                        preferred_element_type=F32)
        yield
        c_half = (-0.5 * LRU_C) * _softplus(-lam_ref[d:d + 1, :])
        a_blocks, b_blocks = [], []
        blk = TILE // 4
        for r0 in range(0, TILE, blk):
            gb = gates[r0:r0 + blk]
            tr = jnp.tanh(gb[:, :LRU_WIDTH] + bg_ref[2 * d:2 * d + 1, :])
            ti = jnp.tanh(gb[:, LRU_WIDTH:] + bg_ref[2 * d + 1:2 * d + 2, :])
            log_a = c_half * tr + c_half
            ab = jnp.exp(log_a)
            b_blocks.append(jnp.sqrt(-jnp.tanh(log_a) * (ab * ab + 1.0)) * ((ti + 1.0) * xh[r0:r0 + blk]))
            a_blocks.append(ab)
            yield
        a = jnp.concatenate(a_blocks, axis=0)
        bb = jnp.concatenate(b_blocks, axis=0)

        order = list(range(steps)) if d == 0 else list(range(steps - 1, -1, -1))
        piece = lambda t, st: t[st * seg_n:(st + 1) * seg_n]
        a_tot = piece(a, order[0])
        b_tot = piece(bb, order[0])
        for n, st in enumerate(order[1:]):
            b_tot = piece(a, st) * b_tot + piece(bb, st)
            a_tot = piece(a, st) * a_tot
            if n % 8 == 7:
                yield
        s = 1
        while s < seg_n:
            if d == 0:
                keep = sub >= s
                a_sh = jnp.where(keep, pltpu.roll(a_tot, s, 0), 1.0)
                b_sh = jnp.where(keep, pltpu.roll(b_tot, s, 0), 0.0)
            else:
                keep = sub < seg_n - s
                a_sh = jnp.where(keep, pltpu.roll(a_tot, seg_n - s, 0), 1.0)
                b_sh = jnp.where(keep, pltpu.roll(b_tot, seg_n - s, 0), 0.0)
            b_tot = a_tot * b_sh + b_tot
            a_tot = a_tot * a_sh
            s *= 2
        h_in0 = carry[bi, d]
        h_out = a_tot * h_in0 + b_tot
        if d == 0:
            h = jnp.where(sub == 0, h_in0, pltpu.roll(h_out, 1, 0))
            carry[bi, d] = h_out[seg_n - 1:seg_n]
        else:
            h = jnp.where(sub == seg_n - 1, h_in0, pltpu.roll(h_out, seg_n - 1, 0))
            carry[bi, d] = h_out[0:1]
        hs = [None] * steps
        for n, st in enumerate(order):
            h = piece(a, st) * h + piece(bb, st)
            hs[st] = h
            if n % 8 == 7:
                yield
        o_ref[bi] = jnp.concatenate(hs, axis=0).astype(o_ref.dtype)

    return [one_dir(0, xf_ref, xfp2_ref, xfp_ref, xfn_ref, j, hf_ref),
            one_dir(1, xb_ref, xbp2_ref, xbp_ref, xbn_ref, _mirror(j, n_tiles), hb_ref)]


N_SCAN_IN = 12
N_LRU_IN = 13


def _round_robin(gens, weights):
    live = list(zip(gens, weights))
    while live:
        nxt = []
        for g, w in live:
            alive = True
            for _ in range(w):
                try:
                    next(g)
                except StopIteration:
                    alive = False
                    break
            if alive:
                nxt.append((g, w))
        live = nxt


def _sweep_kernel(*refs, n_tiles, n_batch):
    scan_in, lru_in = refs[:N_SCAN_IN], refs[N_SCAN_IN:N_SCAN_IN + N_LRU_IN]
    of_ref, ob_ref, hf_ref, hb_ref, state, carry = refs[N_SCAN_IN + N_LRU_IN:]

    @pl.when(pl.program_id(0) == 0)
    def _():
        state[...] = jnp.zeros_like(state)
        carry[...] = jnp.zeros_like(carry)

    scans = [_gdn_scan_body(bi, *scan_in, of_ref, ob_ref, state) for bi in range(n_batch)]
    lrus = [g for bi in range(n_batch) for g in _lru_body(bi, *lru_in, hf_ref, hb_ref, carry, n_tiles=n_tiles)]
    _round_robin(scans + lrus, (1,) * len(scans) + (2,) * len(lrus))


def _sweep(w, u, kd, qg, aqk, gl, lx, conv_w, conv_b, wg, bg, lam):
    b, s, _ = u.shape
    nt = s // TILE
    half = GDN_HEADS * GDN_HEAD_DIM
    hpt = TILE // HALO
    nh = s // HALO
    tile_of = (lambda j: j, lambda j: _mirror(j, nt))

    def scan_specs(d):
        idx = lambda j: (0, tile_of[d](j), d)
        gidx = lambda j: (0, tile_of[d](j), 0, 0)
        return [pl.BlockSpec((b, TILE, half), idx)] * 4 + [pl.BlockSpec((b, TILE, PACK), idx),
                                                           pl.BlockSpec((b, 1, 8, LANES), gidx)]

    def lru_specs(d):
        t = tile_of[d]
        return [pl.BlockSpec((b, TILE, MIX), lambda j: (0, t(j), 0)),
                pl.BlockSpec((b, HALO, MIX), lambda j: (0, jnp.maximum(t(j) * hpt - 2, 0), 0)),
                pl.BlockSpec((b, HALO, MIX), lambda j: (0, jnp.maximum(t(j) * hpt - 1, 0), 0)),
                pl.BlockSpec((b, HALO, MIX), lambda j: (0, jnp.minimum((t(j) + 1) * hpt, nh - 1), 0))]

    full = lambda shape: pl.BlockSpec(shape, lambda j: tuple(0 for _ in shape))
    out_spec = lambda d: pl.BlockSpec((b, TILE, MIX), lambda j: (0, tile_of[d](j), 0))
    in_specs = (scan_specs(0) + scan_specs(1) + lru_specs(0) + lru_specs(1)
                + [full((4, LRU_WIDTH)), full((1, LRU_WIDTH)), full((LRU_WIDTH, 4 * LRU_WIDTH)),
                   full((4, LRU_WIDTH)), full((N_DIR, LRU_WIDTH))])
    assert len(in_specs) == N_SCAN_IN + N_LRU_IN
    return pl.pallas_call(
        functools.partial(_sweep_kernel, n_tiles=nt, n_batch=b),
        grid=(nt,),
        in_specs=in_specs,
        out_specs=[out_spec(0), out_spec(1), out_spec(0), out_spec(1)],
        out_shape=[jax.ShapeDtypeStruct((b, s, MIX), BF16)] * 4,
        scratch_shapes=[pltpu.VMEM((b, N_DIR, GDN_HEADS, GDN_HEAD_DIM, GDN_HEAD_DIM), F32),
                        pltpu.VMEM((b, N_DIR, 1, LRU_WIDTH), F32)],
        compiler_params=_cparams("arbitrary"),
        name="sweep",
    )(w, u, kd, qg, aqk, gl, w, u, kd, qg, aqk, gl, *([lx] * 8), conv_w, conv_b, wg, bg, lam)


def _outproj_kernel(c_ref, x_ref, mod_ref, att_ref, of_ref, ob_ref, gz_ref, hf_ref, hb_ref, lz_ref,
                    gn_ref, w_ref, fg_ref, o_ref, *, final, first_tile):
    hd = GDN_HEAD_DIM
    o = of_ref[0].astype(F32) + ob_ref[0].astype(F32)
    parts = []
    for h in range(GDN_HEADS):
        oh = o[:, h * hd:(h + 1) * hd]
        parts.append(oh * lax.rsqrt(jnp.mean(oh * oh, axis=-1, keepdims=True) + EPS) * gn_ref[...])
    gdn = jnp.concatenate(parts, axis=1) * _silu(gz_ref[0].astype(F32))
    lru = _time_order(hf_ref[0].astype(F32) + hb_ref[0].astype(F32)) * _silu(lz_ref[0].astype(F32))
    acc = (jnp.dot(att_ref[0], w_ref[0:MIX, :], preferred_element_type=F32)
           + jnp.dot(gdn.astype(BF16), w_ref[MIX:2 * MIX, :], preferred_element_type=F32)
           + jnp.dot(lru.astype(BF16), w_ref[2 * MIX:3 * MIX, :], preferred_element_type=F32))
    gate = mod_ref[0, 0, 2:3, :]
    xn = _stream_tile(c_ref, x_ref, pl.program_id(1) + first_tile) + gate * acc
    if final:
        ms = jnp.mean(xn * xn, axis=-1, keepdims=True)
        xn = xn * lax.rsqrt(ms + EPS) * fg_ref[...]
    o_ref[0] = xn


def _outproj(c_src, x_src, src_off, modsel, att, of, ob, pz, hf, hb, gdn_norm, w_out, final_g, final):
    b, s, _ = pz.shape
    d = x_src.shape[2]
    nt = s // TILE
    off = 1 if final else 0
    n_out = nt - off
    tok = lambda col: pl.BlockSpec((1, TILE, MIX), lambda bi, i: (bi, i + off, col))
    return pl.pallas_call(
        functools.partial(_outproj_kernel, final=final, first_tile=off),
        grid=(b, n_out),
        in_specs=_stream_specs(d, src_off, off) + [
                  pl.BlockSpec((1, 1, 8, d), lambda bi, i: (bi, jnp.minimum(i + off, 1), 0, 0)),
                  tok(0), tok(0), tok(0), tok((C_GDN_Z - N_P) // MIX), tok(0), tok(0), tok((C_LRU_Z - N_P) // MIX),
                  pl.BlockSpec((1, GDN_HEAD_DIM), lambda bi, i: (0, 0)),
                  pl.BlockSpec((3 * MIX, d), lambda bi, i: (0, 0)),
                  pl.BlockSpec((1, d), lambda bi, i: (0, 0))],
        out_specs=pl.BlockSpec((1, TILE, d), lambda bi, i: (bi, i, 0)),
        out_shape=jax.ShapeDtypeStruct((b, n_out * TILE, d), F32),
        compiler_params=_cparams("parallel", "parallel"),
        name="outproj",
    )(c_src, x_src, modsel, att, of, ob, pz, hf, hb, pz, gdn_norm, w_out, final_g)


def _rope_tables(s, ctx_len):
    t = np.arange(s - ctx_len)
    n_freq = ATT_HEAD_DIM // 4
    inv = ROPE_BASE ** (-np.arange(n_freq, dtype=np.float64) / n_freq)
    ang = [(t // GRID_W)[:, None] * inv, (t % GRID_W)[:, None] * inv]
    cos = np.concatenate([np.cos(ang[0]), np.cos(ang[0]), np.cos(ang[1]), np.cos(ang[1])], axis=1)
    sin = np.concatenate([-np.sin(ang[0]), np.sin(ang[0]), -np.sin(ang[1]), np.sin(ang[1])], axis=1)
    cos = np.concatenate([np.ones((ctx_len, ATT_HEAD_DIM)), cos], axis=0)
    sin = np.concatenate([np.zeros((ctx_len, ATT_HEAD_DIM)), sin], axis=0)
    return (jnp.asarray(np.tile(cos, (1, 2)), F32), jnp.asarray(np.tile(sin, (1, 2)), F32))


def _block_diag(w):
    eye = jnp.eye(LRU_BLOCKS, dtype=w.dtype)
    return jnp.einsum("nde,nm->ndme", w, eye).reshape(LRU_WIDTH, LRU_WIDTH)


def _pad_lanes(v):
    return jnp.pad(v.reshape(1, -1), ((0, 0), (0, LANES - v.size)))


def kernel(x, c, ctx, c_ctx, norm_g, w_mod, b_mod, w_in, att_sink, gdn_conv, gdn_a_log, gdn_dt_bias, gdn_norm,
           lru_conv_w, lru_conv_b, lru_w_r, lru_b_r, lru_w_i, lru_b_i, lru_lambda, w_out, final_g):
    b, t, d = x.shape
    ctx_len = ctx.shape[1]
    depth = w_in.shape[0]
    assert ctx_len == TILE and t % TILE == 0 and d == D_MODEL and b == 2
    s = ctx_len + t
    nt = s // TILE

    cos_t, sin_t = _rope_tables(s, ctx_len)
    ct = jnp.pad(jnp.concatenate([c, c_ctx[None]], axis=0).T, ((0, 0), (0, LANES - b - 1)))
    mods = _modulation(ct, w_mod, b_mod)

    w_in_t = jnp.swapaxes(w_in, 1, 2)
    c_src, x_src, src_off = ctx, x, 1
    out = None
    for l in range(depth):
        m3 = mods[l, :3].reshape(3, 3, d)
        sel = jnp.stack([jnp.stack([m3[b], m3[bi]], axis=0) for bi in range(b)], axis=0)
        modsel = jnp.pad(sel, ((0, 0), (0, 0), (0, 5), (0, 0)))
        p, pz, lx = _inproj(c_src, x_src, src_off, modsel, norm_g[l].reshape(1, d), cos_t, sin_t, w_in_t, l)

        att = _attention(p, pz, att_sink[l])

        prep = _gdn_prep(p, 0.5 * gdn_conv[l], _pad_lanes(gdn_a_log[l]), _pad_lanes(gdn_dt_bias[l]))

        wg = jnp.concatenate([_block_diag(lru_w_r[l, 0]), _block_diag(lru_w_i[l, 0]),
                              _block_diag(lru_w_r[l, 1]), _block_diag(lru_w_i[l, 1])], axis=1).astype(BF16)
        bg = 0.5 * jnp.stack([lru_b_r[l, 0], lru_b_i[l, 0], lru_b_r[l, 1], lru_b_i[l, 1]], axis=0)
        o_f, o_b, h_f, h_b = _sweep(*prep, lx, 0.5 * lru_conv_w[l], 0.5 * lru_conv_b[l].reshape(1, -1), wg, bg,
                                    lru_lambda[l])

        final = l == depth - 1
        res = _outproj(c_src, x_src, src_off, modsel, att, o_f, o_b, pz, h_f, h_b, gdn_norm[l].reshape(1, -1),
                       w_out[l].astype(BF16), final_g.reshape(1, d), final)
        if final:
            out = res
        else:
            c_src, x_src, src_off = res, res, 0
    return out
```

```python
import functools
import math

import numpy as np
import jax
import jax.numpy as jnp
from jax import lax
from jax.experimental import pallas as pl
from jax.experimental.pallas import tpu as pltpu

F32 = jnp.float32
BF16 = jnp.bfloat16

EPS = 1e-6
NEG_INF = -1e30
LOG2E = math.log2(math.e)
GRID_W = 64
ROPE_BASE = 10000.0

D_MODEL = 1024
ATT_HEADS = 8
ATT_KV_HEADS = 2
ATT_HEAD_DIM = 64
ATT_BLOCK = 128
GDN_HEADS = 4
GDN_HEAD_DIM = 128
GDN_CHUNK = 64
LRU_WIDTH = 512
LRU_BLOCKS = 8
LRU_BLOCK_DIM = 64
LRU_C = 8.0
N_DIR = 2
MIX = 512
PACK = GDN_HEADS * GDN_CHUNK

TILE = 256
HALO = 8
LANES = 128
VMEM_LIMIT = 56 * 1024 * 1024

C_ATT_Q, C_GDN_Q, C_GDN_K, C_GDN_V = (i * MIX for i in range(4))
C_ATT_K = 4 * MIX
C_ATT_V = C_ATT_K + LANES
C_GDN_B = C_ATT_V + LANES
C_GDN_A = C_GDN_B + LANES
N_P = C_GDN_A + LANES
C_ATT_Z, C_GDN_Z, C_LRU_Z = (N_P + i * MIX for i in range(3))
N_Z = 3 * MIX
C_LRU_X = N_P + N_Z
N_IN = C_LRU_X + MIX
SEG = 8
STEPS = TILE // SEG


def _cparams(*sem):
    return pltpu.CompilerParams(dimension_semantics=sem, vmem_limit_bytes=VMEM_LIMIT)


def _sigmoid(x):
    return 1.0 / (1.0 + jnp.exp(-x))


def _silu(x):
    return x * _sigmoid(x)


def _softplus(x):
    return jnp.maximum(x, 0.0) + jnp.log1p(jnp.exp(-jnp.abs(x)))


def _dot_nt(a, b):
    return lax.dot_general(a.astype(BF16), b.astype(BF16), (((1,), (1,)), ((), ())),
                           preferred_element_type=F32)


def _dot_tn(a, b):
    return lax.dot_general(a.astype(BF16), b.astype(BF16), (((0,), (0,)), ((), ())),
                           preferred_element_type=F32)


def _mod_kernel(ct_ref, w_ref, b_ref, o_ref):
    s = _silu(ct_ref[...])
    w = w_ref[0]
    rid = lax.broadcasted_iota(jnp.int32, (8, w.shape[1]), 0)
    out = jnp.zeros((8, w.shape[1]), F32)
    for r in range(3):
        row = jnp.sum(w * s[:, r:r + 1], axis=0, keepdims=True) + b_ref[0]
        out = jnp.where(rid == r, row, out)
    o_ref[0] = out


def _modulation(ct, w_mod, b_mod):
    depth, d, d3 = w_mod.shape
    return pl.pallas_call(
        _mod_kernel,
        grid=(depth, d3 // d),
        in_specs=[pl.BlockSpec((d, LANES), lambda l, j: (0, 0)),
                  pl.BlockSpec((1, d, d), lambda l, j: (l, 0, j)),
                  pl.BlockSpec((1, 1, d), lambda l, j: (l, 0, j))],
        out_specs=pl.BlockSpec((1, 8, d), lambda l, j: (l, 0, j)),
        out_shape=jax.ShapeDtypeStruct((depth, 8, d3), F32),
        compiler_params=_cparams("parallel", "parallel"),
        name="modulation",
    )(ct, w_mod, b_mod.reshape(depth, 1, d3))


def _rope(x, cos, sin):
    lane = lax.broadcasted_iota(jnp.int32, x.shape, 1)
    first = (lane % 32) < 16
    partner = jnp.where(first, pltpu.roll(x, LANES - 16, 1), pltpu.roll(x, 16, 1))
    return x * cos + partner * sin


def _stream_tile(c_ref, x_ref, tile, bi=0):
    is_ctx = (jnp.zeros((TILE, 1), jnp.int32) + tile) == 0
    return jnp.where(is_ctx, c_ref[bi], x_ref[bi])


_IN_COLS = []
_src = 0
for _dst, _w in ((C_ATT_Q, MIX), (C_ATT_K, LANES), (C_ATT_V, LANES), (C_ATT_Z, MIX), (C_GDN_Q, MIX), (C_GDN_K, MIX),
                 (C_GDN_V, MIX), (C_GDN_B, N_DIR * GDN_HEADS), (C_GDN_A, N_DIR * GDN_HEADS), (C_GDN_Z, MIX),
                 (C_LRU_X, MIX), (C_LRU_Z, MIX)):
    _IN_COLS.append((_src, _dst, _w))
    _src += _w
IN_WIDTH = _src


def _scan_order(t):
    return jnp.swapaxes(t.reshape(SEG, STEPS, t.shape[1]), 0, 1).reshape(t.shape)


def _time_order(t):
    return jnp.swapaxes(t.reshape(STEPS, SEG, t.shape[1]), 0, 1).reshape(t.shape)


def _inproj_kernel(c_ref, x_ref, mod_ref, g_ref, cos_ref, sin_ref, w_ref, o_ref, z_ref, lx_ref, w_s, *, n_batch):
    @pl.when(pl.program_id(0) == 0)
    def _():
        lane = lax.broadcasted_iota(jnp.int32, (TILE, LANES), 1)
        n_gate = N_DIR * GDN_HEADS
        for r0 in range(0, w_s.shape[0], TILE):
            for src, dst, width in _IN_COLS:
                if width >= LANES:
                    w_s[r0:r0 + TILE, dst:dst + width] = w_ref[0, src:src + width, r0:r0 + TILE].T.astype(BF16)
            src_b = _IN_COLS[7][0]
            win = w_ref[0, src_b:src_b + LANES, r0:r0 + TILE].T
            w_s[r0:r0 + TILE, C_GDN_B:C_GDN_B + LANES] = jnp.where(lane < n_gate, win, 0.0).astype(BF16)
            w_s[r0:r0 + TILE, C_GDN_A:C_GDN_A + LANES] = jnp.where(
                lane < n_gate, pltpu.roll(win, LANES - n_gate, 1), 0.0).astype(BF16)

    hs = []
    for bi in range(n_batch):
        x = _stream_tile(c_ref, x_ref, pl.program_id(0), bi)
        ms = jnp.mean(x * x, axis=-1, keepdims=True)
        y = x * lax.rsqrt(ms + EPS) * g_ref[...]
        shift = mod_ref[bi, 0, 0:1, :]
        scale = mod_ref[bi, 0, 1:2, :]
        hs.append((y * (1.0 + scale) + shift).astype(BF16))
    hb = jnp.concatenate(hs, axis=0)
    cos = cos_ref[...]
    sin = sin_ref[...]
    for c0 in range(0, N_IN, MIX):
        both = jnp.dot(hb, w_s[:, c0:c0 + MIX], preferred_element_type=F32)
        for bi in range(n_batch):
            res = both[bi * TILE:(bi + 1) * TILE]
            if c0 == C_ATT_Q:
                res = jnp.concatenate(
                    [_rope(res[:, k:k + LANES], cos, sin) for k in range(0, MIX, LANES)], axis=1)
            if c0 == C_ATT_K:
                res = jnp.concatenate([_rope(res[:, :LANES], cos, sin), res[:, LANES:]], axis=1)
            if c0 == C_LRU_X:
                lx_ref[bi] = _scan_order(res)
            elif c0 >= N_P:
                z_ref[bi, :, c0 - N_P:c0 - N_P + MIX] = res.astype(z_ref.dtype)
            else:
                o_ref[bi, :, c0:c0 + MIX] = res


def _inproj(c_src, x_src, off, modsel, g, cos_t, sin_t, w_in, layer):
    b, _, d = x_src.shape
    s = x_src.shape[1] + off * TILE
    nt = s // TILE
    return pl.pallas_call(
        functools.partial(_inproj_kernel, n_batch=b),
        grid=(nt,),
        in_specs=[pl.BlockSpec((b, TILE, d), lambda i: (0, 0, 0)),
                  pl.BlockSpec((b, TILE, d), lambda i: (0, jnp.maximum(i - off, 1 - off), 0)),
                  pl.BlockSpec((b, 1, 8, d), lambda i: (0, jnp.minimum(i, 1), 0, 0)),
                  pl.BlockSpec((1, d), lambda i: (0, 0)),
                  pl.BlockSpec((TILE, LANES), lambda i: (i, 0)),
                  pl.BlockSpec((TILE, LANES), lambda i: (i, 0)),
                  pl.BlockSpec((1, IN_WIDTH, d), lambda i: (layer, 0, 0), pipeline_mode=pl.Buffered(1))],
        out_specs=[pl.BlockSpec((b, TILE, N_P), lambda i: (0, i, 0)),
                   pl.BlockSpec((b, TILE, N_Z), lambda i: (0, i, 0)),
                   pl.BlockSpec((b, TILE, MIX), lambda i: (0, i, 0))],
        out_shape=[jax.ShapeDtypeStruct((b, s, N_P), F32), jax.ShapeDtypeStruct((b, s, N_Z), BF16),
                   jax.ShapeDtypeStruct((b, s, MIX), F32)],
        scratch_shapes=[pltpu.VMEM((d, N_IN), BF16)],
        compiler_params=_cparams("arbitrary"),
        name="inproj",
    )(c_src, x_src, modsel, g, cos_t, sin_t, w_in)


def _att_kernel(sink_ref, q_ref, z_ref, kvp_ref, kvc_ref, kvn_ref, kvx_ref, o_ref, *, n_blocks):
    i = pl.program_id(1)
    n_sub = TILE // ATT_BLOCK
    n_band = 3 * ATT_BLOCK
    half = ATT_HEAD_DIM
    group = ATT_HEADS // ATT_KV_HEADS

    kv = jnp.concatenate([kvp_ref[0], kvc_ref[0], kvn_ref[0], kvx_ref[0]], axis=0)
    nk = kv.shape[0]
    k_all, v_all = kv[:, :LANES], kv[:, LANES:]
    k_sw, v_sw = pltpu.roll(k_all, half, 1), pltpu.roll(v_all, half, 1)
    lo_k = lax.broadcasted_iota(jnp.int32, (nk, LANES), 1) < half
    k_dup = [jnp.where(lo_k, k_all, k_sw).astype(BF16), jnp.where(lo_k, k_sw, k_all).astype(BF16)]
    v_dup = [jnp.where(lo_k, v_all, v_sw).astype(BF16), jnp.where(lo_k, v_sw, v_all).astype(BF16)]
    n_loc = (n_sub + 2) * ATT_BLOCK

    def keys(t, sub):
        return jnp.concatenate([t[sub * ATT_BLOCK:sub * ATT_BLOCK + n_band], t[n_loc:]], axis=0)

    r = lax.broadcasted_iota(jnp.int32, (ATT_BLOCK, n_band), 0)
    c = lax.broadcasted_iota(jnp.int32, (ATT_BLOCK, n_band), 1)
    in_win = jnp.abs(r - (c - ATT_BLOCK)) <= ATT_BLOCK
    lo_q = lax.broadcasted_iota(jnp.int32, (ATT_BLOCK, LANES), 1) < half
    outs = [[None] * (ATT_HEADS // 2) for _ in range(n_sub)]

    def band_bias(sub):
        m = i * n_sub + sub
        c_lo = jnp.where(m >= n_sub + 1, 0, jnp.where(m >= n_sub, ATT_BLOCK, n_band))
        c_hi = jnp.where(m < n_sub, 0, jnp.where(m <= n_blocks - 2, n_band, 2 * ATT_BLOCK))
        return jnp.where(jnp.logical_and(in_win, jnp.logical_and(c >= c_lo, c < c_hi)), 0.0, NEG_INF)

    def stream(sub, h, bias):
        q = q_ref[0, sub * ATT_BLOCK:(sub + 1) * ATT_BLOCK, :] * (ATT_HEAD_DIM ** -0.5 * LOG2E)
        lhs = []
        for j in range(h * group, (h + 1) * group):
            q_t = q[:, (j // 2) * LANES:(j // 2 + 1) * LANES]
            lhs.append(jnp.where(lo_q, q_t, 0.0) if j % 2 == 0 else jnp.where(lo_q, 0.0, q_t))
        qs = jnp.concatenate(lhs, axis=0).astype(BF16)
        s = lax.dot_general(qs, keys(k_dup[h], sub), (((1,), (1,)), ((), ())), preferred_element_type=F32)
        yield
        p_blocks, den_blocks = [], []
        for e in range(group):
            sb = s[e * ATT_BLOCK:(e + 1) * ATT_BLOCK]
            sk = sink_ref[h * group + e] * LOG2E
            s_band = sb[:, :n_band] + bias
            s_ctx = sb[:, n_band:]
            mx = jnp.maximum(jnp.maximum(jnp.max(s_band, axis=-1, keepdims=True),
                                         jnp.max(s_ctx, axis=-1, keepdims=True)), sk)
            p_band = jnp.exp2(s_band - mx)
            p_ctx = jnp.exp2(s_ctx - mx)
            den_blocks.append(jnp.sum(p_band, axis=-1, keepdims=True) + jnp.sum(p_ctx, axis=-1, keepdims=True)
                              + jnp.exp2(sk - mx))
            p_blocks.append(jnp.concatenate([p_band, p_ctx], axis=1).astype(BF16))
            yield
        p = jnp.concatenate(p_blocks, axis=0)
        res = (jnp.dot(p, keys(v_dup[h], sub), preferred_element_type=F32)
               / jnp.concatenate(den_blocks, axis=0))
        for t in range(group // 2):
            even = res[(2 * t) * ATT_BLOCK:(2 * t + 1) * ATT_BLOCK]
            odd = res[(2 * t + 1) * ATT_BLOCK:(2 * t + 2) * ATT_BLOCK]
            outs[sub][h * (group // 2) + t] = jnp.where(lo_q, even, odd)
        yield

    for sub in range(n_sub):
        bias = band_bias(sub)
        gens = [stream(sub, h, bias) for h in range(ATT_KV_HEADS)]
        for g in gens:
            next(g)
        for g in gens:
            for _ in g:
                pass
    for sub in range(n_sub):
        rows = slice(sub * ATT_BLOCK, (sub + 1) * ATT_BLOCK)
        o = jnp.concatenate(outs[sub], axis=1)
        o_ref[0, rows, :] = (o * _silu(z_ref[0, rows, :].astype(F32))).astype(o_ref.dtype)


def _attention(p, pz, sink):
    b, s, _ = p.shape
    nb = s // ATT_BLOCK
    n_sub = TILE // ATT_BLOCK
    kvcol = C_ATT_K // (2 * LANES)
    kv_spec = lambda f: pl.BlockSpec((1, ATT_BLOCK, 2 * LANES), f)
    return pl.pallas_call(
        functools.partial(_att_kernel, n_blocks=nb),
        grid=(b, s // TILE),
        in_specs=[pl.BlockSpec(memory_space=pltpu.SMEM),
                  pl.BlockSpec((1, TILE, MIX), lambda bi, i: (bi, i, C_ATT_Q // MIX)),
                  pl.BlockSpec((1, TILE, MIX), lambda bi, i: (bi, i, (C_ATT_Z - N_P) // MIX)),
                  kv_spec(lambda bi, i: (bi, jnp.maximum(i * n_sub - 1, n_sub), kvcol)),
                  pl.BlockSpec((1, TILE, 2 * LANES), lambda bi, i: (bi, i, kvcol)),
                  kv_spec(lambda bi, i: (bi, jnp.minimum((i + 1) * n_sub, nb - 1), kvcol)),
                  pl.BlockSpec((1, TILE, 2 * LANES), lambda bi, i: (bi, 0, kvcol))],
        out_specs=pl.BlockSpec((1, TILE, MIX), lambda bi, i: (bi, i, 0)),
        out_shape=jax.ShapeDtypeStruct((b, s, MIX), BF16),
        compiler_params=_cparams("parallel", "parallel"),
        name="attention",
    )(sink, p, pz, p, p, p, p)


def _conv4(x, hp, hn, w, prev_ok, next_ok):
    hp = hp * prev_ok
    hn = hn * next_ok
    ext = jnp.concatenate([hp, x, hn], axis=0)
    n = x.shape[0]
    own = slice(HALO, HALO + n)
    back2, back1, fwd1 = (pltpu.roll(ext, sh, 0) for sh in (2, 1, ext.shape[0] - 1))
    return w[0:1] * back2[own] + w[1:2] * back1[own] + w[2:3] * ext[own] + w[3:4] * fwd1[own]


def _halo_ok(i, n_tiles):
    prev_ok = (i >= 2).astype(F32)
    next_ok = jnp.logical_and(i != 0, i != n_tiles - 1).astype(F32)
    return prev_ok, next_ok


def _chunk_cumsum(g, reverse):
    n = g.shape[0]
    row = lax.broadcasted_iota(jnp.int32, g.shape, 0) % GDN_CHUNK
    s = 1
    while s < GDN_CHUNK:
        if reverse:
            g = g + jnp.where(row < GDN_CHUNK - s, pltpu.roll(g, n - s, 0), 0.0)
        else:
            g = g + jnp.where(row >= s, pltpu.roll(g, s, 0), 0.0)
        s *= 2
    return g


def _spread(cols, masks):
    out = cols[-1]
    for m, col in zip(masks[-2::-1], cols[-2::-1]):
        out = jnp.where(m, col, out)
    return out


def _head_block_diag(x, masks):
    return jnp.concatenate([jnp.where(m, x, 0.0) for m in masks], axis=0).astype(BF16)


def _gdn_prep_kernel(q_ref, qp_ref, qn_ref, k_ref, kp_ref, kn_ref, v_ref, vp_ref, vn_ref,
                     b_ref, a_ref, cw_ref, alog_ref, dtb_ref,
                     w_ref, u_ref, kd_ref, qg_ref, aqk_ref, gl_ref, *, n_tiles, n_batch):
    i = pl.program_id(0)
    prev_ok, next_ok = _halo_ok(i, n_tiles)
    hd, ch, nh = GDN_HEAD_DIM, GDN_CHUNK, GDN_HEADS
    nch = TILE // ch

    def prep(bi, x_ref, p_ref, n_ref, col):
        xh = _conv4(x_ref[bi], p_ref[bi], n_ref[bi], cw_ref[:, col:col + MIX], prev_ok, next_ok)
        return xh * (1.0 + jnp.tanh(xh))

    def l2n(t):
        parts = []
        for h in range(nh):
            th = t[:, h * hd:(h + 1) * hd]
            parts.append(th * lax.rsqrt(jnp.sum(th * th, axis=-1, keepdims=True) + EPS))
        return jnp.concatenate(parts, axis=1)

    hp = 2
    pk = hp * ch
    ii = lax.broadcasted_iota(jnp.int32, (ch, pk), 0)
    ll = lax.broadcasted_iota(jnp.int32, (ch, pk), 1)
    jj = ll % ch
    eye_p = jj == ii
    eye_f = eye_p.astype(F32)
    bd16 = (jj // 16) == (ii // 16)
    bd32 = (jj // 32) == (ii // 32)
    merge_masks = (jnp.logical_and(bd32, jnp.logical_not(bd16)), jnp.logical_not(bd32))
    strict = [jj < ii, jj > ii]
    incl = [jj <= ii, jj >= ii]
    head_p = [(ll // ch) == h for h in range(hp)]
    lw = lax.broadcasted_iota(jnp.int32, (ch, hp * hd), 1) // hd
    head_w = [lw == h for h in range(hp)]
    lane_row = lax.broadcasted_iota(jnp.int32, (1, LANES), 1)

    def pdot(a, b):
        return jnp.dot(a.astype(BF16), _head_block_diag(b, head_p), preferred_element_type=F32)

    def build(bi, units):
        q = l2n(prep(bi, q_ref, qp_ref, qn_ref, 0)) * (hd ** -0.5)
        yield
        k = l2n(prep(bi, k_ref, kp_ref, kn_ref, MIX))
        yield
        v = prep(bi, v_ref, vp_ref, vn_ref, 2 * MIX)

        yield
        lane = lax.broadcasted_iota(jnp.int32, (TILE, LANES), 1)
        g = -jnp.exp(alog_ref[...]) * _softplus(a_ref[bi] + dtb_ref[...])
        gcum = jnp.where(lane < nh, _chunk_cumsum(g, False), _chunk_cumsum(g, True))
        beta = _sigmoid(b_ref[bi])

        yield
        gl_rows = []
        for cidx in range(nch):
            rows = slice(cidx * ch, (cidx + 1) * ch)
            gc, bc = gcum[rows], beta[rows]
            g_end = jnp.where(lane_row < nh, gc[ch - 1:ch, :], gc[0:1, :])
            gl_rows.append(jnp.exp(g_end))
            for pair in range(nh // hp):
                lanes = slice(pair * hp * hd, (pair + 1) * hp * hd)
                kc, qc, vc = k[rows, lanes], q[rows, lanes], v[rows, lanes]
                kq = _dot_nt(jnp.concatenate([kc, qc], axis=0), _head_block_diag(kc, head_w))
                for d in range(N_DIR):
                    c0 = d * nh + pair * hp
                    gcols = [gc[:, c0 + h:c0 + h + 1] for h in range(hp)]
                    bcols = [bc[:, c0 + h:c0 + h + 1] for h in range(hp)]
                    gcol_p = _spread(gcols, head_p)
                    grow_p = jnp.sum(jnp.where(eye_p, gcol_p, 0.0), axis=0, keepdims=True)
                    dec = jnp.exp(jnp.where(incl[d], gcol_p - grow_p, 0.0))
                    lm = jnp.where(strict[d], _spread(bcols, head_p) * dec * kq[:ch], 0.0)
                    aqk = jnp.where(incl[d], dec * kq[ch:], 0.0)
                    aqk_ref[bi, rows, d * PACK + pair * pk:d * PACK + (pair + 1) * pk] = aqk.astype(BF16)
                    units.append((bi, rows, d, pair, lm, kc, qc, vc, gcols, bcols,
                                  [g_end[:, c0 + h:c0 + h + 1] for h in range(hp)]))
            yield
        gl_rows.append(jnp.zeros((8 - nch, LANES), F32))
        gl_ref[bi, 0] = jnp.concatenate(gl_rows, axis=0)


    def solve(units):
        lms = [u[4] for u in units]
        m = [jnp.where(bd16, -lm, 0.0) for lm in lms]
        x = [eye_f + mi for mi in m]
        pw = [pdot(mi, mi) for mi in m]
        yield
        for _ in range(2):
            r = [pdot(jnp.concatenate([xi, pi], axis=0), pi) for xi, pi in zip(x, pw)]
            x = [xi + ri[:ch] for xi, ri in zip(x, r)]
            pw = [ri[ch:] for ri in r]
            yield
        x = [xi + pdot(xi, pi) for xi, pi in zip(x, pw)]
        yield
        for mask in merge_masks:
            y = [pdot(xi, jnp.where(mask, lm, 0.0)) for xi, lm in zip(x, lms)]
            yield
            x = [xi - pdot(yi, xi) for xi, yi in zip(x, y)]
            yield

        for tinv, (bi, rows, d, pair, _, kc, qc, vc, gcols, bcols, gend_cols) in zip(x, units):
            slab = lambda t, h: t[:, h * hd:(h + 1) * hd]
            gam = [jnp.exp(gcol) for gcol in gcols]
            bgk = jnp.concatenate([(bcols[h] * gam[h]) * slab(kc, h) for h in range(hp)], axis=1)
            bv = jnp.concatenate([bcols[h] * slab(vc, h) for h in range(hp)], axis=1)
            rhs = jnp.concatenate([_head_block_diag(bgk, head_w), _head_block_diag(bv, head_w)], axis=1)
            wu = jnp.dot(tinv.astype(BF16), rhs, preferred_element_type=F32)
            half = hp * hd
            cols = slice(d * MIX + pair * half, d * MIX + (pair + 1) * half)
            w_ref[bi, rows, cols] = wu[:, :half].astype(BF16)
            u_ref[bi, rows, cols] = wu[:, half:]
            kd_ref[bi, rows, cols] = jnp.concatenate(
                [jnp.exp(gend_cols[h] - gcols[h]) * slab(kc, h) for h in range(hp)], axis=1).astype(BF16)
            qg_ref[bi, rows, cols] = jnp.concatenate([gam[h] * slab(qc, h) for h in range(hp)], axis=1).astype(BF16)
            yield

    units = [[] for _ in range(n_batch)]
    for _ in build(0, units[0]):
        pass
    for bi in range(n_batch):
        stage = [solve(units[bi])]
        if bi + 1 < n_batch:
            stage.append(build(bi + 1, units[bi + 1]))
        _round_robin(stage, (1,) * len(stage))


def _gdn_prep(p, conv_w, alog, dtb):
    b, s, _ = p.shape
    nt = s // TILE
    hpt = TILE // HALO
    nh = s // HALO
    wide = N_DIR * GDN_HEADS * GDN_HEAD_DIM

    def tile_spec(col):
        return pl.BlockSpec((b, TILE, MIX), lambda i: (0, i, col // MIX))

    def prev_spec(col):
        return pl.BlockSpec((b, HALO, MIX), lambda i: (0, jnp.maximum(i * hpt - 1, 0), col // MIX))

    def next_spec(col):
        return pl.BlockSpec((b, HALO, MIX), lambda i: (0, jnp.minimum((i + 1) * hpt, nh - 1), col // MIX))

    in_specs = []
    for col in (C_GDN_Q, C_GDN_K, C_GDN_V):
        in_specs += [tile_spec(col), prev_spec(col), next_spec(col)]
    in_specs += [pl.BlockSpec((b, TILE, LANES), lambda i: (0, i, C_GDN_B // LANES)),
                 pl.BlockSpec((b, TILE, LANES), lambda i: (0, i, C_GDN_A // LANES)),
                 pl.BlockSpec((4, 3 * MIX), lambda i: (0, 0)),
                 pl.BlockSpec((1, LANES), lambda i: (0, 0)),
                 pl.BlockSpec((1, LANES), lambda i: (0, 0))]
    wide_spec = pl.BlockSpec((b, TILE, wide), lambda i: (0, i, 0))
    out_specs = [wide_spec, wide_spec, wide_spec, wide_spec,
                 pl.BlockSpec((b, TILE, N_DIR * PACK), lambda i: (0, i, 0)),
                 pl.BlockSpec((b, 1, 8, LANES), lambda i: (0, i, 0, 0))]
    out_shape = [jax.ShapeDtypeStruct((b, s, wide), BF16),
                 jax.ShapeDtypeStruct((b, s, wide), F32),
                 jax.ShapeDtypeStruct((b, s, wide), BF16),
                 jax.ShapeDtypeStruct((b, s, wide), BF16),
                 jax.ShapeDtypeStruct((b, s, N_DIR * PACK), BF16),
                 jax.ShapeDtypeStruct((b, nt, 8, LANES), F32)]
    return pl.pallas_call(
        functools.partial(_gdn_prep_kernel, n_tiles=nt, n_batch=b),
        grid=(nt,),
        in_specs=in_specs,
        out_specs=out_specs,
        out_shape=out_shape,
        compiler_params=_cparams("parallel"),
        name="gdn_prep",
    )(p, p, p, p, p, p, p, p, p, p, p, conv_w, alog, dtb)


def _gdn_scan_body(bi, wf, uf, kdf, qgf, af, glf, wb, ub, kdb, qgb, ab, glb, of_ref, ob_ref, state):
    hd, ch, nh = GDN_HEAD_DIM, GDN_CHUNK, GDN_HEADS
    nch = TILE // ch
    hp = 2
    lw = lax.broadcasted_iota(jnp.int32, (ch, hp * hd), 1) // hd
    head_w = [lw == h for h in range(hp)]
    dirs = ((wf, uf, kdf, qgf, af, glf, of_ref), (wb, ub, kdb, qgb, ab, glb, ob_ref))
    for step in range(nch):
        cidx = [step, nch - 1 - step]
        rows = [slice(c * ch, (c + 1) * ch) for c in cidx]
        s_old = {}
        ws = {}
        for d, (w_r, _, _, qg_r, _, _, _) in enumerate(dirs):
            for h in range(nh):
                cols = slice(h * hd, (h + 1) * hd)
                s_old[d, h] = state[bi, d, h]
                lhs = jnp.concatenate([w_r[bi, rows[d], cols], qg_r[bi, rows[d], cols]], axis=0)
                ws[d, h] = jnp.dot(lhs, s_old[d, h].astype(BF16), preferred_element_type=F32)
            yield
        for d, (_, u_r, kd_r, _, a_r, gl_r, o_r) in enumerate(dirs):
            un = u_r[bi, rows[d], :] - jnp.concatenate([ws[d, h][:ch] for h in range(nh)], axis=1)
            inter = jnp.concatenate([ws[d, h][ch:] for h in range(nh)], axis=1)
            intra = jnp.concatenate(
                [jnp.dot(a_r[bi, rows[d], g * hp * ch:(g + 1) * hp * ch],
                         _head_block_diag(un[:, g * hp * hd:(g + 1) * hp * hd], head_w),
                         preferred_element_type=F32) for g in range(nh // hp)], axis=1)
            o_r[bi, rows[d], :] = (inter + intra).astype(o_r.dtype)
            unb = un.astype(BF16)
            for h in range(nh):
                cols = slice(h * hd, (h + 1) * hd)
                gl = gl_r[bi, 0, cidx[d]:cidx[d] + 1, d * nh + h:d * nh + h + 1]
                state[bi, d, h] = gl * s_old[d, h] + _dot_tn(kd_r[bi, rows[d], cols], unb[:, cols])
            yield


def _mirror(j, nt):
    return jnp.where(j == 0, 0, nt - j)


def _lru_body(bi, xf_ref, xfp2_ref, xfp_ref, xfn_ref, xb_ref, xbp2_ref, xbp_ref, xbn_ref, cw_ref, cb_ref, wg_ref, bg_ref,
              lam_ref, hf_ref, hb_ref, carry, *, n_tiles):
    j = pl.program_id(0)
    seg_n = SEG
    steps = STEPS
    sub = lax.broadcasted_iota(jnp.int32, (seg_n, LRU_WIDTH), 0)

    def one_dir(d, x_ref, p2_ref, p_ref, n_ref, tile_idx, o_ref):
        prev_ok, next_ok = _halo_ok(tile_idx, n_tiles)
        xs = [x_ref[bi, st * seg_n:(st + 1) * seg_n, :] for st in range(steps)]
        last = seg_n - 1
        p1 = jnp.where(sub == 0, p_ref[bi, last:last + 1, :] * prev_ok, pltpu.roll(xs[steps - 1], 1, 0))
        p2 = jnp.where(sub == 0, p2_ref[bi, last:last + 1, :] * prev_ok, pltpu.roll(xs[steps - 2], 1, 0))
        n1 = jnp.where(sub == last, n_ref[bi, 0:1, :] * next_ok, pltpu.roll(xs[0], last, 0))
        ext = [p2, p1] + xs + [n1]
        cw = cw_ref[...]
        xh = jnp.concatenate(
            [cw[0:1] * ext[st] + cw[1:2] * ext[st + 1] + cw[2:3] * ext[st + 2] + cw[3:4] * ext[st + 3]
             for st in range(steps)], axis=0) + cb_ref[...]
        yield
        gates = jnp.dot(xh.astype(BF16), wg_ref[:, d * 2 * LRU_WIDTH:(d + 1) * 2 * LRU_WIDTH],
                        preferred_element_type=F32)
        yield
        c_half = (-0.5 * LRU_C) * _softplus(-lam_ref[d:d + 1, :])
        a_blocks, b_blocks = [], []
        blk = TILE // 4
        for r0 in range(0, TILE, blk):
            gb = gates[r0:r0 + blk]
            tr = jnp.tanh(gb[:, :LRU_WIDTH] + bg_ref[2 * d:2 * d + 1, :])
            ti = jnp.tanh(gb[:, LRU_WIDTH:] + bg_ref[2 * d + 1:2 * d + 2, :])
            log_a = c_half * tr + c_half
            ab = jnp.exp(log_a)
            b_blocks.append(jnp.sqrt(-jnp.tanh(log_a) * (ab * ab + 1.0)) * ((ti + 1.0) * xh[r0:r0 + blk]))
            a_blocks.append(ab)
            yield
        a = jnp.concatenate(a_blocks, axis=0)
        bb = jnp.concatenate(b_blocks, axis=0)

        order = list(range(steps)) if d == 0 else list(range(steps - 1, -1, -1))
        piece = lambda t, st: t[st * seg_n:(st + 1) * seg_n]
        a_tot = piece(a, order[0])
        b_tot = piece(bb, order[0])
        for n, st in enumerate(order[1:]):
            b_tot = piece(a, st) * b_tot + piece(bb, st)
            a_tot = piece(a, st) * a_tot
            if n % 8 == 7:
                yield
        s = 1
        while s < seg_n:
            if d == 0:
                keep = sub >= s
                a_sh = jnp.where(keep, pltpu.roll(a_tot, s, 0), 1.0)
                b_sh = jnp.where(keep, pltpu.roll(b_tot, s, 0), 0.0)
            else:
                keep = sub < seg_n - s
                a_sh = jnp.where(keep, pltpu.roll(a_tot, seg_n - s, 0), 1.0)
                b_sh = jnp.where(keep, pltpu.roll(b_tot, seg_n - s, 0), 0.0)
            b_tot = a_tot * b_sh + b_tot
            a_tot = a_tot * a_sh
            s *= 2
        h_in0 = carry[bi, d]
        h_out = a_tot * h_in0 + b_tot
        if d == 0:
            h = jnp.where(sub == 0, h_in0, pltpu.roll(h_out, 1, 0))
            carry[bi, d] = h_out[seg_n - 1:seg_n]
        else:
            h = jnp.where(sub == seg_n - 1, h_in0, pltpu.roll(h_out, seg_n - 1, 0))
            carry[bi, d] = h_out[0:1]
        hs = [None] * steps
        for n, st in enumerate(order):
            h = piece(a, st) * h + piece(bb, st)
            hs[st] = h
            if n % 8 == 7:
                yield
        o_ref[bi] = jnp.concatenate(hs, axis=0).astype(o_ref.dtype)

    return [one_dir(0, xf_ref, xfp2_ref, xfp_ref, xfn_ref, j, hf_ref),
            one_dir(1, xb_ref, xbp2_ref, xbp_ref, xbn_ref, _mirror(j, n_tiles), hb_ref)]


N_SCAN_IN = 12
N_LRU_IN = 13


def _round_robin(gens, weights):
    live = list(zip(gens, weights))
    while live:
        nxt = []
        for g, w in live:
            alive = True
            for _ in range(w):
                try:
                    next(g)
                except StopIteration:
                    alive = False
                    break
            if alive:
                nxt.append((g, w))
        live = nxt


def _sweep_kernel(*refs, n_tiles, n_batch):
    scan_in, lru_in = refs[:N_SCAN_IN], refs[N_SCAN_IN:N_SCAN_IN + N_LRU_IN]
    of_ref, ob_ref, hf_ref, hb_ref, state, carry = refs[N_SCAN_IN + N_LRU_IN:]

    @pl.when(pl.program_id(0) == 0)
    def _():
        state[...] = jnp.zeros_like(state)
        carry[...] = jnp.zeros_like(carry)

    scans = [_gdn_scan_body(bi, *scan_in, of_ref, ob_ref, state) for bi in range(n_batch)]
    lrus = [g for bi in range(n_batch) for g in _lru_body(bi, *lru_in, hf_ref, hb_ref, carry, n_tiles=n_tiles)]
    _round_robin(scans + lrus, (1,) * len(scans) + (2,) * len(lrus))


def _sweep(w, u, kd, qg, aqk, gl, lx, conv_w, conv_b, wg, bg, lam):
    b, s, _ = u.shape
    nt = s // TILE
    half = GDN_HEADS * GDN_HEAD_DIM
    hpt = TILE // HALO
    nh = s // HALO
    tile_of = (lambda j: j, lambda j: _mirror(j, nt))

    def scan_specs(d):
        idx = lambda j: (0, tile_of[d](j), d)
        gidx = lambda j: (0, tile_of[d](j), 0, 0)
        return [pl.BlockSpec((b, TILE, half), idx)] * 4 + [pl.BlockSpec((b, TILE, PACK), idx),
                                                           pl.BlockSpec((b, 1, 8, LANES), gidx)]

    def lru_specs(d):
        t = tile_of[d]
        return [pl.BlockSpec((b, TILE, MIX), lambda j: (0, t(j), 0)),
                pl.BlockSpec((b, HALO, MIX), lambda j: (0, jnp.maximum(t(j) * hpt - 2, 0), 0)),
                pl.BlockSpec((b, HALO, MIX), lambda j: (0, jnp.maximum(t(j) * hpt - 1, 0), 0)),
                pl.BlockSpec((b, HALO, MIX), lambda j: (0, jnp.minimum((t(j) + 1) * hpt, nh - 1), 0))]

    full = lambda shape: pl.BlockSpec(shape, lambda j: tuple(0 for _ in shape))
    out_spec = lambda d: pl.BlockSpec((b, TILE, MIX), lambda j: (0, tile_of[d](j), 0))
    in_specs = (scan_specs(0) + scan_specs(1) + lru_specs(0) + lru_specs(1)
                + [full((4, LRU_WIDTH)), full((1, LRU_WIDTH)), full((LRU_WIDTH, 4 * LRU_WIDTH)),
                   full((4, LRU_WIDTH)), full((N_DIR, LRU_WIDTH))])
    assert len(in_specs) == N_SCAN_IN + N_LRU_IN
    return pl.pallas_call(
        functools.partial(_sweep_kernel, n_tiles=nt, n_batch=b),
        grid=(nt,),
        in_specs=in_specs,
        out_specs=[out_spec(0), out_spec(1), out_spec(0), out_spec(1)],
        out_shape=[jax.ShapeDtypeStruct((b, s, MIX), BF16)] * 4,
        scratch_shapes=[pltpu.VMEM((b, N_DIR, GDN_HEADS, GDN_HEAD_DIM, GDN_HEAD_DIM), F32),
                        pltpu.VMEM((b, N_DIR, 1, LRU_WIDTH), F32)],
        compiler_params=_cparams("arbitrary"),
        name="sweep",
    )(w, u, kd, qg, aqk, gl, w, u, kd, qg, aqk, gl, *([lx] * 8), conv_w, conv_b, wg, bg, lam)


def _outproj_kernel(c_ref, x_ref, mod_ref, att_ref, of_ref, ob_ref, gz_ref, hf_ref, hb_ref, lz_ref,
                    gn_ref, w_ref, fg_ref, o_ref, *, final, first_tile, n_batch):
    hd = GDN_HEAD_DIM
    gdns, lrus = [], []
    for bi in range(n_batch):
        o = of_ref[bi].astype(F32) + ob_ref[bi].astype(F32)
        parts = []
        for h in range(GDN_HEADS):
            oh = o[:, h * hd:(h + 1) * hd]
            parts.append(oh * lax.rsqrt(jnp.mean(oh * oh, axis=-1, keepdims=True) + EPS) * gn_ref[...])
        gdns.append((jnp.concatenate(parts, axis=1) * _silu(gz_ref[bi].astype(F32))).astype(BF16))
        lrus.append((_time_order(hf_ref[bi].astype(F32) + hb_ref[bi].astype(F32))
                     * _silu(lz_ref[bi].astype(F32))).astype(BF16))
    att = jnp.concatenate([att_ref[bi] for bi in range(n_batch)], axis=0)
    acc = (jnp.dot(att, w_ref[0:MIX, :], preferred_element_type=F32)
           + jnp.dot(jnp.concatenate(gdns, axis=0), w_ref[MIX:2 * MIX, :], preferred_element_type=F32)
           + jnp.dot(jnp.concatenate(lrus, axis=0), w_ref[2 * MIX:3 * MIX, :], preferred_element_type=F32))
    for bi in range(n_batch):
        gate = mod_ref[bi, 0, 2:3, :]
        xn = _stream_tile(c_ref, x_ref, pl.program_id(0) + first_tile, bi) + gate * acc[bi * TILE:(bi + 1) * TILE]
        if final:
            ms = jnp.mean(xn * xn, axis=-1, keepdims=True)
            xn = xn * lax.rsqrt(ms + EPS) * fg_ref[...]
        o_ref[bi] = xn


def _outproj(c_src, x_src, src_off, modsel, att, of, ob, pz, hf, hb, gdn_norm, w_out, final_g, final):
    b, s, _ = pz.shape
    d = x_src.shape[2]
    nt = s // TILE
    off = 1 if final else 0
    n_out = nt - off
    tok = lambda col: pl.BlockSpec((b, TILE, MIX), lambda i: (0, i + off, col))
    return pl.pallas_call(
        functools.partial(_outproj_kernel, final=final, first_tile=off, n_batch=b),
        grid=(n_out,),
        in_specs=[pl.BlockSpec((b, TILE, d), lambda i: (0, 0, 0)),
                  pl.BlockSpec((b, TILE, d), lambda i: (0, jnp.maximum(i + off - src_off, 1 - src_off), 0)),
                  pl.BlockSpec((b, 1, 8, d), lambda i: (0, jnp.minimum(i + off, 1), 0, 0)),
                  tok(0), tok(0), tok(0), tok((C_GDN_Z - N_P) // MIX), tok(0), tok(0), tok((C_LRU_Z - N_P) // MIX),
                  pl.BlockSpec((1, GDN_HEAD_DIM), lambda i: (0, 0)),
                  pl.BlockSpec((3 * MIX, d), lambda i: (0, 0)),
                  pl.BlockSpec((1, d), lambda i: (0, 0))],
        out_specs=pl.BlockSpec((b, TILE, d), lambda i: (0, i, 0)),
        out_shape=jax.ShapeDtypeStruct((b, n_out * TILE, d), F32),
        compiler_params=_cparams("parallel"),
        name="outproj",
    )(c_src, x_src, modsel, att, of, ob, pz, hf, hb, pz, gdn_norm, w_out, final_g)


def _rope_tables(s, ctx_len):
    t = np.arange(s - ctx_len)
    n_freq = ATT_HEAD_DIM // 4
    inv = ROPE_BASE ** (-np.arange(n_freq, dtype=np.float64) / n_freq)
    ang = [(t // GRID_W)[:, None] * inv, (t % GRID_W)[:, None] * inv]
    cos = np.concatenate([np.cos(ang[0]), np.cos(ang[0]), np.cos(ang[1]), np.cos(ang[1])], axis=1)
    sin = np.concatenate([-np.sin(ang[0]), np.sin(ang[0]), -np.sin(ang[1]), np.sin(ang[1])], axis=1)
    cos = np.concatenate([np.ones((ctx_len, ATT_HEAD_DIM)), cos], axis=0)
    sin = np.concatenate([np.zeros((ctx_len, ATT_HEAD_DIM)), sin], axis=0)
    return (jnp.asarray(np.tile(cos, (1, 2)), F32), jnp.asarray(np.tile(sin, (1, 2)), F32))


def _block_diag(w):
    eye = jnp.eye(LRU_BLOCKS, dtype=w.dtype)
    return jnp.einsum("nde,nm->ndme", w, eye).reshape(LRU_WIDTH, LRU_WIDTH)


def _pad_lanes(v):
    return jnp.pad(v.reshape(1, -1), ((0, 0), (0, LANES - v.size)))


def kernel(x, c, ctx, c_ctx, norm_g, w_mod, b_mod, w_in, att_sink, gdn_conv, gdn_a_log, gdn_dt_bias, gdn_norm,
           lru_conv_w, lru_conv_b, lru_w_r, lru_b_r, lru_w_i, lru_b_i, lru_lambda, w_out, final_g):
    b, t, d = x.shape
    ctx_len = ctx.shape[1]
    depth = w_in.shape[0]
    assert ctx_len == TILE and t % TILE == 0 and d == D_MODEL and b == 2
    s = ctx_len + t
    nt = s // TILE

    cos_t, sin_t = _rope_tables(s, ctx_len)
    ct = jnp.pad(jnp.concatenate([c, c_ctx[None]], axis=0).T, ((0, 0), (0, LANES - b - 1)))
    mods = _modulation(ct, w_mod, b_mod)

    w_in_t = jnp.swapaxes(w_in, 1, 2)
    c_src, x_src, src_off = ctx, x, 1
    out = None
    for l in range(depth):
        m3 = mods[l, :3].reshape(3, 3, d)
        sel = jnp.stack([jnp.stack([m3[b], m3[bi]], axis=0) for bi in range(b)], axis=0)
        modsel = jnp.pad(sel, ((0, 0), (0, 0), (0, 5), (0, 0)))
        p, pz, lx = _inproj(c_src, x_src, src_off, modsel, norm_g[l].reshape(1, d), cos_t, sin_t, w_in_t, l)

        att = _attention(p, pz, att_sink[l])

        prep = _gdn_prep(p, 0.5 * gdn_conv[l], _pad_lanes(gdn_a_log[l]), _pad_lanes(gdn_dt_bias[l]))

        wg = jnp.concatenate([_block_diag(lru_w_r[l, 0]), _block_diag(lru_w_i[l, 0]),
                              _block_diag(lru_w_r[l, 1]), _block_diag(lru_w_i[l, 1])], axis=1).astype(BF16)
        bg = 0.5 * jnp.stack([lru_b_r[l, 0], lru_b_i[l, 0], lru_b_r[l, 1], lru_b_i[l, 1]], axis=0)
        o_f, o_b, h_f, h_b = _sweep(*prep, lx, 0.5 * lru_conv_w[l], 0.5 * lru_conv_b[l].reshape(1, -1), wg, bg,
                                    lru_lambda[l])

        final = l == depth - 1
        res = _outproj(c_src, x_src, src_off, modsel, att, o_f, o_b, pz, h_f, h_b, gdn_norm[l].reshape(1, -1),
                       w_out[l].astype(BF16), final_g.reshape(1, d), final)
        if final:
            out = res
        else:
            c_src, x_src, src_off = res, res, 0
    return out
```
